```python
import jax, jax.numpy as jnp
from jax import lax
import numpy as np

D_MODEL = 2048
BATCH = 32
SEQ = 256
DEPTH = 4
DEC_BATCH = 4
DEC_SEQ = 4096
PAST_LEN = 512

GRID_W = 64
MLA_HEADS = 8
MLA_NOPE = 128
MLA_ROPE = 64
MLA_QK = MLA_NOPE + MLA_ROPE
MLA_V = 128
MLA_KV_RANK = 512
HG_HEADS = 8
HG_K = 128
HG_V = 128
HG_CHUNK = 32
NA_HEADS = 8
NA_DH = 128
NA_WIN_R = 8
NA_WIN_C = 16
NA_RPB_R = 2 * NA_WIN_R - 1
NA_RPB_C = 2 * NA_WIN_C - 1
N_BRANCH = 3
BRANCH_W = 1024
N_IN = MLA_HEADS * MLA_QK + MLA_KV_RANK + MLA_ROPE + 3 * HG_HEADS * HG_K + 2 * HG_HEADS * HG_V + 3 * NA_HEADS * NA_DH + N_BRANCH * D_MODEL
FFN_DIM = 5632
CONV_W = 3
ROPE_BASE = 10000.0
EPS = 1e-6
Q_BLOCK = 128
NEG = -1e30

kernel_name = 'hybrid_mla_hgrn2_natten_prefix_dit_step'


def _rms_norm(x, g):
    xf = x.astype(jnp.float32)
    y = xf * lax.rsqrt(jnp.mean(xf * xf, axis=-1, keepdims=True) + EPS)
    return (y * g.astype(jnp.float32)).astype(x.dtype)


def _split_in(z):
    sizes = (MLA_HEADS * MLA_QK, MLA_KV_RANK, MLA_ROPE,
             HG_HEADS * HG_K, HG_HEADS * HG_K, HG_HEADS * HG_K, HG_HEADS * HG_V, HG_HEADS * HG_V,
             NA_HEADS * NA_DH, NA_HEADS * NA_DH, NA_HEADS * NA_DH,
             D_MODEL, D_MODEL, D_MODEL)
    bounds = np.cumsum(np.array(sizes))[:-1].tolist()
    return jnp.split(z, bounds, axis=-1)


def _modulation(cond, w, b):
    m = jax.nn.silu(cond) @ w + b
    return jnp.split(m[:, None, :], 6, axis=-1)


def _rope_angles(T):
    t = jnp.arange(T, dtype=jnp.int32)
    n_freq = MLA_ROPE // 4
    inv = ROPE_BASE ** (-jnp.arange(n_freq, dtype=jnp.float32) / n_freq)
    row = (t // GRID_W).astype(jnp.float32)
    col = (t % GRID_W).astype(jnp.float32)
    return row[:, None] * inv, col[:, None] * inv


def _rotate_half(x, ang):
    n = ang.shape[-1]
    cos = jnp.cos(ang)[:, None, :].astype(x.dtype)
    sin = jnp.sin(ang)[:, None, :].astype(x.dtype)
    x1, x2 = x[..., :n], x[..., n:]
    return jnp.concatenate([x1 * cos - x2 * sin, x1 * sin + x2 * cos], axis=-1)


def _axial_rope(x, ang_r, ang_c):
    h = x.shape[-1] // 2
    return jnp.concatenate([_rotate_half(x[..., :h], ang_r), _rotate_half(x[..., h:], ang_c)], axis=-1)


def _blocked_attention(q, k, v, scale):
    B, T, H, dq = q.shape
    nb = T // Q_BLOCK
    qb = q.reshape(B, nb, Q_BLOCK, H, dq).transpose(1, 0, 2, 3, 4)

    def one(qi):
        s = jnp.einsum('bqhd,bkhd->bhqk', qi, k).astype(jnp.float32) * scale
        p = jax.nn.softmax(s, axis=-1).astype(v.dtype)
        return jnp.einsum('bhqk,bkhd->bqhd', p, v)

    o = lax.map(one, qb)
    return o.transpose(1, 0, 2, 3, 4).reshape(B, T, H, v.shape[-1])


def _mla_kv(ckv, kpe, w_uk, w_uv, k_g):
    k_nope = jnp.einsum('bsr,rhd->bshd', ckv, w_uk)
    k_pe = jnp.broadcast_to(kpe[:, :, None, :], kpe.shape[:2] + (w_uk.shape[1], kpe.shape[-1]))
    k = _rms_norm(jnp.concatenate([k_nope, k_pe], axis=-1), k_g)
    v = jnp.einsum('bsr,rhd->bshd', ckv, w_uv)
    return k, v


def _hgrn_gates(f_raw, lb):
    x32 = f_raw.astype(jnp.float32)
    log_f = jnp.logaddexp(jnp.log(lb), jnp.log1p(-lb) + jax.nn.log_sigmoid(x32))
    k = (1.0 - lb) * jax.nn.sigmoid(-x32)
    return log_f, k


def _hgrn_scan(q, k, v, log_f, s0):
    B, T, H, K = q.shape
    V = v.shape[-1]
    n = T // HG_CHUNK

    def chunks(a):
        return a.reshape(B, n, HG_CHUNK, H, a.shape[-1]).transpose(1, 0, 3, 2, 4)

    causal = jnp.tril(jnp.ones((HG_CHUNK, HG_CHUNK), dtype=bool))[:, :, None]

    def step(S, inp):
        qi, ki, vi, fi = inp
        qf = qi.astype(jnp.float32)
        vf = vi.astype(jnp.float32)
        b = jnp.cumsum(fi, axis=2)
        diff = b[:, :, :, None, :] - b[:, :, None, :, :]
        decay = jnp.exp(jnp.where(causal, diff, -jnp.inf))
        attn = jnp.einsum('bhtk,bhsk,bhtsk->bhts', qf, ki, decay)
        o = jnp.einsum('bhts,bhsv->bhtv', attn, vf) + jnp.einsum('bhtk,bhkv->bhtv', qf * jnp.exp(b), S)
        b_last = b[:, :, -1:, :]
        S_new = jnp.exp(b_last[:, :, 0, :])[..., None] * S + jnp.einsum('bhsk,bhsv->bhkv', ki * jnp.exp(b_last - b), vf)
        return S_new, o

    sT, o = lax.scan(step, s0.astype(jnp.float32), (chunks(q), chunks(k), chunks(v), chunks(log_f)))
    o = o.transpose(1, 0, 3, 2, 4).reshape(B, T, H, V)
    return o.astype(v.dtype), sT.astype(v.dtype)


def _hgrn_branch(hq, hff, hfb, hi, hg, lb_f, lb_b, s0_f, s0_b, norm_g):
    B, T = hq.shape[:2]
    q = jax.nn.silu(hq).reshape(B, T, HG_HEADS, HG_K)
    v = hi.reshape(B, T, HG_HEADS, HG_V)
    lf_f, k_f = _hgrn_gates(hff.reshape(B, T, HG_HEADS, HG_K), lb_f.reshape(HG_HEADS, HG_K))
    lf_b, k_b = _hgrn_gates(hfb.reshape(B, T, HG_HEADS, HG_K), lb_b.reshape(HG_HEADS, HG_K))
    o_f, s_f = _hgrn_scan(q, k_f, v, lf_f, s0_f)
    o_b, s_b = _hgrn_scan(q[:, ::-1], k_b[:, ::-1], v[:, ::-1], lf_b[:, ::-1], s0_b)
    o = o_f + o_b[:, ::-1]
    o = _rms_norm(o, norm_g) * jax.nn.silu(hg.reshape(B, T, HG_HEADS, HG_V))
    return o.reshape(B, T, HG_HEADS * HG_V), jnp.stack([s_f, s_b], axis=1)


def _neighborhood_attention(q, k, v, k_ctx, v_ctx, rpb, scale):
    B, T, H, Dh = q.shape
    rows = T // GRID_W
    wr = min(NA_WIN_R, rows)
    wc = NA_WIN_C
    qr = q.reshape(B, rows, GRID_W, H, Dh)
    kr = k.reshape(B, rows, GRID_W, H, Dh)
    vr = v.reshape(B, rows, GRID_W, H, Dh)
    qcol = jnp.arange(GRID_W)
    cs = jnp.clip(qcol - wc // 2, 0, GRID_W - wc)
    in_win = (qcol[None, :] >= cs[:, None]) & (qcol[None, :] < cs[:, None] + wc)
    dc_idx = jnp.clip(qcol[None, :] - qcol[:, None], -(wc - 1), wc - 1) + (NA_WIN_C - 1)
    n_loc = wr * GRID_W

    def row_block(r):
        rs = jnp.clip(r - NA_WIN_R // 2, 0, rows - wr)
        kb = lax.dynamic_slice_in_dim(kr, rs, wr, axis=1)
        vb = lax.dynamic_slice_in_dim(vr, rs, wr, axis=1)
        qb = lax.dynamic_index_in_dim(qr, r, axis=1, keepdims=False)
        dr_idx = rs + jnp.arange(wr) - r + (NA_WIN_R - 1)
        bias = rpb[:, dr_idx[None, :, None], dc_idx[:, None, :]].astype(jnp.float32)
        s_loc = jnp.einsum('bqhd,bikhd->bhqik', qb, kb).astype(jnp.float32) * scale + bias
        s_loc = jnp.where(in_win[:, None, :], s_loc, NEG).reshape(B, H, GRID_W, n_loc)
        s_ctx = jnp.einsum('bqhd,bphd->bhqp', qb, k_ctx).astype(jnp.float32) * scale
        p = jax.nn.softmax(jnp.concatenate([s_loc, s_ctx], axis=-1), axis=-1).astype(v.dtype)
        p_loc = p[..., :n_loc].reshape(B, H, GRID_W, wr, GRID_W)
        p_ctx = p[..., n_loc:]
        return jnp.einsum('bhqik,bikhd->bqhd', p_loc, vb) + jnp.einsum('bhqp,bphd->bqhd', p_ctx, v_ctx)

    out = lax.map(row_block, jnp.arange(rows))
    return out.transpose(1, 0, 2, 3, 4).reshape(B, T, H, Dh)


def _merge(o_mla, o_hg, o_na, ga, gb, gc, w_branch, w_out):
    m = (jax.nn.sigmoid(ga) * (o_mla @ w_branch[0])
         + jax.nn.sigmoid(gb) * (o_hg @ w_branch[1])
         + jax.nn.sigmoid(gc) * (o_na @ w_branch[2]))
    return m @ w_out


def _conv_ffn(h, w_up, conv_w, conv_b, w_down):
    a, g = jnp.split(h @ w_up, 2, axis=-1)
    gp = jnp.pad(g, ((0, 0), (1, 1), (0, 0)))
    g = gp[:, :-2] * conv_w[0] + gp[:, 1:-1] * conv_w[1] + gp[:, 2:] * conv_w[2] + conv_b
    return (jax.nn.silu(g) * a) @ w_down


def _context_layer(x, mod, lp, lb_f, lb_b):
    sh1, sc1, g1, sh2, sc2, g2 = mod
    B, S, _ = x.shape
    h = _rms_norm(x, lp['norm1_g']) * (1.0 + sc1) + sh1
    (mq, ckv_raw, kpe, hq, hff, hfb, hi, hg, nq, nk, nv, ga, gb, gc) = _split_in(h @ lp['w_in'])
    q = _rms_norm(mq.reshape(B, S, MLA_HEADS, MLA_QK), lp['mla_q_g'])
    ckv = _rms_norm(ckv_raw, lp['mla_kv_g'])
    k, v = _mla_kv(ckv, kpe, lp['mla_w_uk'], lp['mla_w_uv'], lp['mla_k_g'])
    o_mla = _blocked_attention(q, k, v, MLA_QK ** -0.5).reshape(B, S, MLA_HEADS * MLA_V)
    zeros = jnp.zeros((B, HG_HEADS, HG_K, HG_V), jnp.float32)
    o_hg, s_hg = _hgrn_branch(hq, hff, hfb, hi, hg, lb_f, lb_b, zeros, zeros, lp['hg_norm_g'])
    nqh = _rms_norm(nq.reshape(B, S, NA_HEADS, NA_DH), lp['na_q_g'])
    nkh = _rms_norm(nk.reshape(B, S, NA_HEADS, NA_DH), lp['na_k_g'])
    nvh = nv.reshape(B, S, NA_HEADS, NA_DH)
    o_na = _blocked_attention(nqh, nkh, nvh, NA_DH ** -0.5).reshape(B, S, NA_HEADS * NA_DH)
    x = x + g1 * _merge(o_mla, o_hg, o_na, ga, gb, gc, lp['w_branch'], lp['w_out'])
    h2 = _rms_norm(x, lp['norm2_g']) * (1.0 + sc2) + sh2
    x = x + g2 * _conv_ffn(h2, lp['ffn_w_up'], lp['ffn_conv_w'], lp['ffn_conv_b'], lp['ffn_w_down'])
    return x, (ckv, kpe, nkh, nvh, s_hg)


def _latent_layer(x, mod, lp, lb_f, lb_b, ckv_c, kpe_c, nk_c, nv_c, s_c, ang_r, ang_c):
    sh1, sc1, g1, sh2, sc2, g2 = mod
    B, T, _ = x.shape
    h = _rms_norm(x, lp['norm1_g']) * (1.0 + sc1) + sh1
    (mq, ckv_raw, kpe, hq, hff, hfb, hi, hg, nq, nk, nv, ga, gb, gc) = _split_in(h @ lp['w_in'])
    q = _rms_norm(mq.reshape(B, T, MLA_HEADS, MLA_QK), lp['mla_q_g'])
    q = jnp.concatenate([q[..., :MLA_NOPE], _axial_rope(q[..., MLA_NOPE:], ang_r, ang_c)], axis=-1)
    k_l, v_l = _mla_kv(_rms_norm(ckv_raw, lp['mla_kv_g']), kpe, lp['mla_w_uk'], lp['mla_w_uv'], lp['mla_k_g'])
    k_l = jnp.concatenate([k_l[..., :MLA_NOPE], _axial_rope(k_l[..., MLA_NOPE:], ang_r, ang_c)], axis=-1)
    k_c, v_c = _mla_kv(ckv_c, kpe_c, lp['mla_w_uk'], lp['mla_w_uv'], lp['mla_k_g'])
    o_mla = _blocked_attention(q, jnp.concatenate([k_l, k_c], axis=1), jnp.concatenate([v_l, v_c], axis=1),
                               MLA_QK ** -0.5).reshape(B, T, MLA_HEADS * MLA_V)
    o_hg, _ = _hgrn_branch(hq, hff, hfb, hi, hg, lb_f, lb_b, s_c[:, 0], s_c[:, 1], lp['hg_norm_g'])
    nqh = _rms_norm(nq.reshape(B, T, NA_HEADS, NA_DH), lp['na_q_g'])
    nkh = _rms_norm(nk.reshape(B, T, NA_HEADS, NA_DH), lp['na_k_g'])
    nvh = nv.reshape(B, T, NA_HEADS, NA_DH)
    o_na = _neighborhood_attention(nqh, nkh, nvh, nk_c, nv_c, lp['na_rpb'], NA_DH ** -0.5).reshape(B, T, NA_HEADS * NA_DH)
    x = x + g1 * _merge(o_mla, o_hg, o_na, ga, gb, gc, lp['w_branch'], lp['w_out'])
    h2 = _rms_norm(x, lp['norm2_g']) * (1.0 + sc2) + sh2
    return x + g2 * _conv_ffn(h2, lp['ffn_w_up'], lp['ffn_conv_w'], lp['ffn_conv_b'], lp['ffn_w_down'])


def setup_inputs(seed: int = 0) -> dict:
    key = jax.random.key(seed)
    ks = jax.random.split(key, 32)
    D = D_MODEL

    def nrm(k, shape, scale):
        return jax.random.normal(k, shape, jnp.float32) * scale

    def gain(k, shape):
        return 1.0 + 0.05 * jax.random.normal(k, shape, jnp.float32)

    return {
        'x_prompt': nrm(ks[0], (BATCH, SEQ, D), 1.0),
        'x_sample': nrm(ks[1], (DEC_BATCH, DEC_SEQ, D), 1.0),
        'cache_mla_ckv': nrm(ks[2], (DEC_BATCH, DEPTH, PAST_LEN, MLA_KV_RANK), 1.0),
        'cache_mla_kpe': nrm(ks[3], (DEC_BATCH, DEPTH, PAST_LEN, MLA_ROPE), 1.0),
        'cache_na_k': nrm(ks[4], (DEC_BATCH, DEPTH, PAST_LEN, NA_HEADS, NA_DH), 1.0),
        'cache_na_v': nrm(ks[5], (DEC_BATCH, DEPTH, PAST_LEN, NA_HEADS, NA_DH), 1.0),
        'state_hgrn': nrm(ks[6], (DEC_BATCH, DEPTH, 2, HG_HEADS, HG_K, HG_V), 0.5),
        'c': nrm(ks[7], (DEC_BATCH, D), 1.0),
        'c_ctx': nrm(ks[8], (D,), 1.0),
        'w_mod': nrm(ks[9], (DEPTH, D, 6 * D), 0.5 * D ** -0.5),
        'b_mod': nrm(ks[10], (DEPTH, 6 * D), 0.01),
        'norm1_g': gain(ks[11], (DEPTH, D)),
        'norm2_g': gain(ks[12], (DEPTH, D)),
        'w_in': nrm(ks[13], (DEPTH, D, N_IN), D ** -0.5),
        'mla_kv_norm_g': gain(ks[14], (DEPTH, MLA_KV_RANK)),
        'mla_q_norm_g': gain(ks[15], (DEPTH, MLA_QK)),
        'mla_k_norm_g': gain(ks[16], (DEPTH, MLA_QK)),
        'mla_w_uk': nrm(ks[17], (DEPTH, MLA_KV_RANK, MLA_HEADS, MLA_NOPE), MLA_KV_RANK ** -0.5),
        'mla_w_uv': nrm(ks[18], (DEPTH, MLA_KV_RANK, MLA_HEADS, MLA_V), MLA_KV_RANK ** -0.5),
        'hgrn_lower_bounds': nrm(ks[19], (2, DEPTH, HG_HEADS * HG_K), 1.0),
        'hgrn_norm_g': gain(ks[20], (DEPTH, HG_V)),
        'na_q_norm_g': gain(ks[21], (DEPTH, NA_DH)),
        'na_k_norm_g': gain(ks[22], (DEPTH, NA_DH)),
        'na_rpb': nrm(ks[23], (DEPTH, NA_HEADS, NA_RPB_R, NA_RPB_C), 0.2),
        'w_branch': nrm(ks[24], (DEPTH, N_BRANCH, BRANCH_W, D), BRANCH_W ** -0.5),
        'w_out': nrm(ks[25], (DEPTH, D, D), D ** -0.5),
        'ffn_w_up': nrm(ks[26], (DEPTH, D, 2 * FFN_DIM), D ** -0.5),
        'ffn_conv_w': nrm(ks[27], (DEPTH, CONV_W, FFN_DIM), CONV_W ** -0.5),
        'ffn_conv_b': nrm(ks[28], (DEPTH, FFN_DIM), 0.01),
        'ffn_w_down': nrm(ks[29], (DEPTH, FFN_DIM, D), FFN_DIM ** -0.5),
    }


def reference(x_prompt, x_sample, cache_mla_ckv, cache_mla_kpe, cache_na_k, cache_na_v, state_hgrn, c, c_ctx,
              w_mod, b_mod, norm1_g, norm2_g, w_in, mla_kv_norm_g, mla_q_norm_g, mla_k_norm_g, mla_w_uk, mla_w_uv,
              hgrn_lower_bounds, hgrn_norm_g, na_q_norm_g, na_k_norm_g, na_rpb, w_branch, w_out,
              ffn_w_up, ffn_conv_w, ffn_conv_b, ffn_w_down):
    sm = jax.nn.softmax(hgrn_lower_bounds.astype(jnp.float32), axis=1)
    cs = jnp.cumsum(sm, axis=1)
    lower = cs - cs[:, :1]
    ang_r, ang_c = _rope_angles(x_sample.shape[1])
    xp, xs = x_prompt, x_sample
    ckv_l, kpe_l, nk_l, nv_l, hg_l = [], [], [], [], []
    for l in range(DEPTH):
        lp = {
            'norm1_g': norm1_g[l], 'norm2_g': norm2_g[l], 'w_in': w_in[l],
            'mla_kv_g': mla_kv_norm_g[l], 'mla_q_g': mla_q_norm_g[l], 'mla_k_g': mla_k_norm_g[l],
            'mla_w_uk': mla_w_uk[l], 'mla_w_uv': mla_w_uv[l], 'hg_norm_g': hgrn_norm_g[l],
            'na_q_g': na_q_norm_g[l], 'na_k_g': na_k_norm_g[l], 'na_rpb': na_rpb[l],
            'w_branch': w_branch[l], 'w_out': w_out[l], 'ffn_w_up': ffn_w_up[l],
            'ffn_conv_w': ffn_conv_w[l], 'ffn_conv_b': ffn_conv_b[l], 'ffn_w_down': ffn_w_down[l],
        }
        mod_ctx = _modulation(c_ctx[None, :], w_mod[l], b_mod[l])
        mod_lat = _modulation(c, w_mod[l], b_mod[l])
        xp, (ckv, kpe, nkh, nvh, s_hg) = _context_layer(xp, mod_ctx, lp, lower[0, l], lower[1, l])
        ckv_l.append(ckv)
        kpe_l.append(kpe)
        nk_l.append(nkh)
        nv_l.append(nvh)
        hg_l.append(s_hg)
        xs = _latent_layer(xs, mod_lat, lp, lower[0, l], lower[1, l], cache_mla_ckv[:, l], cache_mla_kpe[:, l],
                           cache_na_k[:, l], cache_na_v[:, l], state_hgrn[:, l], ang_r, ang_c)
    state_mla_ckv = jnp.stack(ckv_l, axis=1)
    state_mla_kpe = jnp.stack(kpe_l, axis=1)
    state_na_k = jnp.stack(nk_l, axis=1)
    state_na_v = jnp.stack(nv_l, axis=1)
    new_state_hgrn = jnp.stack(hg_l, axis=1)
    return (xp, xs, state_mla_ckv, state_mla_kpe, state_na_k, state_na_v, new_state_hgrn)
```

```python
import functools
import math

import numpy as np
import jax
import jax.numpy as jnp
from jax import lax
from jax.experimental import pallas as pl
from jax.experimental.pallas import tpu as pltpu

f32 = jnp.float32
bf16 = jnp.bfloat16

GRID_W = 64
MLA_HEADS, MLA_NOPE, MLA_ROPE, MLA_V, MLA_KV_RANK = 8, 128, 64, 128, 512
MLA_QK = MLA_NOPE + MLA_ROPE
HG_HEADS, HG_K, HG_V = 8, 128, 128
NA_HEADS, NA_DH, NA_WIN_R, NA_WIN_C = 8, 128, 8, 16
ROPE_BASE = 10000.0
EPS = 1e-6
NEG = -1e30

LANES = 128
SUBLANES = 8
VMEM_LIMIT_BYTES = 56 * 1024 * 1024

HEAD_W = 128
MLA_QPAD = 256
HG_CHUNK = 64
HG_LEVELS = int(math.log2(HG_CHUNK))
NA_QROWS = 8
NA_KROWS = 16


def _params(semantics):
    return pltpu.CompilerParams(dimension_semantics=semantics, vmem_limit_bytes=VMEM_LIMIT_BYTES)


def _pow2_tile(pref, *dims):
    t = pref
    while any(d % t for d in dims):
        t //= 2
    assert t >= SUBLANES, (pref, dims)
    return t


def _dot(a, b):
    return jnp.dot(a, b, preferred_element_type=f32)


def _dot_nt(a, b):
    return lax.dot_general(a, b, (((1,), (1,)), ((), ())), preferred_element_type=f32)


def _dot_tn(a, b):
    return lax.dot_general(a, b, (((0,), (0,)), ((), ())), preferred_element_type=f32)


def _silu(x):
    return x * jax.nn.sigmoid(x)


class _ZLayout:
    def __init__(self, d_model, tn):
        self.qn = 0
        self.qp = self.qn + MLA_HEADS * MLA_NOPE
        self.ckv = self.qp + MLA_HEADS * MLA_ROPE
        self.hq = self.ckv + MLA_KV_RANK
        self.hff = self.hq + HG_HEADS * HG_K
        self.hfb = self.hff + HG_HEADS * HG_K
        self.hi = self.hfb + HG_HEADS * HG_K
        self.hg = self.hi + HG_HEADS * HG_V
        self.nq = self.hg + HG_HEADS * HG_V
        self.nk = self.nq + NA_HEADS * NA_DH
        self.nv = self.nk + NA_HEADS * NA_DH
        self.ga = self.nv + NA_HEADS * NA_DH
        self.gb = self.ga + d_model
        self.gc = self.gb + d_model
        self.kpe = self.gc + d_model
        used = self.kpe + MLA_ROPE
        self.width = -(-used // tn) * tn


def _permute_w_in(w, lay):
    d = w.shape[0]
    n_mq = MLA_HEADS * MLA_QK
    mq = w[:, :n_mq].reshape(d, MLA_HEADS, MLA_QK)
    qn = mq[:, :, :MLA_NOPE].reshape(d, MLA_HEADS * MLA_NOPE)
    qp = mq[:, :, MLA_NOPE:].reshape(d, MLA_HEADS * MLA_ROPE)
    ckv = w[:, n_mq:n_mq + MLA_KV_RANK]
    kpe = w[:, n_mq + MLA_KV_RANK:n_mq + MLA_KV_RANK + MLA_ROPE]
    rest = w[:, n_mq + MLA_KV_RANK + MLA_ROPE:]
    pad = jnp.zeros((d, lay.width - lay.kpe - MLA_ROPE), w.dtype)
    return jnp.concatenate([qn, qp, ckv, rest, kpe, pad], axis=1).astype(bf16)


def _mod_kernel(c_ref, w_ref, b_ref, o_ref):
    a = _silu(c_ref[...]).astype(bf16)
    o_ref[...] = _dot(a, w_ref[...].astype(bf16)) + b_ref[...]


def _modulation(cond, w_mod, b_mod):
    depth, d, n = w_mod.shape
    g = cond.shape[0]
    tn = _pow2_tile(1024, n)
    return pl.pallas_call(
        _mod_kernel,
        grid=(depth, n // tn),
        in_specs=[
            pl.BlockSpec((g, d), lambda l, j: (0, 0)),
            pl.BlockSpec((None, d, tn), lambda l, j: (l, 0, j)),
            pl.BlockSpec((None, 1, tn), lambda l, j: (l, 0, j)),
        ],
        out_specs=pl.BlockSpec((None, g, tn), lambda l, j: (l, 0, j)),
        out_shape=jax.ShapeDtypeStruct((depth, g, n), f32),
        compiler_params=_params(("arbitrary", "arbitrary")),
        name="modulation",
    )(cond, w_mod, b_mod.reshape(depth, 1, n))


def _normmod_matmul_kernel(x_ref, g_ref, sc_ref, sh_ref, w_ref, o_ref, h_scr):
    @pl.when(pl.program_id(1) == 0)
    def _():
        x = x_ref[...]
        ms = jnp.mean(x * x, axis=-1, keepdims=True)
        y = x * lax.rsqrt(ms + EPS) * g_ref[...]
        h_scr[...] = (y * (1.0 + sc_ref[...]) + sh_ref[...]).astype(bf16)

    o_ref[...] = _dot(h_scr[...], w_ref[...])


def _normmod_matmul(x, g, sc, sh, w, group_of_tile, tm, tn, name):
    m, d = x.shape
    n = w.shape[1]
    return pl.pallas_call(
        _normmod_matmul_kernel,
        grid=(m // tm, n // tn),
        in_specs=[
            pl.BlockSpec((tm, d), lambda i, j: (i, 0)),
            pl.BlockSpec((1, d), lambda i, j: (0, 0)),
            pl.BlockSpec((None, 1, d), lambda i, j: (group_of_tile(i), 0, 0)),
            pl.BlockSpec((None, 1, d), lambda i, j: (group_of_tile(i), 0, 0)),
            pl.BlockSpec((d, tn), lambda i, j: (0, j)),
        ],
        out_specs=pl.BlockSpec((tm, tn), lambda i, j: (i, j)),
        out_shape=jax.ShapeDtypeStruct((m, n), f32),
        scratch_shapes=[pltpu.VMEM((tm, d), bf16)],
        compiler_params=_params(("arbitrary", "arbitrary")),
        name=name,
    )(x, g.reshape(1, d), sc, sh, w)


def _rope(x, cos, sin_signed):
    n = x.shape[-1]
    half = MLA_ROPE // 4
    lane = lax.broadcasted_iota(jnp.int32, x.shape, 1)
    partner = jnp.where((lane % (2 * half)) < half, pltpu.roll(x, n - half, 1), pltpu.roll(x, half, 1))
    return x * cos + partner * sin_signed


def _mla_q_kernel(qn_ref, qp_ref, gn_ref, gp_ref, cos_ref, sin_ref, o_ref, *, scale):
    qn = qn_ref[...]
    qp = qp_ref[...]
    tm = qn.shape[0]
    lane = lax.broadcasted_iota(jnp.int32, (tm, LANES), 1)
    low = lane < MLA_ROPE
    cos = cos_ref[...]
    sin = sin_ref[...]
    gn = gn_ref[...]
    gp = gp_ref[...]
    for pair in range(MLA_HEADS // 2):
        pcol = qp[:, pair * LANES:(pair + 1) * LANES]
        sq = pcol * pcol
        ss_all = jnp.sum(sq, axis=-1, keepdims=True)
        ss_low = jnp.sum(jnp.where(low, sq, 0.0), axis=-1, keepdims=True)
        rot = _rope(pcol * gp, cos, sin)
        for k, ss_pe in enumerate((ss_low, ss_all - ss_low)):
            h = 2 * pair + k
            nope = qn[:, h * MLA_NOPE:(h + 1) * MLA_NOPE]
            ss = jnp.sum(nope * nope, axis=-1, keepdims=True) + ss_pe
            r = lax.rsqrt(ss * (1.0 / MLA_QK) + EPS) * scale
            pe = rot if k == 0 else pltpu.roll(rot, MLA_ROPE, 1)
            o_ref[:, h * MLA_QPAD:h * MLA_QPAD + MLA_NOPE] = (nope * r * gn).astype(bf16)
            o_ref[:, h * MLA_QPAD + MLA_NOPE:(h + 1) * MLA_QPAD] = jnp.where(low, pe * r, 0.0).astype(bf16)


def _mla_kv_kernel(ckv_ref, kpe_ref, kvg_ref, gn_ref, gp_ref, wuk_ref, wuv_ref, cos_ref, sin_ref,
                   k_ref, v_ref, ckv_out_ref, *, norm_ckv, rope):
    ckv = ckv_ref[...]
    if norm_ckv:
        ms = jnp.mean(ckv * ckv, axis=-1, keepdims=True)
        ckv = ckv * lax.rsqrt(ms + EPS) * kvg_ref[...]
    ckv_out_ref[...] = ckv
    cb = ckv.astype(bf16)
    kn = _dot(cb, wuk_ref[...])
    v_ref[...] = _dot(cb, wuv_ref[...]).astype(bf16)
    tm = ckv.shape[0]
    lane = lax.broadcasted_iota(jnp.int32, (tm, LANES), 1)
    low = lane < MLA_ROPE
    kpe = jnp.where(low, kpe_ref[...], 0.0)
    ss_pe = jnp.sum(kpe * kpe, axis=-1, keepdims=True)
    pe = kpe * gp_ref[...]
    if rope:
        pe = _rope(pe, cos_ref[...], sin_ref[...])
    gn = gn_ref[...]
    for h in range(MLA_HEADS):
        nope = kn[:, h * MLA_NOPE:(h + 1) * MLA_NOPE]
        ss = jnp.sum(nope * nope, axis=-1, keepdims=True) + ss_pe
        r = lax.rsqrt(ss * (1.0 / MLA_QK) + EPS)
        k_ref[:, h * MLA_QPAD:h * MLA_QPAD + MLA_NOPE] = (nope * r * gn).astype(bf16)
        k_ref[:, h * MLA_QPAD + MLA_NOPE:(h + 1) * MLA_QPAD] = jnp.where(low, pe * r, 0.0).astype(bf16)


def _rope_tables(t_len, tm):
    n_freq = MLA_ROPE // 4
    t = jnp.arange(t_len, dtype=jnp.int32)
    inv = ROPE_BASE ** (-jnp.arange(n_freq, dtype=f32) / n_freq)
    ang_r = (t // GRID_W).astype(f32)[:, None] * inv
    ang_c = (t % GRID_W).astype(f32)[:, None] * inv
    cos64 = jnp.concatenate([jnp.cos(ang_r), jnp.cos(ang_r), jnp.cos(ang_c), jnp.cos(ang_c)], axis=1)
    sin64 = jnp.concatenate([-jnp.sin(ang_r), jnp.sin(ang_r), -jnp.sin(ang_c), jnp.sin(ang_c)], axis=1)
    cos = jnp.concatenate([jnp.ones((tm, LANES), f32), jnp.tile(cos64, (1, 2))], axis=0)
    sin = jnp.concatenate([jnp.zeros((tm, LANES), f32), jnp.tile(sin64, (1, 2))], axis=0)
    return cos, sin


def _mla_q(z, lay, q_g, cos, sin, rope_block, tm):
    m = z.shape[0]
    gn = q_g[:MLA_NOPE].reshape(1, MLA_NOPE)
    gp = jnp.tile(q_g[MLA_NOPE:], 2).reshape(1, LANES)
    wn, wp = MLA_HEADS * MLA_NOPE, MLA_HEADS * MLA_ROPE
    return pl.pallas_call(
        functools.partial(_mla_q_kernel, scale=MLA_QK ** -0.5),
        grid=(m // tm,),
        in_specs=[
            pl.BlockSpec((tm, wn), lambda i: (i, lay.qn // wn)),
            pl.BlockSpec((tm, wp), lambda i: (i, lay.qp // wp)),
            pl.BlockSpec((1, MLA_NOPE), lambda i: (0, 0)),
            pl.BlockSpec((1, LANES), lambda i: (0, 0)),
            pl.BlockSpec((tm, LANES), lambda i: (rope_block(i), 0)),
            pl.BlockSpec((tm, LANES), lambda i: (rope_block(i), 0)),
        ],
        out_specs=pl.BlockSpec((tm, MLA_HEADS * MLA_QPAD), lambda i: (i, 0)),
        out_shape=jax.ShapeDtypeStruct((m, MLA_HEADS * MLA_QPAD), bf16),
        compiler_params=_params(("arbitrary",)),
        name="mla_q",
    )(z, z, gn, gp, cos, sin)


def _mla_kv(ckv_src, ckv_col, kpe_src, kpe_col, kv_g, k_g, w_uk, w_uv, cos, sin, rope_block, tm, norm_ckv, rope, name):
    m = ckv_src.shape[0]
    gn = k_g[:MLA_NOPE].reshape(1, MLA_NOPE)
    gp = jnp.concatenate([k_g[MLA_NOPE:], jnp.zeros((LANES - MLA_ROPE,), f32)]).reshape(1, LANES)
    hw = MLA_HEADS * MLA_NOPE
    return pl.pallas_call(
        functools.partial(_mla_kv_kernel, norm_ckv=norm_ckv, rope=rope),
        grid=(m // tm,),
        in_specs=[
            pl.BlockSpec((tm, MLA_KV_RANK), lambda i: (i, ckv_col // MLA_KV_RANK)),
            pl.BlockSpec((tm, LANES), lambda i: (i, kpe_col // LANES)),
            pl.BlockSpec((1, MLA_KV_RANK), lambda i: (0, 0)),
            pl.BlockSpec((1, MLA_NOPE), lambda i: (0, 0)),
            pl.BlockSpec((1, LANES), lambda i: (0, 0)),
            pl.BlockSpec((MLA_KV_RANK, hw), lambda i: (0, 0)),
            pl.BlockSpec((MLA_KV_RANK, hw), lambda i: (0, 0)),
            pl.BlockSpec((tm, LANES), lambda i: (rope_block(i), 0)),
            pl.BlockSpec((tm, LANES), lambda i: (rope_block(i), 0)),
        ],
        out_specs=[
            pl.BlockSpec((tm, MLA_HEADS * MLA_QPAD), lambda i: (i, 0)),
            pl.BlockSpec((tm, hw), lambda i: (i, 0)),
            pl.BlockSpec((tm, MLA_KV_RANK), lambda i: (i, 0)),
        ],
        out_shape=[
            jax.ShapeDtypeStruct((m, MLA_HEADS * MLA_QPAD), bf16),
            jax.ShapeDtypeStruct((m, hw), bf16),
            jax.ShapeDtypeStruct((m, MLA_KV_RANK), f32),
        ],
        compiler_params=_params(("arbitrary",)),
        name=name,
    )(ckv_src, kpe_src, kv_g.reshape(1, MLA_KV_RANK), gn, gp, w_uk, w_uv, cos, sin)


def _flash_kernel(*refs, n_main, tk, n_extra):
    if n_extra:
        q_ref, k_ref, v_ref, kx_ref, vx_ref, o_ref = refs
    else:
        q_ref, k_ref, v_ref, o_ref = refs
    q = q_ref[...]
    tq = q.shape[0]
    dv = v_ref.shape[-1]

    def step(k, v, carry):
        m, l, acc = carry
        s = _dot_nt(q, k)
        m_new = jnp.maximum(m, jnp.max(s, axis=-1, keepdims=True))
        alpha = jnp.exp(m - m_new)
        p = jnp.exp(s - m_new)
        l = alpha * l + jnp.sum(p, axis=-1, keepdims=True)
        acc = alpha * acc + _dot(p.astype(bf16), v)
        return m_new, l, acc

    def main_body(c, carry):
        off = pl.multiple_of(c * tk, tk)
        return step(k_ref[pl.ds(off, tk), :], v_ref[pl.ds(off, tk), :], carry)

    carry = (jnp.full((tq, 1), -jnp.inf, f32), jnp.zeros((tq, 1), f32), jnp.zeros((tq, dv), f32))
    carry = lax.fori_loop(0, n_main, main_body, carry)
    for c in range(n_extra):
        carry = step(kx_ref[c * tk:(c + 1) * tk, :], vx_ref[c * tk:(c + 1) * tk, :], carry)
    _, l, acc = carry
    o_ref[...] = (acc / l).astype(o_ref.dtype)


def _flash(q, k, v, *, batch, heads, t_len, s_len, row0, dq, dv, tq, tk, extra=None, name):
    assert row0 % tq == 0 and row0 % s_len == 0 and t_len % tq == 0 and s_len % tk == 0
    nq = t_len // tq
    in_specs = [
        pl.BlockSpec((tq, dq), lambda b, h, i: (row0 // tq + b * nq + i, h)),
        pl.BlockSpec((s_len, dq), lambda b, h, i: (row0 // s_len + b, h)),
        pl.BlockSpec((s_len, dv), lambda b, h, i: (row0 // s_len + b, h)),
    ]
    args = [q, k, v]
    n_extra = 0
    if extra is not None:
        kx, vx, p_len = extra
        assert p_len % tk == 0
        n_extra = p_len // tk
        in_specs += [
            pl.BlockSpec((p_len, dq), lambda b, h, i: (b, h)),
            pl.BlockSpec((p_len, dv), lambda b, h, i: (b, h)),
        ]
        args += [kx, vx]
    return pl.pallas_call(
        functools.partial(_flash_kernel, n_main=s_len // tk, tk=tk, n_extra=n_extra),
        grid=(batch, heads, nq),
        in_specs=in_specs,
        out_specs=pl.BlockSpec((tq, dv), lambda b, h, i: (b * nq + i, h)),
        out_shape=jax.ShapeDtypeStruct((batch * t_len, heads * dv), bf16),
        compiler_params=_params(("arbitrary", "arbitrary", "arbitrary")),
        name=name,
    )(*args)


def _na_prep_kernel(nq_ref, nk_ref, nv_ref, qg_ref, kg_ref, q_ref, k_ref, v_ref, kf_ref, *, scale):
    qg = qg_ref[...]
    kg = kg_ref[...]
    v_ref[...] = nv_ref[...].astype(bf16)
    for h in range(NA_HEADS):
        sl = slice(h * NA_DH, (h + 1) * NA_DH)
        q = nq_ref[:, sl]
        k = nk_ref[:, sl]
        rq = lax.rsqrt(jnp.mean(q * q, axis=-1, keepdims=True) + EPS)
        rk = lax.rsqrt(jnp.mean(k * k, axis=-1, keepdims=True) + EPS)
        q_ref[:, sl] = (q * rq * qg * scale).astype(bf16)
        kn = k * rk * kg
        kf_ref[:, sl] = kn
        k_ref[:, sl] = kn.astype(bf16)


def _na_prep(z, lay, q_g, k_g, tm):
    m = z.shape[0]
    w = NA_HEADS * NA_DH
    spec = pl.BlockSpec((tm, w), lambda i: (i, 0))
    return pl.pallas_call(
        functools.partial(_na_prep_kernel, scale=NA_DH ** -0.5),
        grid=(m // tm,),
        in_specs=[
            pl.BlockSpec((tm, w), lambda i: (i, lay.nq // w)),
            pl.BlockSpec((tm, w), lambda i: (i, lay.nk // w)),
            pl.BlockSpec((tm, w), lambda i: (i, lay.nv // w)),
            pl.BlockSpec((1, NA_DH), lambda i: (0, 0)),
            pl.BlockSpec((1, NA_DH), lambda i: (0, 0)),
        ],
        out_specs=[spec, spec, spec, spec],
        out_shape=[jax.ShapeDtypeStruct((m, w), bf16)] * 3 + [jax.ShapeDtypeStruct((m, w), f32)],
        compiler_params=_params(("arbitrary",)),
        name="na_prep",
    )(z, z, z, q_g.reshape(1, NA_DH), k_g.reshape(1, NA_DH))


def _na_static_maps(rows):
    nblk = rows // NA_QROWS
    reps = (0, min(1, nblk - 1), nblk - 1)
    dr_map = np.full((3, NA_QROWS, NA_KROWS), 2 * NA_WIN_R - 1, np.int32)
    for v, kb in enumerate(reps):
        ws = min(max(NA_QROWS * kb - NA_WIN_R // 2, 0), rows - NA_KROWS)
        for i in range(NA_QROWS):
            qrow = NA_QROWS * kb + i
            rs = min(max(qrow - NA_WIN_R // 2, 0), rows - NA_WIN_R)
            for j in range(NA_KROWS):
                krow = ws + j
                if rs <= krow < rs + NA_WIN_R:
                    dr_map[v, i, j] = krow - qrow + NA_WIN_R - 1
    qcol = np.arange(GRID_W)
    cs = np.clip(qcol - NA_WIN_C // 2, 0, GRID_W - NA_WIN_C)
    in_win = (qcol[None, :] >= cs[:, None]) & (qcol[None, :] < cs[:, None] + NA_WIN_C)
    dc_idx = np.clip(qcol[None, :] - qcol[:, None], -(NA_WIN_C - 1), NA_WIN_C - 1) + (NA_WIN_C - 1)
    return dr_map, in_win, dc_idx


def _na_bias_table(rpb, rows):
    dr_map, in_win, dc_idx = _na_static_maps(rows)
    h = rpb.shape[0]
    bt = jnp.where(in_win[None, None], jnp.take(rpb, jnp.asarray(dc_idx), axis=2), NEG)
    bt = jnp.concatenate([bt, jnp.full((h, 1, GRID_W, GRID_W), NEG, f32)], axis=1)
    tab = jnp.take(bt, jnp.asarray(dr_map.reshape(-1)), axis=1)
    tab = tab.reshape(h, 3, NA_QROWS, NA_KROWS, GRID_W, GRID_W).transpose(1, 0, 2, 4, 3, 5)
    return tab.reshape(3, h, NA_QROWS * GRID_W, NA_KROWS * GRID_W)


def _na_kernel(q_ref, k_ref, v_ref, kc_ref, vc_ref, bias_ref, o_ref, *, rows):
    kb = pl.program_id(2)
    ws = jnp.clip(NA_QROWS * kb - NA_WIN_R // 2, 0, rows - NA_KROWS)
    off = pl.multiple_of(ws * GRID_W, (NA_WIN_R // 2) * GRID_W)
    n_keys = NA_KROWS * GRID_W
    q = q_ref[...]
    s_loc = _dot_nt(q, k_ref[pl.ds(off, n_keys), :]) + bias_ref[...]
    s_ctx = _dot_nt(q, kc_ref[...])
    m = jnp.maximum(jnp.max(s_loc, axis=-1, keepdims=True), jnp.max(s_ctx, axis=-1, keepdims=True))
    p_loc = jnp.exp(s_loc - m)
    p_ctx = jnp.exp(s_ctx - m)
    l = jnp.sum(p_loc, axis=-1, keepdims=True) + jnp.sum(p_ctx, axis=-1, keepdims=True)
    acc = _dot(p_loc.astype(bf16), v_ref[pl.ds(off, n_keys), :]) + _dot(p_ctx.astype(bf16), vc_ref[...])
    o_ref[...] = (acc / l).astype(o_ref.dtype)


def _na_latent(q, k, v, kc, vc, bias, *, batch, t_len, p_len, row0):
    rows = t_len // GRID_W
    assert t_len % GRID_W == 0 and rows % NA_QROWS == 0 and rows >= NA_KROWS
    tq = NA_QROWS * GRID_W
    assert row0 % tq == 0 and row0 % t_len == 0
    nblk = rows // NA_QROWS

    def variant(i):
        return jnp.where(i == 0, 0, jnp.where(i == nblk - 1, 2, 1))

    return pl.pallas_call(
        functools.partial(_na_kernel, rows=rows),
        grid=(batch, NA_HEADS, nblk),
        in_specs=[
            pl.BlockSpec((tq, NA_DH), lambda b, h, i: (row0 // tq + b * nblk + i, h)),
            pl.BlockSpec((t_len, NA_DH), lambda b, h, i: (row0 // t_len + b, h)),
            pl.BlockSpec((t_len, NA_DH), lambda b, h, i: (row0 // t_len + b, h)),
            pl.BlockSpec((p_len, NA_DH), lambda b, h, i: (b, h)),
            pl.BlockSpec((p_len, NA_DH), lambda b, h, i: (b, h)),
            pl.BlockSpec((None, None, tq, NA_KROWS * GRID_W), lambda b, h, i: (variant(i), h, 0, 0)),
        ],
        out_specs=pl.BlockSpec((tq, NA_DH), lambda b, h, i: (b * nblk + i, h)),
        out_shape=jax.ShapeDtypeStruct((batch * t_len, NA_HEADS * NA_DH), bf16),
        compiler_params=_params(("arbitrary", "arbitrary", "arbitrary")),
        name="na_latent",
    )(q, k, v, kc, vc, bias)


def _hgrn_tables():
    c, lv = HG_CHUNK, HG_LEVELS
    t = np.arange(c)[:, None]
    r = np.arange(c)[None, :]
    wall = np.zeros((2, (lv + 2) * c, c), np.float32)
    qside = np.zeros((2, lv * c, LANES), np.float32)
    mask = np.zeros((2, lv + 1, c, c), np.float32)
    wall[0, 0:c] = r <= t
    wall[0, c:2 * c] = r > t
    wall[1, 0:c] = r >= t
    wall[1, c:2 * c] = r < t
    mask[:, 0] = np.eye(c)
    for l in range(lv):
        hs = 1 << l
        blk = t // (2 * hs)
        mid = blk * (2 * hs) + hs
        qf = t >= mid
        qb = t < mid
        wall[0, (2 + l) * c:(3 + l) * c] = np.where(qf, (r >= mid) & (r <= t), (r > t) & (r < mid))
        wall[1, (2 + l) * c:(3 + l) * c] = np.where(qb, (r >= t) & (r < mid), (r >= mid) & (r < t))
        qside[0, l * c:(l + 1) * c] = qf
        qside[1, l * c:(l + 1) * c] = qb
        same = blk == blk.T
        mask[0, 1 + l] = same & qf & ~qf.T
        mask[1, 1 + l] = same & qb & ~qb.T
    return wall, qside, mask


def _hgrn_kernel(*refs, n_chunks, heads, has_state, emit_state):
    (hq_ref, hff_ref, hfb_ref, hi_ref, hg_ref, la_ref, lc_ref, om_ref, ng_ref,
     wall_ref, qside_ref, mask_ref) = refs[:12]
    rest = list(refs[12:])
    s0_ref = rest.pop(0) if has_state else None
    o_ref = rest.pop(0)
    st_ref = rest.pop(0) if emit_state else None
    o_scr, s_scr = rest
    c, lv = HG_CHUNK, HG_LEVELS

    o_scr[...] = jnp.zeros_like(o_scr)
    for d in range(2):
        for g in range(heads):
            if has_state:
                s_scr[d, g] = s0_ref[d, g].T
            else:
                s_scr[d, g] = jnp.zeros((HG_V, HG_K), f32)

    def chain(row, d, g):
        rs = pl.ds(pl.multiple_of(row, c), c)
        cs = slice(g * HEAD_W, (g + 1) * HEAD_W)
        x = (hff_ref if d == 0 else hfb_ref)[rs, cs]
        log_sig = jnp.minimum(x, 0.0) - jnp.log(1.0 + jnp.exp(-jnp.abs(x)))
        a = la_ref[d:d + 1, cs]
        b = lc_ref[d:d + 1, cs] + log_sig
        lf = jnp.maximum(a, b) + jnp.log(1.0 + jnp.exp(-jnp.abs(a - b)))
        kin = om_ref[d:d + 1, cs] * jax.nn.sigmoid(-x)
        q = _silu(hq_ref[rs, cs])
        v = hi_ref[rs, cs].astype(bf16)
        l1 = lf.astype(bf16)
        r1 = lf - l1.astype(f32)
        l2 = r1.astype(bf16)
        l3 = (r1 - l2.astype(f32)).astype(bf16)
        w = wall_ref[d]
        zz = jnp.exp(_dot(w, l1) + _dot(w, l2) + _dot(w, l3))
        att = mask_ref[d, 0] * _dot_nt(q.astype(bf16), kin.astype(bf16))
        for l in range(lv):
            side = qside_ref[d, l * c:(l + 1) * c, :]
            xl = (jnp.where(side > 0.5, q, kin) * zz[(2 + l) * c:(3 + l) * c]).astype(bf16)
            att = att + mask_ref[d, 1 + l] * _dot_nt(xl, xl)
        st = s_scr[d, g]
        o = _dot(att.astype(bf16), v) + _dot_nt((q * zz[0:c]).astype(bf16), st.astype(bf16))
        z_tot = zz[c - 1:c] if d == 0 else zz[0:1]
        s_scr[d, g] = st * z_tot + _dot_tn(v, (kin * zz[c:2 * c]).astype(bf16))
        o_scr[rs, cs] += o

    def body(i, _):
        for g in range(heads):
            chain(i * c, 0, g)
            chain((n_chunks - 1 - i) * c, 1, g)
        return 0

    lax.fori_loop(0, n_chunks, body, 0)

    ng = ng_ref[...]
    for g in range(heads):
        cs = slice(g * HEAD_W, (g + 1) * HEAD_W)
        o = o_scr[:, cs]
        y = o * lax.rsqrt(jnp.mean(o * o, axis=-1, keepdims=True) + EPS) * ng
        o_ref[:, cs] = (y * _silu(hg_ref[:, cs])).astype(o_ref.dtype)
        if emit_state:
            for d in range(2):
                st_ref[d, g] = s_scr[d, g].T


def _hgrn(z, lay, lb_tabs, norm_g, tables, *, batch, t_len, row0, s0, emit_state, heads_per_step=2):
    assert t_len % HG_CHUNK == 0 and row0 % t_len == 0 and HG_K == HEAD_W and HG_V == HEAD_W
    g = heads_per_step
    gw = g * HEAD_W
    la, lc, om = lb_tabs
    wall, qside, mask = tables
    r0 = row0 // t_len

    def zspec(col):
        return pl.BlockSpec((t_len, gw), lambda b, h: (r0 + b, col // gw + h))

    def const(shape):
        return pl.BlockSpec(shape, lambda b, h: (0,) * len(shape))

    in_specs = [zspec(lay.hq), zspec(lay.hff), zspec(lay.hfb), zspec(lay.hi), zspec(lay.hg),
                pl.BlockSpec((2, gw), lambda b, h: (0, h)), pl.BlockSpec((2, gw), lambda b, h: (0, h)),
                pl.BlockSpec((2, gw), lambda b, h: (0, h)), const((1, HG_V)),
                const(wall.shape), const(qside.shape), const(mask.shape)]
    args = [z, z, z, z, z, la, lc, om, norm_g.reshape(1, HG_V), wall, qside, mask]
    state_spec = pl.BlockSpec((None, 2, g, HG_K, HG_V), lambda b, h: (b, 0, h, 0, 0))
    if s0 is not None:
        in_specs.append(state_spec)
        args.append(s0)
    out_specs = [pl.BlockSpec((t_len, gw), lambda b, h: (b, h))]
    out_shape = [jax.ShapeDtypeStruct((batch * t_len, HG_HEADS * HG_V), bf16)]
    if emit_state:
        out_specs.append(state_spec)
        out_shape.append(jax.ShapeDtypeStruct((batch, 2, HG_HEADS, HG_K, HG_V), f32))
    res = pl.pallas_call(
        functools.partial(_hgrn_kernel, n_chunks=t_len // HG_CHUNK, heads=g, has_state=s0 is not None,
                          emit_state=emit_state),
        grid=(batch, HG_HEADS // g),
        in_specs=in_specs,
        out_specs=out_specs,
        out_shape=out_shape,
        scratch_shapes=[pltpu.VMEM((t_len, gw), f32), pltpu.VMEM((2, g, HG_V, HG_K), f32)],
        compiler_params=_params(("arbitrary", "arbitrary")),
        name="hgrn_ctx" if emit_state else "hgrn_latent",
    )(*args)
    return res if emit_state else (res[0], None)


def _merge_kernel(oa_ref, ob_ref, oc_ref, ga_ref, gb_ref, gc_ref, w_ref, o_ref):
    m = jax.nn.sigmoid(ga_ref[...]) * _dot(oa_ref[...], w_ref[0])
    m = m + jax.nn.sigmoid(gb_ref[...]) * _dot(ob_ref[...], w_ref[1])
    m = m + jax.nn.sigmoid(gc_ref[...]) * _dot(oc_ref[...], w_ref[2])
    o_ref[...] = m.astype(o_ref.dtype)


def _merge(o_a, o_b, o_c, z, lay, w_branch, tm, tn):
    m, bw = o_a.shape
    d = w_branch.shape[2]
    ospec = pl.BlockSpec((tm, bw), lambda i, j: (i, 0))

    def gate(col):
        return pl.BlockSpec((tm, tn), lambda i, j: (i, col // tn + j))

    return pl.pallas_call(
        _merge_kernel,
        grid=(m // tm, d // tn),
        in_specs=[ospec, ospec, ospec, gate(lay.ga), gate(lay.gb), gate(lay.gc),
                  pl.BlockSpec((3, bw, tn), lambda i, j: (0, 0, j))],
        out_specs=pl.BlockSpec((tm, tn), lambda i, j: (i, j)),
        out_shape=jax.ShapeDtypeStruct((m, d), bf16),
        compiler_params=_params(("arbitrary", "arbitrary")),
        name="merge",
    )(o_a, o_b, o_c, z, z, z, w_branch)


def _proj_residual_kernel(m_ref, w_ref, x_ref, g_ref, o_ref):
    o_ref[...] = x_ref[...] + g_ref[...] * _dot(m_ref[...], w_ref[...])


def _proj_residual(mm, w, x, gate, group_of_tile, tm, tn):
    m, k = mm.shape
    d = w.shape[1]
    return pl.pallas_call(
        _proj_residual_kernel,
        grid=(m // tm, d // tn),
        in_specs=[
            pl.BlockSpec((tm, k), lambda i, j: (i, 0)),
            pl.BlockSpec((k, tn), lambda i, j: (0, j)),
            pl.BlockSpec((tm, tn), lambda i, j: (i, j)),
            pl.BlockSpec((None, 1, tn), lambda i, j: (group_of_tile(i), 0, j)),
        ],
        out_specs=pl.BlockSpec((tm, tn), lambda i, j: (i, j)),
        out_shape=jax.ShapeDtypeStruct((m, d), f32),
        compiler_params=_params(("arbitrary", "arbitrary")),
        name="out_proj",
    )(mm, w, x, gate)


def _ffn_down_kernel(a_ref, g_ref, gprev_ref, gnext_ref, cw_ref, cb_ref, w_ref, x_ref, gate_ref, o_ref,
                     *, m_ctx, s_len, t_len):
    i = pl.program_id(0)
    k = pl.program_id(1)
    g = g_ref[...]
    tm = g.shape[0]
    row = i * tm + lax.broadcasted_iota(jnp.int32, (tm, 1), 0)
    pos = jnp.where(row < m_ctx, row % s_len, (row - m_ctx) % t_len)
    last = jnp.where(row < m_ctx, s_len - 1, t_len - 1)
    local = lax.broadcasted_iota(jnp.int32, (tm, 1), 0)
    g_prev = jnp.where(local == 0, gprev_ref[SUBLANES - 1:SUBLANES, :], pltpu.roll(g, 1, 0))
    g_prev = jnp.where(pos == 0, 0.0, g_prev)
    g_next = jnp.where(local == tm - 1, gnext_ref[0:1, :], pltpu.roll(g, tm - 1, 0))
    g_next = jnp.where(pos == last, 0.0, g_next)
    conv = g_prev * cw_ref[0:1, :] + g * cw_ref[1:2, :] + g_next * cw_ref[2:3, :] + cb_ref[...]
    u = (_silu(conv) * a_ref[...]).astype(bf16)
    part = _dot(u, w_ref[...])

    @pl.when(k == 0)
    def _():
        o_ref[...] = part

    @pl.when(k > 0)
    def _():
        o_ref[...] += part

    @pl.when(k == pl.num_programs(1) - 1)
    def _():
        o_ref[...] = x_ref[...] + gate_ref[...] * o_ref[...]


def _ffn_down(ag, conv_w, conv_b, w_down, x, gate, group_of_tile, tm, tk, m_ctx, s_len, t_len):
    m, d = x.shape
    f = w_down.shape[0]
    nk = f // tk
    nsub = tm // SUBLANES
    last_sub = m // SUBLANES - 1
    return pl.pallas_call(
        functools.partial(_ffn_down_kernel, m_ctx=m_ctx, s_len=s_len, t_len=t_len),
        grid=(m // tm, nk),
        in_specs=[
            pl.BlockSpec((tm, tk), lambda i, k: (i, k)),
            pl.BlockSpec((tm, tk), lambda i, k: (i, nk + k)),
            pl.BlockSpec((SUBLANES, tk), lambda i, k: (jnp.maximum(i * nsub - 1, 0), nk + k)),
            pl.BlockSpec((SUBLANES, tk), lambda i, k: (jnp.minimum((i + 1) * nsub, last_sub), nk + k)),
            pl.BlockSpec((3, tk), lambda i, k: (0, k)),
            pl.BlockSpec((1, tk), lambda i, k: (0, k)),
            pl.BlockSpec((tk, d), lambda i, k: (k, 0)),
            pl.BlockSpec((tm, d), lambda i, k: (i, 0)),
            pl.BlockSpec((None, 1, d), lambda i, k: (group_of_tile(i), 0, 0)),
        ],
        out_specs=pl.BlockSpec((tm, d), lambda i, k: (i, 0)),
        out_shape=jax.ShapeDtypeStruct((m, d), f32),
        compiler_params=_params(("arbitrary", "arbitrary")),
        name="ffn_down",
    )(ag, ag, ag, ag, conv_w, conv_b.reshape(1, f), w_down, x, gate)


def kernel(x_prompt, x_sample, cache_mla_ckv, cache_mla_kpe, cache_na_k, cache_na_v, state_hgrn, c, c_ctx, w_mod, b_mod, norm1_g, norm2_g, w_in, mla_kv_norm_g, mla_q_norm_g, mla_k_norm_g, mla_w_uk, mla_w_uv, hgrn_lower_bounds, hgrn_norm_g, na_q_norm_g, na_k_norm_g, na_rpb, w_branch, w_out, ffn_w_up, ffn_conv_w, ffn_conv_b, ffn_w_down):
    n_ctx, s_len, d = x_prompt.shape
    n_lat, t_len, _ = x_sample.shape
    depth = w_in.shape[0]
    p_len = cache_mla_ckv.shape[2]
    ffn = ffn_w_down.shape[1]
    m_ctx, m_lat = n_ctx * s_len, n_lat * t_len
    m = m_ctx + m_lat
    assert m_ctx % t_len == 0, "context rows must be a whole number of latent sequences"

    tm = _pow2_tile(1024, m_ctx, t_len)
    tn = _pow2_tile(1024, d)
    tp = _pow2_tile(512, m_ctx, t_len)
    lay = _ZLayout(d, tn)
    n_ctx_tiles = m_ctx // tm
    tiles_per_seq = t_len // tm

    def group_of_tile(i):
        return jnp.where(i < n_ctx_tiles, 0, 1 + (i - n_ctx_tiles) // tiles_per_seq)

    n_groups = -(-(1 + n_lat) // SUBLANES) * SUBLANES
    cond = jnp.concatenate([c_ctx[None], c, jnp.zeros((n_groups - 1 - n_lat, d), f32)], axis=0)
    mods = _modulation(cond, w_mod, b_mod).reshape(depth, n_groups, 6, d).transpose(0, 2, 1, 3)[:, :, :, None, :]

    sm = jax.nn.softmax(hgrn_lower_bounds.astype(f32), axis=1)
    csum = jnp.cumsum(sm, axis=1)
    lower = csum - csum[:, :1]
    hg_tabs = tuple(jnp.asarray(t, bf16 if i == 0 else f32) for i, t in enumerate(_hgrn_tables()))

    cos, sin = _rope_tables(t_len, tp)
    n_ctx_tp = m_ctx // tp

    def rope_block(i):
        return jnp.where(i < n_ctx_tp, 0, 1 + (i - n_ctx_tp) % (t_len // tp))

    x = jnp.concatenate([x_prompt.reshape(m_ctx, d), x_sample.reshape(m_lat, d)], axis=0)
    st_ckv, st_kpe, st_nk, st_nv, st_hg = [], [], [], [], []
    for l in range(depth):
        sh1, sc1, g1, sh2, sc2, g2 = (mods[l, k] for k in range(6))
        w_in_l = _permute_w_in(w_in[l], lay)
        z = _normmod_matmul(x, norm1_g[l], sc1, sh1, w_in_l, group_of_tile, tm, tn, "in_proj")

        w_uk = mla_w_uk[l].reshape(MLA_KV_RANK, MLA_HEADS * MLA_NOPE).astype(bf16)
        w_uv = mla_w_uv[l].reshape(MLA_KV_RANK, MLA_HEADS * MLA_V).astype(bf16)
        q_mla = _mla_q(z, lay, mla_q_norm_g[l], cos, sin, rope_block, tp)
        k_mla, v_mla, ckv_n = _mla_kv(z, lay.ckv, z, lay.kpe, mla_kv_norm_g[l], mla_k_norm_g[l], w_uk, w_uv,
                                      cos, sin, rope_block, tp, True, True, "mla_kv")
        ckv_c = cache_mla_ckv[:, l].reshape(n_lat * p_len, MLA_KV_RANK)
        kpe_c = jnp.pad(cache_mla_kpe[:, l].reshape(n_lat * p_len, MLA_ROPE), ((0, 0), (0, LANES - MLA_ROPE)))
        tc = _pow2_tile(512, n_lat * p_len)
        k_c, v_c, _ = _mla_kv(ckv_c, 0, kpe_c, 0, mla_kv_norm_g[l], mla_k_norm_g[l], w_uk, w_uv,
                              cos, sin, lambda i: 0, tc, False, False, "mla_kv_cache")
        tq_c = _pow2_tile(256, s_len)
        o_mla_c = _flash(q_mla, k_mla, v_mla, batch=n_ctx, heads=MLA_HEADS, t_len=s_len, s_len=s_len, row0=0,
                         dq=MLA_QPAD, dv=MLA_V, tq=tq_c, tk=tq_c, name="mla_attn_ctx")
        tq_l = _pow2_tile(512, t_len, p_len)
        o_mla_l = _flash(q_mla, k_mla, v_mla, batch=n_lat, heads=MLA_HEADS, t_len=t_len, s_len=t_len, row0=m_ctx,
                         dq=MLA_QPAD, dv=MLA_V, tq=tq_l, tk=tq_l, extra=(k_c, v_c, p_len), name="mla_attn_latent")
        o_mla = jnp.concatenate([o_mla_c, o_mla_l], axis=0)

        lb = lower[:, l]
        lb_tabs = (jnp.log(lb), jnp.log1p(-lb), 1.0 - lb)
        o_hg_c, s_hg = _hgrn(z, lay, lb_tabs, hgrn_norm_g[l], hg_tabs, batch=n_ctx, t_len=s_len, row0=0,
                             s0=None, emit_state=True)
        o_hg_l, _ = _hgrn(z, lay, lb_tabs, hgrn_norm_g[l], hg_tabs, batch=n_lat, t_len=t_len, row0=m_ctx,
                          s0=state_hgrn[:, l], emit_state=False)
        o_hg = jnp.concatenate([o_hg_c, o_hg_l], axis=0)

        q_na, k_na, v_na, k_na_f = _na_prep(z, lay, na_q_norm_g[l], na_k_norm_g[l], tp)
        o_na_c = _flash(q_na, k_na, v_na, batch=n_ctx, heads=NA_HEADS, t_len=s_len, s_len=s_len, row0=0,
                        dq=NA_DH, dv=NA_DH, tq=tq_c, tk=tq_c, name="na_attn_ctx")
        kc_na = cache_na_k[:, l].reshape(n_lat * p_len, NA_HEADS * NA_DH).astype(bf16)
        vc_na = cache_na_v[:, l].reshape(n_lat * p_len, NA_HEADS * NA_DH).astype(bf16)
        bias = _na_bias_table(na_rpb[l], t_len // GRID_W)
        o_na_l = _na_latent(q_na, k_na, v_na, kc_na, vc_na, bias, batch=n_lat, t_len=t_len, p_len=p_len, row0=m_ctx)
        o_na = jnp.concatenate([o_na_c, o_na_l], axis=0)

        merged = _merge(o_mla, o_hg, o_na, z, lay, w_branch[l].astype(bf16), tp, tn)
        x = _proj_residual(merged, w_out[l].astype(bf16), x, g1, group_of_tile, tm, tn)

        ag = _normmod_matmul(x, norm2_g[l], sc2, sh2, ffn_w_up[l].astype(bf16), group_of_tile, tm,
                             _pow2_tile(1024, 2 * ffn), "ffn_up")
        tmd = _pow2_tile(1024, m_ctx, t_len)
        x = _ffn_down(ag, ffn_conv_w[l], ffn_conv_b[l], ffn_w_down[l].astype(bf16), x, g2,
                      lambda i: jnp.where(i < m_ctx // tmd, 0, 1 + (i - m_ctx // tmd) // (t_len // tmd)),
                      tmd, _pow2_tile(512, ffn), m_ctx, s_len, t_len)

        st_ckv.append(ckv_n[:m_ctx].reshape(n_ctx, s_len, MLA_KV_RANK))
        st_kpe.append(z[:m_ctx, lay.kpe:lay.kpe + MLA_ROPE].reshape(n_ctx, s_len, MLA_ROPE))
        st_nk.append(k_na_f[:m_ctx].reshape(n_ctx, s_len, NA_HEADS, NA_DH))
        st_nv.append(z[:m_ctx, lay.nv:lay.nv + NA_HEADS * NA_DH].reshape(n_ctx, s_len, NA_HEADS, NA_DH))
        st_hg.append(s_hg)

    y_prompt = x[:m_ctx].reshape(n_ctx, s_len, d)
    y_sample = x[m_ctx:].reshape(n_lat, t_len, d)
    return (y_prompt, y_sample, jnp.stack(st_ckv, axis=1), jnp.stack(st_kpe, axis=1), jnp.stack(st_nk, axis=1),
            jnp.stack(st_nv, axis=1), jnp.stack(st_hg, axis=1))
```

```python
import functools
import math

import numpy as np
import jax
import jax.numpy as jnp
from jax import lax
from jax.experimental import pallas as pl
from jax.experimental.pallas import tpu as pltpu

f32 = jnp.float32
bf16 = jnp.bfloat16

GRID_W = 64
MLA_HEADS, MLA_NOPE, MLA_ROPE, MLA_V, MLA_KV_RANK = 8, 128, 64, 128, 512
MLA_QK = MLA_NOPE + MLA_ROPE
HG_HEADS, HG_K, HG_V = 8, 128, 128
NA_HEADS, NA_DH, NA_WIN_R, NA_WIN_C = 8, 128, 8, 16
ROPE_BASE = 10000.0
EPS = 1e-6
NEG = -1e30
LOG2E = 1.4426950408889634

LANES = 128
SUBLANES = 8
VMEM_LIMIT_BYTES = 56 * 1024 * 1024

HEAD_W = 128
MLA_QPAD = 256
HG_CHUNK = 128
HG_LEVELS = int(math.log2(HG_CHUNK))
NA_QROWS = 8
NA_KROWS = 16


def _params(semantics):
    return pltpu.CompilerParams(dimension_semantics=semantics, vmem_limit_bytes=VMEM_LIMIT_BYTES)


def _pow2_tile(pref, *dims):
    t = pref
    while any(d % t for d in dims):
        t //= 2
    assert t >= SUBLANES, (pref, dims)
    return t


def _dot(a, b):
    return jnp.dot(a, b, preferred_element_type=f32)


def _dot_nt(a, b):
    return lax.dot_general(a, b, (((1,), (1,)), ((), ())), preferred_element_type=f32)


def _dot_tn(a, b):
    return lax.dot_general(a, b, (((0,), (0,)), ((), ())), preferred_element_type=f32)


def _silu(x):
    return x * jax.nn.sigmoid(x)


class _ZLayout:
    def __init__(self, d_model, tn):
        self.qn = 0
        self.qp = self.qn + MLA_HEADS * MLA_NOPE
        self.ckv = self.qp + MLA_HEADS * MLA_ROPE
        self.hq = self.ckv + MLA_KV_RANK
        self.hff = self.hq + HG_HEADS * HG_K
        self.hfb = self.hff + HG_HEADS * HG_K
        self.hi = self.hfb + HG_HEADS * HG_K
        self.hg = self.hi + HG_HEADS * HG_V
        self.nq = self.hg + HG_HEADS * HG_V
        self.nk = self.nq + NA_HEADS * NA_DH
        self.nv = self.nk + NA_HEADS * NA_DH
        self.ga = self.nv + NA_HEADS * NA_DH
        self.gb = self.ga + d_model
        self.gc = self.gb + d_model
        self.kpe = self.gc + d_model
        used = self.kpe + MLA_ROPE
        self.width = -(-used // tn) * tn


def _permute_w_in(w, lay):
    d = w.shape[0]
    n_mq = MLA_HEADS * MLA_QK
    mq = w[:, :n_mq].reshape(d, MLA_HEADS, MLA_QK)
    qn = mq[:, :, :MLA_NOPE].reshape(d, MLA_HEADS * MLA_NOPE)
    qp = mq[:, :, MLA_NOPE:].reshape(d, MLA_HEADS * MLA_ROPE)
    ckv = w[:, n_mq:n_mq + MLA_KV_RANK]
    kpe = w[:, n_mq + MLA_KV_RANK:n_mq + MLA_KV_RANK + MLA_ROPE]
    rest = w[:, n_mq + MLA_KV_RANK + MLA_ROPE:]
    pad = jnp.zeros((d, lay.width - lay.kpe - MLA_ROPE), w.dtype)
    return jnp.concatenate([qn, qp, ckv, rest, kpe, pad], axis=1).astype(bf16)


def _mod_kernel(c_ref, w_ref, b_ref, o_ref):
    a = _silu(c_ref[...]).astype(bf16)
    o_ref[...] = _dot(a, w_ref[...].astype(bf16)) + b_ref[...]


def _modulation(cond, w_mod, b_mod):
    depth, d, n = w_mod.shape
    g = cond.shape[0]
    tn = _pow2_tile(1024, n)
    return pl.pallas_call(
        _mod_kernel,
        grid=(depth, n // tn),
        in_specs=[
            pl.BlockSpec((g, d), lambda l, j: (0, 0)),
            pl.BlockSpec((None, d, tn), lambda l, j: (l, 0, j)),
            pl.BlockSpec((None, 1, tn), lambda l, j: (l, 0, j)),
        ],
        out_specs=pl.BlockSpec((None, g, tn), lambda l, j: (l, 0, j)),
        out_shape=jax.ShapeDtypeStruct((depth, g, n), f32),
        compiler_params=_params(("arbitrary", "arbitrary")),
        name="modulation",
    )(cond, w_mod, b_mod.reshape(depth, 1, n))


def _normmod_matmul_kernel(x_ref, g_ref, sc_ref, sh_ref, w_ref, o_ref, h_scr):
    @pl.when(pl.program_id(1) == 0)
    def _():
        x = x_ref[...]
        ms = jnp.mean(x * x, axis=-1, keepdims=True)
        y = x * lax.rsqrt(ms + EPS) * g_ref[...]
        h_scr[...] = (y * (1.0 + sc_ref[...]) + sh_ref[...]).astype(bf16)

    o_ref[...] = _dot(h_scr[...], w_ref[...])


def _normmod_matmul(x, g, sc, sh, w, group_of_tile, tm, tn, name):
    m, d = x.shape
    n = w.shape[1]
    return pl.pallas_call(
        _normmod_matmul_kernel,
        grid=(m // tm, n // tn),
        in_specs=[
            pl.BlockSpec((tm, d), lambda i, j: (i, 0)),
            pl.BlockSpec((1, d), lambda i, j: (0, 0)),
            pl.BlockSpec((None, 1, d), lambda i, j: (group_of_tile(i), 0, 0)),
            pl.BlockSpec((None, 1, d), lambda i, j: (group_of_tile(i), 0, 0)),
            pl.BlockSpec((d, tn), lambda i, j: (0, j)),
        ],
        out_specs=pl.BlockSpec((tm, tn), lambda i, j: (i, j)),
        out_shape=jax.ShapeDtypeStruct((m, n), f32),
        scratch_shapes=[pltpu.VMEM((tm, d), bf16)],
        compiler_params=_params(("arbitrary", "arbitrary")),
        name=name,
    )(x, g.reshape(1, d), sc, sh, w)


def _rope(x, cos, sin_signed):
    n = x.shape[-1]
    half = MLA_ROPE // 4
    lane = lax.broadcasted_iota(jnp.int32, x.shape, 1)
    partner = jnp.where((lane % (2 * half)) < half, pltpu.roll(x, n - half, 1), pltpu.roll(x, half, 1))
    return x * cos + partner * sin_signed


def _mla_q_kernel(qn_ref, qp_ref, gn_ref, gp_ref, cos_ref, sin_ref, o_ref, *, scale):
    qn = qn_ref[...]
    qp = qp_ref[...]
    tm = qn.shape[0]
    lane = lax.broadcasted_iota(jnp.int32, (tm, LANES), 1)
    low = lane < MLA_ROPE
    cos = cos_ref[...]
    sin = sin_ref[...]
    gn = gn_ref[...]
    gp = gp_ref[...]
    for pair in range(MLA_HEADS // 2):
        pcol = qp[:, pair * LANES:(pair + 1) * LANES]
        sq = pcol * pcol
        ss_all = jnp.sum(sq, axis=-1, keepdims=True)
        ss_low = jnp.sum(jnp.where(low, sq, 0.0), axis=-1, keepdims=True)
        rot = _rope(pcol * gp, cos, sin)
        for k, ss_pe in enumerate((ss_low, ss_all - ss_low)):
            h = 2 * pair + k
            nope = qn[:, h * MLA_NOPE:(h + 1) * MLA_NOPE]
            ss = jnp.sum(nope * nope, axis=-1, keepdims=True) + ss_pe
            r = lax.rsqrt(ss * (1.0 / MLA_QK) + EPS) * scale
            pe = rot if k == 0 else pltpu.roll(rot, MLA_ROPE, 1)
            o_ref[:, h * MLA_QPAD:h * MLA_QPAD + MLA_NOPE] = (nope * r * gn).astype(bf16)
            o_ref[:, h * MLA_QPAD + MLA_NOPE:(h + 1) * MLA_QPAD] = jnp.where(low, pe * r, 0.0).astype(bf16)


def _mla_kv_kernel(ckv_ref, kpe_ref, kvg_ref, gn_ref, gp_ref, wuk_ref, wuv_ref, cos_ref, sin_ref,
                   k_ref, v_ref, ckv_out_ref, *, norm_ckv, rope):
    ckv = ckv_ref[...]
    if norm_ckv:
        ms = jnp.mean(ckv * ckv, axis=-1, keepdims=True)
        ckv = ckv * lax.rsqrt(ms + EPS) * kvg_ref[...]
    ckv_out_ref[...] = ckv
    cb = ckv.astype(bf16)
    kn = _dot(cb, wuk_ref[...])
    v_ref[...] = _dot(cb, wuv_ref[...]).astype(bf16)
    tm = ckv.shape[0]
    lane = lax.broadcasted_iota(jnp.int32, (tm, LANES), 1)
    low = lane < MLA_ROPE
    kpe = jnp.where(low, kpe_ref[...], 0.0)
    ss_pe = jnp.sum(kpe * kpe, axis=-1, keepdims=True)
    pe = kpe * gp_ref[...]
    if rope:
        pe = _rope(pe, cos_ref[...], sin_ref[...])
    gn = gn_ref[...]
    for h in range(MLA_HEADS):
        nope = kn[:, h * MLA_NOPE:(h + 1) * MLA_NOPE]
        ss = jnp.sum(nope * nope, axis=-1, keepdims=True) + ss_pe
        r = lax.rsqrt(ss * (1.0 / MLA_QK) + EPS)
        k_ref[:, h * MLA_QPAD:h * MLA_QPAD + MLA_NOPE] = (nope * r * gn).astype(bf16)
        k_ref[:, h * MLA_QPAD + MLA_NOPE:(h + 1) * MLA_QPAD] = jnp.where(low, pe * r, 0.0).astype(bf16)


def _rope_tables(t_len, tm):
    n_freq = MLA_ROPE // 4
    t = jnp.arange(t_len, dtype=jnp.int32)
    inv = ROPE_BASE ** (-jnp.arange(n_freq, dtype=f32) / n_freq)
    ang_r = (t // GRID_W).astype(f32)[:, None] * inv
    ang_c = (t % GRID_W).astype(f32)[:, None] * inv
    cos64 = jnp.concatenate([jnp.cos(ang_r), jnp.cos(ang_r), jnp.cos(ang_c), jnp.cos(ang_c)], axis=1)
    sin64 = jnp.concatenate([-jnp.sin(ang_r), jnp.sin(ang_r), -jnp.sin(ang_c), jnp.sin(ang_c)], axis=1)
    cos = jnp.concatenate([jnp.ones((tm, LANES), f32), jnp.tile(cos64, (1, 2))], axis=0)
    sin = jnp.concatenate([jnp.zeros((tm, LANES), f32), jnp.tile(sin64, (1, 2))], axis=0)
    return cos, sin


def _mla_q(z, lay, q_g, cos, sin, rope_block, tm):
    m = z.shape[0]
    gn = q_g[:MLA_NOPE].reshape(1, MLA_NOPE)
    gp = jnp.tile(q_g[MLA_NOPE:], 2).reshape(1, LANES)
    wn, wp = MLA_HEADS * MLA_NOPE, MLA_HEADS * MLA_ROPE
    return pl.pallas_call(
        functools.partial(_mla_q_kernel, scale=MLA_QK ** -0.5),
        grid=(m // tm,),
        in_specs=[
            pl.BlockSpec((tm, wn), lambda i: (i, lay.qn // wn)),
            pl.BlockSpec((tm, wp), lambda i: (i, lay.qp // wp)),
            pl.BlockSpec((1, MLA_NOPE), lambda i: (0, 0)),
            pl.BlockSpec((1, LANES), lambda i: (0, 0)),
            pl.BlockSpec((tm, LANES), lambda i: (rope_block(i), 0)),
            pl.BlockSpec((tm, LANES), lambda i: (rope_block(i), 0)),
        ],
        out_specs=pl.BlockSpec((tm, MLA_HEADS * MLA_QPAD), lambda i: (i, 0)),
        out_shape=jax.ShapeDtypeStruct((m, MLA_HEADS * MLA_QPAD), bf16),
        compiler_params=_params(("arbitrary",)),
        name="mla_q",
    )(z, z, gn, gp, cos, sin)


def _mla_kv(ckv_src, ckv_col, kpe_src, kpe_col, kv_g, k_g, w_uk, w_uv, cos, sin, rope_block, tm, norm_ckv, rope, name):
    m = ckv_src.shape[0]
    gn = k_g[:MLA_NOPE].reshape(1, MLA_NOPE)
    gp = jnp.concatenate([k_g[MLA_NOPE:], jnp.zeros((LANES - MLA_ROPE,), f32)]).reshape(1, LANES)
    hw = MLA_HEADS * MLA_NOPE
    return pl.pallas_call(
        functools.partial(_mla_kv_kernel, norm_ckv=norm_ckv, rope=rope),
        grid=(m // tm,),
        in_specs=[
            pl.BlockSpec((tm, MLA_KV_RANK), lambda i: (i, ckv_col // MLA_KV_RANK)),
            pl.BlockSpec((tm, LANES), lambda i: (i, kpe_col // LANES)),
            pl.BlockSpec((1, MLA_KV_RANK), lambda i: (0, 0)),
            pl.BlockSpec((1, MLA_NOPE), lambda i: (0, 0)),
            pl.BlockSpec((1, LANES), lambda i: (0, 0)),
            pl.BlockSpec((MLA_KV_RANK, hw), lambda i: (0, 0)),
            pl.BlockSpec((MLA_KV_RANK, hw), lambda i: (0, 0)),
            pl.BlockSpec((tm, LANES), lambda i: (rope_block(i), 0)),
            pl.BlockSpec((tm, LANES), lambda i: (rope_block(i), 0)),
        ],
        out_specs=[
            pl.BlockSpec((tm, MLA_HEADS * MLA_QPAD), lambda i: (i, 0)),
            pl.BlockSpec((tm, hw), lambda i: (i, 0)),
            pl.BlockSpec((tm, MLA_KV_RANK), lambda i: (i, 0)),
        ],
        out_shape=[
            jax.ShapeDtypeStruct((m, MLA_HEADS * MLA_QPAD), bf16),
            jax.ShapeDtypeStruct((m, hw), bf16),
            jax.ShapeDtypeStruct((m, MLA_KV_RANK), f32),
        ],
        compiler_params=_params(("arbitrary",)),
        name=name,
    )(ckv_src, kpe_src, kv_g.reshape(1, MLA_KV_RANK), gn, gp, w_uk, w_uv, cos, sin)


def _flash_kernel(*refs, n_main, tk, n_extra):
    if n_extra:
        q_ref, k_ref, v_ref, kx_ref, vx_ref, o_ref = refs
    else:
        q_ref, k_ref, v_ref, o_ref = refs
    q = q_ref[...]
    tq = q.shape[0]
    dv = v_ref.shape[-1]

    def step(k, v, carry):
        m, l, acc = carry
        s = _dot_nt(q, k)
        m_new = jnp.maximum(m, jnp.max(s, axis=-1, keepdims=True))
        alpha = jnp.exp(m - m_new)
        p = jnp.exp(s - m_new)
        l = alpha * l + jnp.sum(p, axis=-1, keepdims=True)
        acc = alpha * acc + _dot(p.astype(bf16), v)
        return m_new, l, acc

    def main_body(c, carry):
        off = pl.multiple_of(c * tk, tk)
        return step(k_ref[pl.ds(off, tk), :], v_ref[pl.ds(off, tk), :], carry)

    carry = (jnp.full((tq, 1), -jnp.inf, f32), jnp.zeros((tq, 1), f32), jnp.zeros((tq, dv), f32))
    carry = lax.fori_loop(0, n_main, main_body, carry)
    for c in range(n_extra):
        carry = step(kx_ref[c * tk:(c + 1) * tk, :], vx_ref[c * tk:(c + 1) * tk, :], carry)
    _, l, acc = carry
    o_ref[...] = (acc / l).astype(o_ref.dtype)


def _flash(q, k, v, *, batch, heads, t_len, s_len, row0, dq, dv, tq, tk, extra=None, name):
    assert row0 % tq == 0 and row0 % s_len == 0 and t_len % tq == 0 and s_len % tk == 0
    nq = t_len // tq
    in_specs = [
        pl.BlockSpec((tq, dq), lambda b, h, i: (row0 // tq + b * nq + i, h)),
        pl.BlockSpec((s_len, dq), lambda b, h, i: (row0 // s_len + b, h)),
        pl.BlockSpec((s_len, dv), lambda b, h, i: (row0 // s_len + b, h)),
    ]
    args = [q, k, v]
    n_extra = 0
    if extra is not None:
        kx, vx, p_len = extra
        assert p_len % tk == 0
        n_extra = p_len // tk
        in_specs += [
            pl.BlockSpec((p_len, dq), lambda b, h, i: (b, h)),
            pl.BlockSpec((p_len, dv), lambda b, h, i: (b, h)),
        ]
        args += [kx, vx]
    return pl.pallas_call(
        functools.partial(_flash_kernel, n_main=s_len // tk, tk=tk, n_extra=n_extra),
        grid=(batch, heads, nq),
        in_specs=in_specs,
        out_specs=pl.BlockSpec((tq, dv), lambda b, h, i: (b * nq + i, h)),
        out_shape=jax.ShapeDtypeStruct((batch * t_len, heads * dv), bf16),
        compiler_params=_params(("arbitrary", "arbitrary", "arbitrary")),
        name=name,
    )(*args)


def _na_prep_kernel(nq_ref, nk_ref, nv_ref, qg_ref, kg_ref, q_ref, k_ref, v_ref, kf_ref, *, scale):
    qg = qg_ref[...]
    kg = kg_ref[...]
    v_ref[...] = nv_ref[...].astype(bf16)
    for h in range(NA_HEADS):
        sl = slice(h * NA_DH, (h + 1) * NA_DH)
        q = nq_ref[:, sl]
        k = nk_ref[:, sl]
        rq = lax.rsqrt(jnp.mean(q * q, axis=-1, keepdims=True) + EPS)
        rk = lax.rsqrt(jnp.mean(k * k, axis=-1, keepdims=True) + EPS)
        q_ref[:, sl] = (q * rq * qg * scale).astype(bf16)
        kn = k * rk * kg
        kf_ref[:, sl] = kn
        k_ref[:, sl] = kn.astype(bf16)


def _na_prep(z, lay, q_g, k_g, tm):
    m = z.shape[0]
    w = NA_HEADS * NA_DH
    spec = pl.BlockSpec((tm, w), lambda i: (i, 0))
    return pl.pallas_call(
        functools.partial(_na_prep_kernel, scale=NA_DH ** -0.5),
        grid=(m // tm,),
        in_specs=[
            pl.BlockSpec((tm, w), lambda i: (i, lay.nq // w)),
            pl.BlockSpec((tm, w), lambda i: (i, lay.nk // w)),
            pl.BlockSpec((tm, w), lambda i: (i, lay.nv // w)),
            pl.BlockSpec((1, NA_DH), lambda i: (0, 0)),
            pl.BlockSpec((1, NA_DH), lambda i: (0, 0)),
        ],
        out_specs=[spec, spec, spec, spec],
        out_shape=[jax.ShapeDtypeStruct((m, w), bf16)] * 3 + [jax.ShapeDtypeStruct((m, w), f32)],
        compiler_params=_params(("arbitrary",)),
        name="na_prep",
    )(z, z, z, q_g.reshape(1, NA_DH), k_g.reshape(1, NA_DH))


def _na_static_maps(rows):
    nblk = rows // NA_QROWS
    reps = (0, min(1, nblk - 1), nblk - 1)
    dr_map = np.full((3, NA_QROWS, NA_KROWS), 2 * NA_WIN_R - 1, np.int32)
    for v, kb in enumerate(reps):
        ws = min(max(NA_QROWS * kb - NA_WIN_R // 2, 0), rows - NA_KROWS)
        for i in range(NA_QROWS):
            qrow = NA_QROWS * kb + i
            rs = min(max(qrow - NA_WIN_R // 2, 0), rows - NA_WIN_R)
            for j in range(NA_KROWS):
                krow = ws + j
                if rs <= krow < rs + NA_WIN_R:
                    dr_map[v, i, j] = krow - qrow + NA_WIN_R - 1
    qcol = np.arange(GRID_W)
    cs = np.clip(qcol - NA_WIN_C // 2, 0, GRID_W - NA_WIN_C)
    in_win = (qcol[None, :] >= cs[:, None]) & (qcol[None, :] < cs[:, None] + NA_WIN_C)
    dc_idx = np.clip(qcol[None, :] - qcol[:, None], -(NA_WIN_C - 1), NA_WIN_C - 1) + (NA_WIN_C - 1)
    return dr_map, in_win, dc_idx


def _na_bias_table(rpb, rows):
    dr_map, in_win, dc_idx = _na_static_maps(rows)
    h = rpb.shape[0]
    bt = jnp.where(in_win[None, None], jnp.take(rpb, jnp.asarray(dc_idx), axis=2), NEG)
    bt = jnp.concatenate([bt, jnp.full((h, 1, GRID_W, GRID_W), NEG, f32)], axis=1)
    tab = jnp.take(bt, jnp.asarray(dr_map.reshape(-1)), axis=1)
    tab = tab.reshape(h, 3, NA_QROWS, NA_KROWS, GRID_W, GRID_W).transpose(1, 0, 2, 4, 3, 5)
    return tab.reshape(3, h, NA_QROWS * GRID_W, NA_KROWS * GRID_W)


def _na_kernel(q_ref, k_ref, v_ref, kc_ref, vc_ref, bias_ref, o_ref, *, rows):
    kb = pl.program_id(2)
    ws = jnp.clip(NA_QROWS * kb - NA_WIN_R // 2, 0, rows - NA_KROWS)
    off = pl.multiple_of(ws * GRID_W, (NA_WIN_R // 2) * GRID_W)
    n_keys = NA_KROWS * GRID_W
    q = q_ref[...]
    s_loc = _dot_nt(q, k_ref[pl.ds(off, n_keys), :]) + bias_ref[...]
    s_ctx = _dot_nt(q, kc_ref[...])
    m = jnp.maximum(jnp.max(s_loc, axis=-1, keepdims=True), jnp.max(s_ctx, axis=-1, keepdims=True))
    p_loc = jnp.exp(s_loc - m)
    p_ctx = jnp.exp(s_ctx - m)
    l = jnp.sum(p_loc, axis=-1, keepdims=True) + jnp.sum(p_ctx, axis=-1, keepdims=True)
    acc = _dot(p_loc.astype(bf16), v_ref[pl.ds(off, n_keys), :]) + _dot(p_ctx.astype(bf16), vc_ref[...])
    o_ref[...] = (acc / l).astype(o_ref.dtype)


def _na_latent(q, k, v, kc, vc, bias, *, batch, t_len, p_len, row0):
    rows = t_len // GRID_W
    assert t_len % GRID_W == 0 and rows % NA_QROWS == 0 and rows >= NA_KROWS
    tq = NA_QROWS * GRID_W
    assert row0 % tq == 0 and row0 % t_len == 0
    nblk = rows // NA_QROWS

    def variant(i):
        return jnp.where(i == 0, 0, jnp.where(i == nblk - 1, 2, 1))

    return pl.pallas_call(
        functools.partial(_na_kernel, rows=rows),
        grid=(batch, NA_HEADS, nblk),
        in_specs=[
            pl.BlockSpec((tq, NA_DH), lambda b, h, i: (row0 // tq + b * nblk + i, h)),
            pl.BlockSpec((t_len, NA_DH), lambda b, h, i: (row0 // t_len + b, h)),
            pl.BlockSpec((t_len, NA_DH), lambda b, h, i: (row0 // t_len + b, h)),
            pl.BlockSpec((p_len, NA_DH), lambda b, h, i: (b, h)),
            pl.BlockSpec((p_len, NA_DH), lambda b, h, i: (b, h)),
            pl.BlockSpec((None, None, tq, NA_KROWS * GRID_W), lambda b, h, i: (variant(i), h, 0, 0)),
        ],
        out_specs=pl.BlockSpec((tq, NA_DH), lambda b, h, i: (b * nblk + i, h)),
        out_shape=jax.ShapeDtypeStruct((batch * t_len, NA_HEADS * NA_DH), bf16),
        compiler_params=_params(("arbitrary", "arbitrary", "arbitrary")),
        name="na_latent",
    )(q, k, v, kc, vc, bias)


def _hgrn_tables():
    c, lv = HG_CHUNK, HG_LEVELS
    t = np.arange(c)[:, None]
    r = np.arange(c)[None, :]
    wcum = np.stack([r <= t, r >= t]).astype(np.float32)
    qside = np.zeros((2, lv * c, LANES), np.float32)
    mask = np.zeros((2, lv + 1, c, c), np.float32)
    mask[:, 0] = np.eye(c)
    for l in range(lv):
        hs = 1 << l
        blk = t // (2 * hs)
        mid = blk * (2 * hs) + hs
        qf = t >= mid
        qb = t < mid
        qside[0, l * c:(l + 1) * c] = np.where(qf, 1.0, -1.0)
        qside[1, l * c:(l + 1) * c] = np.where(qb, 1.0, -1.0)
        same = blk == blk.T
        mask[0, 1 + l] = same & qf & ~qf.T
        mask[1, 1 + l] = same & qb & ~qb.T
    return wcum, qside, mask


def _hgrn_kernel(*refs, n_chunks, heads, has_state, emit_state):
    (hq_ref, hff_ref, hfb_ref, hi_ref, hg_ref, la_ref, lc_ref, om_ref, ng_ref,
     wcum_ref, qside_ref, mask_ref) = refs[:12]
    rest = list(refs[12:])
    s0_ref = rest.pop(0) if has_state else None
    o_ref = rest.pop(0)
    st_ref = rest.pop(0) if emit_state else None
    o_scr, s_scr, b_scr = rest
    c, lv = HG_CHUNK, HG_LEVELS

    o_scr[...] = jnp.zeros_like(o_scr)
    for d in range(2):
        for g in range(heads):
            if has_state:
                s_scr[d, g] = s0_ref[d, g].T
            else:
                s_scr[d, g] = jnp.zeros((HG_V, HG_K), f32)

    chains = [(d, g) for g in range(heads) for d in range(2)]
    low_half = lax.broadcasted_iota(jnp.int32, (SUBLANES, HEAD_W), 0) < SUBLANES // 2

    def gates_and_cumsum(ci, row):
        d, g = chains[ci]
        rs = pl.ds(pl.multiple_of(row, c), c)
        cs = slice(g * HEAD_W, (g + 1) * HEAD_W)
        x = (hff_ref if d == 0 else hfb_ref)[rs, cs]
        sig = jax.nn.sigmoid(x)
        log_sig = jnp.minimum(x, 0.0) - jnp.log(1.0 + jnp.exp(-jnp.abs(x)))
        a = la_ref[d:d + 1, cs]
        bb = lc_ref[d:d + 1, cs] + log_sig
        lf = jnp.maximum(a, bb) + jnp.log(1.0 + jnp.exp(-jnp.abs(a - bb)))
        om = om_ref[d:d + 1, cs]
        f = (1.0 - om) + om * sig
        kin = om * (1.0 - sig)
        q = _silu(hq_ref[rs, cs])
        v = hi_ref[rs, cs].astype(bf16)
        lf = lf * LOG2E
        l1 = lf.astype(bf16)
        r1 = lf - l1.astype(f32)
        l2 = r1.astype(bf16)
        l3 = (r1 - l2.astype(f32)).astype(bf16)
        w = wcum_ref[d]
        b = _dot(w, l1) + _dot(w, l2) + _dot(w, l3)
        b_scr[ci] = b
        return dict(rs=rs, cs=cs, f=f, kin=kin, q=q, v=v, b=b)

    def intra_chunk(ci, s):
        d, _ = chains[ci]
        b, q, kin = s["b"], s["q"], s["kin"]

        def row(r):
            return b_scr[ci, r:r + 1, :]

        s["b_end"] = row(c - 1 if d == 0 else 0)
        xs = [jnp.where(qside_ref[d, 0:c, :] > 0.0, q * s["f"], kin).astype(bf16)]
        for l in range(1, lv):
            hs = 1 << l
            blk = 2 * hs
            pick = hs - 1 if d == 0 else hs
            if blk < SUBLANES:
                pieces = [jnp.where(low_half, jnp.broadcast_to(row(SUBLANES * j + pick), (SUBLANES, HEAD_W)),
                                    jnp.broadcast_to(row(SUBLANES * j + blk + pick), (SUBLANES, HEAD_W)))
                          for j in range(c // SUBLANES)]
            else:
                pieces = [jnp.broadcast_to(row(blk * j + pick), (blk, HEAD_W)) for j in range(c // blk)]
            m = pieces[0] if len(pieces) == 1 else jnp.concatenate(pieces, axis=0)
            sign = qside_ref[d, l * c:(l + 1) * c, :]
            xs.append((jnp.where(sign > 0.0, q, kin) * jnp.exp2((b - m) * sign)).astype(bf16))
        att = mask_ref[d, 0] * _dot_nt(q.astype(bf16), kin.astype(bf16)).astype(bf16)
        for l in range(lv):
            att = att + mask_ref[d, 1 + l] * _dot_nt(xs[l], xs[l]).astype(bf16)
        s["att"] = att

    def state_step(ci, s):
        d, g = chains[ci]
        b, q, kin, v = s["b"], s["q"], s["kin"], s["v"]
        st = s_scr[d, g]
        o = _dot(s["att"], v) + _dot_nt((q * jnp.exp2(b)).astype(bf16), st.astype(bf16))
        b_end = s["b_end"]
        s_scr[d, g] = st * jnp.exp2(b_end) + _dot_tn(v, (kin * jnp.exp2(b_end - b)).astype(bf16))
        o_scr[s["rs"], s["cs"]] += o

    def body(i, _):
        rows = (i * c, (n_chunks - 1 - i) * c)
        states = [gates_and_cumsum(ci, rows[chains[ci][0]]) for ci in range(len(chains))]
        for ci, s in enumerate(states):
            intra_chunk(ci, s)
        for ci, s in enumerate(states):
            state_step(ci, s)
        return 0

    lax.fori_loop(0, n_chunks, body, 0)

    ng = ng_ref[...]
    for g in range(heads):
        cs = slice(g * HEAD_W, (g + 1) * HEAD_W)
        o = o_scr[:, cs]
        y = o * lax.rsqrt(jnp.mean(o * o, axis=-1, keepdims=True) + EPS) * ng
        o_ref[:, cs] = (y * _silu(hg_ref[:, cs])).astype(o_ref.dtype)
        if emit_state:
            for d in range(2):
                st_ref[d, g] = s_scr[d, g].T


def _hgrn(z, lay, lb_tabs, norm_g, tables, *, batch, t_len, row0, s0, emit_state, heads_per_step=2):
    assert t_len % HG_CHUNK == 0 and row0 % t_len == 0 and HG_K == HEAD_W and HG_V == HEAD_W
    g = heads_per_step
    gw = g * HEAD_W
    la, lc, om = lb_tabs
    wcum, qside, mask = tables
    r0 = row0 // t_len

    def zspec(col):
        return pl.BlockSpec((t_len, gw), lambda b, h: (r0 + b, col // gw + h))

    def const(shape):
        return pl.BlockSpec(shape, lambda b, h: (0,) * len(shape))

    in_specs = [zspec(lay.hq), zspec(lay.hff), zspec(lay.hfb), zspec(lay.hi), zspec(lay.hg),
                pl.BlockSpec((2, gw), lambda b, h: (0, h)), pl.BlockSpec((2, gw), lambda b, h: (0, h)),
                pl.BlockSpec((2, gw), lambda b, h: (0, h)), const((1, HG_V)),
                const(wcum.shape), const(qside.shape), const(mask.shape)]
    args = [z, z, z, z, z, la, lc, om, norm_g.reshape(1, HG_V), wcum, qside, mask]
    state_spec = pl.BlockSpec((None, 2, g, HG_K, HG_V), lambda b, h: (b, 0, h, 0, 0))
    if s0 is not None:
        in_specs.append(state_spec)
        args.append(s0)
    out_specs = [pl.BlockSpec((t_len, gw), lambda b, h: (b, h))]
    out_shape = [jax.ShapeDtypeStruct((batch * t_len, HG_HEADS * HG_V), bf16)]
    if emit_state:
        out_specs.append(state_spec)
        out_shape.append(jax.ShapeDtypeStruct((batch, 2, HG_HEADS, HG_K, HG_V), f32))
    res = pl.pallas_call(
        functools.partial(_hgrn_kernel, n_chunks=t_len // HG_CHUNK, heads=g, has_state=s0 is not None,
                          emit_state=emit_state),
        grid=(batch, HG_HEADS // g),
        in_specs=in_specs,
        out_specs=out_specs,
        out_shape=out_shape,
        scratch_shapes=[pltpu.VMEM((t_len, gw), f32), pltpu.VMEM((2, g, HG_V, HG_K), f32),
                        pltpu.VMEM((2 * g, HG_CHUNK, HEAD_W), f32)],
        compiler_params=_params(("arbitrary", "arbitrary")),
        name="hgrn_ctx" if emit_state else "hgrn_latent",
    )(*args)
    return res if emit_state else (res[0], None)


def _merge_kernel(oa_ref, ob_ref, oc_ref, ga_ref, gb_ref, gc_ref, w_ref, o_ref):
    m = jax.nn.sigmoid(ga_ref[...]) * _dot(oa_ref[...], w_ref[0])
    m = m + jax.nn.sigmoid(gb_ref[...]) * _dot(ob_ref[...], w_ref[1])
    m = m + jax.nn.sigmoid(gc_ref[...]) * _dot(oc_ref[...], w_ref[2])
    o_ref[...] = m.astype(o_ref.dtype)


def _merge(o_a, o_b, o_c, z, lay, w_branch, tm, tn):
    m, bw = o_a.shape
    d = w_branch.shape[2]
    ospec = pl.BlockSpec((tm, bw), lambda i, j: (i, 0))

    def gate(col):
        return pl.BlockSpec((tm, tn), lambda i, j: (i, col // tn + j))

    return pl.pallas_call(
        _merge_kernel,
        grid=(m // tm, d // tn),
        in_specs=[ospec, ospec, ospec, gate(lay.ga), gate(lay.gb), gate(lay.gc),
                  pl.BlockSpec((3, bw, tn), lambda i, j: (0, 0, j))],
        out_specs=pl.BlockSpec((tm, tn), lambda i, j: (i, j)),
        out_shape=jax.ShapeDtypeStruct((m, d), bf16),
        compiler_params=_params(("arbitrary", "arbitrary")),
        name="merge",
    )(o_a, o_b, o_c, z, z, z, w_branch)


def _proj_residual_kernel(m_ref, w_ref, x_ref, g_ref, o_ref):
    o_ref[...] = x_ref[...] + g_ref[...] * _dot(m_ref[...], w_ref[...])


def _proj_residual(mm, w, x, gate, group_of_tile, tm, tn):
    m, k = mm.shape
    d = w.shape[1]
    return pl.pallas_call(
        _proj_residual_kernel,
        grid=(m // tm, d // tn),
        in_specs=[
            pl.BlockSpec((tm, k), lambda i, j: (i, 0)),
            pl.BlockSpec((k, tn), lambda i, j: (0, j)),
            pl.BlockSpec((tm, tn), lambda i, j: (i, j)),
            pl.BlockSpec((None, 1, tn), lambda i, j: (group_of_tile(i), 0, j)),
        ],
        out_specs=pl.BlockSpec((tm, tn), lambda i, j: (i, j)),
        out_shape=jax.ShapeDtypeStruct((m, d), f32),
        compiler_params=_params(("arbitrary", "arbitrary")),
        name="out_proj",
    )(mm, w, x, gate)


def _ffn_down_kernel(a_ref, g_ref, gprev_ref, gnext_ref, cw_ref, cb_ref, w_ref, x_ref, gate_ref, o_ref,
                     *, m_ctx, s_len, t_len):
    i = pl.program_id(0)
    k = pl.program_id(1)
    g = g_ref[...]
    tm = g.shape[0]
    row = i * tm + lax.broadcasted_iota(jnp.int32, (tm, 1), 0)
    pos = jnp.where(row < m_ctx, row % s_len, (row - m_ctx) % t_len)
    last = jnp.where(row < m_ctx, s_len - 1, t_len - 1)
    local = lax.broadcasted_iota(jnp.int32, (tm, 1), 0)
    g_prev = jnp.where(local == 0, gprev_ref[SUBLANES - 1:SUBLANES, :], pltpu.roll(g, 1, 0))
    g_prev = jnp.where(pos == 0, 0.0, g_prev)
    g_next = jnp.where(local == tm - 1, gnext_ref[0:1, :], pltpu.roll(g, tm - 1, 0))
    g_next = jnp.where(pos == last, 0.0, g_next)
    conv = g_prev * cw_ref[0:1, :] + g * cw_ref[1:2, :] + g_next * cw_ref[2:3, :] + cb_ref[...]
    u = (_silu(conv) * a_ref[...]).astype(bf16)
    part = _dot(u, w_ref[...])

    @pl.when(k == 0)
    def _():
        o_ref[...] = part

    @pl.when(k > 0)
    def _():
        o_ref[...] += part

    @pl.when(k == pl.num_programs(1) - 1)
    def _():
        o_ref[...] = x_ref[...] + gate_ref[...] * o_ref[...]


def _ffn_down(ag, conv_w, conv_b, w_down, x, gate, group_of_tile, tm, tk, m_ctx, s_len, t_len):
    m, d = x.shape
    f = w_down.shape[0]
    nk = f // tk
    nsub = tm // SUBLANES
    last_sub = m // SUBLANES - 1
    return pl.pallas_call(
        functools.partial(_ffn_down_kernel, m_ctx=m_ctx, s_len=s_len, t_len=t_len),
        grid=(m // tm, nk),
        in_specs=[
            pl.BlockSpec((tm, tk), lambda i, k: (i, k)),
            pl.BlockSpec((tm, tk), lambda i, k: (i, nk + k)),
            pl.BlockSpec((SUBLANES, tk), lambda i, k: (jnp.maximum(i * nsub - 1, 0), nk + k)),
            pl.BlockSpec((SUBLANES, tk), lambda i, k: (jnp.minimum((i + 1) * nsub, last_sub), nk + k)),
            pl.BlockSpec((3, tk), lambda i, k: (0, k)),
            pl.BlockSpec((1, tk), lambda i, k: (0, k)),
            pl.BlockSpec((tk, d), lambda i, k: (k, 0)),
            pl.BlockSpec((tm, d), lambda i, k: (i, 0)),
            pl.BlockSpec((None, 1, d), lambda i, k: (group_of_tile(i), 0, 0)),
        ],
        out_specs=pl.BlockSpec((tm, d), lambda i, k: (i, 0)),
        out_shape=jax.ShapeDtypeStruct((m, d), f32),
        compiler_params=_params(("arbitrary", "arbitrary")),
        name="ffn_down",
    )(ag, ag, ag, ag, conv_w, conv_b.reshape(1, f), w_down, x, gate)


def kernel(x_prompt, x_sample, cache_mla_ckv, cache_mla_kpe, cache_na_k, cache_na_v, state_hgrn, c, c_ctx, w_mod, b_mod, norm1_g, norm2_g, w_in, mla_kv_norm_g, mla_q_norm_g, mla_k_norm_g, mla_w_uk, mla_w_uv, hgrn_lower_bounds, hgrn_norm_g, na_q_norm_g, na_k_norm_g, na_rpb, w_branch, w_out, ffn_w_up, ffn_conv_w, ffn_conv_b, ffn_w_down):
    n_ctx, s_len, d = x_prompt.shape
    n_lat, t_len, _ = x_sample.shape
    depth = w_in.shape[0]
    p_len = cache_mla_ckv.shape[2]
    ffn = ffn_w_down.shape[1]
    m_ctx, m_lat = n_ctx * s_len, n_lat * t_len
    m = m_ctx + m_lat
    assert m_ctx % t_len == 0, "context rows must be a whole number of latent sequences"

    tm = _pow2_tile(1024, m_ctx, t_len)
    tn = _pow2_tile(1024, d)
    tp = _pow2_tile(512, m_ctx, t_len)
    lay = _ZLayout(d, tn)
    n_ctx_tiles = m_ctx // tm
    tiles_per_seq = t_len // tm

    def group_of_tile(i):
        return jnp.where(i < n_ctx_tiles, 0, 1 + (i - n_ctx_tiles) // tiles_per_seq)

    n_groups = -(-(1 + n_lat) // SUBLANES) * SUBLANES
    cond = jnp.concatenate([c_ctx[None], c, jnp.zeros((n_groups - 1 - n_lat, d), f32)], axis=0)
    mods = _modulation(cond, w_mod, b_mod).reshape(depth, n_groups, 6, d).transpose(0, 2, 1, 3)[:, :, :, None, :]

    sm = jax.nn.softmax(hgrn_lower_bounds.astype(f32), axis=1)
    csum = jnp.cumsum(sm, axis=1)
    lower = csum - csum[:, :1]
    hg_tabs = tuple(jnp.asarray(t, dt) for t, dt in zip(_hgrn_tables(), (bf16, f32, bf16)))

    cos, sin = _rope_tables(t_len, tp)
    n_ctx_tp = m_ctx // tp

    def rope_block(i):
        return jnp.where(i < n_ctx_tp, 0, 1 + (i - n_ctx_tp) % (t_len // tp))

    x = jnp.concatenate([x_prompt.reshape(m_ctx, d), x_sample.reshape(m_lat, d)], axis=0)
    st_ckv, st_kpe, st_nk, st_nv, st_hg = [], [], [], [], []
    for l in range(depth):
        sh1, sc1, g1, sh2, sc2, g2 = (mods[l, k] for k in range(6))
        w_in_l = _permute_w_in(w_in[l], lay)
        z = _normmod_matmul(x, norm1_g[l], sc1, sh1, w_in_l, group_of_tile, tm, tn, "in_proj")

        w_uk = mla_w_uk[l].reshape(MLA_KV_RANK, MLA_HEADS * MLA_NOPE).astype(bf16)
        w_uv = mla_w_uv[l].reshape(MLA_KV_RANK, MLA_HEADS * MLA_V).astype(bf16)
        q_mla = _mla_q(z, lay, mla_q_norm_g[l], cos, sin, rope_block, tp)
        k_mla, v_mla, ckv_n = _mla_kv(z, lay.ckv, z, lay.kpe, mla_kv_norm_g[l], mla_k_norm_g[l], w_uk, w_uv,
                                      cos, sin, rope_block, tp, True, True, "mla_kv")
        ckv_c = cache_mla_ckv[:, l].reshape(n_lat * p_len, MLA_KV_RANK)
        kpe_c = jnp.pad(cache_mla_kpe[:, l].reshape(n_lat * p_len, MLA_ROPE), ((0, 0), (0, LANES - MLA_ROPE)))
        tc = _pow2_tile(512, n_lat * p_len)
        k_c, v_c, _ = _mla_kv(ckv_c, 0, kpe_c, 0, mla_kv_norm_g[l], mla_k_norm_g[l], w_uk, w_uv,
                              cos, sin, lambda i: 0, tc, False, False, "mla_kv_cache")
        tq_c = _pow2_tile(256, s_len)
        o_mla_c = _flash(q_mla, k_mla, v_mla, batch=n_ctx, heads=MLA_HEADS, t_len=s_len, s_len=s_len, row0=0,
                         dq=MLA_QPAD, dv=MLA_V, tq=tq_c, tk=tq_c, name="mla_attn_ctx")
        tq_l = _pow2_tile(512, t_len, p_len)
        o_mla_l = _flash(q_mla, k_mla, v_mla, batch=n_lat, heads=MLA_HEADS, t_len=t_len, s_len=t_len, row0=m_ctx,
                         dq=MLA_QPAD, dv=MLA_V, tq=tq_l, tk=tq_l, extra=(k_c, v_c, p_len), name="mla_attn_latent")
        o_mla = jnp.concatenate([o_mla_c, o_mla_l], axis=0)

        lb = lower[:, l]
        lb_tabs = (jnp.log(lb), jnp.log1p(-lb), 1.0 - lb)
        o_hg_c, s_hg = _hgrn(z, lay, lb_tabs, hgrn_norm_g[l], hg_tabs, batch=n_ctx, t_len=s_len, row0=0,
                             s0=None, emit_state=True)
        o_hg_l, _ = _hgrn(z, lay, lb_tabs, hgrn_norm_g[l], hg_tabs, batch=n_lat, t_len=t_len, row0=m_ctx,
                          s0=state_hgrn[:, l], emit_state=False)
        o_hg = jnp.concatenate([o_hg_c, o_hg_l], axis=0)

        q_na, k_na, v_na, k_na_f = _na_prep(z, lay, na_q_norm_g[l], na_k_norm_g[l], tp)
        o_na_c = _flash(q_na, k_na, v_na, batch=n_ctx, heads=NA_HEADS, t_len=s_len, s_len=s_len, row0=0,
                        dq=NA_DH, dv=NA_DH, tq=tq_c, tk=tq_c, name="na_attn_ctx")
        kc_na = cache_na_k[:, l].reshape(n_lat * p_len, NA_HEADS * NA_DH).astype(bf16)
        vc_na = cache_na_v[:, l].reshape(n_lat * p_len, NA_HEADS * NA_DH).astype(bf16)
        bias = _na_bias_table(na_rpb[l], t_len // GRID_W)
        o_na_l = _na_latent(q_na, k_na, v_na, kc_na, vc_na, bias, batch=n_lat, t_len=t_len, p_len=p_len, row0=m_ctx)
        o_na = jnp.concatenate([o_na_c, o_na_l], axis=0)

        merged = _merge(o_mla, o_hg, o_na, z, lay, w_branch[l].astype(bf16), tp, tn)
        x = _proj_residual(merged, w_out[l].astype(bf16), x, g1, group_of_tile, tm, tn)

        ag = _normmod_matmul(x, norm2_g[l], sc2, sh2, ffn_w_up[l].astype(bf16), group_of_tile, tm,
                             _pow2_tile(1024, 2 * ffn), "ffn_up")
        tmd = _pow2_tile(1024, m_ctx, t_len)
        x = _ffn_down(ag, ffn_conv_w[l], ffn_conv_b[l], ffn_w_down[l].astype(bf16), x, g2,
                      lambda i: jnp.where(i < m_ctx // tmd, 0, 1 + (i - m_ctx // tmd) // (t_len // tmd)),
                      tmd, _pow2_tile(512, ffn), m_ctx, s_len, t_len)

        st_ckv.append(ckv_n[:m_ctx].reshape(n_ctx, s_len, MLA_KV_RANK))
        st_kpe.append(z[:m_ctx, lay.kpe:lay.kpe + MLA_ROPE].reshape(n_ctx, s_len, MLA_ROPE))
        st_nk.append(k_na_f[:m_ctx].reshape(n_ctx, s_len, NA_HEADS, NA_DH))
        st_nv.append(z[:m_ctx, lay.nv:lay.nv + NA_HEADS * NA_DH].reshape(n_ctx, s_len, NA_HEADS, NA_DH))
        st_hg.append(s_hg)

    y_prompt = x[:m_ctx].reshape(n_ctx, s_len, d)
    y_sample = x[m_ctx:].reshape(n_lat, t_len, d)
    return (y_prompt, y_sample, jnp.stack(st_ckv, axis=1), jnp.stack(st_kpe, axis=1), jnp.stack(st_nk, axis=1),
            jnp.stack(st_nv, axis=1), jnp.stack(st_hg, axis=1))
```

```python
import functools
import math

import numpy as np
import jax
import jax.numpy as jnp
from jax import lax
from jax.experimental import pallas as pl
from jax.experimental.pallas import tpu as pltpu

f32 = jnp.float32
bf16 = jnp.bfloat16

GRID_W = 64
MLA_HEADS, MLA_NOPE, MLA_ROPE, MLA_V, MLA_KV_RANK = 8, 128, 64, 128, 512
MLA_QK = MLA_NOPE + MLA_ROPE
HG_HEADS, HG_K, HG_V = 8, 128, 128
NA_HEADS, NA_DH, NA_WIN_R, NA_WIN_C = 8, 128, 8, 16
ROPE_BASE = 10000.0
EPS = 1e-6
NEG = -1e30
LOG2E = 1.4426950408889634

LANES = 128
SUBLANES = 8
VMEM_LIMIT_BYTES = 56 * 1024 * 1024

HEAD_W = 128
MLA_QPAD = 256
HG_CHUNK = 128
HG_LEVELS = int(math.log2(HG_CHUNK))
NA_QROWS = 8
NA_KROWS = 16


def _params(semantics):
    return pltpu.CompilerParams(dimension_semantics=semantics, vmem_limit_bytes=VMEM_LIMIT_BYTES)


def _pow2_tile(pref, *dims):
    t = pref
    while any(d % t for d in dims):
        t //= 2
    assert t >= SUBLANES, (pref, dims)
    return t


def _dot(a, b):
    return jnp.dot(a, b, preferred_element_type=f32)


def _dot_nt(a, b):
    return lax.dot_general(a, b, (((1,), (1,)), ((), ())), preferred_element_type=f32)


def _dot_tn(a, b):
    return lax.dot_general(a, b, (((0,), (0,)), ((), ())), preferred_element_type=f32)


def _silu(x):
    return x * jax.nn.sigmoid(x)


class _ZLayout:
    def __init__(self, d_model, tn):
        self.qn = 0
        self.qp = self.qn + MLA_HEADS * MLA_NOPE
        self.ckv = self.qp + MLA_HEADS * MLA_ROPE
        self.hq = self.ckv + MLA_KV_RANK
        self.hff = self.hq + HG_HEADS * HG_K
        self.hfb = self.hff + HG_HEADS * HG_K
        self.hi = self.hfb + HG_HEADS * HG_K
        self.hg = self.hi + HG_HEADS * HG_V
        self.nq = self.hg + HG_HEADS * HG_V
        self.nk = self.nq + NA_HEADS * NA_DH
        self.nv = self.nk + NA_HEADS * NA_DH
        self.ga = self.nv + NA_HEADS * NA_DH
        self.gb = self.ga + d_model
        self.gc = self.gb + d_model
        self.kpe = self.gc + d_model
        used = self.kpe + MLA_ROPE
        self.width = -(-used // tn) * tn


def _permute_w_in(w, lay):
    d = w.shape[0]
    n_mq = MLA_HEADS * MLA_QK
    mq = w[:, :n_mq].reshape(d, MLA_HEADS, MLA_QK)
    qn = mq[:, :, :MLA_NOPE].reshape(d, MLA_HEADS * MLA_NOPE)
    qp = mq[:, :, MLA_NOPE:].reshape(d, MLA_HEADS * MLA_ROPE)
    ckv = w[:, n_mq:n_mq + MLA_KV_RANK]
    kpe = w[:, n_mq + MLA_KV_RANK:n_mq + MLA_KV_RANK + MLA_ROPE]
    rest = w[:, n_mq + MLA_KV_RANK + MLA_ROPE:]
    pad = jnp.zeros((d, lay.width - lay.kpe - MLA_ROPE), w.dtype)
    return jnp.concatenate([qn, qp, ckv, rest, kpe, pad], axis=1).astype(bf16)


def _mod_kernel(c_ref, w_ref, b_ref, o_ref):
    a = _silu(c_ref[...]).astype(bf16)
    o_ref[...] = _dot(a, w_ref[...].astype(bf16)) + b_ref[...]


def _modulation(cond, w_mod, b_mod):
    depth, d, n = w_mod.shape
    g = cond.shape[0]
    tn = _pow2_tile(1024, n)
    return pl.pallas_call(
        _mod_kernel,
        grid=(depth, n // tn),
        in_specs=[
            pl.BlockSpec((g, d), lambda l, j: (0, 0)),
            pl.BlockSpec((None, d, tn), lambda l, j: (l, 0, j)),
            pl.BlockSpec((None, 1, tn), lambda l, j: (l, 0, j)),
        ],
        out_specs=pl.BlockSpec((None, g, tn), lambda l, j: (l, 0, j)),
        out_shape=jax.ShapeDtypeStruct((depth, g, n), f32),
        compiler_params=_params(("arbitrary", "arbitrary")),
        name="modulation",
    )(cond, w_mod, b_mod.reshape(depth, 1, n))


def _normmod_matmul_kernel(x_ref, g_ref, sc_ref, sh_ref, w_ref, o_ref, h_scr):
    @pl.when(pl.program_id(1) == 0)
    def _():
        x = x_ref[...]
        ms = jnp.mean(x * x, axis=-1, keepdims=True)
        y = x * lax.rsqrt(ms + EPS) * g_ref[...]
        h_scr[...] = (y * (1.0 + sc_ref[...]) + sh_ref[...]).astype(bf16)

    o_ref[...] = _dot(h_scr[...], w_ref[...])


def _normmod_matmul(x, g, sc, sh, w, group_of_tile, tm, tn, name):
    m, d = x.shape
    n = w.shape[1]
    return pl.pallas_call(
        _normmod_matmul_kernel,
        grid=(m // tm, n // tn),
        in_specs=[
            pl.BlockSpec((tm, d), lambda i, j: (i, 0)),
            pl.BlockSpec((1, d), lambda i, j: (0, 0)),
            pl.BlockSpec((None, 1, d), lambda i, j: (group_of_tile(i), 0, 0)),
            pl.BlockSpec((None, 1, d), lambda i, j: (group_of_tile(i), 0, 0)),
            pl.BlockSpec((d, tn), lambda i, j: (0, j)),
        ],
        out_specs=pl.BlockSpec((tm, tn), lambda i, j: (i, j)),
        out_shape=jax.ShapeDtypeStruct((m, n), f32),
        scratch_shapes=[pltpu.VMEM((tm, d), bf16)],
        compiler_params=_params(("arbitrary", "arbitrary")),
        name=name,
    )(x, g.reshape(1, d), sc, sh, w)


def _rope(x, cos, sin_signed):
    n = x.shape[-1]
    half = MLA_ROPE // 4
    lane = lax.broadcasted_iota(jnp.int32, x.shape, 1)
    partner = jnp.where((lane % (2 * half)) < half, pltpu.roll(x, n - half, 1), pltpu.roll(x, half, 1))
    return x * cos + partner * sin_signed


def _mla_q_kernel(qn_ref, qp_ref, gn_ref, gp_ref, cos_ref, sin_ref, o_ref, *, scale):
    qn = qn_ref[...]
    qp = qp_ref[...]
    tm = qn.shape[0]
    lane = lax.broadcasted_iota(jnp.int32, (tm, LANES), 1)
    low = lane < MLA_ROPE
    cos = cos_ref[...]
    sin = sin_ref[...]
    gn = gn_ref[...]
    gp = gp_ref[...]
    for pair in range(MLA_HEADS // 2):
        pcol = qp[:, pair * LANES:(pair + 1) * LANES]
        sq = pcol * pcol
        ss_all = jnp.sum(sq, axis=-1, keepdims=True)
        ss_low = jnp.sum(jnp.where(low, sq, 0.0), axis=-1, keepdims=True)
        rot = _rope(pcol * gp, cos, sin)
        for k, ss_pe in enumerate((ss_low, ss_all - ss_low)):
            h = 2 * pair + k
            nope = qn[:, h * MLA_NOPE:(h + 1) * MLA_NOPE]
            ss = jnp.sum(nope * nope, axis=-1, keepdims=True) + ss_pe
            r = lax.rsqrt(ss * (1.0 / MLA_QK) + EPS) * scale
            pe = rot if k == 0 else pltpu.roll(rot, MLA_ROPE, 1)
            o_ref[:, h * MLA_QPAD:h * MLA_QPAD + MLA_NOPE] = (nope * r * gn).astype(bf16)
            o_ref[:, h * MLA_QPAD + MLA_NOPE:(h + 1) * MLA_QPAD] = jnp.where(low, pe * r, 0.0).astype(bf16)


def _mla_kv_kernel(ckv_ref, kpe_ref, kvg_ref, gn_ref, gp_ref, wuk_ref, wuv_ref, cos_ref, sin_ref,
                   k_ref, v_ref, ckv_out_ref, *, norm_ckv, rope):
    ckv = ckv_ref[...]
    if norm_ckv:
        ms = jnp.mean(ckv * ckv, axis=-1, keepdims=True)
        ckv = ckv * lax.rsqrt(ms + EPS) * kvg_ref[...]
    ckv_out_ref[...] = ckv
    cb = ckv.astype(bf16)
    kn = _dot(cb, wuk_ref[...])
    v_ref[...] = _dot(cb, wuv_ref[...]).astype(bf16)
    tm = ckv.shape[0]
    lane = lax.broadcasted_iota(jnp.int32, (tm, LANES), 1)
    low = lane < MLA_ROPE
    kpe = jnp.where(low, kpe_ref[...], 0.0)
    ss_pe = jnp.sum(kpe * kpe, axis=-1, keepdims=True)
    pe = kpe * gp_ref[...]
    if rope:
        pe = _rope(pe, cos_ref[...], sin_ref[...])
    gn = gn_ref[...]
    for h in range(MLA_HEADS):
        nope = kn[:, h * MLA_NOPE:(h + 1) * MLA_NOPE]
        ss = jnp.sum(nope * nope, axis=-1, keepdims=True) + ss_pe
        r = lax.rsqrt(ss * (1.0 / MLA_QK) + EPS)
        k_ref[:, h * MLA_QPAD:h * MLA_QPAD + MLA_NOPE] = (nope * r * gn).astype(bf16)
        k_ref[:, h * MLA_QPAD + MLA_NOPE:(h + 1) * MLA_QPAD] = jnp.where(low, pe * r, 0.0).astype(bf16)


def _rope_tables(t_len, tm):
    n_freq = MLA_ROPE // 4
    t = jnp.arange(t_len, dtype=jnp.int32)
    inv = ROPE_BASE ** (-jnp.arange(n_freq, dtype=f32) / n_freq)
    ang_r = (t // GRID_W).astype(f32)[:, None] * inv
    ang_c = (t % GRID_W).astype(f32)[:, None] * inv
    cos64 = jnp.concatenate([jnp.cos(ang_r), jnp.cos(ang_r), jnp.cos(ang_c), jnp.cos(ang_c)], axis=1)
    sin64 = jnp.concatenate([-jnp.sin(ang_r), jnp.sin(ang_r), -jnp.sin(ang_c), jnp.sin(ang_c)], axis=1)
    cos = jnp.concatenate([jnp.ones((tm, LANES), f32), jnp.tile(cos64, (1, 2))], axis=0)
    sin = jnp.concatenate([jnp.zeros((tm, LANES), f32), jnp.tile(sin64, (1, 2))], axis=0)
    return cos, sin


def _mla_q(z, lay, q_g, cos, sin, rope_block, tm):
    m = z.shape[0]
    gn = q_g[:MLA_NOPE].reshape(1, MLA_NOPE)
    gp = jnp.tile(q_g[MLA_NOPE:], 2).reshape(1, LANES)
    wn, wp = MLA_HEADS * MLA_NOPE, MLA_HEADS * MLA_ROPE
    return pl.pallas_call(
        functools.partial(_mla_q_kernel, scale=MLA_QK ** -0.5 * LOG2E),
        grid=(m // tm,),
        in_specs=[
            pl.BlockSpec((tm, wn), lambda i: (i, lay.qn // wn)),
            pl.BlockSpec((tm, wp), lambda i: (i, lay.qp // wp)),
            pl.BlockSpec((1, MLA_NOPE), lambda i: (0, 0)),
            pl.BlockSpec((1, LANES), lambda i: (0, 0)),
            pl.BlockSpec((tm, LANES), lambda i: (rope_block(i), 0)),
            pl.BlockSpec((tm, LANES), lambda i: (rope_block(i), 0)),
        ],
        out_specs=pl.BlockSpec((tm, MLA_HEADS * MLA_QPAD), lambda i: (i, 0)),
        out_shape=jax.ShapeDtypeStruct((m, MLA_HEADS * MLA_QPAD), bf16),
        compiler_params=_params(("arbitrary",)),
        name="mla_q",
    )(z, z, gn, gp, cos, sin)


def _mla_kv(ckv_src, ckv_col, kpe_src, kpe_col, kv_g, k_g, w_uk, w_uv, cos, sin, rope_block, tm, norm_ckv, rope, name):
    m = ckv_src.shape[0]
    gn = k_g[:MLA_NOPE].reshape(1, MLA_NOPE)
    gp = jnp.concatenate([k_g[MLA_NOPE:], jnp.zeros((LANES - MLA_ROPE,), f32)]).reshape(1, LANES)
    hw = MLA_HEADS * MLA_NOPE
    return pl.pallas_call(
        functools.partial(_mla_kv_kernel, norm_ckv=norm_ckv, rope=rope),
        grid=(m // tm,),
        in_specs=[
            pl.BlockSpec((tm, MLA_KV_RANK), lambda i: (i, ckv_col // MLA_KV_RANK)),
            pl.BlockSpec((tm, LANES), lambda i: (i, kpe_col // LANES)),
            pl.BlockSpec((1, MLA_KV_RANK), lambda i: (0, 0)),
            pl.BlockSpec((1, MLA_NOPE), lambda i: (0, 0)),
            pl.BlockSpec((1, LANES), lambda i: (0, 0)),
            pl.BlockSpec((MLA_KV_RANK, hw), lambda i: (0, 0)),
            pl.BlockSpec((MLA_KV_RANK, hw), lambda i: (0, 0)),
            pl.BlockSpec((tm, LANES), lambda i: (rope_block(i), 0)),
            pl.BlockSpec((tm, LANES), lambda i: (rope_block(i), 0)),
        ],
        out_specs=[
            pl.BlockSpec((tm, MLA_HEADS * MLA_QPAD), lambda i: (i, 0)),
            pl.BlockSpec((tm, hw), lambda i: (i, 0)),
            pl.BlockSpec((tm, MLA_KV_RANK), lambda i: (i, 0)),
        ],
        out_shape=[
            jax.ShapeDtypeStruct((m, MLA_HEADS * MLA_QPAD), bf16),
            jax.ShapeDtypeStruct((m, hw), bf16),
            jax.ShapeDtypeStruct((m, MLA_KV_RANK), f32),
        ],
        compiler_params=_params(("arbitrary",)),
        name=name,
    )(ckv_src, kpe_src, kv_g.reshape(1, MLA_KV_RANK), gn, gp, w_uk, w_uv, cos, sin)


def _flash_kernel(*refs, n_main, tk, n_extra, heads, dq, dv):
    if n_extra:
        q_ref, k_ref, v_ref, kx_ref, vx_ref, o_ref = refs
    else:
        q_ref, k_ref, v_ref, o_ref = refs
        kx_ref = vx_ref = None
    tq = q_ref.shape[0]
    chunks = [(k_ref, v_ref, c) for c in range(n_main)] + [(kx_ref, vx_ref, c) for c in range(n_extra)]
    for g in range(heads):
        q = q_ref[:, g * dq:(g + 1) * dq]

        def scores(i):
            kr, _, c = chunks[i]
            return _dot_nt(kr[c * tk:(c + 1) * tk, g * dq:(g + 1) * dq], q)

        m = jnp.full((1, tq), -jnp.inf, f32)
        l = jnp.zeros((1, tq), f32)
        acc = jnp.zeros((dv, tq), f32)
        s = scores(0)
        for i, (_, vr, c) in enumerate(chunks):
            s_next = scores(i + 1) if i + 1 < len(chunks) else None
            m_new = jnp.maximum(m, jnp.max(s, axis=0, keepdims=True))
            alpha = jnp.exp2(m - m_new)
            p = jnp.exp2(s - m_new)
            l = alpha * l + jnp.sum(p, axis=0, keepdims=True)
            acc = alpha * acc + _dot_tn(vr[c * tk:(c + 1) * tk, g * dv:(g + 1) * dv], p.astype(bf16))
            m, s = m_new, s_next
        o_ref[:, g * dv:(g + 1) * dv] = (acc / l).T.astype(o_ref.dtype)


def _flash(q, k, v, *, batch, heads, t_len, s_len, row0, dq, dv, tq, tk, heads_per_step=1, extra=None, name):
    assert row0 % tq == 0 and row0 % s_len == 0 and t_len % tq == 0 and s_len % tk == 0
    assert heads % heads_per_step == 0
    nq = t_len // tq
    g = heads_per_step
    in_specs = [
        pl.BlockSpec((tq, g * dq), lambda b, h, i: (row0 // tq + b * nq + i, h)),
        pl.BlockSpec((s_len, g * dq), lambda b, h, i: (row0 // s_len + b, h)),
        pl.BlockSpec((s_len, g * dv), lambda b, h, i: (row0 // s_len + b, h)),
    ]
    args = [q, k, v]
    n_extra = 0
    if extra is not None:
        kx, vx, p_len = extra
        assert p_len % tk == 0
        n_extra = p_len // tk
        in_specs += [
            pl.BlockSpec((p_len, g * dq), lambda b, h, i: (b, h)),
            pl.BlockSpec((p_len, g * dv), lambda b, h, i: (b, h)),
        ]
        args += [kx, vx]
    return pl.pallas_call(
        functools.partial(_flash_kernel, n_main=s_len // tk, tk=tk, n_extra=n_extra, heads=g, dq=dq, dv=dv),
        grid=(batch, heads // g, nq),
        in_specs=in_specs,
        out_specs=pl.BlockSpec((tq, g * dv), lambda b, h, i: (b * nq + i, h)),
        out_shape=jax.ShapeDtypeStruct((batch * t_len, heads * dv), bf16),
        compiler_params=_params(("arbitrary", "arbitrary", "arbitrary")),
        name=name,
    )(*args)


def _na_prep_kernel(nq_ref, nk_ref, nv_ref, qg_ref, kg_ref, q_ref, k_ref, v_ref, kf_ref, *, scale):
    qg = qg_ref[...]
    kg = kg_ref[...]
    v_ref[...] = nv_ref[...].astype(bf16)
    for h in range(NA_HEADS):
        sl = slice(h * NA_DH, (h + 1) * NA_DH)
        q = nq_ref[:, sl]
        k = nk_ref[:, sl]
        rq = lax.rsqrt(jnp.mean(q * q, axis=-1, keepdims=True) + EPS)
        rk = lax.rsqrt(jnp.mean(k * k, axis=-1, keepdims=True) + EPS)
        q_ref[:, sl] = (q * rq * qg * scale).astype(bf16)
        kn = k * rk * kg
        kf_ref[:, sl] = kn
        k_ref[:, sl] = kn.astype(bf16)


def _na_prep(z, lay, q_g, k_g, tm):
    m = z.shape[0]
    w = NA_HEADS * NA_DH
    spec = pl.BlockSpec((tm, w), lambda i: (i, 0))
    return pl.pallas_call(
        functools.partial(_na_prep_kernel, scale=NA_DH ** -0.5 * LOG2E),
        grid=(m // tm,),
        in_specs=[
            pl.BlockSpec((tm, w), lambda i: (i, lay.nq // w)),
            pl.BlockSpec((tm, w), lambda i: (i, lay.nk // w)),
            pl.BlockSpec((tm, w), lambda i: (i, lay.nv // w)),
            pl.BlockSpec((1, NA_DH), lambda i: (0, 0)),
            pl.BlockSpec((1, NA_DH), lambda i: (0, 0)),
        ],
        out_specs=[spec, spec, spec, spec],
        out_shape=[jax.ShapeDtypeStruct((m, w), bf16)] * 3 + [jax.ShapeDtypeStruct((m, w), f32)],
        compiler_params=_params(("arbitrary",)),
        name="na_prep",
    )(z, z, z, q_g.reshape(1, NA_DH), k_g.reshape(1, NA_DH))


def _na_static_maps(rows):
    nblk = rows // NA_QROWS
    reps = (0, min(1, nblk - 1), nblk - 1)
    dr_map = np.full((3, NA_QROWS, NA_KROWS), 2 * NA_WIN_R - 1, np.int32)
    for v, kb in enumerate(reps):
        ws = min(max(NA_QROWS * kb - NA_WIN_R // 2, 0), rows - NA_KROWS)
        for i in range(NA_QROWS):
            qrow = NA_QROWS * kb + i
            rs = min(max(qrow - NA_WIN_R // 2, 0), rows - NA_WIN_R)
            for j in range(NA_KROWS):
                krow = ws + j
                if rs <= krow < rs + NA_WIN_R:
                    dr_map[v, i, j] = krow - qrow + NA_WIN_R - 1
    qcol = np.arange(GRID_W)
    cs = np.clip(qcol - NA_WIN_C // 2, 0, GRID_W - NA_WIN_C)
    in_win = (qcol[None, :] >= cs[:, None]) & (qcol[None, :] < cs[:, None] + NA_WIN_C)
    dc_idx = np.clip(qcol[None, :] - qcol[:, None], -(NA_WIN_C - 1), NA_WIN_C - 1) + (NA_WIN_C - 1)
    return dr_map, in_win, dc_idx


def _na_bias_table(rpb, rows):
    dr_map, in_win, dc_idx = _na_static_maps(rows)
    h = rpb.shape[0]
    bt = jnp.where(in_win[None, None], jnp.take(rpb, jnp.asarray(dc_idx), axis=2), NEG)
    bt = jnp.concatenate([bt, jnp.full((h, 1, GRID_W, GRID_W), NEG, f32)], axis=1)
    tab = jnp.take(bt, jnp.asarray(dr_map.reshape(-1)), axis=1)
    tab = tab.reshape(h, 3, NA_QROWS, NA_KROWS, GRID_W, GRID_W).transpose(1, 0, 3, 5, 2, 4) * LOG2E
    return tab.reshape(3, h, NA_KROWS * GRID_W, NA_QROWS * GRID_W)


def _na_kernel(q_ref, k_ref, v_ref, kc_ref, vc_ref, bias_ref, o_ref, *, rows):
    kb = pl.program_id(2)
    ws = jnp.clip(NA_QROWS * kb - NA_WIN_R // 2, 0, rows - NA_KROWS)
    off = pl.multiple_of(ws * GRID_W, (NA_WIN_R // 2) * GRID_W)
    n_keys = NA_KROWS * GRID_W
    q = q_ref[...]
    s_loc = _dot_nt(k_ref[pl.ds(off, n_keys), :], q) + bias_ref[...]
    s_ctx = _dot_nt(kc_ref[...], q)
    m = jnp.maximum(jnp.max(s_loc, axis=0, keepdims=True), jnp.max(s_ctx, axis=0, keepdims=True))
    p_loc = jnp.exp2(s_loc - m)
    p_ctx = jnp.exp2(s_ctx - m)
    l = jnp.sum(p_loc, axis=0, keepdims=True) + jnp.sum(p_ctx, axis=0, keepdims=True)
    acc = _dot_tn(v_ref[pl.ds(off, n_keys), :], p_loc.astype(bf16)) + _dot_tn(vc_ref[...], p_ctx.astype(bf16))
    o_ref[...] = (acc / l).T.astype(o_ref.dtype)


def _na_latent(q, k, v, kc, vc, bias, *, batch, t_len, p_len, row0):
    rows = t_len // GRID_W
    assert t_len % GRID_W == 0 and rows % NA_QROWS == 0 and rows >= NA_KROWS
    tq = NA_QROWS * GRID_W
    assert row0 % tq == 0 and row0 % t_len == 0
    nblk = rows // NA_QROWS

    def variant(i):
        return jnp.where(i == 0, 0, jnp.where(i == nblk - 1, 2, 1))

    return pl.pallas_call(
        functools.partial(_na_kernel, rows=rows),
        grid=(batch, NA_HEADS, nblk),
        in_specs=[
            pl.BlockSpec((tq, NA_DH), lambda b, h, i: (row0 // tq + b * nblk + i, h)),
            pl.BlockSpec((t_len, NA_DH), lambda b, h, i: (row0 // t_len + b, h)),
            pl.BlockSpec((t_len, NA_DH), lambda b, h, i: (row0 // t_len + b, h)),
            pl.BlockSpec((p_len, NA_DH), lambda b, h, i: (b, h)),
            pl.BlockSpec((p_len, NA_DH), lambda b, h, i: (b, h)),
            pl.BlockSpec((None, None, NA_KROWS * GRID_W, tq), lambda b, h, i: (variant(i), h, 0, 0)),
        ],
        out_specs=pl.BlockSpec((tq, NA_DH), lambda b, h, i: (b * nblk + i, h)),
        out_shape=jax.ShapeDtypeStruct((batch * t_len, NA_HEADS * NA_DH), bf16),
        compiler_params=_params(("arbitrary", "arbitrary", "arbitrary")),
        name="na_latent",
    )(q, k, v, kc, vc, bias)


def _hgrn_tables():
    c, lv = HG_CHUNK, HG_LEVELS
    t = np.arange(c)[:, None]
    r = np.arange(c)[None, :]
    wcum = np.stack([r <= t, r >= t]).astype(np.float32)
    qside = np.zeros((2, lv * c, LANES), np.float32)
    mask = np.zeros((2, lv + 1, c, c), np.float32)
    mask[:, 0] = np.eye(c)
    for l in range(lv):
        hs = 1 << l
        blk = t // (2 * hs)
        mid = blk * (2 * hs) + hs
        qf = t >= mid
        qb = t < mid
        qside[0, l * c:(l + 1) * c] = np.where(qf, 1.0, -1.0)
        qside[1, l * c:(l + 1) * c] = np.where(qb, 1.0, -1.0)
        same = blk == blk.T
        mask[0, 1 + l] = same & qf & ~qf.T
        mask[1, 1 + l] = same & qb & ~qb.T
    return wcum, qside, mask


def _hgrn_kernel(*refs, n_chunks, heads, has_state, emit_state):
    (hq_ref, hff_ref, hfb_ref, hi_ref, hg_ref, la_ref, lc_ref, om_ref, ng_ref,
     wcum_ref, qside_ref, mask_ref) = refs[:12]
    rest = list(refs[12:])
    s0_ref = rest.pop(0) if has_state else None
    o_ref = rest.pop(0)
    st_ref = rest.pop(0) if emit_state else None
    o_scr, s_scr, b_scr = rest
    c, lv = HG_CHUNK, HG_LEVELS

    o_scr[...] = jnp.zeros_like(o_scr)
    for d in range(2):
        for g in range(heads):
            if has_state:
                s_scr[d, g] = s0_ref[d, g].T
            else:
                s_scr[d, g] = jnp.zeros((HG_V, HG_K), f32)

    chains = [(d, g) for g in range(heads) for d in range(2)]
    low_half = lax.broadcasted_iota(jnp.int32, (SUBLANES, HEAD_W), 0) < SUBLANES // 2

    def gates_and_cumsum(ci, row):
        d, g = chains[ci]
        rs = pl.ds(pl.multiple_of(row, c), c)
        cs = slice(g * HEAD_W, (g + 1) * HEAD_W)
        x = (hff_ref if d == 0 else hfb_ref)[rs, cs]
        sig = jax.nn.sigmoid(x)
        log_sig = jnp.minimum(x, 0.0) - jnp.log(1.0 + jnp.exp(-jnp.abs(x)))
        a = la_ref[d:d + 1, cs]
        bb = lc_ref[d:d + 1, cs] + log_sig
        lf = jnp.maximum(a, bb) + jnp.log(1.0 + jnp.exp(-jnp.abs(a - bb)))
        om = om_ref[d:d + 1, cs]
        f = (1.0 - om) + om * sig
        kin = om * (1.0 - sig)
        q = _silu(hq_ref[rs, cs])
        v = hi_ref[rs, cs].astype(bf16)
        lf = lf * LOG2E
        l1 = lf.astype(bf16)
        r1 = lf - l1.astype(f32)
        l2 = r1.astype(bf16)
        l3 = (r1 - l2.astype(f32)).astype(bf16)
        w = wcum_ref[d]
        b = _dot(w, l1) + _dot(w, l2) + _dot(w, l3)
        b_scr[ci] = b
        return dict(rs=rs, cs=cs, f=f, kin=kin, q=q, v=v, b=b)

    def intra_chunk(ci, s):
        d, _ = chains[ci]
        b, q, kin = s["b"], s["q"], s["kin"]

        def row(r):
            return b_scr[ci, r:r + 1, :]

        s["b_end"] = row(c - 1 if d == 0 else 0)
        xs = [jnp.where(qside_ref[d, 0:c, :] > 0.0, q * s["f"], kin).astype(bf16)]
        for l in range(1, lv):
            hs = 1 << l
            blk = 2 * hs
            pick = hs - 1 if d == 0 else hs
            if blk < SUBLANES:
                pieces = [jnp.where(low_half, jnp.broadcast_to(row(SUBLANES * j + pick), (SUBLANES, HEAD_W)),
                                    jnp.broadcast_to(row(SUBLANES * j + blk + pick), (SUBLANES, HEAD_W)))
                          for j in range(c // SUBLANES)]
            else:
                pieces = [jnp.broadcast_to(row(blk * j + pick), (blk, HEAD_W)) for j in range(c // blk)]
            m = pieces[0] if len(pieces) == 1 else jnp.concatenate(pieces, axis=0)
            sign = qside_ref[d, l * c:(l + 1) * c, :]
            xs.append((jnp.where(sign > 0.0, q, kin) * jnp.exp2((b - m) * sign)).astype(bf16))
        att = mask_ref[d, 0] * _dot_nt(q.astype(bf16), kin.astype(bf16)).astype(bf16)
        for l in range(lv):
            att = att + mask_ref[d, 1 + l] * _dot_nt(xs[l], xs[l]).astype(bf16)
        s["att"] = att

    def state_step(ci, s):
        d, g = chains[ci]
        b, q, kin, v = s["b"], s["q"], s["kin"], s["v"]
        st = s_scr[d, g]
        o = _dot(s["att"], v) + _dot_nt((q * jnp.exp2(b)).astype(bf16), st.astype(bf16))
        b_end = s["b_end"]
        s_scr[d, g] = st * jnp.exp2(b_end) + _dot_tn(v, (kin * jnp.exp2(b_end - b)).astype(bf16))
        o_scr[s["rs"], s["cs"]] += o

    def body(i, _):
        rows = (i * c, (n_chunks - 1 - i) * c)
        states = [gates_and_cumsum(ci, rows[chains[ci][0]]) for ci in range(len(chains))]
        for ci, s in enumerate(states):
            intra_chunk(ci, s)
        for ci, s in enumerate(states):
            state_step(ci, s)
        return 0

    lax.fori_loop(0, n_chunks, body, 0)

    ng = ng_ref[...]
    for g in range(heads):
        cs = slice(g * HEAD_W, (g + 1) * HEAD_W)
        o = o_scr[:, cs]
        y = o * lax.rsqrt(jnp.mean(o * o, axis=-1, keepdims=True) + EPS) * ng
        o_ref[:, cs] = (y * _silu(hg_ref[:, cs])).astype(o_ref.dtype)
        if emit_state:
            for d in range(2):
                st_ref[d, g] = s_scr[d, g].T


def _hgrn(z, lay, lb_tabs, norm_g, tables, *, batch, t_len, row0, s0, emit_state, heads_per_step=2):
    assert t_len % HG_CHUNK == 0 and row0 % t_len == 0 and HG_K == HEAD_W and HG_V == HEAD_W
    g = heads_per_step
    gw = g * HEAD_W
    la, lc, om = lb_tabs
    wcum, qside, mask = tables
    r0 = row0 // t_len

    def zspec(col):
        return pl.BlockSpec((t_len, gw), lambda b, h: (r0 + b, col // gw + h))

    def const(shape):
        return pl.BlockSpec(shape, lambda b, h: (0,) * len(shape))

    in_specs = [zspec(lay.hq), zspec(lay.hff), zspec(lay.hfb), zspec(lay.hi), zspec(lay.hg),
                pl.BlockSpec((2, gw), lambda b, h: (0, h)), pl.BlockSpec((2, gw), lambda b, h: (0, h)),
                pl.BlockSpec((2, gw), lambda b, h: (0, h)), const((1, HG_V)),
                const(wcum.shape), const(qside.shape), const(mask.shape)]
    args = [z, z, z, z, z, la, lc, om, norm_g.reshape(1, HG_V), wcum, qside, mask]
    state_spec = pl.BlockSpec((None, 2, g, HG_K, HG_V), lambda b, h: (b, 0, h, 0, 0))
    if s0 is not None:
        in_specs.append(state_spec)
        args.append(s0)
    out_specs = [pl.BlockSpec((t_len, gw), lambda b, h: (b, h))]
    out_shape = [jax.ShapeDtypeStruct((batch * t_len, HG_HEADS * HG_V), bf16)]
    if emit_state:
        out_specs.append(state_spec)
        out_shape.append(jax.ShapeDtypeStruct((batch, 2, HG_HEADS, HG_K, HG_V), f32))
    res = pl.pallas_call(
        functools.partial(_hgrn_kernel, n_chunks=t_len // HG_CHUNK, heads=g, has_state=s0 is not None,
                          emit_state=emit_state),
        grid=(batch, HG_HEADS // g),
        in_specs=in_specs,
        out_specs=out_specs,
        out_shape=out_shape,
        scratch_shapes=[pltpu.VMEM((t_len, gw), f32), pltpu.VMEM((2, g, HG_V, HG_K), f32),
                        pltpu.VMEM((2 * g, HG_CHUNK, HEAD_W), f32)],
        compiler_params=_params(("arbitrary", "arbitrary")),
        name="hgrn_ctx" if emit_state else "hgrn_latent",
    )(*args)
    return res if emit_state else (res[0], None)


def _merge_kernel(oac_ref, oal_ref, obc_ref, obl_ref, occ_ref, ocl_ref, ga_ref, gb_ref, gc_ref, w_ref, o_ref,
                  *, n_ctx_tiles):
    def compute(oa_ref, ob_ref, oc_ref):
        m = jax.nn.sigmoid(ga_ref[...]) * _dot(oa_ref[...], w_ref[0])
        m = m + jax.nn.sigmoid(gb_ref[...]) * _dot(ob_ref[...], w_ref[1])
        m = m + jax.nn.sigmoid(gc_ref[...]) * _dot(oc_ref[...], w_ref[2])
        o_ref[...] = m.astype(o_ref.dtype)

    is_ctx = pl.program_id(0) < n_ctx_tiles

    @pl.when(is_ctx)
    def _():
        compute(oac_ref, obc_ref, occ_ref)

    @pl.when(jnp.logical_not(is_ctx))
    def _():
        compute(oal_ref, obl_ref, ocl_ref)


def _merge(o_a, o_b, o_c, z, lay, w_branch, tm, tn):
    bw = o_a[0].shape[1]
    m = z.shape[0]
    d = w_branch.shape[2]
    nct = o_a[0].shape[0] // tm
    cspec = pl.BlockSpec((tm, bw), lambda i, j: (jnp.minimum(i, nct - 1), 0))
    lspec = pl.BlockSpec((tm, bw), lambda i, j: (jnp.maximum(i - nct, 0), 0))

    def gate(col):
        return pl.BlockSpec((tm, tn), lambda i, j: (i, col // tn + j))

    return pl.pallas_call(
        functools.partial(_merge_kernel, n_ctx_tiles=nct),
        grid=(m // tm, d // tn),
        in_specs=[cspec, lspec, cspec, lspec, cspec, lspec, gate(lay.ga), gate(lay.gb), gate(lay.gc),
                  pl.BlockSpec((3, bw, tn), lambda i, j: (0, 0, j))],
        out_specs=pl.BlockSpec((tm, tn), lambda i, j: (i, j)),
        out_shape=jax.ShapeDtypeStruct((m, d), bf16),
        compiler_params=_params(("arbitrary", "arbitrary")),
        name="merge",
    )(o_a[0], o_a[1], o_b[0], o_b[1], o_c[0], o_c[1], z, z, z, w_branch)


def _proj_residual_kernel(m_ref, w_ref, x_ref, g_ref, o_ref):
    o_ref[...] = x_ref[...] + g_ref[...] * _dot(m_ref[...], w_ref[...])


def _proj_residual(mm, w, x, gate, group_of_tile, tm, tn):
    m, k = mm.shape
    d = w.shape[1]
    return pl.pallas_call(
        _proj_residual_kernel,
        grid=(m // tm, d // tn),
        in_specs=[
            pl.BlockSpec((tm, k), lambda i, j: (i, 0)),
            pl.BlockSpec((k, tn), lambda i, j: (0, j)),
            pl.BlockSpec((tm, tn), lambda i, j: (i, j)),
            pl.BlockSpec((None, 1, tn), lambda i, j: (group_of_tile(i), 0, j)),
        ],
        out_specs=pl.BlockSpec((tm, tn), lambda i, j: (i, j)),
        out_shape=jax.ShapeDtypeStruct((m, d), f32),
        compiler_params=_params(("arbitrary", "arbitrary")),
        name="out_proj",
    )(mm, w, x, gate)


def _ffn_down_kernel(a_ref, g_ref, gprev_ref, gnext_ref, cw_ref, cb_ref, w_ref, x_ref, gate_ref, o_ref,
                     *, m_ctx, s_len, t_len):
    i = pl.program_id(0)
    k = pl.program_id(1)
    g = g_ref[...]
    tm = g.shape[0]
    row = i * tm + lax.broadcasted_iota(jnp.int32, (tm, 1), 0)
    pos = jnp.where(row < m_ctx, row % s_len, (row - m_ctx) % t_len)
    last = jnp.where(row < m_ctx, s_len - 1, t_len - 1)
    local = lax.broadcasted_iota(jnp.int32, (tm, 1), 0)
    g_prev = jnp.where(local == 0, gprev_ref[SUBLANES - 1:SUBLANES, :], pltpu.roll(g, 1, 0))
    g_prev = jnp.where(pos == 0, 0.0, g_prev)
    g_next = jnp.where(local == tm - 1, gnext_ref[0:1, :], pltpu.roll(g, tm - 1, 0))
    g_next = jnp.where(pos == last, 0.0, g_next)
    conv = g_prev * cw_ref[0:1, :] + g * cw_ref[1:2, :] + g_next * cw_ref[2:3, :] + cb_ref[...]
    u = (_silu(conv) * a_ref[...]).astype(bf16)
    part = _dot(u, w_ref[...])

    @pl.when(k == 0)
    def _():
        o_ref[...] = part

    @pl.when(k > 0)
    def _():
        o_ref[...] += part

    @pl.when(k == pl.num_programs(1) - 1)
    def _():
        o_ref[...] = x_ref[...] + gate_ref[...] * o_ref[...]


def _ffn_down(ag, conv_w, conv_b, w_down, x, gate, group_of_tile, tm, tk, m_ctx, s_len, t_len):
    m, d = x.shape
    f = w_down.shape[0]
    nk = f // tk
    nsub = tm // SUBLANES
    last_sub = m // SUBLANES - 1
    return pl.pallas_call(
        functools.partial(_ffn_down_kernel, m_ctx=m_ctx, s_len=s_len, t_len=t_len),
        grid=(m // tm, nk),
        in_specs=[
            pl.BlockSpec((tm, tk), lambda i, k: (i, k)),
            pl.BlockSpec((tm, tk), lambda i, k: (i, nk + k)),
            pl.BlockSpec((SUBLANES, tk), lambda i, k: (jnp.maximum(i * nsub - 1, 0), nk + k)),
            pl.BlockSpec((SUBLANES, tk), lambda i, k: (jnp.minimum((i + 1) * nsub, last_sub), nk + k)),
            pl.BlockSpec((3, tk), lambda i, k: (0, k)),
            pl.BlockSpec((1, tk), lambda i, k: (0, k)),
            pl.BlockSpec((tk, d), lambda i, k: (k, 0)),
            pl.BlockSpec((tm, d), lambda i, k: (i, 0)),
            pl.BlockSpec((None, 1, d), lambda i, k: (group_of_tile(i), 0, 0)),
        ],
        out_specs=pl.BlockSpec((tm, d), lambda i, k: (i, 0)),
        out_shape=jax.ShapeDtypeStruct((m, d), f32),
        compiler_params=_params(("arbitrary", "arbitrary")),
        name="ffn_down",
    )(ag, ag, ag, ag, conv_w, conv_b.reshape(1, f), w_down, x, gate)


def kernel(x_prompt, x_sample, cache_mla_ckv, cache_mla_kpe, cache_na_k, cache_na_v, state_hgrn, c, c_ctx, w_mod, b_mod, norm1_g, norm2_g, w_in, mla_kv_norm_g, mla_q_norm_g, mla_k_norm_g, mla_w_uk, mla_w_uv, hgrn_lower_bounds, hgrn_norm_g, na_q_norm_g, na_k_norm_g, na_rpb, w_branch, w_out, ffn_w_up, ffn_conv_w, ffn_conv_b, ffn_w_down):
    n_ctx, s_len, d = x_prompt.shape
    n_lat, t_len, _ = x_sample.shape
    depth = w_in.shape[0]
    p_len = cache_mla_ckv.shape[2]
    ffn = ffn_w_down.shape[1]
    m_ctx, m_lat = n_ctx * s_len, n_lat * t_len
    m = m_ctx + m_lat
    assert m_ctx % t_len == 0, "context rows must be a whole number of latent sequences"

    tm = _pow2_tile(1024, m_ctx, t_len)
    tn = _pow2_tile(1024, d)
    tp = _pow2_tile(512, m_ctx, t_len)
    lay = _ZLayout(d, tn)
    n_ctx_tiles = m_ctx // tm
    tiles_per_seq = t_len // tm

    def group_of_tile(i):
        return jnp.where(i < n_ctx_tiles, 0, 1 + (i - n_ctx_tiles) // tiles_per_seq)

    n_groups = -(-(1 + n_lat) // SUBLANES) * SUBLANES
    cond = jnp.concatenate([c_ctx[None], c, jnp.zeros((n_groups - 1 - n_lat, d), f32)], axis=0)
    mods = _modulation(cond, w_mod, b_mod).reshape(depth, n_groups, 6, d).transpose(0, 2, 1, 3)[:, :, :, None, :]

    sm = jax.nn.softmax(hgrn_lower_bounds.astype(f32), axis=1)
    csum = jnp.cumsum(sm, axis=1)
    lower = csum - csum[:, :1]
    hg_tabs = tuple(jnp.asarray(t, dt) for t, dt in zip(_hgrn_tables(), (bf16, f32, bf16)))

    cos, sin = _rope_tables(t_len, tp)
    n_ctx_tp = m_ctx // tp

    def rope_block(i):
        return jnp.where(i < n_ctx_tp, 0, 1 + (i - n_ctx_tp) % (t_len // tp))

    x = jnp.concatenate([x_prompt.reshape(m_ctx, d), x_sample.reshape(m_lat, d)], axis=0)
    st_ckv, st_kpe, st_nk, st_nv, st_hg = [], [], [], [], []
    for l in range(depth):
        sh1, sc1, g1, sh2, sc2, g2 = (mods[l, k] for k in range(6))
        w_in_l = _permute_w_in(w_in[l], lay)
        z = _normmod_matmul(x, norm1_g[l], sc1, sh1, w_in_l, group_of_tile, tm, tn, "in_proj")

        w_uk = mla_w_uk[l].reshape(MLA_KV_RANK, MLA_HEADS * MLA_NOPE).astype(bf16)
        w_uv = mla_w_uv[l].reshape(MLA_KV_RANK, MLA_HEADS * MLA_V).astype(bf16)
        q_mla = _mla_q(z, lay, mla_q_norm_g[l], cos, sin, rope_block, tp)
        k_mla, v_mla, ckv_n = _mla_kv(z, lay.ckv, z, lay.kpe, mla_kv_norm_g[l], mla_k_norm_g[l], w_uk, w_uv,
                                      cos, sin, rope_block, tp, True, True, "mla_kv")
        ckv_c = cache_mla_ckv[:, l].reshape(n_lat * p_len, MLA_KV_RANK)
        kpe_c = jnp.pad(cache_mla_kpe[:, l].reshape(n_lat * p_len, MLA_ROPE), ((0, 0), (0, LANES - MLA_ROPE)))
        tc = _pow2_tile(512, n_lat * p_len)
        k_c, v_c, _ = _mla_kv(ckv_c, 0, kpe_c, 0, mla_kv_norm_g[l], mla_k_norm_g[l], w_uk, w_uv,
                              cos, sin, lambda i: 0, tc, False, False, "mla_kv_cache")
        tq_c = _pow2_tile(256, s_len)
        o_mla_c = _flash(q_mla, k_mla, v_mla, batch=n_ctx, heads=MLA_HEADS, t_len=s_len, s_len=s_len, row0=0,
                         dq=MLA_QPAD, dv=MLA_V, tq=tq_c, tk=tq_c, heads_per_step=MLA_HEADS, name="mla_attn_ctx")
        tq_l = _pow2_tile(512, t_len, p_len)
        o_mla_l = _flash(q_mla, k_mla, v_mla, batch=n_lat, heads=MLA_HEADS, t_len=t_len, s_len=t_len, row0=m_ctx,
                         dq=MLA_QPAD, dv=MLA_V, tq=tq_l, tk=tq_l, extra=(k_c, v_c, p_len), name="mla_attn_latent")

        lb = lower[:, l]
        lb_tabs = (jnp.log(lb), jnp.log1p(-lb), 1.0 - lb)
        o_hg_c, s_hg = _hgrn(z, lay, lb_tabs, hgrn_norm_g[l], hg_tabs, batch=n_ctx, t_len=s_len, row0=0,
                             s0=None, emit_state=True)
        o_hg_l, _ = _hgrn(z, lay, lb_tabs, hgrn_norm_g[l], hg_tabs, batch=n_lat, t_len=t_len, row0=m_ctx,
                          s0=state_hgrn[:, l], emit_state=False)

        q_na, k_na, v_na, k_na_f = _na_prep(z, lay, na_q_norm_g[l], na_k_norm_g[l], tp)
        o_na_c = _flash(q_na, k_na, v_na, batch=n_ctx, heads=NA_HEADS, t_len=s_len, s_len=s_len, row0=0,
                        dq=NA_DH, dv=NA_DH, tq=tq_c, tk=tq_c, heads_per_step=NA_HEADS, name="na_attn_ctx")
        kc_na = cache_na_k[:, l].reshape(n_lat * p_len, NA_HEADS * NA_DH).astype(bf16)
        vc_na = cache_na_v[:, l].reshape(n_lat * p_len, NA_HEADS * NA_DH).astype(bf16)
        bias = _na_bias_table(na_rpb[l], t_len // GRID_W)
        o_na_l = _na_latent(q_na, k_na, v_na, kc_na, vc_na, bias, batch=n_lat, t_len=t_len, p_len=p_len, row0=m_ctx)

        merged = _merge((o_mla_c, o_mla_l), (o_hg_c, o_hg_l), (o_na_c, o_na_l), z, lay, w_branch[l].astype(bf16), tp, tn)
        x = _proj_residual(merged, w_out[l].astype(bf16), x, g1, group_of_tile, tm, tn)

        ag = _normmod_matmul(x, norm2_g[l], sc2, sh2, ffn_w_up[l].astype(bf16), group_of_tile, tm,
                             _pow2_tile(1024, 2 * ffn), "ffn_up")
        tmd = _pow2_tile(1024, m_ctx, t_len)
        x = _ffn_down(ag, ffn_conv_w[l], ffn_conv_b[l], ffn_w_down[l].astype(bf16), x, g2,
                      lambda i: jnp.where(i < m_ctx // tmd, 0, 1 + (i - m_ctx // tmd) // (t_len // tmd)),
                      tmd, _pow2_tile(512, ffn), m_ctx, s_len, t_len)

        st_ckv.append(ckv_n[:m_ctx].reshape(n_ctx, s_len, MLA_KV_RANK))
        st_kpe.append(z[:m_ctx, lay.kpe:lay.kpe + MLA_ROPE].reshape(n_ctx, s_len, MLA_ROPE))
        st_nk.append(k_na_f[:m_ctx].reshape(n_ctx, s_len, NA_HEADS, NA_DH))
        st_nv.append(z[:m_ctx, lay.nv:lay.nv + NA_HEADS * NA_DH].reshape(n_ctx, s_len, NA_HEADS, NA_DH))
        st_hg.append(s_hg)

    y_prompt = x[:m_ctx].reshape(n_ctx, s_len, d)
    y_sample = x[m_ctx:].reshape(n_lat, t_len, d)
    return (y_prompt, y_sample, jnp.stack(st_ckv, axis=1), jnp.stack(st_kpe, axis=1), jnp.stack(st_nk, axis=1),
            jnp.stack(st_nv, axis=1), jnp.stack(st_hg, axis=1))
```

```python
import functools
import math

import numpy as np
import jax
import jax.numpy as jnp
from jax import lax
from jax.experimental import pallas as pl
from jax.experimental.pallas import tpu as pltpu

f32 = jnp.float32
bf16 = jnp.bfloat16

GRID_W = 64
MLA_HEADS, MLA_NOPE, MLA_ROPE, MLA_V, MLA_KV_RANK = 8, 128, 64, 128, 512
MLA_QK = MLA_NOPE + MLA_ROPE
HG_HEADS, HG_K, HG_V = 8, 128, 128
NA_HEADS, NA_DH, NA_WIN_R, NA_WIN_C = 8, 128, 8, 16
ROPE_BASE = 10000.0
EPS = 1e-6
NEG = -1e30
LOG2E = 1.4426950408889634

LANES = 128
SUBLANES = 8
VMEM_LIMIT_BYTES = 56 * 1024 * 1024

HEAD_W = 128
MLA_QPAD = 256
HG_CHUNK = 128
HG_LEVELS = int(math.log2(HG_CHUNK))
NA_QROWS = 8
NA_KROWS = 16


def _params(semantics):
    return pltpu.CompilerParams(dimension_semantics=semantics, vmem_limit_bytes=VMEM_LIMIT_BYTES)


def _pow2_tile(pref, *dims):
    t = pref
    while any(d % t for d in dims):
        t //= 2
    assert t >= SUBLANES, (pref, dims)
    return t


def _dot(a, b):
    return jnp.dot(a, b, preferred_element_type=f32)


def _dot_nt(a, b):
    return lax.dot_general(a, b, (((1,), (1,)), ((), ())), preferred_element_type=f32)


def _dot_tn(a, b):
    return lax.dot_general(a, b, (((0,), (0,)), ((), ())), preferred_element_type=f32)


def _silu(x):
    return x * jax.nn.sigmoid(x)


def _mod_const(x, n):
    return jnp.bitwise_and(x, n - 1) if n & (n - 1) == 0 else lax.rem(x, n)


class _ZLayout:
    def __init__(self, d_model, tn):
        self.qn = 0
        self.qp = self.qn + MLA_HEADS * MLA_NOPE
        self.ckv = self.qp + MLA_HEADS * MLA_ROPE
        self.hq = self.ckv + MLA_KV_RANK
        self.hff = self.hq + HG_HEADS * HG_K
        self.hfb = self.hff + HG_HEADS * HG_K
        self.hi = self.hfb + HG_HEADS * HG_K
        self.hg = self.hi + HG_HEADS * HG_V
        self.nq = self.hg + HG_HEADS * HG_V
        self.nk = self.nq + NA_HEADS * NA_DH
        self.nv = self.nk + NA_HEADS * NA_DH
        self.ga = self.nv + NA_HEADS * NA_DH
        self.gb = self.ga + d_model
        self.gc = self.gb + d_model
        self.kpe = self.gc + d_model
        used = self.kpe + MLA_ROPE
        self.width = -(-used // tn) * tn


def _permute_w_in(w, lay):
    d = w.shape[0]
    n_mq = MLA_HEADS * MLA_QK
    mq = w[:, :n_mq].reshape(d, MLA_HEADS, MLA_QK)
    qn = mq[:, :, :MLA_NOPE].reshape(d, MLA_HEADS * MLA_NOPE)
    qp = mq[:, :, MLA_NOPE:].reshape(d, MLA_HEADS * MLA_ROPE)
    ckv = w[:, n_mq:n_mq + MLA_KV_RANK]
    kpe = w[:, n_mq + MLA_KV_RANK:n_mq + MLA_KV_RANK + MLA_ROPE]
    rest = w[:, n_mq + MLA_KV_RANK + MLA_ROPE:]
    pad = jnp.zeros((d, lay.width - lay.kpe - MLA_ROPE), w.dtype)
    return jnp.concatenate([qn, qp, ckv, rest, kpe, pad], axis=1).astype(bf16)


def _mod_kernel(c_ref, w_ref, b_ref, o_ref):
    a = _silu(c_ref[...]).astype(bf16)
    o_ref[...] = _dot(a, w_ref[...].astype(bf16)) + b_ref[...]


def _modulation(cond, w_mod, b_mod):
    depth, d, n = w_mod.shape
    g = cond.shape[0]
    tn = _pow2_tile(1024, n)
    return pl.pallas_call(
        _mod_kernel,
        grid=(depth, n // tn),
        in_specs=[
            pl.BlockSpec((g, d), lambda l, j: (0, 0)),
            pl.BlockSpec((None, d, tn), lambda l, j: (l, 0, j)),
            pl.BlockSpec((None, 1, tn), lambda l, j: (l, 0, j)),
        ],
        out_specs=pl.BlockSpec((None, g, tn), lambda l, j: (l, 0, j)),
        out_shape=jax.ShapeDtypeStruct((depth, g, n), f32),
        compiler_params=_params(("arbitrary", "arbitrary")),
        name="modulation",
    )(cond, w_mod, b_mod.reshape(depth, 1, n))


def _normmod_matmul_kernel(x_ref, g_ref, sc_ref, sh_ref, w_ref, o_ref, h_scr):
    @pl.when(pl.program_id(1) == 0)
    def _():
        x = x_ref[...]
        ms = jnp.mean(x * x, axis=-1, keepdims=True)
        y = x * lax.rsqrt(ms + EPS) * g_ref[...]
        h_scr[...] = (y * (1.0 + sc_ref[...]) + sh_ref[...]).astype(bf16)

    o_ref[...] = _dot(h_scr[...], w_ref[...])


def _normmod_matmul(x, g, sc, sh, w, group_of_tile, tm, tn, name):
    m, d = x.shape
    n = w.shape[1]
    return pl.pallas_call(
        _normmod_matmul_kernel,
        grid=(m // tm, n // tn),
        in_specs=[
            pl.BlockSpec((tm, d), lambda i, j: (i, 0)),
            pl.BlockSpec((1, d), lambda i, j: (0, 0)),
            pl.BlockSpec((None, 1, d), lambda i, j: (group_of_tile(i), 0, 0)),
            pl.BlockSpec((None, 1, d), lambda i, j: (group_of_tile(i), 0, 0)),
            pl.BlockSpec((d, tn), lambda i, j: (0, j)),
        ],
        out_specs=pl.BlockSpec((tm, tn), lambda i, j: (i, j)),
        out_shape=jax.ShapeDtypeStruct((m, n), f32),
        scratch_shapes=[pltpu.VMEM((tm, d), bf16)],
        compiler_params=_params(("arbitrary", "arbitrary")),
        name=name,
    )(x, g.reshape(1, d), sc, sh, w)


def _rope(x, cos, sin_signed):
    n = x.shape[-1]
    half = MLA_ROPE // 4
    lane = lax.broadcasted_iota(jnp.int32, x.shape, 1)
    partner = jnp.where((lane % (2 * half)) < half, pltpu.roll(x, n - half, 1), pltpu.roll(x, half, 1))
    return x * cos + partner * sin_signed


def _mla_q_kernel(qn_ref, qp_ref, gn_ref, gp_ref, cos_ref, sin_ref, o_ref, *, scale):
    qn = qn_ref[...]
    qp = qp_ref[...]
    tm = qn.shape[0]
    lane = lax.broadcasted_iota(jnp.int32, (tm, LANES), 1)
    low = lane < MLA_ROPE
    cos = cos_ref[...]
    sin = sin_ref[...]
    gn = gn_ref[...]
    gp = gp_ref[...]
    for pair in range(MLA_HEADS // 2):
        pcol = qp[:, pair * LANES:(pair + 1) * LANES]
        sq = pcol * pcol
        ss_all = jnp.sum(sq, axis=-1, keepdims=True)
        ss_low = jnp.sum(jnp.where(low, sq, 0.0), axis=-1, keepdims=True)
        rot = _rope(pcol * gp, cos, sin)
        for k, ss_pe in enumerate((ss_low, ss_all - ss_low)):
            h = 2 * pair + k
            nope = qn[:, h * MLA_NOPE:(h + 1) * MLA_NOPE]
            ss = jnp.sum(nope * nope, axis=-1, keepdims=True) + ss_pe
            r = lax.rsqrt(ss * (1.0 / MLA_QK) + EPS) * scale
            pe = rot if k == 0 else pltpu.roll(rot, MLA_ROPE, 1)
            o_ref[:, h * MLA_QPAD:h * MLA_QPAD + MLA_NOPE] = (nope * r * gn).astype(bf16)
            o_ref[:, h * MLA_QPAD + MLA_NOPE:(h + 1) * MLA_QPAD] = jnp.where(low, pe * r, 0.0).astype(bf16)


def _mla_kv_kernel(ckv_ref, kpe_ref, kvg_ref, gn_ref, gp_ref, wuk_ref, wuv_ref, cos_ref, sin_ref,
                   k_ref, v_ref, ckv_out_ref, *, norm_ckv, rope):
    ckv = ckv_ref[...]
    if norm_ckv:
        ms = jnp.mean(ckv * ckv, axis=-1, keepdims=True)
        ckv = ckv * lax.rsqrt(ms + EPS) * kvg_ref[...]
    ckv_out_ref[...] = ckv
    cb = ckv.astype(bf16)
    kn = _dot(cb, wuk_ref[...])
    v_ref[...] = _dot(cb, wuv_ref[...]).astype(bf16)
    tm = ckv.shape[0]
    lane = lax.broadcasted_iota(jnp.int32, (tm, LANES), 1)
    low = lane < MLA_ROPE
    kpe = jnp.where(low, kpe_ref[...], 0.0)
    ss_pe = jnp.sum(kpe * kpe, axis=-1, keepdims=True)
    pe = kpe * gp_ref[...]
    if rope:
        pe = _rope(pe, cos_ref[...], sin_ref[...])
    gn = gn_ref[...]
    for h in range(MLA_HEADS):
        nope = kn[:, h * MLA_NOPE:(h + 1) * MLA_NOPE]
        ss = jnp.sum(nope * nope, axis=-1, keepdims=True) + ss_pe
        r = lax.rsqrt(ss * (1.0 / MLA_QK) + EPS)
        k_ref[:, h * MLA_QPAD:h * MLA_QPAD + MLA_NOPE] = (nope * r * gn).astype(bf16)
        k_ref[:, h * MLA_QPAD + MLA_NOPE:(h + 1) * MLA_QPAD] = jnp.where(low, pe * r, 0.0).astype(bf16)


def _rope_tables(t_len, tm):
    n_freq = MLA_ROPE // 4
    t = jnp.arange(t_len, dtype=jnp.int32)
    inv = ROPE_BASE ** (-jnp.arange(n_freq, dtype=f32) / n_freq)
    ang_r = (t // GRID_W).astype(f32)[:, None] * inv
    ang_c = (t % GRID_W).astype(f32)[:, None] * inv
    cos64 = jnp.concatenate([jnp.cos(ang_r), jnp.cos(ang_r), jnp.cos(ang_c), jnp.cos(ang_c)], axis=1)
    sin64 = jnp.concatenate([-jnp.sin(ang_r), jnp.sin(ang_r), -jnp.sin(ang_c), jnp.sin(ang_c)], axis=1)
    cos = jnp.concatenate([jnp.ones((tm, LANES), f32), jnp.tile(cos64, (1, 2))], axis=0)
    sin = jnp.concatenate([jnp.zeros((tm, LANES), f32), jnp.tile(sin64, (1, 2))], axis=0)
    return cos, sin


def _mla_q(z, lay, q_g, cos, sin, rope_block, tm):
    m = z.shape[0]
    gn = q_g[:MLA_NOPE].reshape(1, MLA_NOPE)
    gp = jnp.tile(q_g[MLA_NOPE:], 2).reshape(1, LANES)
    wn, wp = MLA_HEADS * MLA_NOPE, MLA_HEADS * MLA_ROPE
    return pl.pallas_call(
        functools.partial(_mla_q_kernel, scale=MLA_QK ** -0.5 * LOG2E),
        grid=(m // tm,),
        in_specs=[
            pl.BlockSpec((tm, wn), lambda i: (i, lay.qn // wn)),
            pl.BlockSpec((tm, wp), lambda i: (i, lay.qp // wp)),
            pl.BlockSpec((1, MLA_NOPE), lambda i: (0, 0)),
            pl.BlockSpec((1, LANES), lambda i: (0, 0)),
            pl.BlockSpec((tm, LANES), lambda i: (rope_block(i), 0)),
            pl.BlockSpec((tm, LANES), lambda i: (rope_block(i), 0)),
        ],
        out_specs=pl.BlockSpec((tm, MLA_HEADS * MLA_QPAD), lambda i: (i, 0)),
        out_shape=jax.ShapeDtypeStruct((m, MLA_HEADS * MLA_QPAD), bf16),
        compiler_params=_params(("arbitrary",)),
        name="mla_q",
    )(z, z, gn, gp, cos, sin)


def _mla_kv(ckv_src, ckv_col, kpe_src, kpe_col, kv_g, k_g, w_uk, w_uv, cos, sin, rope_block, tm, norm_ckv, rope, name):
    m = ckv_src.shape[0]
    gn = k_g[:MLA_NOPE].reshape(1, MLA_NOPE)
    gp = jnp.concatenate([k_g[MLA_NOPE:], jnp.zeros((LANES - MLA_ROPE,), f32)]).reshape(1, LANES)
    hw = MLA_HEADS * MLA_NOPE
    return pl.pallas_call(
        functools.partial(_mla_kv_kernel, norm_ckv=norm_ckv, rope=rope),
        grid=(m // tm,),
        in_specs=[
            pl.BlockSpec((tm, MLA_KV_RANK), lambda i: (i, ckv_col // MLA_KV_RANK)),
            pl.BlockSpec((tm, LANES), lambda i: (i, kpe_col // LANES)),
            pl.BlockSpec((1, MLA_KV_RANK), lambda i: (0, 0)),
            pl.BlockSpec((1, MLA_NOPE), lambda i: (0, 0)),
            pl.BlockSpec((1, LANES), lambda i: (0, 0)),
            pl.BlockSpec((MLA_KV_RANK, hw), lambda i: (0, 0)),
            pl.BlockSpec((MLA_KV_RANK, hw), lambda i: (0, 0)),
            pl.BlockSpec((tm, LANES), lambda i: (rope_block(i), 0)),
            pl.BlockSpec((tm, LANES), lambda i: (rope_block(i), 0)),
        ],
        out_specs=[
            pl.BlockSpec((tm, MLA_HEADS * MLA_QPAD), lambda i: (i, 0)),
            pl.BlockSpec((tm, hw), lambda i: (i, 0)),
            pl.BlockSpec((tm, MLA_KV_RANK), lambda i: (i, 0)),
        ],
        out_shape=[
            jax.ShapeDtypeStruct((m, MLA_HEADS * MLA_QPAD), bf16),
            jax.ShapeDtypeStruct((m, hw), bf16),
            jax.ShapeDtypeStruct((m, MLA_KV_RANK), f32),
        ],
        compiler_params=_params(("arbitrary",)),
        name=name,
    )(ckv_src, kpe_src, kv_g.reshape(1, MLA_KV_RANK), gn, gp, w_uk, w_uv, cos, sin)


def _flash_kernel(*refs, n_main, tk, n_extra, heads, dq, dv):
    if n_extra:
        q_ref, k_ref, v_ref, kx_ref, vx_ref, o_ref = refs
    else:
        q_ref, k_ref, v_ref, o_ref = refs
        kx_ref = vx_ref = None
    tq = q_ref.shape[0]
    chunks = [(k_ref, v_ref, c) for c in range(n_main)] + [(kx_ref, vx_ref, c) for c in range(n_extra)]
    def scores(i, g):
        kr, _, c = chunks[i]
        return _dot_nt(kr[c * tk:(c + 1) * tk, g * dq:(g + 1) * dq], q_ref[:, g * dq:(g + 1) * dq])

    m = [jnp.full((1, tq), -jnp.inf, f32)] * heads
    l = [jnp.zeros((1, tq), f32)] * heads
    acc = [jnp.zeros((dv, tq), f32)] * heads
    s = [scores(0, g) for g in range(heads)]
    for i, (_, vr, c) in enumerate(chunks):
        for g in range(heads):
            s_next = scores(i + 1, g) if i + 1 < len(chunks) else None
            m_new = jnp.maximum(m[g], jnp.max(s[g], axis=0, keepdims=True))
            alpha = jnp.exp2(m[g] - m_new)
            p = jnp.exp2(s[g] - m_new)
            l[g] = alpha * l[g] + jnp.sum(p, axis=0, keepdims=True)
            acc[g] = alpha * acc[g] + _dot_tn(vr[c * tk:(c + 1) * tk, g * dv:(g + 1) * dv], p.astype(bf16))
            m[g], s[g] = m_new, s_next
    for g in range(heads):
        o_ref[:, g * dv:(g + 1) * dv] = (acc[g] / l[g]).T.astype(o_ref.dtype)


def _flash(q, k, v, *, batch, heads, t_len, s_len, row0, dq, dv, tq, tk, heads_per_step=1, extra=None, name):
    assert row0 % tq == 0 and row0 % s_len == 0 and t_len % tq == 0 and s_len % tk == 0
    assert heads % heads_per_step == 0
    nq = t_len // tq
    g = heads_per_step
    in_specs = [
        pl.BlockSpec((tq, g * dq), lambda b, h, i: (row0 // tq + b * nq + i, h)),
        pl.BlockSpec((s_len, g * dq), lambda b, h, i: (row0 // s_len + b, h)),
        pl.BlockSpec((s_len, g * dv), lambda b, h, i: (row0 // s_len + b, h)),
    ]
    args = [q, k, v]
    n_extra = 0
    if extra is not None:
        kx, vx, p_len = extra
        assert p_len % tk == 0
        n_extra = p_len // tk
        in_specs += [
            pl.BlockSpec((p_len, g * dq), lambda b, h, i: (b, h)),
            pl.BlockSpec((p_len, g * dv), lambda b, h, i: (b, h)),
        ]
        args += [kx, vx]
    return pl.pallas_call(
        functools.partial(_flash_kernel, n_main=s_len // tk, tk=tk, n_extra=n_extra, heads=g, dq=dq, dv=dv),
        grid=(batch, heads // g, nq),
        in_specs=in_specs,
        out_specs=pl.BlockSpec((tq, g * dv), lambda b, h, i: (b * nq + i, h)),
        out_shape=jax.ShapeDtypeStruct((batch * t_len, heads * dv), bf16),
        compiler_params=_params(("arbitrary", "arbitrary", "arbitrary")),
        name=name,
    )(*args)


def _na_prep_kernel(nq_ref, nk_ref, nv_ref, qg_ref, kg_ref, q_ref, k_ref, v_ref, kf_ref, *, scale):
    qg = qg_ref[...]
    kg = kg_ref[...]
    v_ref[...] = nv_ref[...].astype(bf16)
    for h in range(NA_HEADS):
        sl = slice(h * NA_DH, (h + 1) * NA_DH)
        q = nq_ref[:, sl]
        k = nk_ref[:, sl]
        rq = lax.rsqrt(jnp.mean(q * q, axis=-1, keepdims=True) + EPS)
        rk = lax.rsqrt(jnp.mean(k * k, axis=-1, keepdims=True) + EPS)
        q_ref[:, sl] = (q * rq * qg * scale).astype(bf16)
        kn = k * rk * kg
        kf_ref[:, sl] = kn
        k_ref[:, sl] = kn.astype(bf16)


def _na_prep(z, lay, q_g, k_g, tm):
    m = z.shape[0]
    w = NA_HEADS * NA_DH
    spec = pl.BlockSpec((tm, w), lambda i: (i, 0))
    return pl.pallas_call(
        functools.partial(_na_prep_kernel, scale=NA_DH ** -0.5 * LOG2E),
        grid=(m // tm,),
        in_specs=[
            pl.BlockSpec((tm, w), lambda i: (i, lay.nq // w)),
            pl.BlockSpec((tm, w), lambda i: (i, lay.nk // w)),
            pl.BlockSpec((tm, w), lambda i: (i, lay.nv // w)),
            pl.BlockSpec((1, NA_DH), lambda i: (0, 0)),
            pl.BlockSpec((1, NA_DH), lambda i: (0, 0)),
        ],
        out_specs=[spec, spec, spec, spec],
        out_shape=[jax.ShapeDtypeStruct((m, w), bf16)] * 3 + [jax.ShapeDtypeStruct((m, w), f32)],
        compiler_params=_params(("arbitrary",)),
        name="na_prep",
    )(z, z, z, q_g.reshape(1, NA_DH), k_g.reshape(1, NA_DH))


def _na_static_maps(rows):
    nblk = rows // NA_QROWS
    reps = (0, min(1, nblk - 1), nblk - 1)
    dr_map = np.full((3, NA_QROWS, NA_KROWS), 2 * NA_WIN_R - 1, np.int32)
    for v, kb in enumerate(reps):
        ws = min(max(NA_QROWS * kb - NA_WIN_R // 2, 0), rows - NA_KROWS)
        for i in range(NA_QROWS):
            qrow = NA_QROWS * kb + i
            rs = min(max(qrow - NA_WIN_R // 2, 0), rows - NA_WIN_R)
            for j in range(NA_KROWS):
                krow = ws + j
                if rs <= krow < rs + NA_WIN_R:
                    dr_map[v, i, j] = krow - qrow + NA_WIN_R - 1
    qcol = np.arange(GRID_W)
    cs = np.clip(qcol - NA_WIN_C // 2, 0, GRID_W - NA_WIN_C)
    in_win = (qcol[None, :] >= cs[:, None]) & (qcol[None, :] < cs[:, None] + NA_WIN_C)
    dc_idx = np.clip(qcol[None, :] - qcol[:, None], -(NA_WIN_C - 1), NA_WIN_C - 1) + (NA_WIN_C - 1)
    return dr_map, in_win, dc_idx


def _na_bias_table(rpb, rows):
    dr_map, in_win, dc_idx = _na_static_maps(rows)
    h = rpb.shape[0]
    bt = jnp.where(in_win.T[None, None], jnp.take(rpb * LOG2E, jnp.asarray(dc_idx.T), axis=2), NEG)
    bt = jnp.concatenate([bt, jnp.full((h, 1, GRID_W, GRID_W), NEG, f32)], axis=1)
    tab = jnp.take(bt, jnp.asarray(dr_map.transpose(0, 2, 1).reshape(-1)), axis=1)
    tab = tab.reshape(h, 3, NA_KROWS, NA_QROWS, GRID_W, GRID_W).transpose(1, 0, 2, 4, 3, 5)
    return tab.reshape(3, h, NA_KROWS * GRID_W, NA_QROWS * GRID_W)


def _na_kernel(q_ref, k_ref, v_ref, kc_ref, vc_ref, bias_ref, o_ref, *, rows):
    kb = pl.program_id(2)
    ws = jnp.clip(NA_QROWS * kb - NA_WIN_R // 2, 0, rows - NA_KROWS)
    off = pl.multiple_of(ws * GRID_W, (NA_WIN_R // 2) * GRID_W)
    n_keys = NA_KROWS * GRID_W
    q = q_ref[...]
    s_loc = _dot_nt(k_ref[pl.ds(off, n_keys), :], q) + bias_ref[...]
    s_ctx = _dot_nt(kc_ref[...], q)
    m = jnp.maximum(jnp.max(s_loc, axis=0, keepdims=True), jnp.max(s_ctx, axis=0, keepdims=True))
    p_loc = jnp.exp2(s_loc - m)
    p_ctx = jnp.exp2(s_ctx - m)
    l = jnp.sum(p_loc, axis=0, keepdims=True) + jnp.sum(p_ctx, axis=0, keepdims=True)
    acc = _dot_tn(v_ref[pl.ds(off, n_keys), :], p_loc.astype(bf16)) + _dot_tn(vc_ref[...], p_ctx.astype(bf16))
    o_ref[...] = (acc / l).T.astype(o_ref.dtype)


def _na_latent(q, k, v, kc, vc, bias, *, batch, t_len, p_len, row0):
    rows = t_len // GRID_W
    assert t_len % GRID_W == 0 and rows % NA_QROWS == 0 and rows >= NA_KROWS
    tq = NA_QROWS * GRID_W
    assert row0 % tq == 0 and row0 % t_len == 0
    nblk = rows // NA_QROWS

    def variant(i):
        return jnp.where(i == 0, 0, jnp.where(i == nblk - 1, 2, 1))

    return pl.pallas_call(
        functools.partial(_na_kernel, rows=rows),
        grid=(batch, NA_HEADS, nblk),
        in_specs=[
            pl.BlockSpec((tq, NA_DH), lambda b, h, i: (row0 // tq + b * nblk + i, h)),
            pl.BlockSpec((t_len, NA_DH), lambda b, h, i: (row0 // t_len + b, h)),
            pl.BlockSpec((t_len, NA_DH), lambda b, h, i: (row0 // t_len + b, h)),
            pl.BlockSpec((p_len, NA_DH), lambda b, h, i: (b, h)),
            pl.BlockSpec((p_len, NA_DH), lambda b, h, i: (b, h)),
            pl.BlockSpec((None, None, NA_KROWS * GRID_W, tq), lambda b, h, i: (variant(i), h, 0, 0)),
        ],
        out_specs=pl.BlockSpec((tq, NA_DH), lambda b, h, i: (b * nblk + i, h)),
        out_shape=jax.ShapeDtypeStruct((batch * t_len, NA_HEADS * NA_DH), bf16),
        compiler_params=_params(("arbitrary", "arbitrary", "arbitrary")),
        name="na_latent",
    )(q, k, v, kc, vc, bias)


def _hgrn_tables():
    c, lv = HG_CHUNK, HG_LEVELS
    t = np.arange(c)[:, None]
    r = np.arange(c)[None, :]
    wcum = np.stack([r <= t, r >= t]).astype(np.float32)
    qside = np.zeros((2, lv * c, LANES), np.float32)
    mask = np.zeros((2, lv + 1, c, c), np.float32)
    mask[:, 0] = np.eye(c)
    for l in range(lv):
        hs = 1 << l
        blk = t // (2 * hs)
        mid = blk * (2 * hs) + hs
        qf = t >= mid
        qb = t < mid
        qside[0, l * c:(l + 1) * c] = np.where(qf, 1.0, -1.0)
        qside[1, l * c:(l + 1) * c] = np.where(qb, 1.0, -1.0)
        same = blk == blk.T
        mask[0, 1 + l] = same & qf & ~qf.T
        mask[1, 1 + l] = same & qb & ~qb.T
    return wcum, qside, mask


def _hgrn_kernel(*refs, n_chunks, heads, has_state, emit_state):
    (hq_ref, hff_ref, hfb_ref, hi_ref, hg_ref, la_ref, lc_ref, om_ref, ng_ref,
     wcum_ref, qside_ref, mask_ref) = refs[:12]
    rest = list(refs[12:])
    s0_ref = rest.pop(0) if has_state else None
    o_ref = rest.pop(0)
    st_ref = rest.pop(0) if emit_state else None
    o_scr, s_scr, b_scr = rest
    c, lv = HG_CHUNK, HG_LEVELS

    o_scr[...] = jnp.zeros_like(o_scr)
    for d in range(2):
        for g in range(heads):
            if has_state:
                s_scr[d, g] = s0_ref[d, g].T
            else:
                s_scr[d, g] = jnp.zeros((HG_V, HG_K), f32)

    chains = [(d, g) for g in range(heads) for d in range(2)]
    low_half = lax.broadcasted_iota(jnp.int32, (SUBLANES, HEAD_W), 0) < SUBLANES // 2

    def gates_and_cumsum(ci, row):
        d, g = chains[ci]
        rs = pl.ds(pl.multiple_of(row, c), c)
        cs = slice(g * HEAD_W, (g + 1) * HEAD_W)
        x = (hff_ref if d == 0 else hfb_ref)[rs, cs]
        x2 = x * LOG2E
        u = jnp.exp2(-jnp.abs(x2))
        inv = 1.0 / (1.0 + u)
        pos = x >= 0.0
        log_sig = jnp.minimum(x2, 0.0) - jnp.log2(1.0 + u)
        a = la_ref[d:d + 1, cs]
        bb = lc_ref[d:d + 1, cs] + log_sig
        lf = jnp.maximum(a, bb) + jnp.log2(1.0 + jnp.exp2(-jnp.abs(a - bb)))
        om = om_ref[d:d + 1, cs]
        f = (1.0 - om) + om * (jnp.where(pos, 1.0, u) * inv)
        kin = om * (jnp.where(pos, u, 1.0) * inv)
        q = _silu(hq_ref[rs, cs])
        v = hi_ref[rs, cs].astype(bf16)
        l1 = lf.astype(bf16)
        r1 = lf - l1.astype(f32)
        l2 = r1.astype(bf16)
        l3 = (r1 - l2.astype(f32)).astype(bf16)
        w = wcum_ref[d]
        b = _dot(w, l1) + _dot(w, l2) + _dot(w, l3)
        b_scr[ci] = b
        return dict(rs=rs, cs=cs, f=f, kin=kin, q=q, v=v, b=b)

    def intra_chunk(ci, s):
        d, _ = chains[ci]
        b, q, kin = s["b"], s["q"], s["kin"]

        def row(r):
            return b_scr[ci, r:r + 1, :]

        s["b_end"] = row(c - 1 if d == 0 else 0)
        xs = [jnp.where(qside_ref[d, 0:c, :] > 0.0, q * s["f"], kin).astype(bf16)]
        for l in range(1, lv):
            hs = 1 << l
            blk = 2 * hs
            pick = hs - 1 if d == 0 else hs
            if hs >= SUBLANES:
                pieces = []
                for j in range(c // blk):
                    m = row(blk * j + pick)
                    early = slice(blk * j, blk * j + hs)
                    late = slice(blk * j + hs, blk * (j + 1))
                    q_sl, k_sl = (late, early) if d == 0 else (early, late)
                    xq = q[q_sl] * jnp.exp2(b[q_sl] - m)
                    xk = kin[k_sl] * jnp.exp2(m - b[k_sl])
                    pieces += [xk, xq] if d == 0 else [xq, xk]
                xs.append(jnp.concatenate(pieces, axis=0).astype(bf16))
                continue
            if blk < SUBLANES:
                pieces = [jnp.where(low_half, jnp.broadcast_to(row(SUBLANES * j + pick), (SUBLANES, HEAD_W)),
                                    jnp.broadcast_to(row(SUBLANES * j + blk + pick), (SUBLANES, HEAD_W)))
                          for j in range(c // SUBLANES)]
            else:
                pieces = [jnp.broadcast_to(row(blk * j + pick), (blk, HEAD_W)) for j in range(c // blk)]
            m = pieces[0] if len(pieces) == 1 else jnp.concatenate(pieces, axis=0)
            sign = qside_ref[d, l * c:(l + 1) * c, :]
            xs.append((jnp.where(sign > 0.0, q, kin) * jnp.exp2((b - m) * sign)).astype(bf16))
        att = mask_ref[d, 0] * _dot_nt(q.astype(bf16), kin.astype(bf16)).astype(bf16)
        for l in range(lv):
            att = att + mask_ref[d, 1 + l] * _dot_nt(xs[l], xs[l]).astype(bf16)
        s["att"] = att

    def state_step(ci, s):
        d, g = chains[ci]
        b, q, kin, v = s["b"], s["q"], s["kin"], s["v"]
        st = s_scr[d, g]
        o = _dot(s["att"], v) + _dot_nt((q * jnp.exp2(b)).astype(bf16), st.astype(bf16))
        b_end = s["b_end"]
        s_scr[d, g] = st * jnp.exp2(b_end) + _dot_tn(v, (kin * jnp.exp2(b_end - b)).astype(bf16))
        o_scr[s["rs"], s["cs"]] += o

    def body(i, _):
        rows = (i * c, (n_chunks - 1 - i) * c)
        states = [gates_and_cumsum(ci, rows[chains[ci][0]]) for ci in range(len(chains))]
        for ci, s in enumerate(states):
            intra_chunk(ci, s)
        for ci, s in enumerate(states):
            state_step(ci, s)
        return 0

    lax.fori_loop(0, n_chunks, body, 0)

    ng = ng_ref[...]
    for g in range(heads):
        cs = slice(g * HEAD_W, (g + 1) * HEAD_W)
        o = o_scr[:, cs]
        y = o * lax.rsqrt(jnp.mean(o * o, axis=-1, keepdims=True) + EPS) * ng
        o_ref[:, cs] = (y * _silu(hg_ref[:, cs])).astype(o_ref.dtype)
        if emit_state:
            for d in range(2):
                st_ref[d, g] = s_scr[d, g].T


def _hgrn(z, lay, lb_tabs, norm_g, tables, *, batch, t_len, row0, s0, emit_state, heads_per_step=2):
    assert t_len % HG_CHUNK == 0 and row0 % t_len == 0 and HG_K == HEAD_W and HG_V == HEAD_W
    g = heads_per_step
    gw = g * HEAD_W
    la, lc, om = lb_tabs
    wcum, qside, mask = tables
    r0 = row0 // t_len

    def zspec(col):
        return pl.BlockSpec((t_len, gw), lambda b, h: (r0 + b, col // gw + h))

    def const(shape):
        return pl.BlockSpec(shape, lambda b, h: (0,) * len(shape))

    in_specs = [zspec(lay.hq), zspec(lay.hff), zspec(lay.hfb), zspec(lay.hi), zspec(lay.hg),
                pl.BlockSpec((2, gw), lambda b, h: (0, h)), pl.BlockSpec((2, gw), lambda b, h: (0, h)),
                pl.BlockSpec((2, gw), lambda b, h: (0, h)), const((1, HG_V)),
                const(wcum.shape), const(qside.shape), const(mask.shape)]
    args = [z, z, z, z, z, la, lc, om, norm_g.reshape(1, HG_V), wcum, qside, mask]
    state_spec = pl.BlockSpec((None, 2, g, HG_K, HG_V), lambda b, h: (b, 0, h, 0, 0))
    if s0 is not None:
        in_specs.append(state_spec)
        args.append(s0)
    out_specs = [pl.BlockSpec((t_len, gw), lambda b, h: (b, h))]
    out_shape = [jax.ShapeDtypeStruct((batch * t_len, HG_HEADS * HG_V), bf16)]
    if emit_state:
        out_specs.append(state_spec)
        out_shape.append(jax.ShapeDtypeStruct((batch, 2, HG_HEADS, HG_K, HG_V), f32))
    res = pl.pallas_call(
        functools.partial(_hgrn_kernel, n_chunks=t_len // HG_CHUNK, heads=g, has_state=s0 is not None,
                          emit_state=emit_state),
        grid=(batch, HG_HEADS // g),
        in_specs=in_specs,
        out_specs=out_specs,
        out_shape=out_shape,
        scratch_shapes=[pltpu.VMEM((t_len, gw), f32), pltpu.VMEM((2, g, HG_V, HG_K), f32),
                        pltpu.VMEM((2 * g, HG_CHUNK, HEAD_W), f32)],
        compiler_params=_params(("arbitrary", "arbitrary")),
        name="hgrn_ctx" if emit_state else "hgrn_latent",
    )(*args)
    return res if emit_state else (res[0], None)


def _merge_kernel(oac_ref, oal_ref, obc_ref, obl_ref, occ_ref, ocl_ref, ga_ref, gb_ref, gc_ref, w_ref, o_ref,
                  *, n_ctx_tiles):
    def compute(oa_ref, ob_ref, oc_ref):
        m = jax.nn.sigmoid(ga_ref[...]) * _dot(oa_ref[...], w_ref[0])
        m = m + jax.nn.sigmoid(gb_ref[...]) * _dot(ob_ref[...], w_ref[1])
        m = m + jax.nn.sigmoid(gc_ref[...]) * _dot(oc_ref[...], w_ref[2])
        o_ref[...] = m.astype(o_ref.dtype)

    is_ctx = pl.program_id(0) < n_ctx_tiles

    @pl.when(is_ctx)
    def _():
        compute(oac_ref, obc_ref, occ_ref)

    @pl.when(jnp.logical_not(is_ctx))
    def _():
        compute(oal_ref, obl_ref, ocl_ref)


def _merge(o_a, o_b, o_c, z, lay, w_branch, tm, tn):
    bw = o_a[0].shape[1]
    m = z.shape[0]
    d = w_branch.shape[2]
    nct = o_a[0].shape[0] // tm
    cspec = pl.BlockSpec((tm, bw), lambda i, j: (jnp.minimum(i, nct - 1), 0))
    lspec = pl.BlockSpec((tm, bw), lambda i, j: (jnp.maximum(i - nct, 0), 0))

    def gate(col):
        return pl.BlockSpec((tm, tn), lambda i, j: (i, col // tn + j))

    return pl.pallas_call(
        functools.partial(_merge_kernel, n_ctx_tiles=nct),
        grid=(m // tm, d // tn),
        in_specs=[cspec, lspec, cspec, lspec, cspec, lspec, gate(lay.ga), gate(lay.gb), gate(lay.gc),
                  pl.BlockSpec((3, bw, tn), lambda i, j: (0, 0, j))],
        out_specs=pl.BlockSpec((tm, tn), lambda i, j: (i, j)),
        out_shape=jax.ShapeDtypeStruct((m, d), bf16),
        compiler_params=_params(("arbitrary", "arbitrary")),
        name="merge",
    )(o_a[0], o_a[1], o_b[0], o_b[1], o_c[0], o_c[1], z, z, z, w_branch)


def _proj_residual_kernel(m_ref, w_ref, x_ref, g_ref, o_ref):
    o_ref[...] = x_ref[...] + g_ref[...] * _dot(m_ref[...], w_ref[...])


def _proj_residual(mm, w, x, gate, group_of_tile, tm, tn):
    m, k = mm.shape
    d = w.shape[1]
    return pl.pallas_call(
        _proj_residual_kernel,
        grid=(m // tm, d // tn),
        in_specs=[
            pl.BlockSpec((tm, k), lambda i, j: (i, 0)),
            pl.BlockSpec((k, tn), lambda i, j: (0, j)),
            pl.BlockSpec((tm, tn), lambda i, j: (i, j)),
            pl.BlockSpec((None, 1, tn), lambda i, j: (group_of_tile(i), 0, j)),
        ],
        out_specs=pl.BlockSpec((tm, tn), lambda i, j: (i, j)),
        out_shape=jax.ShapeDtypeStruct((m, d), f32),
        compiler_params=_params(("arbitrary", "arbitrary")),
        name="out_proj",
    )(mm, w, x, gate)


def _ffn_down_kernel(a_ref, g_ref, gprev_ref, gnext_ref, cw_ref, cb_ref, w_ref, x_ref, gate_ref, o_ref,
                     *, m_ctx, s_len, t_len):
    i = pl.program_id(0)
    k = pl.program_id(1)
    g = g_ref[...]
    tm, tk = g.shape
    row = i * tm + lax.broadcasted_iota(jnp.int32, (tm, LANES), 0)
    pos = jnp.where(row < m_ctx, _mod_const(row, s_len), _mod_const(row - m_ctx, t_len))
    last = jnp.where(row < m_ctx, s_len - 1, t_len - 1)
    keep_prev = jnp.tile(jnp.where(pos == 0, 0.0, 1.0), (1, tk // LANES))
    keep_next = jnp.tile(jnp.where(pos == last, 0.0, 1.0), (1, tk // LANES))
    sub = lax.broadcasted_iota(jnp.int32, (SUBLANES, tk), 0)
    down = pltpu.roll(g, 1, 0)
    g_prev = jnp.concatenate(
        [jnp.where(sub == 0, gprev_ref[SUBLANES - 1:SUBLANES, :], down[:SUBLANES]), down[SUBLANES:]], axis=0)
    up = pltpu.roll(g, tm - 1, 0)
    g_next = jnp.concatenate(
        [up[:tm - SUBLANES], jnp.where(sub == SUBLANES - 1, gnext_ref[0:1, :], up[tm - SUBLANES:])], axis=0)
    conv = (g_prev * keep_prev * cw_ref[0:1, :] + g * cw_ref[1:2, :] + g_next * keep_next * cw_ref[2:3, :]
            + cb_ref[...])
    u = (_silu(conv) * a_ref[...]).astype(bf16)
    part = _dot(u, w_ref[...])

    @pl.when(k == 0)
    def _():
        o_ref[...] = part

    @pl.when(k > 0)
    def _():
        o_ref[...] += part

    @pl.when(k == pl.num_programs(1) - 1)
    def _():
        o_ref[...] = x_ref[...] + gate_ref[...] * o_ref[...]


def _ffn_down(ag, conv_w, conv_b, w_down, x, gate, group_of_tile, tm, tk, m_ctx, s_len, t_len):
    m, d = x.shape
    f = w_down.shape[0]
    nk = f // tk
    nsub = tm // SUBLANES
    last_sub = m // SUBLANES - 1
    return pl.pallas_call(
        functools.partial(_ffn_down_kernel, m_ctx=m_ctx, s_len=s_len, t_len=t_len),
        grid=(m // tm, nk),
        in_specs=[
            pl.BlockSpec((tm, tk), lambda i, k: (i, k)),
            pl.BlockSpec((tm, tk), lambda i, k: (i, nk + k)),
            pl.BlockSpec((SUBLANES, tk), lambda i, k: (jnp.maximum(i * nsub - 1, 0), nk + k)),
            pl.BlockSpec((SUBLANES, tk), lambda i, k: (jnp.minimum((i + 1) * nsub, last_sub), nk + k)),
            pl.BlockSpec((3, tk), lambda i, k: (0, k)),
            pl.BlockSpec((1, tk), lambda i, k: (0, k)),
            pl.BlockSpec((tk, d), lambda i, k: (k, 0)),
            pl.BlockSpec((tm, d), lambda i, k: (i, 0)),
            pl.BlockSpec((None, 1, d), lambda i, k: (group_of_tile(i), 0, 0)),
        ],
        out_specs=pl.BlockSpec((tm, d), lambda i, k: (i, 0)),
        out_shape=jax.ShapeDtypeStruct((m, d), f32),
        compiler_params=_params(("arbitrary", "arbitrary")),
        name="ffn_down",
    )(ag, ag, ag, ag, conv_w, conv_b.reshape(1, f), w_down, x, gate)


def kernel(x_prompt, x_sample, cache_mla_ckv, cache_mla_kpe, cache_na_k, cache_na_v, state_hgrn, c, c_ctx, w_mod, b_mod, norm1_g, norm2_g, w_in, mla_kv_norm_g, mla_q_norm_g, mla_k_norm_g, mla_w_uk, mla_w_uv, hgrn_lower_bounds, hgrn_norm_g, na_q_norm_g, na_k_norm_g, na_rpb, w_branch, w_out, ffn_w_up, ffn_conv_w, ffn_conv_b, ffn_w_down):
    n_ctx, s_len, d = x_prompt.shape
    n_lat, t_len, _ = x_sample.shape
    depth = w_in.shape[0]
    p_len = cache_mla_ckv.shape[2]
    ffn = ffn_w_down.shape[1]
    m_ctx, m_lat = n_ctx * s_len, n_lat * t_len
    m = m_ctx + m_lat
    assert m_ctx % t_len == 0, "context rows must be a whole number of latent sequences"

    tm = _pow2_tile(1024, m_ctx, t_len)
    tn = _pow2_tile(1024, d)
    tp = _pow2_tile(512, m_ctx, t_len)
    lay = _ZLayout(d, tn)
    n_ctx_tiles = m_ctx // tm
    tiles_per_seq = t_len // tm

    def group_of_tile(i):
        return jnp.where(i < n_ctx_tiles, 0, 1 + (i - n_ctx_tiles) // tiles_per_seq)

    n_groups = -(-(1 + n_lat) // SUBLANES) * SUBLANES
    cond = jnp.concatenate([c_ctx[None], c, jnp.zeros((n_groups - 1 - n_lat, d), f32)], axis=0)
    mods = _modulation(cond, w_mod, b_mod).reshape(depth, n_groups, 6, d).transpose(0, 2, 1, 3)[:, :, :, None, :]

    sm = jax.nn.softmax(hgrn_lower_bounds.astype(f32), axis=1)
    csum = jnp.cumsum(sm, axis=1)
    lower = csum - csum[:, :1]
    hg_tabs = tuple(jnp.asarray(t, dt) for t, dt in zip(_hgrn_tables(), (bf16, f32, bf16)))

    cos, sin = _rope_tables(t_len, tp)
    n_ctx_tp = m_ctx // tp

    def rope_block(i):
        return jnp.where(i < n_ctx_tp, 0, 1 + (i - n_ctx_tp) % (t_len // tp))

    x = jnp.concatenate([x_prompt.reshape(m_ctx, d), x_sample.reshape(m_lat, d)], axis=0)
    st_ckv, st_kpe, st_nk, st_nv, st_hg = [], [], [], [], []
    for l in range(depth):
        sh1, sc1, g1, sh2, sc2, g2 = (mods[l, k] for k in range(6))
        w_in_l = _permute_w_in(w_in[l], lay)
        z = _normmod_matmul(x, norm1_g[l], sc1, sh1, w_in_l, group_of_tile, tm, tn, "in_proj")

        w_uk = mla_w_uk[l].reshape(MLA_KV_RANK, MLA_HEADS * MLA_NOPE).astype(bf16)
        w_uv = mla_w_uv[l].reshape(MLA_KV_RANK, MLA_HEADS * MLA_V).astype(bf16)
        q_mla = _mla_q(z, lay, mla_q_norm_g[l], cos, sin, rope_block, tp)
        k_mla, v_mla, ckv_n = _mla_kv(z, lay.ckv, z, lay.kpe, mla_kv_norm_g[l], mla_k_norm_g[l], w_uk, w_uv,
                                      cos, sin, rope_block, tp, True, True, "mla_kv")
        ckv_c = cache_mla_ckv[:, l].reshape(n_lat * p_len, MLA_KV_RANK)
        kpe_c = jnp.pad(cache_mla_kpe[:, l].reshape(n_lat * p_len, MLA_ROPE), ((0, 0), (0, LANES - MLA_ROPE)))
        tc = _pow2_tile(512, n_lat * p_len)
        k_c, v_c, _ = _mla_kv(ckv_c, 0, kpe_c, 0, mla_kv_norm_g[l], mla_k_norm_g[l], w_uk, w_uv,
                              cos, sin, lambda i: 0, tc, False, False, "mla_kv_cache")
        tq_c = _pow2_tile(256, s_len)
        o_mla_c = _flash(q_mla, k_mla, v_mla, batch=n_ctx, heads=MLA_HEADS, t_len=s_len, s_len=s_len, row0=0,
                         dq=MLA_QPAD, dv=MLA_V, tq=tq_c, tk=tq_c, heads_per_step=MLA_HEADS, name="mla_attn_ctx")
        tq_l = _pow2_tile(512, t_len, p_len)
        o_mla_l = _flash(q_mla, k_mla, v_mla, batch=n_lat, heads=MLA_HEADS, t_len=t_len, s_len=t_len, row0=m_ctx,
                         dq=MLA_QPAD, dv=MLA_V, tq=tq_l, tk=tq_l, heads_per_step=2, extra=(k_c, v_c, p_len),
                         name="mla_attn_latent")

        lb = lower[:, l]
        lb_tabs = (jnp.log(lb) * LOG2E, jnp.log1p(-lb) * LOG2E, 1.0 - lb)
        o_hg_c, s_hg = _hgrn(z, lay, lb_tabs, hgrn_norm_g[l], hg_tabs, batch=n_ctx, t_len=s_len, row0=0,
                             s0=None, emit_state=True)
        o_hg_l, _ = _hgrn(z, lay, lb_tabs, hgrn_norm_g[l], hg_tabs, batch=n_lat, t_len=t_len, row0=m_ctx,
                          s0=state_hgrn[:, l], emit_state=False)

        q_na, k_na, v_na, k_na_f = _na_prep(z, lay, na_q_norm_g[l], na_k_norm_g[l], tp)
        o_na_c = _flash(q_na, k_na, v_na, batch=n_ctx, heads=NA_HEADS, t_len=s_len, s_len=s_len, row0=0,
                        dq=NA_DH, dv=NA_DH, tq=tq_c, tk=tq_c, heads_per_step=NA_HEADS, name="na_attn_ctx")
        kc_na = cache_na_k[:, l].reshape(n_lat * p_len, NA_HEADS * NA_DH).astype(bf16)
        vc_na = cache_na_v[:, l].reshape(n_lat * p_len, NA_HEADS * NA_DH).astype(bf16)
        bias = _na_bias_table(na_rpb[l], t_len // GRID_W)
        o_na_l = _na_latent(q_na, k_na, v_na, kc_na, vc_na, bias, batch=n_lat, t_len=t_len, p_len=p_len, row0=m_ctx)

        merged = _merge((o_mla_c, o_mla_l), (o_hg_c, o_hg_l), (o_na_c, o_na_l), z, lay, w_branch[l].astype(bf16), tp, tn)
        x = _proj_residual(merged, w_out[l].astype(bf16), x, g1, group_of_tile, tm, tn)

        ag = _normmod_matmul(x, norm2_g[l], sc2, sh2, ffn_w_up[l].astype(bf16), group_of_tile, tm,
                             _pow2_tile(1024, 2 * ffn), "ffn_up")
        tmd = _pow2_tile(1024, m_ctx, t_len)
        x = _ffn_down(ag, ffn_conv_w[l], ffn_conv_b[l], ffn_w_down[l].astype(bf16), x, g2,
                      lambda i: jnp.where(i < m_ctx // tmd, 0, 1 + (i - m_ctx // tmd) // (t_len // tmd)),
                      tmd, _pow2_tile(512, ffn), m_ctx, s_len, t_len)

        st_ckv.append(ckv_n[:m_ctx].reshape(n_ctx, s_len, MLA_KV_RANK))
        st_kpe.append(z[:m_ctx, lay.kpe:lay.kpe + MLA_ROPE].reshape(n_ctx, s_len, MLA_ROPE))
        st_nk.append(k_na_f[:m_ctx].reshape(n_ctx, s_len, NA_HEADS, NA_DH))
        st_nv.append(z[:m_ctx, lay.nv:lay.nv + NA_HEADS * NA_DH].reshape(n_ctx, s_len, NA_HEADS, NA_DH))
        st_hg.append(s_hg)

    y_prompt = x[:m_ctx].reshape(n_ctx, s_len, d)
    y_sample = x[m_ctx:].reshape(n_lat, t_len, d)
    return (y_prompt, y_sample, jnp.stack(st_ckv, axis=1), jnp.stack(st_kpe, axis=1), jnp.stack(st_nk, axis=1),
            jnp.stack(st_nv, axis=1), jnp.stack(st_hg, axis=1))
```

```python
import functools
import math

import numpy as np
import jax
import jax.numpy as jnp
from jax import lax
from jax.experimental import pallas as pl
from jax.experimental.pallas import tpu as pltpu

f32 = jnp.float32
bf16 = jnp.bfloat16

GRID_W = 64
MLA_HEADS, MLA_NOPE, MLA_ROPE, MLA_V, MLA_KV_RANK = 8, 128, 64, 128, 512
MLA_QK = MLA_NOPE + MLA_ROPE
HG_HEADS, HG_K, HG_V = 8, 128, 128
NA_HEADS, NA_DH, NA_WIN_R, NA_WIN_C = 8, 128, 8, 16
ROPE_BASE = 10000.0
EPS = 1e-6
NEG = -1e30
LOG2E = 1.4426950408889634

LANES = 128
SUBLANES = 8
VMEM_LIMIT_BYTES = 56 * 1024 * 1024

HEAD_W = 128
MLA_QPAD = 256
HG_CHUNK = 128
HG_LEVELS = int(math.log2(HG_CHUNK))
NA_QROWS = 8
NA_KROWS = 16


def _params(semantics):
    return pltpu.CompilerParams(dimension_semantics=semantics, vmem_limit_bytes=VMEM_LIMIT_BYTES)


def _pow2_tile(pref, *dims):
    t = pref
    while any(d % t for d in dims):
        t //= 2
    assert t >= SUBLANES, (pref, dims)
    return t


def _dot(a, b):
    return jnp.dot(a, b, preferred_element_type=f32)


def _dot_nt(a, b):
    return lax.dot_general(a, b, (((1,), (1,)), ((), ())), preferred_element_type=f32)


def _dot_tn(a, b):
    return lax.dot_general(a, b, (((0,), (0,)), ((), ())), preferred_element_type=f32)


def _silu(x):
    return x * jax.nn.sigmoid(x)


def _mod_const(x, n):
    return jnp.bitwise_and(x, n - 1) if n & (n - 1) == 0 else lax.rem(x, n)


class _ZLayout:
    def __init__(self, d_model, tn):
        self.qn = 0
        self.qp = self.qn + MLA_HEADS * MLA_NOPE
        self.ckv = self.qp + MLA_HEADS * MLA_ROPE
        self.hq = self.ckv + MLA_KV_RANK
        self.hff = self.hq + HG_HEADS * HG_K
        self.hfb = self.hff + HG_HEADS * HG_K
        self.hi = self.hfb + HG_HEADS * HG_K
        self.hg = self.hi + HG_HEADS * HG_V
        self.nq = self.hg + HG_HEADS * HG_V
        self.nk = self.nq + NA_HEADS * NA_DH
        self.nv = self.nk + NA_HEADS * NA_DH
        self.ga = self.nv + NA_HEADS * NA_DH
        self.gb = self.ga + d_model
        self.gc = self.gb + d_model
        self.width = self.gc + d_model
        assert self.width % tn == 0


def _permute_w_in(w, lay):
    d = w.shape[0]
    n_mq = MLA_HEADS * MLA_QK
    mq = w[:, :n_mq].reshape(d, MLA_HEADS, MLA_QK)
    qn = mq[:, :, :MLA_NOPE].reshape(d, MLA_HEADS * MLA_NOPE)
    qp = mq[:, :, MLA_NOPE:].reshape(d, MLA_HEADS * MLA_ROPE)
    ckv = w[:, n_mq:n_mq + MLA_KV_RANK]
    kpe = w[:, n_mq + MLA_KV_RANK:n_mq + MLA_KV_RANK + MLA_ROPE]
    rest = w[:, n_mq + MLA_KV_RANK + MLA_ROPE:]
    main = jnp.concatenate([qn, qp, ckv, rest], axis=1).astype(bf16)
    assert main.shape[1] == lay.width
    return main, jnp.pad(kpe, ((0, 0), (0, LANES - MLA_ROPE))).astype(bf16)


def _mod_kernel(c_ref, w_ref, b_ref, o_ref):
    a = _silu(c_ref[...]).astype(bf16)
    o_ref[...] = _dot(a, w_ref[...].astype(bf16)) + b_ref[...]


def _modulation(cond, w_mod, b_mod):
    depth, d, n = w_mod.shape
    g = cond.shape[0]
    tn = _pow2_tile(1024, n)
    return pl.pallas_call(
        _mod_kernel,
        grid=(depth, n // tn),
        in_specs=[
            pl.BlockSpec((g, d), lambda l, j: (0, 0)),
            pl.BlockSpec((None, d, tn), lambda l, j: (l, 0, j)),
            pl.BlockSpec((None, 1, tn), lambda l, j: (l, 0, j)),
        ],
        out_specs=pl.BlockSpec((None, g, tn), lambda l, j: (l, 0, j)),
        out_shape=jax.ShapeDtypeStruct((depth, g, n), f32),
        compiler_params=_params(("arbitrary", "arbitrary")),
        name="modulation",
    )(cond, w_mod, b_mod.reshape(depth, 1, n))


def _normmod_matmul_kernel(x_ref, g_ref, sc_ref, sh_ref, w_ref, *rest):
    if len(rest) == 4:
        ws_ref, o_ref, os_ref, h_scr = rest
    else:
        (o_ref, h_scr), ws_ref, os_ref = rest, None, None

    @pl.when(pl.program_id(1) == 0)
    def _():
        x = x_ref[...]
        ms = jnp.mean(x * x, axis=-1, keepdims=True)
        y = x * lax.rsqrt(ms + EPS) * g_ref[...]
        h_scr[...] = (y * (1.0 + sc_ref[...]) + sh_ref[...]).astype(bf16)
        if ws_ref is not None:
            os_ref[...] = _dot(h_scr[...], ws_ref[...])

    o_ref[...] = _dot(h_scr[...], w_ref[...])


def _normmod_matmul(x, g, sc, sh, w, group_of_tile, tm, tn, name, w_side=None):
    m, d = x.shape
    n = w.shape[1]
    in_specs = [
        pl.BlockSpec((tm, d), lambda i, j: (i, 0)),
        pl.BlockSpec((1, d), lambda i, j: (0, 0)),
        pl.BlockSpec((None, 1, d), lambda i, j: (group_of_tile(i), 0, 0)),
        pl.BlockSpec((None, 1, d), lambda i, j: (group_of_tile(i), 0, 0)),
        pl.BlockSpec((d, tn), lambda i, j: (0, j)),
    ]
    args = [x, g.reshape(1, d), sc, sh, w]
    out_specs = [pl.BlockSpec((tm, tn), lambda i, j: (i, j))]
    out_shape = [jax.ShapeDtypeStruct((m, n), f32)]
    if w_side is not None:
        ns = w_side.shape[1]
        in_specs.append(pl.BlockSpec((d, ns), lambda i, j: (0, 0)))
        args.append(w_side)
        out_specs.append(pl.BlockSpec((tm, ns), lambda i, j: (i, 0)))
        out_shape.append(jax.ShapeDtypeStruct((m, ns), f32))
    res = pl.pallas_call(
        _normmod_matmul_kernel,
        grid=(m // tm, n // tn),
        in_specs=in_specs,
        out_specs=out_specs,
        out_shape=out_shape,
        scratch_shapes=[pltpu.VMEM((tm, d), bf16)],
        compiler_params=_params(("arbitrary", "arbitrary")),
        name=name,
    )(*args)
    return res if w_side is not None else res[0]


def _rope(x, cos, sin_signed):
    n = x.shape[-1]
    half = MLA_ROPE // 4
    lane = lax.broadcasted_iota(jnp.int32, x.shape, 1)
    partner = jnp.where((lane % (2 * half)) < half, pltpu.roll(x, n - half, 1), pltpu.roll(x, half, 1))
    return x * cos + partner * sin_signed


def _mla_q_kernel(qn_ref, qp_ref, gn_ref, gp_ref, cos_ref, sin_ref, o_ref, *, scale):
    qn = qn_ref[...]
    qp = qp_ref[...]
    tm = qn.shape[0]
    lane = lax.broadcasted_iota(jnp.int32, (tm, LANES), 1)
    low = lane < MLA_ROPE
    cos = cos_ref[...]
    sin = sin_ref[...]
    gn = gn_ref[...]
    gp = gp_ref[...]
    for pair in range(MLA_HEADS // 2):
        pcol = qp[:, pair * LANES:(pair + 1) * LANES]
        sq = pcol * pcol
        ss_all = jnp.sum(sq, axis=-1, keepdims=True)
        ss_low = jnp.sum(jnp.where(low, sq, 0.0), axis=-1, keepdims=True)
        rot = _rope(pcol * gp, cos, sin)
        for k, ss_pe in enumerate((ss_low, ss_all - ss_low)):
            h = 2 * pair + k
            nope = qn[:, h * MLA_NOPE:(h + 1) * MLA_NOPE]
            ss = jnp.sum(nope * nope, axis=-1, keepdims=True) + ss_pe
            r = lax.rsqrt(ss * (1.0 / MLA_QK) + EPS) * scale
            pe = rot if k == 0 else pltpu.roll(rot, MLA_ROPE, 1)
            o_ref[:, h * MLA_QPAD:h * MLA_QPAD + MLA_NOPE] = (nope * r * gn).astype(bf16)
            o_ref[:, h * MLA_QPAD + MLA_NOPE:(h + 1) * MLA_QPAD] = jnp.where(low, pe * r, 0.0).astype(bf16)


class _CtxState:
    def __init__(self, prevs, widths, layer, depth, n_ctx, s_len, tm):
        assert tm % s_len == 0
        self.n_seq = tm // s_len
        self.s_len = s_len
        self.n_tiles = n_ctx * s_len // tm
        self.prevs = [] if prevs is None else list(prevs)
        n_tiles = self.n_tiles
        self.out_specs = [pl.BlockSpec((self.n_seq, None, s_len, w), lambda i: (jnp.minimum(i, n_tiles - 1), layer, 0, 0))
                          for w in widths]
        self.out_shape = [jax.ShapeDtypeStruct((n_ctx, depth, s_len, w), f32) for w in widths]
        self.in_specs = [pl.BlockSpec(memory_space=pl.ANY)] * len(self.prevs)

    def aliases(self, n_inputs_before, n_outputs_before):
        return {n_inputs_before + k: n_outputs_before + k for k in range(len(self.prevs))}

    def kernel_kwargs(self):
        return dict(n_prev=len(self.prevs), n_ctx_tiles=self.n_tiles, s_len=self.s_len)


def _mla_kv_kernel(*refs, norm_ckv, rope, n_prev=0, n_ctx_tiles=0, s_len=0):
    ckv_ref, kpe_ref, kvg_ref, gn_ref, gp_ref, wuk_ref, wuv_ref, cos_ref, sin_ref = refs[:9]
    k_ref, v_ref, *state_refs = refs[9 + n_prev:]
    ckv = ckv_ref[...]
    if norm_ckv:
        ms = jnp.mean(ckv * ckv, axis=-1, keepdims=True)
        ckv = ckv * lax.rsqrt(ms + EPS) * kvg_ref[...]
    if state_refs:
        ckv_state_ref, kpe_state_ref = state_refs

        @pl.when(pl.program_id(0) < n_ctx_tiles)
        def _():
            for n in range(ckv.shape[0] // s_len):
                ckv_state_ref[n] = ckv[n * s_len:(n + 1) * s_len]
                kpe_state_ref[n] = kpe_ref[n * s_len:(n + 1) * s_len, :MLA_ROPE]

    cb = ckv.astype(bf16)
    kn = _dot(cb, wuk_ref[...])
    v_ref[...] = _dot(cb, wuv_ref[...]).astype(bf16)
    tm = ckv.shape[0]
    lane = lax.broadcasted_iota(jnp.int32, (tm, LANES), 1)
    low = lane < MLA_ROPE
    kpe = jnp.where(low, kpe_ref[...], 0.0)
    ss_pe = jnp.sum(kpe * kpe, axis=-1, keepdims=True)
    pe = kpe * gp_ref[...]
    if rope:
        pe = _rope(pe, cos_ref[...], sin_ref[...])
    gn = gn_ref[...]
    for h in range(MLA_HEADS):
        nope = kn[:, h * MLA_NOPE:(h + 1) * MLA_NOPE]
        ss = jnp.sum(nope * nope, axis=-1, keepdims=True) + ss_pe
        r = lax.rsqrt(ss * (1.0 / MLA_QK) + EPS)
        k_ref[:, h * MLA_QPAD:h * MLA_QPAD + MLA_NOPE] = (nope * r * gn).astype(bf16)
        k_ref[:, h * MLA_QPAD + MLA_NOPE:(h + 1) * MLA_QPAD] = jnp.where(low, pe * r, 0.0).astype(bf16)


def _rope_tables(t_len, tm):
    n_freq = MLA_ROPE // 4
    t = jnp.arange(t_len, dtype=jnp.int32)
    inv = ROPE_BASE ** (-jnp.arange(n_freq, dtype=f32) / n_freq)
    ang_r = (t // GRID_W).astype(f32)[:, None] * inv
    ang_c = (t % GRID_W).astype(f32)[:, None] * inv
    cos64 = jnp.concatenate([jnp.cos(ang_r), jnp.cos(ang_r), jnp.cos(ang_c), jnp.cos(ang_c)], axis=1)
    sin64 = jnp.concatenate([-jnp.sin(ang_r), jnp.sin(ang_r), -jnp.sin(ang_c), jnp.sin(ang_c)], axis=1)
    cos = jnp.concatenate([jnp.ones((tm, LANES), f32), jnp.tile(cos64, (1, 2))], axis=0)
    sin = jnp.concatenate([jnp.zeros((tm, LANES), f32), jnp.tile(sin64, (1, 2))], axis=0)
    return cos, sin


def _mla_q(z, lay, q_g, cos, sin, rope_block, tm):
    m = z.shape[0]
    gn = q_g[:MLA_NOPE].reshape(1, MLA_NOPE)
    gp = jnp.tile(q_g[MLA_NOPE:], 2).reshape(1, LANES)
    wn, wp = MLA_HEADS * MLA_NOPE, MLA_HEADS * MLA_ROPE
    return pl.pallas_call(
        functools.partial(_mla_q_kernel, scale=MLA_QK ** -0.5 * LOG2E),
        grid=(m // tm,),
        in_specs=[
            pl.BlockSpec((tm, wn), lambda i: (i, lay.qn // wn)),
            pl.BlockSpec((tm, wp), lambda i: (i, lay.qp // wp)),
            pl.BlockSpec((1, MLA_NOPE), lambda i: (0, 0)),
            pl.BlockSpec((1, LANES), lambda i: (0, 0)),
            pl.BlockSpec((tm, LANES), lambda i: (rope_block(i), 0)),
            pl.BlockSpec((tm, LANES), lambda i: (rope_block(i), 0)),
        ],
        out_specs=pl.BlockSpec((tm, MLA_HEADS * MLA_QPAD), lambda i: (i, 0)),
        out_shape=jax.ShapeDtypeStruct((m, MLA_HEADS * MLA_QPAD), bf16),
        compiler_params=_params(("arbitrary",)),
        name="mla_q",
    )(z, z, gn, gp, cos, sin)


def _mla_kv(ckv_src, ckv_col, kpe_src, kpe_col, kv_g, k_g, w_uk, w_uv, cos, sin, rope_block, tm, norm_ckv, rope, name,
            state=None):
    m = ckv_src.shape[0]
    gn = k_g[:MLA_NOPE].reshape(1, MLA_NOPE)
    gp = jnp.concatenate([k_g[MLA_NOPE:], jnp.zeros((LANES - MLA_ROPE,), f32)]).reshape(1, LANES)
    hw = MLA_HEADS * MLA_NOPE
    in_specs = [
        pl.BlockSpec((tm, MLA_KV_RANK), lambda i: (i, ckv_col // MLA_KV_RANK)),
        pl.BlockSpec((tm, LANES), lambda i: (i, kpe_col // LANES)),
        pl.BlockSpec((1, MLA_KV_RANK), lambda i: (0, 0)),
        pl.BlockSpec((1, MLA_NOPE), lambda i: (0, 0)),
        pl.BlockSpec((1, LANES), lambda i: (0, 0)),
        pl.BlockSpec((MLA_KV_RANK, hw), lambda i: (0, 0)),
        pl.BlockSpec((MLA_KV_RANK, hw), lambda i: (0, 0)),
        pl.BlockSpec((tm, LANES), lambda i: (rope_block(i), 0)),
        pl.BlockSpec((tm, LANES), lambda i: (rope_block(i), 0)),
    ]
    args = [ckv_src, kpe_src, kv_g.reshape(1, MLA_KV_RANK), gn, gp, w_uk, w_uv, cos, sin]
    out_specs = [pl.BlockSpec((tm, MLA_HEADS * MLA_QPAD), lambda i: (i, 0)), pl.BlockSpec((tm, hw), lambda i: (i, 0))]
    out_shape = [jax.ShapeDtypeStruct((m, MLA_HEADS * MLA_QPAD), bf16), jax.ShapeDtypeStruct((m, hw), bf16)]
    kwargs, aliases = {}, {}
    if state is not None:
        aliases = state.aliases(len(args), len(out_specs))
        kwargs = state.kernel_kwargs()
        in_specs += state.in_specs
        args += state.prevs
        out_specs += state.out_specs
        out_shape += state.out_shape
    return pl.pallas_call(
        functools.partial(_mla_kv_kernel, norm_ckv=norm_ckv, rope=rope, **kwargs),
        grid=(m // tm,),
        in_specs=in_specs,
        out_specs=out_specs,
        out_shape=out_shape,
        input_output_aliases=aliases,
        compiler_params=_params(("arbitrary",)),
        name=name,
    )(*args)


def _flash_kernel(*refs, n_main, tk, n_extra, heads, dq, dv):
    if n_extra:
        q_ref, k_ref, v_ref, kx_ref, vx_ref, o_ref = refs
    else:
        q_ref, k_ref, v_ref, o_ref = refs
        kx_ref = vx_ref = None
    tq = q_ref.shape[0]
    chunks = [(k_ref, v_ref, c) for c in range(n_main)] + [(kx_ref, vx_ref, c) for c in range(n_extra)]
    def scores(i, g):
        kr, _, c = chunks[i]
        return _dot_nt(kr[c * tk:(c + 1) * tk, g * dq:(g + 1) * dq], q_ref[:, g * dq:(g + 1) * dq])

    m = [jnp.full((1, tq), -jnp.inf, f32)] * heads
    l = [jnp.zeros((1, tq), f32)] * heads
    acc = [jnp.zeros((dv, tq), f32)] * heads
    s = [scores(0, g) for g in range(heads)]
    for i, (_, vr, c) in enumerate(chunks):
        for g in range(heads):
            s_next = scores(i + 1, g) if i + 1 < len(chunks) else None
            m_new = jnp.maximum(m[g], jnp.max(s[g], axis=0, keepdims=True))
            alpha = jnp.exp2(m[g] - m_new)
            p = jnp.exp2(s[g] - m_new)
            l[g] = alpha * l[g] + jnp.sum(p, axis=0, keepdims=True)
            acc[g] = alpha * acc[g] + _dot_tn(vr[c * tk:(c + 1) * tk, g * dv:(g + 1) * dv], p.astype(bf16))
            m[g], s[g] = m_new, s_next
    for g in range(heads):
        o_ref[:, g * dv:(g + 1) * dv] = (acc[g] / l[g]).T.astype(o_ref.dtype)


def _flash(q, k, v, *, batch, heads, t_len, s_len, row0, dq, dv, tq, tk, heads_per_step=1, extra=None, name):
    assert row0 % tq == 0 and row0 % s_len == 0 and t_len % tq == 0 and s_len % tk == 0
    assert heads % heads_per_step == 0
    nq = t_len // tq
    g = heads_per_step
    in_specs = [
        pl.BlockSpec((tq, g * dq), lambda b, h, i: (row0 // tq + b * nq + i, h)),
        pl.BlockSpec((s_len, g * dq), lambda b, h, i: (row0 // s_len + b, h)),
        pl.BlockSpec((s_len, g * dv), lambda b, h, i: (row0 // s_len + b, h)),
    ]
    args = [q, k, v]
    n_extra = 0
    if extra is not None:
        kx, vx, p_len = extra
        assert p_len % tk == 0
        n_extra = p_len // tk
        in_specs += [
            pl.BlockSpec((p_len, g * dq), lambda b, h, i: (b, h)),
            pl.BlockSpec((p_len, g * dv), lambda b, h, i: (b, h)),
        ]
        args += [kx, vx]
    return pl.pallas_call(
        functools.partial(_flash_kernel, n_main=s_len // tk, tk=tk, n_extra=n_extra, heads=g, dq=dq, dv=dv),
        grid=(batch, heads // g, nq),
        in_specs=in_specs,
        out_specs=pl.BlockSpec((tq, g * dv), lambda b, h, i: (b * nq + i, h)),
        out_shape=jax.ShapeDtypeStruct((batch * t_len, heads * dv), bf16),
        compiler_params=_params(("arbitrary", "arbitrary", "arbitrary")),
        name=name,
    )(*args)


def _na_prep_kernel(*refs, scale, n_prev, n_ctx_tiles, s_len):
    nq_ref, nk_ref, nv_ref, qg_ref, kg_ref = refs[:5]
    q_ref, k_ref, v_ref, k_state_ref, v_state_ref = refs[5 + n_prev:]
    qg = qg_ref[...]
    kg = kg_ref[...]
    tm = nv_ref.shape[0]
    is_ctx = pl.program_id(0) < n_ctx_tiles
    v_ref[...] = nv_ref[...].astype(bf16)

    @pl.when(is_ctx)
    def _():
        for n in range(tm // s_len):
            v_state_ref[n] = nv_ref[n * s_len:(n + 1) * s_len, :]

    for h in range(NA_HEADS):
        sl = slice(h * NA_DH, (h + 1) * NA_DH)
        q = nq_ref[:, sl]
        k = nk_ref[:, sl]
        rq = lax.rsqrt(jnp.mean(q * q, axis=-1, keepdims=True) + EPS)
        rk = lax.rsqrt(jnp.mean(k * k, axis=-1, keepdims=True) + EPS)
        q_ref[:, sl] = (q * rq * qg * scale).astype(bf16)
        kn = k * rk * kg
        k_ref[:, sl] = kn.astype(bf16)

        @pl.when(is_ctx)
        def _():
            for n in range(tm // s_len):
                k_state_ref[n, :, sl] = kn[n * s_len:(n + 1) * s_len]


def _na_prep(z, lay, q_g, k_g, tm, state):
    m = z.shape[0]
    w = NA_HEADS * NA_DH
    spec = pl.BlockSpec((tm, w), lambda i: (i, 0))
    in_specs = [
        pl.BlockSpec((tm, w), lambda i: (i, lay.nq // w)),
        pl.BlockSpec((tm, w), lambda i: (i, lay.nk // w)),
        pl.BlockSpec((tm, w), lambda i: (i, lay.nv // w)),
        pl.BlockSpec((1, NA_DH), lambda i: (0, 0)),
        pl.BlockSpec((1, NA_DH), lambda i: (0, 0)),
    ]
    args = [z, z, z, q_g.reshape(1, NA_DH), k_g.reshape(1, NA_DH)]
    return pl.pallas_call(
        functools.partial(_na_prep_kernel, scale=NA_DH ** -0.5 * LOG2E, **state.kernel_kwargs()),
        grid=(m // tm,),
        in_specs=in_specs + state.in_specs,
        out_specs=[spec, spec, spec] + state.out_specs,
        out_shape=[jax.ShapeDtypeStruct((m, w), bf16)] * 3 + state.out_shape,
        input_output_aliases=state.aliases(len(args), 3),
        compiler_params=_params(("arbitrary",)),
        name="na_prep",
    )(*args, *state.prevs)


def _na_static_maps(rows):
    nblk = rows // NA_QROWS
    reps = (0, min(1, nblk - 1), nblk - 1)
    dr_map = np.full((3, NA_QROWS, NA_KROWS), 2 * NA_WIN_R - 1, np.int32)
    for v, kb in enumerate(reps):
        ws = min(max(NA_QROWS * kb - NA_WIN_R // 2, 0), rows - NA_KROWS)
        for i in range(NA_QROWS):
            qrow = NA_QROWS * kb + i
            rs = min(max(qrow - NA_WIN_R // 2, 0), rows - NA_WIN_R)
            for j in range(NA_KROWS):
                krow = ws + j
                if rs <= krow < rs + NA_WIN_R:
                    dr_map[v, i, j] = krow - qrow + NA_WIN_R - 1
    qcol = np.arange(GRID_W)
    cs = np.clip(qcol - NA_WIN_C // 2, 0, GRID_W - NA_WIN_C)
    in_win = (qcol[None, :] >= cs[:, None]) & (qcol[None, :] < cs[:, None] + NA_WIN_C)
    dc_idx = np.clip(qcol[None, :] - qcol[:, None], -(NA_WIN_C - 1), NA_WIN_C - 1) + (NA_WIN_C - 1)
    return dr_map, in_win, dc_idx


def _na_bias_table(rpb, rows):
    dr_map, in_win, dc_idx = _na_static_maps(rows)
    h = rpb.shape[0]
    bt = jnp.where(in_win.T[None, None], jnp.take(rpb * LOG2E, jnp.asarray(dc_idx.T), axis=2), NEG)
    bt = jnp.concatenate([bt, jnp.full((h, 1, GRID_W, GRID_W), NEG, f32)], axis=1)
    tab = jnp.take(bt, jnp.asarray(dr_map.transpose(0, 2, 1).reshape(-1)), axis=1)
    tab = tab.reshape(h, 3, NA_KROWS, NA_QROWS, GRID_W, GRID_W).transpose(1, 0, 2, 4, 3, 5)
    return tab.reshape(3, h, NA_KROWS * GRID_W, NA_QROWS * GRID_W)


def _na_kernel(q_ref, k_ref, v_ref, kc_ref, vc_ref, bias_ref, o_ref, *, rows):
    kb = pl.program_id(2)
    ws = jnp.clip(NA_QROWS * kb - NA_WIN_R // 2, 0, rows - NA_KROWS)
    off = pl.multiple_of(ws * GRID_W, (NA_WIN_R // 2) * GRID_W)
    n_keys = NA_KROWS * GRID_W
    q = q_ref[...]
    s_loc = _dot_nt(k_ref[pl.ds(off, n_keys), :], q) + bias_ref[...]
    s_ctx = _dot_nt(kc_ref[...], q)
    m = jnp.maximum(jnp.max(s_loc, axis=0, keepdims=True), jnp.max(s_ctx, axis=0, keepdims=True))
    p_loc = jnp.exp2(s_loc - m)
    p_ctx = jnp.exp2(s_ctx - m)
    l = jnp.sum(p_loc, axis=0, keepdims=True) + jnp.sum(p_ctx, axis=0, keepdims=True)
    acc = _dot_tn(v_ref[pl.ds(off, n_keys), :], p_loc.astype(bf16)) + _dot_tn(vc_ref[...], p_ctx.astype(bf16))
    o_ref[...] = (acc / l).T.astype(o_ref.dtype)


def _na_latent(q, k, v, kc, vc, bias, *, batch, t_len, p_len, row0):
    rows = t_len // GRID_W
    assert t_len % GRID_W == 0 and rows % NA_QROWS == 0 and rows >= NA_KROWS
    tq = NA_QROWS * GRID_W
    assert row0 % tq == 0 and row0 % t_len == 0
    nblk = rows // NA_QROWS

    def variant(i):
        return jnp.where(i == 0, 0, jnp.where(i == nblk - 1, 2, 1))

    return pl.pallas_call(
        functools.partial(_na_kernel, rows=rows),
        grid=(batch, NA_HEADS, nblk),
        in_specs=[
            pl.BlockSpec((tq, NA_DH), lambda b, h, i: (row0 // tq + b * nblk + i, h)),
            pl.BlockSpec((t_len, NA_DH), lambda b, h, i: (row0 // t_len + b, h)),
            pl.BlockSpec((t_len, NA_DH), lambda b, h, i: (row0 // t_len + b, h)),
            pl.BlockSpec((p_len, NA_DH), lambda b, h, i: (b, h)),
            pl.BlockSpec((p_len, NA_DH), lambda b, h, i: (b, h)),
            pl.BlockSpec((None, None, NA_KROWS * GRID_W, tq), lambda b, h, i: (variant(i), h, 0, 0)),
        ],
        out_specs=pl.BlockSpec((tq, NA_DH), lambda b, h, i: (b * nblk + i, h)),
        out_shape=jax.ShapeDtypeStruct((batch * t_len, NA_HEADS * NA_DH), bf16),
        compiler_params=_params(("arbitrary", "arbitrary", "arbitrary")),
        name="na_latent",
    )(q, k, v, kc, vc, bias)


def _hgrn_tables():
    c, lv = HG_CHUNK, HG_LEVELS
    t = np.arange(c)[:, None]
    r = np.arange(c)[None, :]
    wcum = np.stack([r <= t, r >= t]).astype(np.float32)
    qside = np.zeros((2, lv * c, LANES), np.float32)
    mask = np.zeros((2, lv + 1, c, c), np.float32)
    mask[:, 0] = np.eye(c)
    for l in range(lv):
        hs = 1 << l
        blk = t // (2 * hs)
        mid = blk * (2 * hs) + hs
        qf = t >= mid
        qb = t < mid
        qside[0, l * c:(l + 1) * c] = np.where(qf, 1.0, -1.0)
        qside[1, l * c:(l + 1) * c] = np.where(qb, 1.0, -1.0)
        same = blk == blk.T
        mask[0, 1 + l] = same & qf & ~qf.T
        mask[1, 1 + l] = same & qb & ~qb.T
    return wcum, qside, mask


def _hgrn_kernel(*refs, n_chunks, heads, has_state, emit_state, n_prev=0):
    (hq_ref, hff_ref, hfb_ref, hi_ref, hg_ref, la_ref, lc_ref, om_ref, ng_ref,
     wcum_ref, qside_ref, mask_ref) = refs[:12]
    rest = list(refs[12:])
    s0_ref = rest.pop(0) if has_state else None
    del rest[:n_prev]
    o_ref = rest.pop(0)
    st_ref = rest.pop(0) if emit_state else None
    o_scr, s_scr, b_scr = rest
    c, lv = HG_CHUNK, HG_LEVELS

    o_scr[...] = jnp.zeros_like(o_scr)
    for d in range(2):
        for g in range(heads):
            if has_state:
                s_scr[d, g] = s0_ref[d, g].T
            else:
                s_scr[d, g] = jnp.zeros((HG_V, HG_K), f32)

    chains = [(d, g) for g in range(heads) for d in range(2)]
    low_half = lax.broadcasted_iota(jnp.int32, (SUBLANES, HEAD_W), 0) < SUBLANES // 2

    def gates_and_cumsum(ci, row):
        d, g = chains[ci]
        rs = pl.ds(pl.multiple_of(row, c), c)
        cs = slice(g * HEAD_W, (g + 1) * HEAD_W)
        x = (hff_ref if d == 0 else hfb_ref)[rs, cs]
        x2 = x * LOG2E
        u = jnp.exp2(-jnp.abs(x2))
        inv = 1.0 / (1.0 + u)
        pos = x >= 0.0
        log_sig = jnp.minimum(x2, 0.0) - jnp.log2(1.0 + u)
        a = la_ref[d:d + 1, cs]
        bb = lc_ref[d:d + 1, cs] + log_sig
        lf = jnp.maximum(a, bb) + jnp.log2(1.0 + jnp.exp2(-jnp.abs(a - bb)))
        om = om_ref[d:d + 1, cs]
        f = (1.0 - om) + om * (jnp.where(pos, 1.0, u) * inv)
        kin = om * (jnp.where(pos, u, 1.0) * inv)
        q = _silu(hq_ref[rs, cs])
        v = hi_ref[rs, cs].astype(bf16)
        l1 = lf.astype(bf16)
        r1 = lf - l1.astype(f32)
        l2 = r1.astype(bf16)
        l3 = (r1 - l2.astype(f32)).astype(bf16)
        w = wcum_ref[d]
        b = _dot(w, l1) + _dot(w, l2) + _dot(w, l3)
        b_scr[ci] = b
        return dict(rs=rs, cs=cs, f=f, kin=kin, q=q, v=v, b=b)

    def intra_chunk(ci, s):
        d, _ = chains[ci]
        b, q, kin = s["b"], s["q"], s["kin"]

        def row(r):
            return b_scr[ci, r:r + 1, :]

        s["b_end"] = row(c - 1 if d == 0 else 0)
        xs = [jnp.where(qside_ref[d, 0:c, :] > 0.0, q * s["f"], kin).astype(bf16)]
        for l in range(1, lv):
            hs = 1 << l
            blk = 2 * hs
            pick = hs - 1 if d == 0 else hs
            if hs >= SUBLANES:
                pieces = []
                for j in range(c // blk):
                    m = row(blk * j + pick)
                    early = slice(blk * j, blk * j + hs)
                    late = slice(blk * j + hs, blk * (j + 1))
                    q_sl, k_sl = (late, early) if d == 0 else (early, late)
                    xq = q[q_sl] * jnp.exp2(b[q_sl] - m)
                    xk = kin[k_sl] * jnp.exp2(m - b[k_sl])
                    pieces += [xk, xq] if d == 0 else [xq, xk]
                xs.append(jnp.concatenate(pieces, axis=0).astype(bf16))
                continue
            if blk < SUBLANES:
                pieces = [jnp.where(low_half, jnp.broadcast_to(row(SUBLANES * j + pick), (SUBLANES, HEAD_W)),
                                    jnp.broadcast_to(row(SUBLANES * j + blk + pick), (SUBLANES, HEAD_W)))
                          for j in range(c // SUBLANES)]
            else:
                pieces = [jnp.broadcast_to(row(blk * j + pick), (blk, HEAD_W)) for j in range(c // blk)]
            m = pieces[0] if len(pieces) == 1 else jnp.concatenate(pieces, axis=0)
            sign = qside_ref[d, l * c:(l + 1) * c, :]
            xs.append((jnp.where(sign > 0.0, q, kin) * jnp.exp2((b - m) * sign)).astype(bf16))
        att = mask_ref[d, 0] * _dot_nt(q.astype(bf16), kin.astype(bf16)).astype(bf16)
        for l in range(lv):
            att = att + mask_ref[d, 1 + l] * _dot_nt(xs[l], xs[l]).astype(bf16)
        s["att"] = att

    def state_step(ci, s):
        d, g = chains[ci]
        b, q, kin, v = s["b"], s["q"], s["kin"], s["v"]
        st = s_scr[d, g]
        o = _dot(s["att"], v) + _dot_nt((q * jnp.exp2(b)).astype(bf16), st.astype(bf16))
        b_end = s["b_end"]
        s_scr[d, g] = st * jnp.exp2(b_end) + _dot_tn(v, (kin * jnp.exp2(b_end - b)).astype(bf16))
        o_scr[s["rs"], s["cs"]] += o

    def body(i, _):
        rows = (i * c, (n_chunks - 1 - i) * c)
        states = [gates_and_cumsum(ci, rows[chains[ci][0]]) for ci in range(len(chains))]
        for ci, s in enumerate(states):
            intra_chunk(ci, s)
        for ci, s in enumerate(states):
            state_step(ci, s)
        return 0

    lax.fori_loop(0, n_chunks, body, 0)

    ng = ng_ref[...]
    for g in range(heads):
        cs = slice(g * HEAD_W, (g + 1) * HEAD_W)
        o = o_scr[:, cs]
        y = o * lax.rsqrt(jnp.mean(o * o, axis=-1, keepdims=True) + EPS) * ng
        o_ref[:, cs] = (y * _silu(hg_ref[:, cs])).astype(o_ref.dtype)
        if emit_state:
            for d in range(2):
                st_ref[d, g] = s_scr[d, g].T


def _hgrn(z, lay, lb_tabs, norm_g, tables, *, batch, t_len, row0, s0, state_out=None, heads_per_step=2):
    emit_state = state_out is not None
    assert t_len % HG_CHUNK == 0 and row0 % t_len == 0 and HG_K == HEAD_W and HG_V == HEAD_W
    g = heads_per_step
    gw = g * HEAD_W
    la, lc, om = lb_tabs
    wcum, qside, mask = tables
    r0 = row0 // t_len

    def zspec(col):
        return pl.BlockSpec((t_len, gw), lambda b, h: (r0 + b, col // gw + h))

    def const(shape):
        return pl.BlockSpec(shape, lambda b, h: (0,) * len(shape))

    in_specs = [zspec(lay.hq), zspec(lay.hff), zspec(lay.hfb), zspec(lay.hi), zspec(lay.hg),
                pl.BlockSpec((2, gw), lambda b, h: (0, h)), pl.BlockSpec((2, gw), lambda b, h: (0, h)),
                pl.BlockSpec((2, gw), lambda b, h: (0, h)), const((1, HG_V)),
                const(wcum.shape), const(qside.shape), const(mask.shape)]
    args = [z, z, z, z, z, la, lc, om, norm_g.reshape(1, HG_V), wcum, qside, mask]
    state_spec = pl.BlockSpec((None, 2, g, HG_K, HG_V), lambda b, h: (b, 0, h, 0, 0))
    if s0 is not None:
        in_specs.append(state_spec)
        args.append(s0)
    out_specs = [pl.BlockSpec((t_len, gw), lambda b, h: (b, h))]
    out_shape = [jax.ShapeDtypeStruct((batch * t_len, HG_HEADS * HG_V), bf16)]
    aliases, n_prev = {}, 0
    if emit_state:
        prev, layer, depth = state_out
        if prev is not None:
            aliases, n_prev = {len(args): 1}, 1
            in_specs.append(pl.BlockSpec(memory_space=pl.ANY))
            args.append(prev)
        out_specs.append(pl.BlockSpec((None, None, 2, g, HG_K, HG_V), lambda b, h: (b, layer, 0, h, 0, 0)))
        out_shape.append(jax.ShapeDtypeStruct((batch, depth, 2, HG_HEADS, HG_K, HG_V), f32))
    res = pl.pallas_call(
        functools.partial(_hgrn_kernel, n_chunks=t_len // HG_CHUNK, heads=g, has_state=s0 is not None,
                          emit_state=emit_state, n_prev=n_prev),
        grid=(batch, HG_HEADS // g),
        in_specs=in_specs,
        out_specs=out_specs,
        out_shape=out_shape,
        input_output_aliases=aliases,
        scratch_shapes=[pltpu.VMEM((t_len, gw), f32), pltpu.VMEM((2, g, HG_V, HG_K), f32),
                        pltpu.VMEM((2 * g, HG_CHUNK, HEAD_W), f32)],
        compiler_params=_params(("arbitrary", "arbitrary")),
        name="hgrn_ctx" if emit_state else "hgrn_latent",
    )(*args)
    return res if emit_state else (res[0], None)


def _merge_kernel(oac_ref, oal_ref, obc_ref, obl_ref, occ_ref, ocl_ref, ga_ref, gb_ref, gc_ref, w_ref, o_ref,
                  *, n_ctx_tiles):
    def compute(oa_ref, ob_ref, oc_ref):
        m = jax.nn.sigmoid(ga_ref[...]) * _dot(oa_ref[...], w_ref[0])
        m = m + jax.nn.sigmoid(gb_ref[...]) * _dot(ob_ref[...], w_ref[1])
        m = m + jax.nn.sigmoid(gc_ref[...]) * _dot(oc_ref[...], w_ref[2])
        o_ref[...] = m.astype(o_ref.dtype)

    is_ctx = pl.program_id(0) < n_ctx_tiles

    @pl.when(is_ctx)
    def _():
        compute(oac_ref, obc_ref, occ_ref)

    @pl.when(jnp.logical_not(is_ctx))
    def _():
        compute(oal_ref, obl_ref, ocl_ref)


def _merge(o_a, o_b, o_c, z, lay, w_branch, tm, tn):
    bw = o_a[0].shape[1]
    m = z.shape[0]
    d = w_branch.shape[2]
    nct = o_a[0].shape[0] // tm
    cspec = pl.BlockSpec((tm, bw), lambda i, j: (jnp.minimum(i, nct - 1), 0))
    lspec = pl.BlockSpec((tm, bw), lambda i, j: (jnp.maximum(i - nct, 0), 0))

    def gate(col):
        return pl.BlockSpec((tm, tn), lambda i, j: (i, col // tn + j))

    return pl.pallas_call(
        functools.partial(_merge_kernel, n_ctx_tiles=nct),
        grid=(m // tm, d // tn),
        in_specs=[cspec, lspec, cspec, lspec, cspec, lspec, gate(lay.ga), gate(lay.gb), gate(lay.gc),
                  pl.BlockSpec((3, bw, tn), lambda i, j: (0, 0, j))],
        out_specs=pl.BlockSpec((tm, tn), lambda i, j: (i, j)),
        out_shape=jax.ShapeDtypeStruct((m, d), bf16),
        compiler_params=_params(("arbitrary", "arbitrary")),
        name="merge",
    )(o_a[0], o_a[1], o_b[0], o_b[1], o_c[0], o_c[1], z, z, z, w_branch)


def _proj_residual_kernel(m_ref, w_ref, x_ref, g_ref, o_ref):
    o_ref[...] = x_ref[...] + g_ref[...] * _dot(m_ref[...], w_ref[...])


def _proj_residual(mm, w, x, gate, group_of_tile, tm, tn):
    m, k = mm.shape
    d = w.shape[1]
    return pl.pallas_call(
        _proj_residual_kernel,
        grid=(m // tm, d // tn),
        in_specs=[
            pl.BlockSpec((tm, k), lambda i, j: (i, 0)),
            pl.BlockSpec((k, tn), lambda i, j: (0, j)),
            pl.BlockSpec((tm, tn), lambda i, j: (i, j)),
            pl.BlockSpec((None, 1, tn), lambda i, j: (group_of_tile(i), 0, j)),
        ],
        out_specs=pl.BlockSpec((tm, tn), lambda i, j: (i, j)),
        out_shape=jax.ShapeDtypeStruct((m, d), f32),
        compiler_params=_params(("arbitrary", "arbitrary")),
        name="out_proj",
    )(mm, w, x, gate)


def _ffn_down_kernel(a_ref, g_ref, gprev_ref, gnext_ref, cw_ref, cb_ref, w_ref, x_ref, gate_ref, o_ref,
                     *, m_ctx, s_len, t_len, n_k):
    i = pl.program_id(0)
    k = pl.program_id(1)

    @pl.when(k == 0)
    def _():
        o_ref[...] = jnp.zeros_like(o_ref)

    g = g_ref[...]
    tm, tk = g.shape
    row = i * tm + lax.broadcasted_iota(jnp.int32, (tm, LANES), 0)
    pos = jnp.where(row < m_ctx, _mod_const(row, s_len), _mod_const(row - m_ctx, t_len))
    last = jnp.where(row < m_ctx, s_len - 1, t_len - 1)
    keep_prev = jnp.tile(jnp.where(pos == 0, 0.0, 1.0), (1, tk // LANES))
    keep_next = jnp.tile(jnp.where(pos == last, 0.0, 1.0), (1, tk // LANES))
    sub = lax.broadcasted_iota(jnp.int32, (SUBLANES, tk), 0)
    down = pltpu.roll(g, 1, 0)
    g_prev = jnp.concatenate(
        [jnp.where(sub == 0, gprev_ref[SUBLANES - 1:SUBLANES, :], down[:SUBLANES]), down[SUBLANES:]], axis=0)
    up = pltpu.roll(g, tm - 1, 0)
    g_next = jnp.concatenate(
        [up[:tm - SUBLANES], jnp.where(sub == SUBLANES - 1, gnext_ref[0:1, :], up[tm - SUBLANES:])], axis=0)
    conv = (g_prev * keep_prev * cw_ref[0:1, :] + g * cw_ref[1:2, :] + g_next * keep_next * cw_ref[2:3, :]
            + cb_ref[...])
    u = (_silu(conv) * a_ref[...]).astype(bf16)

    o_ref[...] = _dot(u, w_ref[...]) + o_ref[...]

    @pl.when(k == n_k - 1)
    def _():
        o_ref[...] = x_ref[...] + gate_ref[...] * o_ref[...]


def _ffn_down(ag, conv_w, conv_b, w_down, x, gate, group_of_tile, tm, tk, m_ctx, s_len, t_len):
    m, d = x.shape
    f = w_down.shape[0]
    nk = f // tk
    nsub = tm // SUBLANES
    last_sub = m // SUBLANES - 1
    return pl.pallas_call(
        functools.partial(_ffn_down_kernel, m_ctx=m_ctx, s_len=s_len, t_len=t_len, n_k=nk),
        grid=(m // tm, nk),
        in_specs=[
            pl.BlockSpec((tm, tk), lambda i, k: (i, k)),
            pl.BlockSpec((tm, tk), lambda i, k: (i, nk + k)),
            pl.BlockSpec((SUBLANES, tk), lambda i, k: (jnp.maximum(i * nsub - 1, 0), nk + k)),
            pl.BlockSpec((SUBLANES, tk), lambda i, k: (jnp.minimum((i + 1) * nsub, last_sub), nk + k)),
            pl.BlockSpec((3, tk), lambda i, k: (0, k)),
            pl.BlockSpec((1, tk), lambda i, k: (0, k)),
            pl.BlockSpec((tk, d), lambda i, k: (k, 0)),
            pl.BlockSpec((tm, d), lambda i, k: (i, 0)),
            pl.BlockSpec((None, 1, d), lambda i, k: (group_of_tile(i), 0, 0)),
        ],
        out_specs=pl.BlockSpec((tm, d), lambda i, k: (i, 0)),
        out_shape=jax.ShapeDtypeStruct((m, d), f32),
        compiler_params=_params(("arbitrary", "arbitrary")),
        name="ffn_down",
    )(ag, ag, ag, ag, conv_w, conv_b.reshape(1, f), w_down, x, gate)


def kernel(x_prompt, x_sample, cache_mla_ckv, cache_mla_kpe, cache_na_k, cache_na_v, state_hgrn, c, c_ctx, w_mod, b_mod, norm1_g, norm2_g, w_in, mla_kv_norm_g, mla_q_norm_g, mla_k_norm_g, mla_w_uk, mla_w_uv, hgrn_lower_bounds, hgrn_norm_g, na_q_norm_g, na_k_norm_g, na_rpb, w_branch, w_out, ffn_w_up, ffn_conv_w, ffn_conv_b, ffn_w_down):
    n_ctx, s_len, d = x_prompt.shape
    n_lat, t_len, _ = x_sample.shape
    depth = w_in.shape[0]
    p_len = cache_mla_ckv.shape[2]
    ffn = ffn_w_down.shape[1]
    m_ctx, m_lat = n_ctx * s_len, n_lat * t_len
    m = m_ctx + m_lat
    assert m_ctx % t_len == 0, "context rows must be a whole number of latent sequences"

    tm = _pow2_tile(1024, m_ctx, t_len)
    tn = _pow2_tile(1024, d)
    tp = _pow2_tile(512, m_ctx, t_len)
    lay = _ZLayout(d, tn)
    n_ctx_tiles = m_ctx // tm
    tiles_per_seq = t_len // tm

    def group_of_tile(i):
        return jnp.where(i < n_ctx_tiles, 0, 1 + (i - n_ctx_tiles) // tiles_per_seq)

    n_groups = -(-(1 + n_lat) // SUBLANES) * SUBLANES
    cond = jnp.concatenate([c_ctx[None], c, jnp.zeros((n_groups - 1 - n_lat, d), f32)], axis=0)
    mods = _modulation(cond, w_mod, b_mod).reshape(depth, n_groups, 6, d).transpose(0, 2, 1, 3)[:, :, :, None, :]

    sm = jax.nn.softmax(hgrn_lower_bounds.astype(f32), axis=1)
    csum = jnp.cumsum(sm, axis=1)
    lower = csum - csum[:, :1]
    hg_tabs = tuple(jnp.asarray(t, dt) for t, dt in zip(_hgrn_tables(), (bf16, f32, bf16)))

    cos, sin = _rope_tables(t_len, tp)
    n_ctx_tp = m_ctx // tp

    def rope_block(i):
        return jnp.where(i < n_ctx_tp, 0, 1 + (i - n_ctx_tp) % (t_len // tp))

    x = jnp.concatenate([x_prompt.reshape(m_ctx, d), x_sample.reshape(m_lat, d)], axis=0)
    st_mla = st_na = st_hg = None
    for l in range(depth):
        sh1, sc1, g1, sh2, sc2, g2 = (mods[l, k] for k in range(6))
        w_in_l, w_kpe_l = _permute_w_in(w_in[l], lay)
        z, kpe = _normmod_matmul(x, norm1_g[l], sc1, sh1, w_in_l, group_of_tile, tm, tn, "in_proj", w_side=w_kpe_l)

        w_uk = mla_w_uk[l].reshape(MLA_KV_RANK, MLA_HEADS * MLA_NOPE).astype(bf16)
        w_uv = mla_w_uv[l].reshape(MLA_KV_RANK, MLA_HEADS * MLA_V).astype(bf16)
        q_mla = _mla_q(z, lay, mla_q_norm_g[l], cos, sin, rope_block, tp)
        k_mla, v_mla, *st_mla = _mla_kv(
            z, lay.ckv, kpe, 0, mla_kv_norm_g[l], mla_k_norm_g[l], w_uk, w_uv, cos, sin, rope_block, tp, True, True,
            "mla_kv", state=_CtxState(st_mla, (MLA_KV_RANK, MLA_ROPE), l, depth, n_ctx, s_len, tp))
        ckv_c = cache_mla_ckv[:, l].reshape(n_lat * p_len, MLA_KV_RANK)
        kpe_c = jnp.pad(cache_mla_kpe[:, l].reshape(n_lat * p_len, MLA_ROPE), ((0, 0), (0, LANES - MLA_ROPE)))
        tc = _pow2_tile(512, n_lat * p_len)
        k_c, v_c = _mla_kv(ckv_c, 0, kpe_c, 0, mla_kv_norm_g[l], mla_k_norm_g[l], w_uk, w_uv,
                              cos, sin, lambda i: 0, tc, False, False, "mla_kv_cache")
        tq_c = _pow2_tile(256, s_len)
        o_mla_c = _flash(q_mla, k_mla, v_mla, batch=n_ctx, heads=MLA_HEADS, t_len=s_len, s_len=s_len, row0=0,
                         dq=MLA_QPAD, dv=MLA_V, tq=tq_c, tk=tq_c, heads_per_step=MLA_HEADS, name="mla_attn_ctx")
        tq_l = _pow2_tile(512, t_len, p_len)
        o_mla_l = _flash(q_mla, k_mla, v_mla, batch=n_lat, heads=MLA_HEADS, t_len=t_len, s_len=t_len, row0=m_ctx,
                         dq=MLA_QPAD, dv=MLA_V, tq=tq_l, tk=tq_l, heads_per_step=2, extra=(k_c, v_c, p_len),
                         name="mla_attn_latent")

        lb = lower[:, l]
        lb_tabs = (jnp.log(lb) * LOG2E, jnp.log1p(-lb) * LOG2E, 1.0 - lb)
        o_hg_c, st_hg = _hgrn(z, lay, lb_tabs, hgrn_norm_g[l], hg_tabs, batch=n_ctx, t_len=s_len, row0=0,
                              s0=None, state_out=(st_hg, l, depth))
        o_hg_l, _ = _hgrn(z, lay, lb_tabs, hgrn_norm_g[l], hg_tabs, batch=n_lat, t_len=t_len, row0=m_ctx,
                          s0=state_hgrn[:, l])

        q_na, k_na, v_na, *st_na = _na_prep(
            z, lay, na_q_norm_g[l], na_k_norm_g[l], tp,
            _CtxState(st_na, (NA_HEADS * NA_DH, NA_HEADS * NA_DH), l, depth, n_ctx, s_len, tp))
        o_na_c = _flash(q_na, k_na, v_na, batch=n_ctx, heads=NA_HEADS, t_len=s_len, s_len=s_len, row0=0,
                        dq=NA_DH, dv=NA_DH, tq=tq_c, tk=tq_c, heads_per_step=NA_HEADS, name="na_attn_ctx")
        kc_na = cache_na_k[:, l].reshape(n_lat * p_len, NA_HEADS * NA_DH).astype(bf16)
        vc_na = cache_na_v[:, l].reshape(n_lat * p_len, NA_HEADS * NA_DH).astype(bf16)
        bias = _na_bias_table(na_rpb[l], t_len // GRID_W)
        o_na_l = _na_latent(q_na, k_na, v_na, kc_na, vc_na, bias, batch=n_lat, t_len=t_len, p_len=p_len, row0=m_ctx)

        merged = _merge((o_mla_c, o_mla_l), (o_hg_c, o_hg_l), (o_na_c, o_na_l), z, lay, w_branch[l].astype(bf16), tp, tn)
        x = _proj_residual(merged, w_out[l].astype(bf16), x, g1, group_of_tile, tm, tn)

        ag = _normmod_matmul(x, norm2_g[l], sc2, sh2, ffn_w_up[l].astype(bf16), group_of_tile, tm,
                             _pow2_tile(1024, 2 * ffn), "ffn_up")
        tmd = _pow2_tile(1024, m_ctx, t_len)
        x = _ffn_down(ag, ffn_conv_w[l], ffn_conv_b[l], ffn_w_down[l].astype(bf16), x, g2,
                      lambda i: jnp.where(i < m_ctx // tmd, 0, 1 + (i - m_ctx // tmd) // (t_len // tmd)),
                      tmd, _pow2_tile(512, ffn), m_ctx, s_len, t_len)


    y_prompt = x[:m_ctx].reshape(n_ctx, s_len, d)
    y_sample = x[m_ctx:].reshape(n_lat, t_len, d)
    na_shape = (n_ctx, depth, s_len, NA_HEADS, NA_DH)
    return (y_prompt, y_sample, st_mla[0], st_mla[1], st_na[0].reshape(na_shape), st_na[1].reshape(na_shape), st_hg)
```

```python
import functools
import math

import numpy as np
import jax
import jax.numpy as jnp
from jax import lax
from jax.experimental import pallas as pl
from jax.experimental.pallas import tpu as pltpu

f32 = jnp.float32
bf16 = jnp.bfloat16

GRID_W = 64
MLA_HEADS, MLA_NOPE, MLA_ROPE, MLA_V, MLA_KV_RANK = 8, 128, 64, 128, 512
MLA_QK = MLA_NOPE + MLA_ROPE
HG_HEADS, HG_K, HG_V = 8, 128, 128
NA_HEADS, NA_DH, NA_WIN_R, NA_WIN_C = 8, 128, 8, 16
ROPE_BASE = 10000.0
EPS = 1e-6
NEG = -1e30
LOG2E = 1.4426950408889634

LANES = 128
SUBLANES = 8
VMEM_LIMIT_BYTES = 56 * 1024 * 1024

HEAD_W = 128
MLA_QPAD = 256
HG_CHUNK = 128
HG_LEVELS = int(math.log2(HG_CHUNK))
NA_QROWS = 8
NA_KROWS = 16


def _params(semantics):
    return pltpu.CompilerParams(dimension_semantics=semantics, vmem_limit_bytes=VMEM_LIMIT_BYTES)


def _pow2_tile(pref, *dims):
    t = pref
    while any(d % t for d in dims):
        t //= 2
    assert t >= SUBLANES, (pref, dims)
    return t


def _dot(a, b):
    return jnp.dot(a, b, preferred_element_type=f32)


def _dot_nt(a, b):
    return lax.dot_general(a, b, (((1,), (1,)), ((), ())), preferred_element_type=f32)


def _dot_tn(a, b):
    return lax.dot_general(a, b, (((0,), (0,)), ((), ())), preferred_element_type=f32)


def _silu(x):
    return x * jax.nn.sigmoid(x)


def _mod_const(x, n):
    return jnp.bitwise_and(x, n - 1) if n & (n - 1) == 0 else lax.rem(x, n)


class _ZLayout:
    def __init__(self, d_model, tn):
        self.qn = 0
        self.qp = self.qn + MLA_HEADS * MLA_NOPE
        self.ckv = self.qp + MLA_HEADS * MLA_ROPE
        self.hq = self.ckv + MLA_KV_RANK
        self.hff = self.hq + HG_HEADS * HG_K
        self.hfb = self.hff + HG_HEADS * HG_K
        self.hi = self.hfb + HG_HEADS * HG_K
        self.hg = self.hi + HG_HEADS * HG_V
        self.nq = self.hg + HG_HEADS * HG_V
        self.nk = self.nq + NA_HEADS * NA_DH
        self.nv = self.nk + NA_HEADS * NA_DH
        self.ga = self.nv + NA_HEADS * NA_DH
        self.gb = self.ga + d_model
        self.gc = self.gb + d_model
        self.width = self.gc + d_model
        assert self.width % tn == 0


def _permute_w_in(w, lay):
    d = w.shape[0]
    n_mq = MLA_HEADS * MLA_QK
    qn = [w[:, h * MLA_QK:h * MLA_QK + MLA_NOPE] for h in range(MLA_HEADS)]
    qp = [w[:, h * MLA_QK + MLA_NOPE:(h + 1) * MLA_QK] for h in range(MLA_HEADS)]
    ckv = w[:, n_mq:n_mq + MLA_KV_RANK]
    kpe = w[:, n_mq + MLA_KV_RANK:n_mq + MLA_KV_RANK + MLA_ROPE]
    rest = w[:, n_mq + MLA_KV_RANK + MLA_ROPE:]
    main = jnp.concatenate(qn + qp + [ckv, rest], axis=1).astype(bf16)
    assert main.shape[1] == lay.width
    return main, jnp.pad(kpe, ((0, 0), (0, LANES - MLA_ROPE))).astype(bf16)


def _mod_kernel(c_ref, w_ref, b_ref, o_ref):
    a = _silu(c_ref[...]).astype(bf16)
    o_ref[...] = _dot(a, w_ref[...].astype(bf16)) + b_ref[...]


def _modulation(cond, w_mod, b_mod):
    depth, d, n = w_mod.shape
    g = cond.shape[0]
    tn = _pow2_tile(1024, n)
    return pl.pallas_call(
        _mod_kernel,
        grid=(depth, n // tn),
        in_specs=[
            pl.BlockSpec((g, d), lambda l, j: (0, 0)),
            pl.BlockSpec((None, d, tn), lambda l, j: (l, 0, j)),
            pl.BlockSpec((None, 1, tn), lambda l, j: (l, 0, j)),
        ],
        out_specs=pl.BlockSpec((None, g, tn), lambda l, j: (l, 0, j)),
        out_shape=jax.ShapeDtypeStruct((depth, g, n), f32),
        compiler_params=_params(("arbitrary", "arbitrary")),
        name="modulation",
    )(cond, w_mod, b_mod.reshape(depth, 1, n))


def _normmod_matmul_kernel(x_ref, g_ref, sc_ref, sh_ref, w_ref, *rest):
    if len(rest) == 4:
        ws_ref, o_ref, os_ref, h_scr = rest
    else:
        (o_ref, h_scr), ws_ref, os_ref = rest, None, None

    @pl.when(pl.program_id(1) == 0)
    def _():
        x = x_ref[...]
        ms = jnp.mean(x * x, axis=-1, keepdims=True)
        y = x * lax.rsqrt(ms + EPS) * g_ref[...]
        h_scr[...] = (y * (1.0 + sc_ref[...]) + sh_ref[...]).astype(bf16)
        if ws_ref is not None:
            os_ref[...] = _dot(h_scr[...], ws_ref[...])

    o_ref[...] = _dot(h_scr[...], w_ref[...])


def _normmod_matmul(x, g, sc, sh, w, group_of_tile, tm, tn, name, w_side=None):
    m, d = x.shape
    n = w.shape[1]
    in_specs = [
        pl.BlockSpec((tm, d), lambda i, j: (i, 0)),
        pl.BlockSpec((1, d), lambda i, j: (0, 0)),
        pl.BlockSpec((None, 1, d), lambda i, j: (group_of_tile(i), 0, 0)),
        pl.BlockSpec((None, 1, d), lambda i, j: (group_of_tile(i), 0, 0)),
        pl.BlockSpec((d, tn), lambda i, j: (0, j)),
    ]
    args = [x, g.reshape(1, d), sc, sh, w]
    out_specs = [pl.BlockSpec((tm, tn), lambda i, j: (i, j))]
    out_shape = [jax.ShapeDtypeStruct((m, n), f32)]
    if w_side is not None:
        ns = w_side.shape[1]
        in_specs.append(pl.BlockSpec((d, ns), lambda i, j: (0, 0)))
        args.append(w_side)
        out_specs.append(pl.BlockSpec((tm, ns), lambda i, j: (i, 0)))
        out_shape.append(jax.ShapeDtypeStruct((m, ns), f32))
    res = pl.pallas_call(
        _normmod_matmul_kernel,
        grid=(m // tm, n // tn),
        in_specs=in_specs,
        out_specs=out_specs,
        out_shape=out_shape,
        scratch_shapes=[pltpu.VMEM((tm, d), bf16)],
        compiler_params=_params(("arbitrary", "arbitrary")),
        name=name,
    )(*args)
    return res if w_side is not None else res[0]


def _rope(x, cos, sin_signed):
    n = x.shape[-1]
    half = MLA_ROPE // 4
    lane = lax.broadcasted_iota(jnp.int32, x.shape, 1)
    partner = jnp.where((lane % (2 * half)) < half, pltpu.roll(x, n - half, 1), pltpu.roll(x, half, 1))
    return x * cos + partner * sin_signed


def _mla_q_kernel(qn_ref, qp_ref, gn_ref, gp_ref, cos_ref, sin_ref, o_ref, *, scale):
    qn = qn_ref[...]
    qp = qp_ref[...]
    tm = qn.shape[0]
    lane = lax.broadcasted_iota(jnp.int32, (tm, LANES), 1)
    low = lane < MLA_ROPE
    cos = cos_ref[...]
    sin = sin_ref[...]
    gn = gn_ref[...]
    gp = gp_ref[...]
    for pair in range(MLA_HEADS // 2):
        pcol = qp[:, pair * LANES:(pair + 1) * LANES]
        sq = pcol * pcol
        ss_all = jnp.sum(sq, axis=-1, keepdims=True)
        ss_low = jnp.sum(jnp.where(low, sq, 0.0), axis=-1, keepdims=True)
        rot = _rope(pcol * gp, cos, sin)
        for k, ss_pe in enumerate((ss_low, ss_all - ss_low)):
            h = 2 * pair + k
            nope = qn[:, h * MLA_NOPE:(h + 1) * MLA_NOPE]
            ss = jnp.sum(nope * nope, axis=-1, keepdims=True) + ss_pe
            r = lax.rsqrt(ss * (1.0 / MLA_QK) + EPS) * scale
            pe = rot if k == 0 else pltpu.roll(rot, MLA_ROPE, 1)
            o_ref[:, h * MLA_QPAD:h * MLA_QPAD + MLA_NOPE] = (nope * r * gn).astype(bf16)
            o_ref[:, h * MLA_QPAD + MLA_NOPE:(h + 1) * MLA_QPAD] = jnp.where(low, pe * r, 0.0).astype(bf16)


class _CtxState:
    def __init__(self, prevs, widths, layer, depth, n_ctx, s_len, tm):
        assert tm % s_len == 0
        self.n_seq = tm // s_len
        self.s_len = s_len
        self.n_tiles = n_ctx * s_len // tm
        self.prevs = [] if prevs is None else list(prevs)
        n_tiles = self.n_tiles
        self.out_specs = [pl.BlockSpec((self.n_seq, None, s_len, w), lambda i: (jnp.minimum(i, n_tiles - 1), layer, 0, 0))
                          for w in widths]
        self.out_shape = [jax.ShapeDtypeStruct((n_ctx, depth, s_len, w), f32) for w in widths]
        self.in_specs = [pl.BlockSpec(memory_space=pl.ANY)] * len(self.prevs)

    def aliases(self, n_inputs_before, n_outputs_before):
        return {n_inputs_before + k: n_outputs_before + k for k in range(len(self.prevs))}

    def kernel_kwargs(self):
        return dict(n_prev=len(self.prevs), n_ctx_tiles=self.n_tiles, s_len=self.s_len)


def _mla_kv_kernel(*refs, norm_ckv, rope, n_prev=0, n_ctx_tiles=0, s_len=0):
    ckv_ref, kpe_ref, kvg_ref, gn_ref, gp_ref, wuk_ref, wuv_ref, cos_ref, sin_ref = refs[:9]
    k_ref, v_ref, *state_refs = refs[9 + n_prev:]
    ckv = ckv_ref[...]
    if norm_ckv:
        ms = jnp.mean(ckv * ckv, axis=-1, keepdims=True)
        ckv = ckv * lax.rsqrt(ms + EPS) * kvg_ref[...]
    if state_refs:
        ckv_state_ref, kpe_state_ref = state_refs

        @pl.when(pl.program_id(0) < n_ctx_tiles)
        def _():
            for n in range(ckv.shape[0] // s_len):
                ckv_state_ref[n] = ckv[n * s_len:(n + 1) * s_len]
                kpe_state_ref[n] = kpe_ref[n * s_len:(n + 1) * s_len, :MLA_ROPE]

    cb = ckv.astype(bf16)
    kn = _dot(cb, wuk_ref[...])
    v_ref[...] = _dot(cb, wuv_ref[...]).astype(bf16)
    tm = ckv.shape[0]
    lane = lax.broadcasted_iota(jnp.int32, (tm, LANES), 1)
    low = lane < MLA_ROPE
    kpe = jnp.where(low, kpe_ref[...], 0.0)
    ss_pe = jnp.sum(kpe * kpe, axis=-1, keepdims=True)
    pe = kpe * gp_ref[...]
    if rope:
        pe = _rope(pe, cos_ref[...], sin_ref[...])
    gn = gn_ref[...]
    for h in range(MLA_HEADS):
        nope = kn[:, h * MLA_NOPE:(h + 1) * MLA_NOPE]
        ss = jnp.sum(nope * nope, axis=-1, keepdims=True) + ss_pe
        r = lax.rsqrt(ss * (1.0 / MLA_QK) + EPS)
        k_ref[:, h * MLA_QPAD:h * MLA_QPAD + MLA_NOPE] = (nope * r * gn).astype(bf16)
        k_ref[:, h * MLA_QPAD + MLA_NOPE:(h + 1) * MLA_QPAD] = jnp.where(low, pe * r, 0.0).astype(bf16)


def _rope_tables(t_len, tm):
    n_freq = MLA_ROPE // 4
    t = jnp.arange(t_len, dtype=jnp.int32)
    inv = ROPE_BASE ** (-jnp.arange(n_freq, dtype=f32) / n_freq)
    ang_r = (t // GRID_W).astype(f32)[:, None] * inv
    ang_c = (t % GRID_W).astype(f32)[:, None] * inv
    cos64 = jnp.concatenate([jnp.cos(ang_r), jnp.cos(ang_r), jnp.cos(ang_c), jnp.cos(ang_c)], axis=1)
    sin64 = jnp.concatenate([-jnp.sin(ang_r), jnp.sin(ang_r), -jnp.sin(ang_c), jnp.sin(ang_c)], axis=1)
    cos = jnp.concatenate([jnp.ones((tm, LANES), f32), jnp.tile(cos64, (1, 2))], axis=0)
    sin = jnp.concatenate([jnp.zeros((tm, LANES), f32), jnp.tile(sin64, (1, 2))], axis=0)
    return cos, sin


def _mla_q(z, lay, q_g, cos, sin, rope_block, tm):
    m = z.shape[0]
    gn = q_g[:MLA_NOPE].reshape(1, MLA_NOPE)
    gp = jnp.tile(q_g[MLA_NOPE:], 2).reshape(1, LANES)
    wn, wp = MLA_HEADS * MLA_NOPE, MLA_HEADS * MLA_ROPE
    return pl.pallas_call(
        functools.partial(_mla_q_kernel, scale=MLA_QK ** -0.5 * LOG2E),
        grid=(m // tm,),
        in_specs=[
            pl.BlockSpec((tm, wn), lambda i: (i, lay.qn // wn)),
            pl.BlockSpec((tm, wp), lambda i: (i, lay.qp // wp)),
            pl.BlockSpec((1, MLA_NOPE), lambda i: (0, 0)),
            pl.BlockSpec((1, LANES), lambda i: (0, 0)),
            pl.BlockSpec((tm, LANES), lambda i: (rope_block(i), 0)),
            pl.BlockSpec((tm, LANES), lambda i: (rope_block(i), 0)),
        ],
        out_specs=pl.BlockSpec((tm, MLA_HEADS * MLA_QPAD), lambda i: (i, 0)),
        out_shape=jax.ShapeDtypeStruct((m, MLA_HEADS * MLA_QPAD), bf16),
        compiler_params=_params(("arbitrary",)),
        name="mla_q",
    )(z, z, gn, gp, cos, sin)


def _mla_kv(ckv_src, ckv_col, kpe_src, kpe_col, kv_g, k_g, w_uk, w_uv, cos, sin, rope_block, tm, norm_ckv, rope, name,
            state=None):
    m = ckv_src.shape[0]
    gn = k_g[:MLA_NOPE].reshape(1, MLA_NOPE)
    gp = jnp.concatenate([k_g[MLA_NOPE:], jnp.zeros((LANES - MLA_ROPE,), f32)]).reshape(1, LANES)
    hw = MLA_HEADS * MLA_NOPE
    in_specs = [
        pl.BlockSpec((tm, MLA_KV_RANK), lambda i: (i, ckv_col // MLA_KV_RANK)),
        pl.BlockSpec((tm, LANES), lambda i: (i, kpe_col // LANES)),
        pl.BlockSpec((1, MLA_KV_RANK), lambda i: (0, 0)),
        pl.BlockSpec((1, MLA_NOPE), lambda i: (0, 0)),
        pl.BlockSpec((1, LANES), lambda i: (0, 0)),
        pl.BlockSpec((MLA_KV_RANK, hw), lambda i: (0, 0)),
        pl.BlockSpec((MLA_KV_RANK, hw), lambda i: (0, 0)),
        pl.BlockSpec((tm, LANES), lambda i: (rope_block(i), 0)),
        pl.BlockSpec((tm, LANES), lambda i: (rope_block(i), 0)),
    ]
    args = [ckv_src, kpe_src, kv_g.reshape(1, MLA_KV_RANK), gn, gp, w_uk, w_uv, cos, sin]
    out_specs = [pl.BlockSpec((tm, MLA_HEADS * MLA_QPAD), lambda i: (i, 0)), pl.BlockSpec((tm, hw), lambda i: (i, 0))]
    out_shape = [jax.ShapeDtypeStruct((m, MLA_HEADS * MLA_QPAD), bf16), jax.ShapeDtypeStruct((m, hw), bf16)]
    kwargs, aliases = {}, {}
    if state is not None:
        aliases = state.aliases(len(args), len(out_specs))
        kwargs = state.kernel_kwargs()
        in_specs += state.in_specs
        args += state.prevs
        out_specs += state.out_specs
        out_shape += state.out_shape
    return pl.pallas_call(
        functools.partial(_mla_kv_kernel, norm_ckv=norm_ckv, rope=rope, **kwargs),
        grid=(m // tm,),
        in_specs=in_specs,
        out_specs=out_specs,
        out_shape=out_shape,
        input_output_aliases=aliases,
        compiler_params=_params(("arbitrary",)),
        name=name,
    )(*args)


def _flash_kernel(*refs, n_main, tk, n_extra, heads, dq, dv):
    if n_extra:
        q_ref, k_ref, v_ref, kx_ref, vx_ref, o_ref = refs
    else:
        q_ref, k_ref, v_ref, o_ref = refs
        kx_ref = vx_ref = None
    tq = q_ref.shape[0]
    chunks = [(k_ref, v_ref, c) for c in range(n_main)] + [(kx_ref, vx_ref, c) for c in range(n_extra)]
    def scores(i, g):
        kr, _, c = chunks[i]
        return _dot_nt(kr[c * tk:(c + 1) * tk, g * dq:(g + 1) * dq], q_ref[:, g * dq:(g + 1) * dq])

    m = [jnp.full((1, tq), -jnp.inf, f32)] * heads
    l = [jnp.zeros((1, tq), f32)] * heads
    acc = [jnp.zeros((dv, tq), f32)] * heads
    s = [scores(0, g) for g in range(heads)]
    for i, (_, vr, c) in enumerate(chunks):
        for g in range(heads):
            s_next = scores(i + 1, g) if i + 1 < len(chunks) else None
            m_new = jnp.maximum(m[g], jnp.max(s[g], axis=0, keepdims=True))
            alpha = jnp.exp2(m[g] - m_new)
            p = jnp.exp2(s[g] - m_new)
            l[g] = alpha * l[g] + jnp.sum(p, axis=0, keepdims=True)
            acc[g] = alpha * acc[g] + _dot_tn(vr[c * tk:(c + 1) * tk, g * dv:(g + 1) * dv], p.astype(bf16))
            m[g], s[g] = m_new, s_next
    for g in range(heads):
        o_ref[:, g * dv:(g + 1) * dv] = (acc[g] / l[g]).T.astype(o_ref.dtype)


def _flash(q, k, v, *, batch, heads, t_len, s_len, row0, dq, dv, tq, tk, heads_per_step=1, extra=None, name):
    assert row0 % tq == 0 and row0 % s_len == 0 and t_len % tq == 0 and s_len % tk == 0
    assert heads % heads_per_step == 0
    nq = t_len // tq
    g = heads_per_step
    in_specs = [
        pl.BlockSpec((tq, g * dq), lambda b, h, i: (row0 // tq + b * nq + i, h)),
        pl.BlockSpec((s_len, g * dq), lambda b, h, i: (row0 // s_len + b, h)),
        pl.BlockSpec((s_len, g * dv), lambda b, h, i: (row0 // s_len + b, h)),
    ]
    args = [q, k, v]
    n_extra = 0
    if extra is not None:
        kx, vx, p_len = extra
        assert p_len % tk == 0
        n_extra = p_len // tk
        in_specs += [
            pl.BlockSpec((p_len, g * dq), lambda b, h, i: (b, h)),
            pl.BlockSpec((p_len, g * dv), lambda b, h, i: (b, h)),
        ]
        args += [kx, vx]
    return pl.pallas_call(
        functools.partial(_flash_kernel, n_main=s_len // tk, tk=tk, n_extra=n_extra, heads=g, dq=dq, dv=dv),
        grid=(batch, heads // g, nq),
        in_specs=in_specs,
        out_specs=pl.BlockSpec((tq, g * dv), lambda b, h, i: (b * nq + i, h)),
        out_shape=jax.ShapeDtypeStruct((batch * t_len, heads * dv), bf16),
        compiler_params=_params(("arbitrary", "arbitrary", "arbitrary")),
        name=name,
    )(*args)


def _na_prep_kernel(*refs, scale, n_prev, n_ctx_tiles, s_len):
    nq_ref, nk_ref, nv_ref, qg_ref, kg_ref = refs[:5]
    q_ref, k_ref, v_ref, k_state_ref, v_state_ref = refs[5 + n_prev:]
    qg = qg_ref[...]
    kg = kg_ref[...]
    tm = nv_ref.shape[0]
    is_ctx = pl.program_id(0) < n_ctx_tiles
    v_ref[...] = nv_ref[...].astype(bf16)

    @pl.when(is_ctx)
    def _():
        for n in range(tm // s_len):
            v_state_ref[n] = nv_ref[n * s_len:(n + 1) * s_len, :]

    for h in range(NA_HEADS):
        sl = slice(h * NA_DH, (h + 1) * NA_DH)
        q = nq_ref[:, sl]
        k = nk_ref[:, sl]
        rq = lax.rsqrt(jnp.mean(q * q, axis=-1, keepdims=True) + EPS)
        rk = lax.rsqrt(jnp.mean(k * k, axis=-1, keepdims=True) + EPS)
        q_ref[:, sl] = (q * rq * qg * scale).astype(bf16)
        kn = k * rk * kg
        k_ref[:, sl] = kn.astype(bf16)

        @pl.when(is_ctx)
        def _():
            for n in range(tm // s_len):
                k_state_ref[n, :, sl] = kn[n * s_len:(n + 1) * s_len]


def _na_prep(z, lay, q_g, k_g, tm, state):
    m = z.shape[0]
    w = NA_HEADS * NA_DH
    spec = pl.BlockSpec((tm, w), lambda i: (i, 0))
    in_specs = [
        pl.BlockSpec((tm, w), lambda i: (i, lay.nq // w)),
        pl.BlockSpec((tm, w), lambda i: (i, lay.nk // w)),
        pl.BlockSpec((tm, w), lambda i: (i, lay.nv // w)),
        pl.BlockSpec((1, NA_DH), lambda i: (0, 0)),
        pl.BlockSpec((1, NA_DH), lambda i: (0, 0)),
    ]
    args = [z, z, z, q_g.reshape(1, NA_DH), k_g.reshape(1, NA_DH)]
    return pl.pallas_call(
        functools.partial(_na_prep_kernel, scale=NA_DH ** -0.5 * LOG2E, **state.kernel_kwargs()),
        grid=(m // tm,),
        in_specs=in_specs + state.in_specs,
        out_specs=[spec, spec, spec] + state.out_specs,
        out_shape=[jax.ShapeDtypeStruct((m, w), bf16)] * 3 + state.out_shape,
        input_output_aliases=state.aliases(len(args), 3),
        compiler_params=_params(("arbitrary",)),
        name="na_prep",
    )(*args, *state.prevs)


def _na_static_maps(rows):
    nblk = rows // NA_QROWS
    reps = (0, min(1, nblk - 1), nblk - 1)
    dr_map = np.full((3, NA_QROWS, NA_KROWS), 2 * NA_WIN_R - 1, np.int32)
    for v, kb in enumerate(reps):
        ws = min(max(NA_QROWS * kb - NA_WIN_R // 2, 0), rows - NA_KROWS)
        for i in range(NA_QROWS):
            qrow = NA_QROWS * kb + i
            rs = min(max(qrow - NA_WIN_R // 2, 0), rows - NA_WIN_R)
            for j in range(NA_KROWS):
                krow = ws + j
                if rs <= krow < rs + NA_WIN_R:
                    dr_map[v, i, j] = krow - qrow + NA_WIN_R - 1
    qcol = np.arange(GRID_W)
    cs = np.clip(qcol - NA_WIN_C // 2, 0, GRID_W - NA_WIN_C)
    in_win = (qcol[None, :] >= cs[:, None]) & (qcol[None, :] < cs[:, None] + NA_WIN_C)
    dc_idx = np.clip(qcol[None, :] - qcol[:, None], -(NA_WIN_C - 1), NA_WIN_C - 1) + (NA_WIN_C - 1)
    return dr_map, in_win, dc_idx


def _na_bias_table(rpb, rows):
    dr_map, in_win, dc_idx = _na_static_maps(rows)
    h = rpb.shape[0]
    bt = jnp.where(in_win.T[None, None], jnp.take(rpb * LOG2E, jnp.asarray(dc_idx.T), axis=2), NEG)
    bt = jnp.concatenate([bt, jnp.full((h, 1, GRID_W, GRID_W), NEG, f32)], axis=1)
    tab = jnp.take(bt, jnp.asarray(dr_map.transpose(0, 2, 1).reshape(-1)), axis=1)
    tab = tab.reshape(h, 3, NA_KROWS, NA_QROWS, GRID_W, GRID_W).transpose(1, 0, 2, 4, 3, 5)
    return tab.reshape(3, h, NA_KROWS * GRID_W, NA_QROWS * GRID_W)


def _na_kernel(q_ref, k_ref, v_ref, kc_ref, vc_ref, bias_ref, o_ref, *, rows, heads):
    kb = pl.program_id(2)
    ws = jnp.clip(NA_QROWS * kb - NA_WIN_R // 2, 0, rows - NA_KROWS)
    win = pl.ds(pl.multiple_of(ws * GRID_W, (NA_WIN_R // 2) * GRID_W), NA_KROWS * GRID_W)
    cols = [slice(g * NA_DH, (g + 1) * NA_DH) for g in range(heads)]
    s_loc = [_dot_nt(k_ref[win, c], q_ref[:, c]) + bias_ref[g] for g, c in enumerate(cols)]
    s_ctx = [_dot_nt(kc_ref[:, c], q_ref[:, c]) for c in cols]
    p_loc, p_ctx, l = [], [], []
    for g in range(heads):
        m = jnp.maximum(jnp.max(s_loc[g], axis=0, keepdims=True), jnp.max(s_ctx[g], axis=0, keepdims=True))
        pl_g = jnp.exp2(s_loc[g] - m)
        pc_g = jnp.exp2(s_ctx[g] - m)
        l.append(jnp.sum(pl_g, axis=0, keepdims=True) + jnp.sum(pc_g, axis=0, keepdims=True))
        p_loc.append(pl_g.astype(bf16))
        p_ctx.append(pc_g.astype(bf16))
    for g, c in enumerate(cols):
        acc = _dot_tn(v_ref[win, c], p_loc[g]) + _dot_tn(vc_ref[:, c], p_ctx[g])
        o_ref[:, c] = (acc / l[g]).T.astype(o_ref.dtype)


def _na_latent(q, k, v, kc, vc, bias, *, batch, t_len, p_len, row0, heads_per_step=2):
    rows = t_len // GRID_W
    assert t_len % GRID_W == 0 and rows % NA_QROWS == 0 and rows >= NA_KROWS
    tq = NA_QROWS * GRID_W
    assert row0 % tq == 0 and row0 % t_len == 0 and NA_HEADS % heads_per_step == 0
    nblk = rows // NA_QROWS
    g = heads_per_step
    gw = g * NA_DH

    def variant(i):
        return jnp.where(i == 0, 0, jnp.where(i == nblk - 1, 2, 1))

    return pl.pallas_call(
        functools.partial(_na_kernel, rows=rows, heads=g),
        grid=(batch, NA_HEADS // g, nblk),
        in_specs=[
            pl.BlockSpec((tq, gw), lambda b, h, i: (row0 // tq + b * nblk + i, h)),
            pl.BlockSpec((t_len, gw), lambda b, h, i: (row0 // t_len + b, h)),
            pl.BlockSpec((t_len, gw), lambda b, h, i: (row0 // t_len + b, h)),
            pl.BlockSpec((p_len, gw), lambda b, h, i: (b, h)),
            pl.BlockSpec((p_len, gw), lambda b, h, i: (b, h)),
            pl.BlockSpec((None, g, NA_KROWS * GRID_W, tq), lambda b, h, i: (variant(i), h, 0, 0)),
        ],
        out_specs=pl.BlockSpec((tq, gw), lambda b, h, i: (b * nblk + i, h)),
        out_shape=jax.ShapeDtypeStruct((batch * t_len, NA_HEADS * NA_DH), bf16),
        compiler_params=_params(("arbitrary", "arbitrary", "arbitrary")),
        name="na_latent",
    )(q, k, v, kc, vc, bias)


def _hgrn_tables():
    c, lv = HG_CHUNK, HG_LEVELS
    t = np.arange(c)[:, None]
    r = np.arange(c)[None, :]
    wcum = np.stack([r <= t, r >= t]).astype(np.float32)
    qside = np.zeros((2, lv * c, LANES), np.float32)
    mask = np.zeros((2, lv + 1, c, c), np.float32)
    mask[:, 0] = np.eye(c)
    for l in range(lv):
        hs = 1 << l
        blk = t // (2 * hs)
        mid = blk * (2 * hs) + hs
        qf = t >= mid
        qb = t < mid
        qside[0, l * c:(l + 1) * c] = np.where(qf, 1.0, -1.0)
        qside[1, l * c:(l + 1) * c] = np.where(qb, 1.0, -1.0)
        same = blk == blk.T
        mask[0, 1 + l] = same & qf & ~qf.T
        mask[1, 1 + l] = same & qb & ~qb.T
    return wcum, qside, mask


def _hgrn_kernel(*refs, n_chunks, heads, has_state, emit_state, n_prev=0):
    (hq_ref, hff_ref, hfb_ref, hi_ref, hg_ref, la_ref, lc_ref, om_ref, ng_ref,
     wcum_ref, qside_ref, mask_ref) = refs[:12]
    rest = list(refs[12:])
    s0_ref = rest.pop(0) if has_state else None
    del rest[:n_prev]
    o_ref = rest.pop(0)
    st_ref = rest.pop(0) if emit_state else None
    o_scr, s_scr, b_scr = rest
    c, lv = HG_CHUNK, HG_LEVELS

    o_scr[...] = jnp.zeros_like(o_scr)
    for d in range(2):
        for g in range(heads):
            if has_state:
                s_scr[d, g] = s0_ref[d, g].T
            else:
                s_scr[d, g] = jnp.zeros((HG_V, HG_K), f32)

    chains = [(d, g) for g in range(heads) for d in range(2)]
    low_half = lax.broadcasted_iota(jnp.int32, (SUBLANES, HEAD_W), 0) < SUBLANES // 2

    def gates_and_cumsum(ci, row):
        d, g = chains[ci]
        rs = pl.ds(pl.multiple_of(row, c), c)
        cs = slice(g * HEAD_W, (g + 1) * HEAD_W)
        x = (hff_ref if d == 0 else hfb_ref)[rs, cs]
        x2 = x * LOG2E
        u = jnp.exp2(-jnp.abs(x2))
        inv = 1.0 / (1.0 + u)
        pos = x >= 0.0
        log_sig = jnp.minimum(x2, 0.0) - jnp.log2(1.0 + u)
        a = la_ref[d:d + 1, cs]
        bb = lc_ref[d:d + 1, cs] + log_sig
        lf = jnp.maximum(a, bb) + jnp.log2(1.0 + jnp.exp2(-jnp.abs(a - bb)))
        om = om_ref[d:d + 1, cs]
        f = (1.0 - om) + om * (jnp.where(pos, 1.0, u) * inv)
        kin = om * (jnp.where(pos, u, 1.0) * inv)
        q = _silu(hq_ref[rs, cs])
        v = hi_ref[rs, cs].astype(bf16)
        l1 = lf.astype(bf16)
        r1 = lf - l1.astype(f32)
        l2 = r1.astype(bf16)
        l3 = (r1 - l2.astype(f32)).astype(bf16)
        w = wcum_ref[d]
        b = _dot(w, l1) + _dot(w, l2) + _dot(w, l3)
        b_scr[ci] = b
        return dict(rs=rs, cs=cs, f=f, kin=kin, q=q, v=v, b=b)

    def intra_chunk(ci, s):
        d, _ = chains[ci]
        b, q, kin = s["b"], s["q"], s["kin"]

        def row(r):
            return b_scr[ci, r:r + 1, :]

        s["b_end"] = row(c - 1 if d == 0 else 0)
        xs = [jnp.where(qside_ref[d, 0:c, :] > 0.0, q * s["f"], kin).astype(bf16)]
        for l in range(1, lv):
            hs = 1 << l
            blk = 2 * hs
            pick = hs - 1 if d == 0 else hs
            if hs >= SUBLANES:
                pieces = []
                for j in range(c // blk):
                    m = row(blk * j + pick)
                    early = slice(blk * j, blk * j + hs)
                    late = slice(blk * j + hs, blk * (j + 1))
                    q_sl, k_sl = (late, early) if d == 0 else (early, late)
                    xq = q[q_sl] * jnp.exp2(b[q_sl] - m)
                    xk = kin[k_sl] * jnp.exp2(m - b[k_sl])
                    pieces += [xk, xq] if d == 0 else [xq, xk]
                xs.append(jnp.concatenate(pieces, axis=0).astype(bf16))
                continue
            if blk < SUBLANES:
                pieces = [jnp.where(low_half, jnp.broadcast_to(row(SUBLANES * j + pick), (SUBLANES, HEAD_W)),
                                    jnp.broadcast_to(row(SUBLANES * j + blk + pick), (SUBLANES, HEAD_W)))
                          for j in range(c // SUBLANES)]
            else:
                pieces = [jnp.broadcast_to(row(blk * j + pick), (blk, HEAD_W)) for j in range(c // blk)]
            m = pieces[0] if len(pieces) == 1 else jnp.concatenate(pieces, axis=0)
            sign = qside_ref[d, l * c:(l + 1) * c, :]
            xs.append((jnp.where(sign > 0.0, q, kin) * jnp.exp2((b - m) * sign)).astype(bf16))
        att = mask_ref[d, 0] * _dot_nt(q.astype(bf16), kin.astype(bf16)).astype(bf16)
        for l in range(lv):
            att = att + mask_ref[d, 1 + l] * _dot_nt(xs[l], xs[l]).astype(bf16)
        s["att"] = att

    def state_step(ci, s):
        d, g = chains[ci]
        b, q, kin, v = s["b"], s["q"], s["kin"], s["v"]
        st = s_scr[d, g]
        o = _dot(s["att"], v) + _dot_nt((q * jnp.exp2(b)).astype(bf16), st.astype(bf16))
        b_end = s["b_end"]
        s_scr[d, g] = st * jnp.exp2(b_end) + _dot_tn(v, (kin * jnp.exp2(b_end - b)).astype(bf16))
        o_scr[s["rs"], s["cs"]] += o

    def body(i, _):
        rows = (i * c, (n_chunks - 1 - i) * c)
        states = [gates_and_cumsum(ci, rows[chains[ci][0]]) for ci in range(len(chains))]
        for ci, s in enumerate(states):
            intra_chunk(ci, s)
        for ci, s in enumerate(states):
            state_step(ci, s)
        return 0

    lax.fori_loop(0, n_chunks, body, 0)

    ng = ng_ref[...]
    for g in range(heads):
        cs = slice(g * HEAD_W, (g + 1) * HEAD_W)
        o = o_scr[:, cs]
        y = o * lax.rsqrt(jnp.mean(o * o, axis=-1, keepdims=True) + EPS) * ng
        o_ref[:, cs] = (y * _silu(hg_ref[:, cs])).astype(o_ref.dtype)
        if emit_state:
            for d in range(2):
                st_ref[d, g] = s_scr[d, g].T


def _hgrn(z, lay, lb_tabs, norm_g, tables, *, batch, t_len, row0, s0, state_out=None, heads_per_step=2):
    emit_state = state_out is not None
    assert t_len % HG_CHUNK == 0 and row0 % t_len == 0 and HG_K == HEAD_W and HG_V == HEAD_W
    g = heads_per_step
    gw = g * HEAD_W
    la, lc, om = lb_tabs
    wcum, qside, mask = tables
    r0 = row0 // t_len

    def zspec(col):
        return pl.BlockSpec((t_len, gw), lambda b, h: (r0 + b, col // gw + h))

    def const(shape):
        return pl.BlockSpec(shape, lambda b, h: (0,) * len(shape))

    in_specs = [zspec(lay.hq), zspec(lay.hff), zspec(lay.hfb), zspec(lay.hi), zspec(lay.hg),
                pl.BlockSpec((2, gw), lambda b, h: (0, h)), pl.BlockSpec((2, gw), lambda b, h: (0, h)),
                pl.BlockSpec((2, gw), lambda b, h: (0, h)), const((1, HG_V)),
                const(wcum.shape), const(qside.shape), const(mask.shape)]
    args = [z, z, z, z, z, la, lc, om, norm_g.reshape(1, HG_V), wcum, qside, mask]
    state_spec = pl.BlockSpec((None, 2, g, HG_K, HG_V), lambda b, h: (b, 0, h, 0, 0))
    if s0 is not None:
        in_specs.append(state_spec)
        args.append(s0)
    out_specs = [pl.BlockSpec((t_len, gw), lambda b, h: (b, h))]
    out_shape = [jax.ShapeDtypeStruct((batch * t_len, HG_HEADS * HG_V), bf16)]
    aliases, n_prev = {}, 0
    if emit_state:
        prev, layer, depth = state_out
        if prev is not None:
            aliases, n_prev = {len(args): 1}, 1
            in_specs.append(pl.BlockSpec(memory_space=pl.ANY))
            args.append(prev)
        out_specs.append(pl.BlockSpec((None, None, 2, g, HG_K, HG_V), lambda b, h: (b, layer, 0, h, 0, 0)))
        out_shape.append(jax.ShapeDtypeStruct((batch, depth, 2, HG_HEADS, HG_K, HG_V), f32))
    res = pl.pallas_call(
        functools.partial(_hgrn_kernel, n_chunks=t_len // HG_CHUNK, heads=g, has_state=s0 is not None,
                          emit_state=emit_state, n_prev=n_prev),
        grid=(batch, HG_HEADS // g),
        in_specs=in_specs,
        out_specs=out_specs,
        out_shape=out_shape,
        input_output_aliases=aliases,
        scratch_shapes=[pltpu.VMEM((t_len, gw), f32), pltpu.VMEM((2, g, HG_V, HG_K), f32),
                        pltpu.VMEM((2 * g, HG_CHUNK, HEAD_W), f32)],
        compiler_params=_params(("arbitrary", "arbitrary")),
        name="hgrn_ctx" if emit_state else "hgrn_latent",
    )(*args)
    return res if emit_state else (res[0], None)


def _merge_kernel(oac_ref, oal_ref, obc_ref, obl_ref, occ_ref, ocl_ref, ga_ref, gb_ref, gc_ref, w_ref, o_ref,
                  *, n_ctx_tiles):
    def compute(oa_ref, ob_ref, oc_ref):
        m = jax.nn.sigmoid(ga_ref[...]) * _dot(oa_ref[...], w_ref[0])
        m = m + jax.nn.sigmoid(gb_ref[...]) * _dot(ob_ref[...], w_ref[1])
        m = m + jax.nn.sigmoid(gc_ref[...]) * _dot(oc_ref[...], w_ref[2])
        o_ref[...] = m.astype(o_ref.dtype)

    is_ctx = pl.program_id(1) < n_ctx_tiles

    @pl.when(is_ctx)
    def _():
        compute(oac_ref, obc_ref, occ_ref)

    @pl.when(jnp.logical_not(is_ctx))
    def _():
        compute(oal_ref, obl_ref, ocl_ref)


def _merge(o_a, o_b, o_c, z, lay, w_branch, tm, tn):
    bw = o_a[0].shape[1]
    m = z.shape[0]
    d = w_branch.shape[2]
    nct = o_a[0].shape[0] // tm
    cspec = pl.BlockSpec((tm, bw), lambda j, i: (jnp.minimum(i, nct - 1), 0))
    lspec = pl.BlockSpec((tm, bw), lambda j, i: (jnp.maximum(i - nct, 0), 0))

    def gate(col):
        return pl.BlockSpec((tm, tn), lambda j, i: (i, col // tn + j))

    return pl.pallas_call(
        functools.partial(_merge_kernel, n_ctx_tiles=nct),
        grid=(d // tn, m // tm),
        in_specs=[cspec, lspec, cspec, lspec, cspec, lspec, gate(lay.ga), gate(lay.gb), gate(lay.gc),
                  pl.BlockSpec((3, bw, tn), lambda j, i: (0, 0, j))],
        out_specs=pl.BlockSpec((tm, tn), lambda j, i: (i, j)),
        out_shape=jax.ShapeDtypeStruct((m, d), bf16),
        compiler_params=_params(("arbitrary", "arbitrary")),
        name="merge",
    )(o_a[0], o_a[1], o_b[0], o_b[1], o_c[0], o_c[1], z, z, z, w_branch)


def _proj_residual_kernel(m_ref, w_ref, x_ref, g_ref, o_ref):
    o_ref[...] = x_ref[...] + g_ref[...] * _dot(m_ref[...], w_ref[...])


def _proj_residual(mm, w, x, gate, group_of_tile, tm, tn):
    m, k = mm.shape
    d = w.shape[1]
    return pl.pallas_call(
        _proj_residual_kernel,
        grid=(m // tm, d // tn),
        in_specs=[
            pl.BlockSpec((tm, k), lambda i, j: (i, 0)),
            pl.BlockSpec((k, tn), lambda i, j: (0, j)),
            pl.BlockSpec((tm, tn), lambda i, j: (i, j)),
            pl.BlockSpec((None, 1, tn), lambda i, j: (group_of_tile(i), 0, j)),
        ],
        out_specs=pl.BlockSpec((tm, tn), lambda i, j: (i, j)),
        out_shape=jax.ShapeDtypeStruct((m, d), f32),
        compiler_params=_params(("arbitrary", "arbitrary")),
        name="out_proj",
    )(mm, w, x, gate)


def _ffn_down_kernel(a_ref, g_ref, gprev_ref, gnext_ref, cw_ref, cb_ref, w_ref, x_ref, gate_ref, o_ref,
                     *, m_ctx, s_len, t_len, n_k):
    i = pl.program_id(0)
    k = pl.program_id(1)

    @pl.when(k == 0)
    def _():
        o_ref[...] = jnp.zeros_like(o_ref)

    g = g_ref[...]
    tm, tk = g.shape
    row = i * tm + lax.broadcasted_iota(jnp.int32, (tm, LANES), 0)
    pos = jnp.where(row < m_ctx, _mod_const(row, s_len), _mod_const(row - m_ctx, t_len))
    last = jnp.where(row < m_ctx, s_len - 1, t_len - 1)
    keep_prev = jnp.tile(jnp.where(pos == 0, 0.0, 1.0), (1, tk // LANES))
    keep_next = jnp.tile(jnp.where(pos == last, 0.0, 1.0), (1, tk // LANES))
    sub = lax.broadcasted_iota(jnp.int32, (SUBLANES, tk), 0)
    down = pltpu.roll(g, 1, 0)
    g_prev = jnp.concatenate(
        [jnp.where(sub == 0, gprev_ref[SUBLANES - 1:SUBLANES, :], down[:SUBLANES]), down[SUBLANES:]], axis=0)
    up = pltpu.roll(g, tm - 1, 0)
    g_next = jnp.concatenate(
        [up[:tm - SUBLANES], jnp.where(sub == SUBLANES - 1, gnext_ref[0:1, :], up[tm - SUBLANES:])], axis=0)
    conv = (g_prev * keep_prev * cw_ref[0:1, :] + g * cw_ref[1:2, :] + g_next * keep_next * cw_ref[2:3, :]
            + cb_ref[...])
    u = (_silu(conv) * a_ref[...]).astype(bf16)

    o_ref[...] = _dot(u, w_ref[...]) + o_ref[...]

    @pl.when(k == n_k - 1)
    def _():
        o_ref[...] = x_ref[...] + gate_ref[...] * o_ref[...]


def _ffn_down(ag, conv_w, conv_b, w_down, x, gate, group_of_tile, tm, tk, m_ctx, s_len, t_len):
    m, d = x.shape
    f = w_down.shape[0]
    nk = f // tk
    nsub = tm // SUBLANES
    last_sub = m // SUBLANES - 1
    return pl.pallas_call(
        functools.partial(_ffn_down_kernel, m_ctx=m_ctx, s_len=s_len, t_len=t_len, n_k=nk),
        grid=(m // tm, nk),
        in_specs=[
            pl.BlockSpec((tm, tk), lambda i, k: (i, k)),
            pl.BlockSpec((tm, tk), lambda i, k: (i, nk + k)),
            pl.BlockSpec((SUBLANES, tk), lambda i, k: (jnp.maximum(i * nsub - 1, 0), nk + k)),
            pl.BlockSpec((SUBLANES, tk), lambda i, k: (jnp.minimum((i + 1) * nsub, last_sub), nk + k)),
            pl.BlockSpec((3, tk), lambda i, k: (0, k)),
            pl.BlockSpec((1, tk), lambda i, k: (0, k)),
            pl.BlockSpec((tk, d), lambda i, k: (k, 0)),
            pl.BlockSpec((tm, d), lambda i, k: (i, 0)),
            pl.BlockSpec((None, 1, d), lambda i, k: (group_of_tile(i), 0, 0)),
        ],
        out_specs=pl.BlockSpec((tm, d), lambda i, k: (i, 0)),
        out_shape=jax.ShapeDtypeStruct((m, d), f32),
        compiler_params=_params(("arbitrary", "arbitrary")),
        name="ffn_down",
    )(ag, ag, ag, ag, conv_w, conv_b.reshape(1, f), w_down, x, gate)


def kernel(x_prompt, x_sample, cache_mla_ckv, cache_mla_kpe, cache_na_k, cache_na_v, state_hgrn, c, c_ctx, w_mod, b_mod, norm1_g, norm2_g, w_in, mla_kv_norm_g, mla_q_norm_g, mla_k_norm_g, mla_w_uk, mla_w_uv, hgrn_lower_bounds, hgrn_norm_g, na_q_norm_g, na_k_norm_g, na_rpb, w_branch, w_out, ffn_w_up, ffn_conv_w, ffn_conv_b, ffn_w_down):
    n_ctx, s_len, d = x_prompt.shape
    n_lat, t_len, _ = x_sample.shape
    depth = w_in.shape[0]
    p_len = cache_mla_ckv.shape[2]
    ffn = ffn_w_down.shape[1]
    m_ctx, m_lat = n_ctx * s_len, n_lat * t_len
    m = m_ctx + m_lat
    assert m_ctx % t_len == 0, "context rows must be a whole number of latent sequences"

    tm = _pow2_tile(1024, m_ctx, t_len)
    tn = _pow2_tile(1024, d)
    tp = _pow2_tile(512, m_ctx, t_len)
    lay = _ZLayout(d, tn)

    def group_fn(rows):
        return lambda i: jnp.where(i < m_ctx // rows, 0, 1 + (i - m_ctx // rows) // (t_len // rows))

    group_of_tile = group_fn(tm)

    n_groups = -(-(1 + n_lat) // SUBLANES) * SUBLANES
    cond = jnp.concatenate([c_ctx[None], c, jnp.zeros((n_groups - 1 - n_lat, d), f32)], axis=0)
    mods = _modulation(cond, w_mod, b_mod).reshape(depth, n_groups, 6, d).transpose(0, 2, 1, 3)[:, :, :, None, :]

    sm = jax.nn.softmax(hgrn_lower_bounds.astype(f32), axis=1)
    csum = jnp.cumsum(sm, axis=1)
    lower = csum - csum[:, :1]
    hg_tabs = tuple(jnp.asarray(t, dt) for t, dt in zip(_hgrn_tables(), (bf16, f32, bf16)))

    cos, sin = _rope_tables(t_len, tp)
    n_ctx_tp = m_ctx // tp

    def rope_block(i):
        return jnp.where(i < n_ctx_tp, 0, 1 + (i - n_ctx_tp) % (t_len // tp))

    x = jnp.concatenate([x_prompt.reshape(m_ctx, d), x_sample.reshape(m_lat, d)], axis=0)
    st_mla = st_na = st_hg = None
    for l in range(depth):
        sh1, sc1, g1, sh2, sc2, g2 = (mods[l, k] for k in range(6))
        w_in_l, w_kpe_l = _permute_w_in(w_in[l], lay)
        z, kpe = _normmod_matmul(x, norm1_g[l], sc1, sh1, w_in_l, group_of_tile, tm, tn, "in_proj", w_side=w_kpe_l)

        w_uk = mla_w_uk[l].reshape(MLA_KV_RANK, MLA_HEADS * MLA_NOPE).astype(bf16)
        w_uv = mla_w_uv[l].reshape(MLA_KV_RANK, MLA_HEADS * MLA_V).astype(bf16)
        q_mla = _mla_q(z, lay, mla_q_norm_g[l], cos, sin, rope_block, tp)
        k_mla, v_mla, *st_mla = _mla_kv(
            z, lay.ckv, kpe, 0, mla_kv_norm_g[l], mla_k_norm_g[l], w_uk, w_uv, cos, sin, rope_block, tp, True, True,
            "mla_kv", state=_CtxState(st_mla, (MLA_KV_RANK, MLA_ROPE), l, depth, n_ctx, s_len, tp))
        ckv_c = cache_mla_ckv[:, l].reshape(n_lat * p_len, MLA_KV_RANK)
        kpe_c = jnp.pad(cache_mla_kpe[:, l].reshape(n_lat * p_len, MLA_ROPE), ((0, 0), (0, LANES - MLA_ROPE)))
        tc = _pow2_tile(512, n_lat * p_len)
        k_c, v_c = _mla_kv(ckv_c, 0, kpe_c, 0, mla_kv_norm_g[l], mla_k_norm_g[l], w_uk, w_uv,
                              cos, sin, lambda i: 0, tc, False, False, "mla_kv_cache")
        tq_c = _pow2_tile(256, s_len)
        o_mla_c = _flash(q_mla, k_mla, v_mla, batch=n_ctx, heads=MLA_HEADS, t_len=s_len, s_len=s_len, row0=0,
                         dq=MLA_QPAD, dv=MLA_V, tq=tq_c, tk=tq_c, heads_per_step=MLA_HEADS, name="mla_attn_ctx")
        tq_l = _pow2_tile(512, t_len, p_len)
        o_mla_l = _flash(q_mla, k_mla, v_mla, batch=n_lat, heads=MLA_HEADS, t_len=t_len, s_len=t_len, row0=m_ctx,
                         dq=MLA_QPAD, dv=MLA_V, tq=tq_l, tk=tq_l, heads_per_step=2, extra=(k_c, v_c, p_len),
                         name="mla_attn_latent")

        lb = lower[:, l]
        lb_tabs = (jnp.log(lb) * LOG2E, jnp.log1p(-lb) * LOG2E, 1.0 - lb)
        o_hg_c, st_hg = _hgrn(z, lay, lb_tabs, hgrn_norm_g[l], hg_tabs, batch=n_ctx, t_len=s_len, row0=0,
                              s0=None, state_out=(st_hg, l, depth))
        o_hg_l, _ = _hgrn(z, lay, lb_tabs, hgrn_norm_g[l], hg_tabs, batch=n_lat, t_len=t_len, row0=m_ctx,
                          s0=state_hgrn[:, l])

        q_na, k_na, v_na, *st_na = _na_prep(
            z, lay, na_q_norm_g[l], na_k_norm_g[l], tp,
            _CtxState(st_na, (NA_HEADS * NA_DH, NA_HEADS * NA_DH), l, depth, n_ctx, s_len, tp))
        o_na_c = _flash(q_na, k_na, v_na, batch=n_ctx, heads=NA_HEADS, t_len=s_len, s_len=s_len, row0=0,
                        dq=NA_DH, dv=NA_DH, tq=tq_c, tk=tq_c, heads_per_step=NA_HEADS, name="na_attn_ctx")
        kc_na = cache_na_k[:, l].reshape(n_lat * p_len, NA_HEADS * NA_DH).astype(bf16)
        vc_na = cache_na_v[:, l].reshape(n_lat * p_len, NA_HEADS * NA_DH).astype(bf16)
        bias = _na_bias_table(na_rpb[l], t_len // GRID_W)
        o_na_l = _na_latent(q_na, k_na, v_na, kc_na, vc_na, bias, batch=n_lat, t_len=t_len, p_len=p_len, row0=m_ctx)

        merged = _merge((o_mla_c, o_mla_l), (o_hg_c, o_hg_l), (o_na_c, o_na_l), z, lay, w_branch[l].astype(bf16), tp, tn)
        x = _proj_residual(merged, w_out[l].astype(bf16), x, g1, group_fn(tp), tp, d)

        ag = _normmod_matmul(x, norm2_g[l], sc2, sh2, ffn_w_up[l].astype(bf16), group_of_tile, tm,
                             _pow2_tile(1024, 2 * ffn), "ffn_up")
        tmd = _pow2_tile(1024, m_ctx, t_len)
        x = _ffn_down(ag, ffn_conv_w[l], ffn_conv_b[l], ffn_w_down[l].astype(bf16), x, g2, group_fn(tmd), tmd,
                      _pow2_tile(512, ffn), m_ctx, s_len, t_len)


    y_prompt = x[:m_ctx].reshape(n_ctx, s_len, d)
    y_sample = x[m_ctx:].reshape(n_lat, t_len, d)
    na_shape = (n_ctx, depth, s_len, NA_HEADS, NA_DH)
    return (y_prompt, y_sample, st_mla[0], st_mla[1], st_na[0].reshape(na_shape), st_na[1].reshape(na_shape), st_hg)
```

```python
import functools
import math

import numpy as np
import jax
import jax.numpy as jnp
from jax import lax
from jax.experimental import pallas as pl
from jax.experimental.pallas import tpu as pltpu

f32 = jnp.float32
bf16 = jnp.bfloat16

GRID_W = 64
MLA_HEADS, MLA_NOPE, MLA_ROPE, MLA_V, MLA_KV_RANK = 8, 128, 64, 128, 512
MLA_QK = MLA_NOPE + MLA_ROPE
HG_HEADS, HG_K, HG_V = 8, 128, 128
NA_HEADS, NA_DH, NA_WIN_R, NA_WIN_C = 8, 128, 8, 16
ROPE_BASE = 10000.0
EPS = 1e-6
NEG = -1e30
LOG2E = 1.4426950408889634

LANES = 128
SUBLANES = 8
VMEM_LIMIT_BYTES = 56 * 1024 * 1024

HEAD_W = 128
MLA_QPAD = 256
HG_CHUNK = 128
HG_LEVELS = int(math.log2(HG_CHUNK))
NA_QROWS = 8
NA_KROWS = 16


def _params(semantics):
    return pltpu.CompilerParams(dimension_semantics=semantics, vmem_limit_bytes=VMEM_LIMIT_BYTES)


def _pow2_tile(pref, *dims):
    t = pref
    while any(d % t for d in dims):
        t //= 2
    assert t >= SUBLANES, (pref, dims)
    return t


def _dot(a, b):
    return jnp.dot(a, b, preferred_element_type=f32)


def _dot_nt(a, b):
    return lax.dot_general(a, b, (((1,), (1,)), ((), ())), preferred_element_type=f32)


def _dot_tn(a, b):
    return lax.dot_general(a, b, (((0,), (0,)), ((), ())), preferred_element_type=f32)


def _silu(x):
    return x * jax.nn.sigmoid(x)


def _mod_const(x, n):
    return jnp.bitwise_and(x, n - 1) if n & (n - 1) == 0 else lax.rem(x, n)


class _ZLayout:
    def __init__(self, d_model, tn):
        self.qn = 0
        self.qp = self.qn + MLA_HEADS * MLA_NOPE
        self.ckv = self.qp + MLA_HEADS * MLA_ROPE
        self.hq = self.ckv + MLA_KV_RANK
        self.hff = self.hq + HG_HEADS * HG_K
        self.hfb = self.hff + HG_HEADS * HG_K
        self.hi = self.hfb + HG_HEADS * HG_K
        self.hg = self.hi + HG_HEADS * HG_V
        self.nq = self.hg + HG_HEADS * HG_V
        self.nk = self.nq + NA_HEADS * NA_DH
        self.nv = self.nk + NA_HEADS * NA_DH
        self.ga = self.nv + NA_HEADS * NA_DH
        self.gb = self.ga + d_model
        self.gc = self.gb + d_model
        self.width = self.gc + d_model
        assert self.width % tn == 0


def _permute_w_in(w, lay):
    d = w.shape[0]
    n_mq = MLA_HEADS * MLA_QK
    mq = w[:, :n_mq].reshape(d, MLA_HEADS, MLA_QK)
    qn = mq[:, :, :MLA_NOPE].reshape(d, MLA_HEADS * MLA_NOPE)
    qp = mq[:, :, MLA_NOPE:].reshape(d, MLA_HEADS * MLA_ROPE)
    ckv = w[:, n_mq:n_mq + MLA_KV_RANK]
    kpe = w[:, n_mq + MLA_KV_RANK:n_mq + MLA_KV_RANK + MLA_ROPE]
    rest = w[:, n_mq + MLA_KV_RANK + MLA_ROPE:]
    main = jnp.concatenate([qn, qp, ckv, rest], axis=1).astype(bf16)
    assert main.shape[1] == lay.width
    return main, jnp.pad(kpe, ((0, 0), (0, LANES - MLA_ROPE))).astype(bf16)


def _mod_kernel(c_ref, w_ref, b_ref, o_ref):
    a = _silu(c_ref[...]).astype(bf16)
    o_ref[...] = _dot(a, w_ref[...].astype(bf16)) + b_ref[...]


def _modulation(cond, w_mod, b_mod):
    depth, d, n = w_mod.shape
    g = cond.shape[0]
    tn = _pow2_tile(1024, n)
    return pl.pallas_call(
        _mod_kernel,
        grid=(depth, n // tn),
        in_specs=[
            pl.BlockSpec((g, d), lambda l, j: (0, 0)),
            pl.BlockSpec((None, d, tn), lambda l, j: (l, 0, j)),
            pl.BlockSpec((None, 1, tn), lambda l, j: (l, 0, j)),
        ],
        out_specs=pl.BlockSpec((None, g, tn), lambda l, j: (l, 0, j)),
        out_shape=jax.ShapeDtypeStruct((depth, g, n), f32),
        compiler_params=_params(("arbitrary", "arbitrary")),
        name="modulation",
    )(cond, w_mod, b_mod.reshape(depth, 1, n))


def _normmod_matmul_kernel(x_ref, g_ref, sc_ref, sh_ref, w_ref, *rest):
    if len(rest) == 4:
        ws_ref, o_ref, os_ref, h_scr = rest
    else:
        (o_ref, h_scr), ws_ref, os_ref = rest, None, None

    @pl.when(pl.program_id(1) == 0)
    def _():
        x = x_ref[...]
        ms = jnp.mean(x * x, axis=-1, keepdims=True)
        y = x * lax.rsqrt(ms + EPS) * g_ref[...]
        h_scr[...] = (y * (1.0 + sc_ref[...]) + sh_ref[...]).astype(bf16)
        if ws_ref is not None:
            os_ref[...] = _dot(h_scr[...], ws_ref[...])

    o_ref[...] = _dot(h_scr[...], w_ref[...])


def _normmod_matmul(x, g, sc, sh, w, group_of_tile, tm, tn, name, w_side=None):
    m, d = x.shape
    n = w.shape[1]
    in_specs = [
        pl.BlockSpec((tm, d), lambda i, j: (i, 0)),
        pl.BlockSpec((1, d), lambda i, j: (0, 0)),
        pl.BlockSpec((None, 1, d), lambda i, j: (group_of_tile(i), 0, 0)),
        pl.BlockSpec((None, 1, d), lambda i, j: (group_of_tile(i), 0, 0)),
        pl.BlockSpec((d, tn), lambda i, j: (0, j)),
    ]
    args = [x, g.reshape(1, d), sc, sh, w]
    out_specs = [pl.BlockSpec((tm, tn), lambda i, j: (i, j))]
    out_shape = [jax.ShapeDtypeStruct((m, n), f32)]
    if w_side is not None:
        ns = w_side.shape[1]
        in_specs.append(pl.BlockSpec((d, ns), lambda i, j: (0, 0)))
        args.append(w_side)
        out_specs.append(pl.BlockSpec((tm, ns), lambda i, j: (i, 0)))
        out_shape.append(jax.ShapeDtypeStruct((m, ns), f32))
    res = pl.pallas_call(
        _normmod_matmul_kernel,
        grid=(m // tm, n // tn),
        in_specs=in_specs,
        out_specs=out_specs,
        out_shape=out_shape,
        scratch_shapes=[pltpu.VMEM((tm, d), bf16)],
        compiler_params=_params(("arbitrary", "arbitrary")),
        name=name,
    )(*args)
    return res if w_side is not None else res[0]


def _rope(x, cos, sin_signed):
    n = x.shape[-1]
    half = MLA_ROPE // 4
    lane = lax.broadcasted_iota(jnp.int32, x.shape, 1)
    partner = jnp.where((lane % (2 * half)) < half, pltpu.roll(x, n - half, 1), pltpu.roll(x, half, 1))
    return x * cos + partner * sin_signed


def _mla_q_kernel(qn_ref, qp_ref, gn_ref, gp_ref, cos_ref, sin_ref, o_ref, *, scale):
    qn = qn_ref[...]
    qp = qp_ref[...]
    tm = qn.shape[0]
    lane = lax.broadcasted_iota(jnp.int32, (tm, LANES), 1)
    low = lane < MLA_ROPE
    cos = cos_ref[...]
    sin = sin_ref[...]
    gn = gn_ref[...]
    gp = gp_ref[...]
    for pair in range(MLA_HEADS // 2):
        pcol = qp[:, pair * LANES:(pair + 1) * LANES]
        sq = pcol * pcol
        ss_all = jnp.sum(sq, axis=-1, keepdims=True)
        ss_low = jnp.sum(jnp.where(low, sq, 0.0), axis=-1, keepdims=True)
        rot = _rope(pcol * gp, cos, sin)
        for k, ss_pe in enumerate((ss_low, ss_all - ss_low)):
            h = 2 * pair + k
            nope = qn[:, h * MLA_NOPE:(h + 1) * MLA_NOPE]
            ss = jnp.sum(nope * nope, axis=-1, keepdims=True) + ss_pe
            r = lax.rsqrt(ss * (1.0 / MLA_QK) + EPS) * scale
            pe = rot if k == 0 else pltpu.roll(rot, MLA_ROPE, 1)
            o_ref[:, h * MLA_QPAD:h * MLA_QPAD + MLA_NOPE] = (nope * r * gn).astype(bf16)
            o_ref[:, h * MLA_QPAD + MLA_NOPE:(h + 1) * MLA_QPAD] = jnp.where(low, pe * r, 0.0).astype(bf16)


class _CtxState:
    def __init__(self, prevs, widths, layer, depth, n_ctx, s_len, tm):
        assert tm % s_len == 0
        self.n_seq = tm // s_len
        self.s_len = s_len
        self.n_tiles = n_ctx * s_len // tm
        self.prevs = [] if prevs is None else list(prevs)
        n_tiles = self.n_tiles
        self.out_specs = [pl.BlockSpec((self.n_seq, None, s_len, w), lambda i: (jnp.minimum(i, n_tiles - 1), layer, 0, 0))
                          for w in widths]
        self.out_shape = [jax.ShapeDtypeStruct((n_ctx, depth, s_len, w), f32) for w in widths]
        self.in_specs = [pl.BlockSpec(memory_space=pl.ANY)] * len(self.prevs)

    def aliases(self, n_inputs_before, n_outputs_before):
        return {n_inputs_before + k: n_outputs_before + k for k in range(len(self.prevs))}

    def kernel_kwargs(self):
        return dict(n_prev=len(self.prevs), n_ctx_tiles=self.n_tiles, s_len=self.s_len)


def _mla_kv_kernel(*refs, norm_ckv, rope, n_prev=0, n_ctx_tiles=0, s_len=0):
    ckv_ref, kpe_ref, kvg_ref, gn_ref, gp_ref, wuk_ref, wuv_ref, cos_ref, sin_ref = refs[:9]
    k_ref, v_ref, *state_refs = refs[9 + n_prev:]
    ckv = ckv_ref[...]
    if norm_ckv:
        ms = jnp.mean(ckv * ckv, axis=-1, keepdims=True)
        ckv = ckv * lax.rsqrt(ms + EPS) * kvg_ref[...]
    if state_refs:
        ckv_state_ref, kpe_state_ref = state_refs

        @pl.when(pl.program_id(0) < n_ctx_tiles)
        def _():
            for n in range(ckv.shape[0] // s_len):
                ckv_state_ref[n] = ckv[n * s_len:(n + 1) * s_len]
                kpe_state_ref[n] = kpe_ref[n * s_len:(n + 1) * s_len, :MLA_ROPE]

    cb = ckv.astype(bf16)
    kn = _dot(cb, wuk_ref[...])
    v_ref[...] = _dot(cb, wuv_ref[...]).astype(bf16)
    tm = ckv.shape[0]
    lane = lax.broadcasted_iota(jnp.int32, (tm, LANES), 1)
    low = lane < MLA_ROPE
    kpe = jnp.where(low, kpe_ref[...], 0.0)
    ss_pe = jnp.sum(kpe * kpe, axis=-1, keepdims=True)
    pe = kpe * gp_ref[...]
    if rope:
        pe = _rope(pe, cos_ref[...], sin_ref[...])
    gn = gn_ref[...]
    for h in range(MLA_HEADS):
        nope = kn[:, h * MLA_NOPE:(h + 1) * MLA_NOPE]
        ss = jnp.sum(nope * nope, axis=-1, keepdims=True) + ss_pe
        r = lax.rsqrt(ss * (1.0 / MLA_QK) + EPS)
        k_ref[:, h * MLA_QPAD:h * MLA_QPAD + MLA_NOPE] = (nope * r * gn).astype(bf16)
        k_ref[:, h * MLA_QPAD + MLA_NOPE:(h + 1) * MLA_QPAD] = jnp.where(low, pe * r, 0.0).astype(bf16)


def _rope_tables(t_len, tm):
    n_freq = MLA_ROPE // 4
    t = jnp.arange(t_len, dtype=jnp.int32)
    inv = ROPE_BASE ** (-jnp.arange(n_freq, dtype=f32) / n_freq)
    ang_r = (t // GRID_W).astype(f32)[:, None] * inv
    ang_c = (t % GRID_W).astype(f32)[:, None] * inv
    cos64 = jnp.concatenate([jnp.cos(ang_r), jnp.cos(ang_r), jnp.cos(ang_c), jnp.cos(ang_c)], axis=1)
    sin64 = jnp.concatenate([-jnp.sin(ang_r), jnp.sin(ang_r), -jnp.sin(ang_c), jnp.sin(ang_c)], axis=1)
    cos = jnp.concatenate([jnp.ones((tm, LANES), f32), jnp.tile(cos64, (1, 2))], axis=0)
    sin = jnp.concatenate([jnp.zeros((tm, LANES), f32), jnp.tile(sin64, (1, 2))], axis=0)
    return cos, sin


def _mla_q(z, lay, q_g, cos, sin, rope_block, tm):
    m = z.shape[0]
    gn = q_g[:MLA_NOPE].reshape(1, MLA_NOPE)
    gp = jnp.tile(q_g[MLA_NOPE:], 2).reshape(1, LANES)
    wn, wp = MLA_HEADS * MLA_NOPE, MLA_HEADS * MLA_ROPE
    return pl.pallas_call(
        functools.partial(_mla_q_kernel, scale=MLA_QK ** -0.5 * LOG2E),
        grid=(m // tm,),
        in_specs=[
            pl.BlockSpec((tm, wn), lambda i: (i, lay.qn // wn)),
            pl.BlockSpec((tm, wp), lambda i: (i, lay.qp // wp)),
            pl.BlockSpec((1, MLA_NOPE), lambda i: (0, 0)),
            pl.BlockSpec((1, LANES), lambda i: (0, 0)),
            pl.BlockSpec((tm, LANES), lambda i: (rope_block(i), 0)),
            pl.BlockSpec((tm, LANES), lambda i: (rope_block(i), 0)),
        ],
        out_specs=pl.BlockSpec((tm, MLA_HEADS * MLA_QPAD), lambda i: (i, 0)),
        out_shape=jax.ShapeDtypeStruct((m, MLA_HEADS * MLA_QPAD), bf16),
        compiler_params=_params(("arbitrary",)),
        name="mla_q",
    )(z, z, gn, gp, cos, sin)


def _mla_kv(ckv_src, ckv_col, kpe_src, kpe_col, kv_g, k_g, w_uk, w_uv, cos, sin, rope_block, tm, norm_ckv, rope, name,
            state=None):
    m = ckv_src.shape[0]
    gn = k_g[:MLA_NOPE].reshape(1, MLA_NOPE)
    gp = jnp.concatenate([k_g[MLA_NOPE:], jnp.zeros((LANES - MLA_ROPE,), f32)]).reshape(1, LANES)
    hw = MLA_HEADS * MLA_NOPE
    in_specs = [
        pl.BlockSpec((tm, MLA_KV_RANK), lambda i: (i, ckv_col // MLA_KV_RANK)),
        pl.BlockSpec((tm, LANES), lambda i: (i, kpe_col // LANES)),
        pl.BlockSpec((1, MLA_KV_RANK), lambda i: (0, 0)),
        pl.BlockSpec((1, MLA_NOPE), lambda i: (0, 0)),
        pl.BlockSpec((1, LANES), lambda i: (0, 0)),
        pl.BlockSpec((MLA_KV_RANK, hw), lambda i: (0, 0)),
        pl.BlockSpec((MLA_KV_RANK, hw), lambda i: (0, 0)),
        pl.BlockSpec((tm, LANES), lambda i: (rope_block(i), 0)),
        pl.BlockSpec((tm, LANES), lambda i: (rope_block(i), 0)),
    ]
    args = [ckv_src, kpe_src, kv_g.reshape(1, MLA_KV_RANK), gn, gp, w_uk, w_uv, cos, sin]
    out_specs = [pl.BlockSpec((tm, MLA_HEADS * MLA_QPAD), lambda i: (i, 0)), pl.BlockSpec((tm, hw), lambda i: (i, 0))]
    out_shape = [jax.ShapeDtypeStruct((m, MLA_HEADS * MLA_QPAD), bf16), jax.ShapeDtypeStruct((m, hw), bf16)]
    kwargs, aliases = {}, {}
    if state is not None:
        aliases = state.aliases(len(args), len(out_specs))
        kwargs = state.kernel_kwargs()
        in_specs += state.in_specs
        args += state.prevs
        out_specs += state.out_specs
        out_shape += state.out_shape
    return pl.pallas_call(
        functools.partial(_mla_kv_kernel, norm_ckv=norm_ckv, rope=rope, **kwargs),
        grid=(m // tm,),
        in_specs=in_specs,
        out_specs=out_specs,
        out_shape=out_shape,
        input_output_aliases=aliases,
        compiler_params=_params(("arbitrary",)),
        name=name,
    )(*args)


def _flash_kernel(*refs, n_main, tk, n_extra, heads, dq, dv):
    if n_extra:
        q_ref, k_ref, v_ref, kx_ref, vx_ref, o_ref = refs
    else:
        q_ref, k_ref, v_ref, o_ref = refs
        kx_ref = vx_ref = None
    tq = q_ref.shape[0]
    chunks = [(k_ref, v_ref, c) for c in range(n_main)] + [(kx_ref, vx_ref, c) for c in range(n_extra)]
    def scores(i, g):
        kr, _, c = chunks[i]
        return _dot_nt(kr[c * tk:(c + 1) * tk, g * dq:(g + 1) * dq], q_ref[:, g * dq:(g + 1) * dq])

    m = [jnp.full((1, tq), -jnp.inf, f32)] * heads
    l = [jnp.zeros((1, tq), f32)] * heads
    acc = [jnp.zeros((dv, tq), f32)] * heads
    s = [scores(0, g) for g in range(heads)]
    for i, (_, vr, c) in enumerate(chunks):
        for g in range(heads):
            s_next = scores(i + 1, g) if i + 1 < len(chunks) else None
            m_new = jnp.maximum(m[g], jnp.max(s[g], axis=0, keepdims=True))
            alpha = jnp.exp2(m[g] - m_new)
            p = jnp.exp2(s[g] - m_new)
            l[g] = alpha * l[g] + jnp.sum(p, axis=0, keepdims=True)
            acc[g] = alpha * acc[g] + _dot_tn(vr[c * tk:(c + 1) * tk, g * dv:(g + 1) * dv], p.astype(bf16))
            m[g], s[g] = m_new, s_next
    for g in range(heads):
        o_ref[:, g * dv:(g + 1) * dv] = (acc[g] / l[g]).T.astype(o_ref.dtype)


def _flash(q, k, v, *, batch, heads, t_len, s_len, row0, dq, dv, tq, tk, heads_per_step=1, extra=None, name):
    assert row0 % tq == 0 and row0 % s_len == 0 and t_len % tq == 0 and s_len % tk == 0
    assert heads % heads_per_step == 0
    nq = t_len // tq
    g = heads_per_step
    in_specs = [
        pl.BlockSpec((tq, g * dq), lambda b, h, i: (row0 // tq + b * nq + i, h)),
        pl.BlockSpec((s_len, g * dq), lambda b, h, i: (row0 // s_len + b, h)),
        pl.BlockSpec((s_len, g * dv), lambda b, h, i: (row0 // s_len + b, h)),
    ]
    args = [q, k, v]
    n_extra = 0
    if extra is not None:
        kx, vx, p_len = extra
        assert p_len % tk == 0
        n_extra = p_len // tk
        in_specs += [
            pl.BlockSpec((p_len, g * dq), lambda b, h, i: (b, h)),
            pl.BlockSpec((p_len, g * dv), lambda b, h, i: (b, h)),
        ]
        args += [kx, vx]
    return pl.pallas_call(
        functools.partial(_flash_kernel, n_main=s_len // tk, tk=tk, n_extra=n_extra, heads=g, dq=dq, dv=dv),
        grid=(batch, heads // g, nq),
        in_specs=in_specs,
        out_specs=pl.BlockSpec((tq, g * dv), lambda b, h, i: (b * nq + i, h)),
        out_shape=jax.ShapeDtypeStruct((batch * t_len, heads * dv), bf16),
        compiler_params=_params(("arbitrary", "arbitrary", "arbitrary")),
        name=name,
    )(*args)


def _na_prep_kernel(*refs, scale, n_prev, n_ctx_tiles, s_len):
    nq_ref, nk_ref, nv_ref, qg_ref, kg_ref = refs[:5]
    q_ref, k_ref, v_ref, k_state_ref, v_state_ref = refs[5 + n_prev:]
    qg = qg_ref[...]
    kg = kg_ref[...]
    tm = nv_ref.shape[0]
    is_ctx = pl.program_id(0) < n_ctx_tiles
    v_ref[...] = nv_ref[...].astype(bf16)

    @pl.when(is_ctx)
    def _():
        for n in range(tm // s_len):
            v_state_ref[n] = nv_ref[n * s_len:(n + 1) * s_len, :]

    for h in range(NA_HEADS):
        sl = slice(h * NA_DH, (h + 1) * NA_DH)
        q = nq_ref[:, sl]
        k = nk_ref[:, sl]
        rq = lax.rsqrt(jnp.mean(q * q, axis=-1, keepdims=True) + EPS)
        rk = lax.rsqrt(jnp.mean(k * k, axis=-1, keepdims=True) + EPS)
        q_ref[:, sl] = (q * rq * qg * scale).astype(bf16)
        kn = k * rk * kg
        k_ref[:, sl] = kn.astype(bf16)

        @pl.when(is_ctx)
        def _():
            for n in range(tm // s_len):
                k_state_ref[n, :, sl] = kn[n * s_len:(n + 1) * s_len]


def _na_prep(z, lay, q_g, k_g, tm, state):
    m = z.shape[0]
    w = NA_HEADS * NA_DH
    spec = pl.BlockSpec((tm, w), lambda i: (i, 0))
    in_specs = [
        pl.BlockSpec((tm, w), lambda i: (i, lay.nq // w)),
        pl.BlockSpec((tm, w), lambda i: (i, lay.nk // w)),
        pl.BlockSpec((tm, w), lambda i: (i, lay.nv // w)),
        pl.BlockSpec((1, NA_DH), lambda i: (0, 0)),
        pl.BlockSpec((1, NA_DH), lambda i: (0, 0)),
    ]
    args = [z, z, z, q_g.reshape(1, NA_DH), k_g.reshape(1, NA_DH)]
    return pl.pallas_call(
        functools.partial(_na_prep_kernel, scale=NA_DH ** -0.5 * LOG2E, **state.kernel_kwargs()),
        grid=(m // tm,),
        in_specs=in_specs + state.in_specs,
        out_specs=[spec, spec, spec] + state.out_specs,
        out_shape=[jax.ShapeDtypeStruct((m, w), bf16)] * 3 + state.out_shape,
        input_output_aliases=state.aliases(len(args), 3),
        compiler_params=_params(("arbitrary",)),
        name="na_prep",
    )(*args, *state.prevs)


def _na_static_maps(rows):
    nblk = rows // NA_QROWS
    reps = (0, min(1, nblk - 1), nblk - 1)
    dr_map = np.full((3, NA_QROWS, NA_KROWS), 2 * NA_WIN_R - 1, np.int32)
    for v, kb in enumerate(reps):
        ws = min(max(NA_QROWS * kb - NA_WIN_R // 2, 0), rows - NA_KROWS)
        for i in range(NA_QROWS):
            qrow = NA_QROWS * kb + i
            rs = min(max(qrow - NA_WIN_R // 2, 0), rows - NA_WIN_R)
            for j in range(NA_KROWS):
                krow = ws + j
                if rs <= krow < rs + NA_WIN_R:
                    dr_map[v, i, j] = krow - qrow + NA_WIN_R - 1
    qcol = np.arange(GRID_W)
    cs = np.clip(qcol - NA_WIN_C // 2, 0, GRID_W - NA_WIN_C)
    in_win = (qcol[None, :] >= cs[:, None]) & (qcol[None, :] < cs[:, None] + NA_WIN_C)
    dc_idx = np.clip(qcol[None, :] - qcol[:, None], -(NA_WIN_C - 1), NA_WIN_C - 1) + (NA_WIN_C - 1)
    return dr_map, in_win, dc_idx


def _na_bias_table(rpb, rows):
    dr_map, in_win, dc_idx = _na_static_maps(rows)
    h = rpb.shape[0]
    bt = jnp.where(in_win.T[None, None], jnp.take(rpb * LOG2E, jnp.asarray(dc_idx.T), axis=2), NEG)
    bt = jnp.concatenate([bt, jnp.full((h, 1, GRID_W, GRID_W), NEG, f32)], axis=1)
    tab = jnp.take(bt, jnp.asarray(dr_map.transpose(0, 2, 1).reshape(-1)), axis=1)
    tab = tab.reshape(h, 3, NA_KROWS, NA_QROWS, GRID_W, GRID_W).transpose(1, 0, 2, 4, 3, 5)
    return tab.reshape(3, h, NA_KROWS * GRID_W, NA_QROWS * GRID_W)


def _na_kernel(q_ref, k_ref, v_ref, kc_ref, vc_ref, bias_ref, o_ref, *, rows, heads):
    kb = pl.program_id(2)
    ws = jnp.clip(NA_QROWS * kb - NA_WIN_R // 2, 0, rows - NA_KROWS)
    win = pl.ds(pl.multiple_of(ws * GRID_W, (NA_WIN_R // 2) * GRID_W), NA_KROWS * GRID_W)
    cols = [slice(g * NA_DH, (g + 1) * NA_DH) for g in range(heads)]
    s_loc = [_dot_nt(k_ref[win, c], q_ref[:, c]) + bias_ref[g] for g, c in enumerate(cols)]
    s_ctx = [_dot_nt(kc_ref[:, c], q_ref[:, c]) for c in cols]
    p_loc, p_ctx, l = [], [], []
    for g in range(heads):
        m = jnp.maximum(jnp.max(s_loc[g], axis=0, keepdims=True), jnp.max(s_ctx[g], axis=0, keepdims=True))
        pl_g = jnp.exp2(s_loc[g] - m)
        pc_g = jnp.exp2(s_ctx[g] - m)
        l.append(jnp.sum(pl_g, axis=0, keepdims=True) + jnp.sum(pc_g, axis=0, keepdims=True))
        p_loc.append(pl_g.astype(bf16))
        p_ctx.append(pc_g.astype(bf16))
    for g, c in enumerate(cols):
        acc = _dot_tn(v_ref[win, c], p_loc[g]) + _dot_tn(vc_ref[:, c], p_ctx[g])
        o_ref[:, c] = (acc / l[g]).T.astype(o_ref.dtype)


def _na_latent(q, k, v, kc, vc, bias, *, batch, t_len, p_len, row0, heads_per_step=2):
    rows = t_len // GRID_W
    assert t_len % GRID_W == 0 and rows % NA_QROWS == 0 and rows >= NA_KROWS
    tq = NA_QROWS * GRID_W
    assert row0 % tq == 0 and row0 % t_len == 0 and NA_HEADS % heads_per_step == 0
    nblk = rows // NA_QROWS
    g = heads_per_step
    gw = g * NA_DH

    def variant(i):
        return jnp.where(i == 0, 0, jnp.where(i == nblk - 1, 2, 1))

    return pl.pallas_call(
        functools.partial(_na_kernel, rows=rows, heads=g),
        grid=(batch, NA_HEADS // g, nblk),
        in_specs=[
            pl.BlockSpec((tq, gw), lambda b, h, i: (row0 // tq + b * nblk + i, h)),
            pl.BlockSpec((t_len, gw), lambda b, h, i: (row0 // t_len + b, h)),
            pl.BlockSpec((t_len, gw), lambda b, h, i: (row0 // t_len + b, h)),
            pl.BlockSpec((p_len, gw), lambda b, h, i: (b, h)),
            pl.BlockSpec((p_len, gw), lambda b, h, i: (b, h)),
            pl.BlockSpec((None, g, NA_KROWS * GRID_W, tq), lambda b, h, i: (variant(i), h, 0, 0)),
        ],
        out_specs=pl.BlockSpec((tq, gw), lambda b, h, i: (b * nblk + i, h)),
        out_shape=jax.ShapeDtypeStruct((batch * t_len, NA_HEADS * NA_DH), bf16),
        compiler_params=_params(("arbitrary", "arbitrary", "arbitrary")),
        name="na_latent",
    )(q, k, v, kc, vc, bias)


def _hgrn_tables():
    c, lv = HG_CHUNK, HG_LEVELS
    t = np.arange(c)[:, None]
    r = np.arange(c)[None, :]
    wcum = np.stack([r <= t, r >= t]).astype(np.float32)
    qside = np.zeros((2, lv * c, LANES), np.float32)
    mask = np.zeros((2, lv + 1, c, c), np.float32)
    mask[:, 0] = np.eye(c)
    for l in range(lv):
        hs = 1 << l
        blk = t // (2 * hs)
        mid = blk * (2 * hs) + hs
        qf = t >= mid
        qb = t < mid
        qside[0, l * c:(l + 1) * c] = np.where(qf, 1.0, -1.0)
        qside[1, l * c:(l + 1) * c] = np.where(qb, 1.0, -1.0)
        same = blk == blk.T
        mask[0, 1 + l] = same & qf & ~qf.T
        mask[1, 1 + l] = same & qb & ~qb.T
    return wcum, qside, mask


def _hgrn_kernel(*refs, n_chunks, heads, has_state, emit_state, n_prev=0):
    (hq_ref, hff_ref, hfb_ref, hi_ref, hg_ref, la_ref, lc_ref, om_ref, ng_ref,
     wcum_ref, qside_ref, mask_ref) = refs[:12]
    rest = list(refs[12:])
    s0_ref = rest.pop(0) if has_state else None
    del rest[:n_prev]
    o_ref = rest.pop(0)
    st_ref = rest.pop(0) if emit_state else None
    o_scr, s_scr, b_scr = rest
    c, lv = HG_CHUNK, HG_LEVELS

    o_scr[...] = jnp.zeros_like(o_scr)
    for d in range(2):
        for g in range(heads):
            if has_state:
                s_scr[d, g] = s0_ref[d, g].T
            else:
                s_scr[d, g] = jnp.zeros((HG_V, HG_K), f32)

    chains = [(d, g) for g in range(heads) for d in range(2)]
    low_half = lax.broadcasted_iota(jnp.int32, (SUBLANES, HEAD_W), 0) < SUBLANES // 2

    def gates_and_cumsum(ci, row):
        d, g = chains[ci]
        rs = pl.ds(pl.multiple_of(row, c), c)
        cs = slice(g * HEAD_W, (g + 1) * HEAD_W)
        x = (hff_ref if d == 0 else hfb_ref)[rs, cs]
        x2 = x * LOG2E
        u = jnp.exp2(-jnp.abs(x2))
        inv = 1.0 / (1.0 + u)
        pos = x >= 0.0
        log_sig = jnp.minimum(x2, 0.0) - jnp.log2(1.0 + u)
        a = la_ref[d:d + 1, cs]
        bb = lc_ref[d:d + 1, cs] + log_sig
        lf = jnp.maximum(a, bb) + jnp.log2(1.0 + jnp.exp2(-jnp.abs(a - bb)))
        om = om_ref[d:d + 1, cs]
        f = (1.0 - om) + om * (jnp.where(pos, 1.0, u) * inv)
        kin = om * (jnp.where(pos, u, 1.0) * inv)
        q = _silu(hq_ref[rs, cs])
        v = hi_ref[rs, cs].astype(bf16)
        l1 = lf.astype(bf16)
        r1 = lf - l1.astype(f32)
        l2 = r1.astype(bf16)
        l3 = (r1 - l2.astype(f32)).astype(bf16)
        w = wcum_ref[d]
        b = _dot(w, l1) + _dot(w, l2) + _dot(w, l3)
        b_scr[ci] = b
        return dict(rs=rs, cs=cs, f=f, kin=kin, q=q, v=v, b=b)

    def intra_chunk(ci, s):
        d, _ = chains[ci]
        b, q, kin = s["b"], s["q"], s["kin"]

        def row(r):
            return b_scr[ci, r:r + 1, :]

        s["b_end"] = row(c - 1 if d == 0 else 0)
        xs = [jnp.where(qside_ref[d, 0:c, :] > 0.0, q * s["f"], kin).astype(bf16)]
        for l in range(1, lv):
            hs = 1 << l
            blk = 2 * hs
            pick = hs - 1 if d == 0 else hs
            if hs >= SUBLANES:
                pieces = []
                for j in range(c // blk):
                    m = row(blk * j + pick)
                    early = slice(blk * j, blk * j + hs)
                    late = slice(blk * j + hs, blk * (j + 1))
                    q_sl, k_sl = (late, early) if d == 0 else (early, late)
                    xq = q[q_sl] * jnp.exp2(b[q_sl] - m)
                    xk = kin[k_sl] * jnp.exp2(m - b[k_sl])
                    pieces += [xk, xq] if d == 0 else [xq, xk]
                xs.append(jnp.concatenate(pieces, axis=0).astype(bf16))
                continue
            if blk < SUBLANES:
                pieces = [jnp.where(low_half, jnp.broadcast_to(row(SUBLANES * j + pick), (SUBLANES, HEAD_W)),
                                    jnp.broadcast_to(row(SUBLANES * j + blk + pick), (SUBLANES, HEAD_W)))
                          for j in range(c // SUBLANES)]
            else:
                pieces = [jnp.broadcast_to(row(blk * j + pick), (blk, HEAD_W)) for j in range(c // blk)]
            m = pieces[0] if len(pieces) == 1 else jnp.concatenate(pieces, axis=0)
            sign = qside_ref[d, l * c:(l + 1) * c, :]
            xs.append((jnp.where(sign > 0.0, q, kin) * jnp.exp2((b - m) * sign)).astype(bf16))
        att = mask_ref[d, 0] * _dot_nt(q.astype(bf16), kin.astype(bf16)).astype(bf16)
        for l in range(lv):
            att = att + mask_ref[d, 1 + l] * _dot_nt(xs[l], xs[l]).astype(bf16)
        s["att"] = att

    def state_step(ci, s):
        d, g = chains[ci]
        b, q, kin, v = s["b"], s["q"], s["kin"], s["v"]
        st = s_scr[d, g]
        o = _dot(s["att"], v) + _dot_nt((q * jnp.exp2(b)).astype(bf16), st.astype(bf16))
        b_end = s["b_end"]
        s_scr[d, g] = st * jnp.exp2(b_end) + _dot_tn(v, (kin * jnp.exp2(b_end - b)).astype(bf16))
        o_scr[s["rs"], s["cs"]] += o

    def body(i, _):
        rows = (i * c, (n_chunks - 1 - i) * c)
        states = [gates_and_cumsum(ci, rows[chains[ci][0]]) for ci in range(len(chains))]
        for ci, s in enumerate(states):
            intra_chunk(ci, s)
        for ci, s in enumerate(states):
            state_step(ci, s)
        return 0

    lax.fori_loop(0, n_chunks, body, 0)

    ng = ng_ref[...]
    for g in range(heads):
        cs = slice(g * HEAD_W, (g + 1) * HEAD_W)
        o = o_scr[:, cs]
        y = o * lax.rsqrt(jnp.mean(o * o, axis=-1, keepdims=True) + EPS) * ng
        o_ref[:, cs] = (y * _silu(hg_ref[:, cs])).astype(o_ref.dtype)
        if emit_state:
            for d in range(2):
                st_ref[d, g] = s_scr[d, g].T


def _hgrn(z, lay, lb_tabs, norm_g, tables, *, batch, t_len, row0, s0, state_out=None, heads_per_step=2):
    emit_state = state_out is not None
    assert t_len % HG_CHUNK == 0 and row0 % t_len == 0 and HG_K == HEAD_W and HG_V == HEAD_W
    g = heads_per_step
    gw = g * HEAD_W
    la, lc, om = lb_tabs
    wcum, qside, mask = tables
    r0 = row0 // t_len

    def zspec(col):
        return pl.BlockSpec((t_len, gw), lambda b, h: (r0 + b, col // gw + h))

    def const(shape):
        return pl.BlockSpec(shape, lambda b, h: (0,) * len(shape))

    in_specs = [zspec(lay.hq), zspec(lay.hff), zspec(lay.hfb), zspec(lay.hi), zspec(lay.hg),
                pl.BlockSpec((2, gw), lambda b, h: (0, h)), pl.BlockSpec((2, gw), lambda b, h: (0, h)),
                pl.BlockSpec((2, gw), lambda b, h: (0, h)), const((1, HG_V)),
                const(wcum.shape), const(qside.shape), const(mask.shape)]
    args = [z, z, z, z, z, la, lc, om, norm_g.reshape(1, HG_V), wcum, qside, mask]
    state_spec = pl.BlockSpec((None, 2, g, HG_K, HG_V), lambda b, h: (b, 0, h, 0, 0))
    if s0 is not None:
        in_specs.append(state_spec)
        args.append(s0)
    out_specs = [pl.BlockSpec((t_len, gw), lambda b, h: (b, h))]
    out_shape = [jax.ShapeDtypeStruct((batch * t_len, HG_HEADS * HG_V), bf16)]
    aliases, n_prev = {}, 0
    if emit_state:
        prev, layer, depth = state_out
        if prev is not None:
            aliases, n_prev = {len(args): 1}, 1
            in_specs.append(pl.BlockSpec(memory_space=pl.ANY))
            args.append(prev)
        out_specs.append(pl.BlockSpec((None, None, 2, g, HG_K, HG_V), lambda b, h: (b, layer, 0, h, 0, 0)))
        out_shape.append(jax.ShapeDtypeStruct((batch, depth, 2, HG_HEADS, HG_K, HG_V), f32))
    res = pl.pallas_call(
        functools.partial(_hgrn_kernel, n_chunks=t_len // HG_CHUNK, heads=g, has_state=s0 is not None,
                          emit_state=emit_state, n_prev=n_prev),
        grid=(batch, HG_HEADS // g),
        in_specs=in_specs,
        out_specs=out_specs,
        out_shape=out_shape,
        input_output_aliases=aliases,
        scratch_shapes=[pltpu.VMEM((t_len, gw), f32), pltpu.VMEM((2, g, HG_V, HG_K), f32),
                        pltpu.VMEM((2 * g, HG_CHUNK, HEAD_W), f32)],
        compiler_params=_params(("arbitrary", "arbitrary")),
        name="hgrn_ctx" if emit_state else "hgrn_latent",
    )(*args)
    return res if emit_state else (res[0], None)


def _merge_kernel(oac_ref, oal_ref, obc_ref, obl_ref, occ_ref, ocl_ref, ga_ref, gb_ref, gc_ref, w_ref, o_ref,
                  *, n_ctx_tiles):
    def compute(oa_ref, ob_ref, oc_ref):
        m = jax.nn.sigmoid(ga_ref[...]) * _dot(oa_ref[...], w_ref[0])
        m = m + jax.nn.sigmoid(gb_ref[...]) * _dot(ob_ref[...], w_ref[1])
        m = m + jax.nn.sigmoid(gc_ref[...]) * _dot(oc_ref[...], w_ref[2])
        o_ref[...] = m.astype(o_ref.dtype)

    is_ctx = pl.program_id(1) < n_ctx_tiles

    @pl.when(is_ctx)
    def _():
        compute(oac_ref, obc_ref, occ_ref)

    @pl.when(jnp.logical_not(is_ctx))
    def _():
        compute(oal_ref, obl_ref, ocl_ref)


def _merge(o_a, o_b, o_c, z, lay, w_branch, tm, tn):
    bw = o_a[0].shape[1]
    m = z.shape[0]
    d = w_branch.shape[2]
    nct = o_a[0].shape[0] // tm
    cspec = pl.BlockSpec((tm, bw), lambda j, i: (jnp.minimum(i, nct - 1), 0))
    lspec = pl.BlockSpec((tm, bw), lambda j, i: (jnp.maximum(i - nct, 0), 0))

    def gate(col):
        return pl.BlockSpec((tm, tn), lambda j, i: (i, col // tn + j))

    return pl.pallas_call(
        functools.partial(_merge_kernel, n_ctx_tiles=nct),
        grid=(d // tn, m // tm),
        in_specs=[cspec, lspec, cspec, lspec, cspec, lspec, gate(lay.ga), gate(lay.gb), gate(lay.gc),
                  pl.BlockSpec((3, bw, tn), lambda j, i: (0, 0, j))],
        out_specs=pl.BlockSpec((tm, tn), lambda j, i: (i, j)),
        out_shape=jax.ShapeDtypeStruct((m, d), bf16),
        compiler_params=_params(("arbitrary", "arbitrary")),
        name="merge",
    )(o_a[0], o_a[1], o_b[0], o_b[1], o_c[0], o_c[1], z, z, z, w_branch)


def _proj_residual_kernel(m_ref, w_ref, x_ref, g_ref, o_ref):
    o_ref[...] = x_ref[...] + g_ref[...] * _dot(m_ref[...], w_ref[...])


def _proj_residual(mm, w, x, gate, group_of_tile, tm, tn):
    m, k = mm.shape
    d = w.shape[1]
    return pl.pallas_call(
        _proj_residual_kernel,
        grid=(m // tm, d // tn),
        in_specs=[
            pl.BlockSpec((tm, k), lambda i, j: (i, 0)),
            pl.BlockSpec((k, tn), lambda i, j: (0, j)),
            pl.BlockSpec((tm, tn), lambda i, j: (i, j)),
            pl.BlockSpec((None, 1, tn), lambda i, j: (group_of_tile(i), 0, j)),
        ],
        out_specs=pl.BlockSpec((tm, tn), lambda i, j: (i, j)),
        out_shape=jax.ShapeDtypeStruct((m, d), f32),
        compiler_params=_params(("arbitrary", "arbitrary")),
        name="out_proj",
    )(mm, w, x, gate)


def _ffn_kernel(x_ref, xprev_ref, xnext_ref, ng_ref, sc_ref, sh_ref, wa_ref, wg_ref, cw_ref, cb_ref, wd_ref, gate_ref,
                o_ref, h_scr, *, m_ctx, s_len, t_len, n_k):
    i = pl.program_id(0)
    k = pl.program_id(1)
    tm = x_ref.shape[0]
    tk = wa_ref.shape[1]
    halo = SUBLANES

    @pl.when(k == 0)
    def _():
        def normmod(x):
            ms = jnp.mean(x * x, axis=-1, keepdims=True)
            y = x * lax.rsqrt(ms + EPS) * ng_ref[...]
            return (y * (1.0 + sc_ref[...]) + sh_ref[...]).astype(bf16)

        h_scr[0:halo] = normmod(xprev_ref[...])
        h_scr[halo:halo + tm] = normmod(x_ref[...])
        h_scr[halo + tm:halo + tm + halo] = normmod(xnext_ref[...])
        o_ref[...] = jnp.zeros_like(o_ref)

    row = i * tm + lax.broadcasted_iota(jnp.int32, (tm, LANES), 0)
    pos = jnp.where(row < m_ctx, _mod_const(row, s_len), _mod_const(row - m_ctx, t_len))
    last = jnp.where(row < m_ctx, s_len - 1, t_len - 1)
    keep_prev = jnp.tile(jnp.where(pos == 0, 0.0, 1.0), (1, tk // LANES))
    keep_next = jnp.tile(jnp.where(pos == last, 0.0, 1.0), (1, tk // LANES))

    g_ext = _dot(h_scr[...], wg_ref[...])
    a = _dot(h_scr[halo:halo + tm], wa_ref[...])
    n_ext = tm + 2 * halo
    g = g_ext[halo:halo + tm]
    g_prev = pltpu.roll(g_ext, 1, 0)[halo:halo + tm]
    g_next = pltpu.roll(g_ext, n_ext - 1, 0)[halo:halo + tm]
    conv = (g_prev * keep_prev * cw_ref[0:1, :] + g * cw_ref[1:2, :] + g_next * keep_next * cw_ref[2:3, :]
            + cb_ref[...])
    u = (_silu(conv) * a).astype(bf16)
    o_ref[...] = _dot(u, wd_ref[...]) + o_ref[...]

    @pl.when(k == n_k - 1)
    def _():
        o_ref[...] = x_ref[...] + gate_ref[...] * o_ref[...]


def _ffn(x, norm_g, sc, sh, w_up, conv_w, conv_b, w_down, gate, group_of_tile, tm, tk, m_ctx, s_len, t_len):
    m, d = x.shape
    f = w_down.shape[0]
    nk = f // tk
    nsub = tm // SUBLANES
    last_sub = m // SUBLANES - 1
    return pl.pallas_call(
        functools.partial(_ffn_kernel, m_ctx=m_ctx, s_len=s_len, t_len=t_len, n_k=nk),
        grid=(m // tm, nk),
        in_specs=[
            pl.BlockSpec((tm, d), lambda i, k: (i, 0), pipeline_mode=pl.Buffered(1)),
            pl.BlockSpec((SUBLANES, d), lambda i, k: (jnp.maximum(i * nsub - 1, 0), 0)),
            pl.BlockSpec((SUBLANES, d), lambda i, k: (jnp.minimum((i + 1) * nsub, last_sub), 0)),
            pl.BlockSpec((1, d), lambda i, k: (0, 0)),
            pl.BlockSpec((None, 1, d), lambda i, k: (group_of_tile(i), 0, 0)),
            pl.BlockSpec((None, 1, d), lambda i, k: (group_of_tile(i), 0, 0)),
            pl.BlockSpec((d, tk), lambda i, k: (0, k)),
            pl.BlockSpec((d, tk), lambda i, k: (0, nk + k)),
            pl.BlockSpec((3, tk), lambda i, k: (0, k)),
            pl.BlockSpec((1, tk), lambda i, k: (0, k)),
            pl.BlockSpec((tk, d), lambda i, k: (k, 0)),
            pl.BlockSpec((None, 1, d), lambda i, k: (group_of_tile(i), 0, 0)),
        ],
        out_specs=pl.BlockSpec((tm, d), lambda i, k: (i, 0), pipeline_mode=pl.Buffered(1)),
        out_shape=jax.ShapeDtypeStruct((m, d), f32),
        scratch_shapes=[pltpu.VMEM((tm + 2 * SUBLANES, d), bf16)],
        compiler_params=_params(("arbitrary", "arbitrary")),
        name="ffn",
    )(x, x, x, norm_g.reshape(1, d), sc, sh, w_up, w_up, conv_w, conv_b.reshape(1, f), w_down, gate)


def kernel(x_prompt, x_sample, cache_mla_ckv, cache_mla_kpe, cache_na_k, cache_na_v, state_hgrn, c, c_ctx, w_mod, b_mod, norm1_g, norm2_g, w_in, mla_kv_norm_g, mla_q_norm_g, mla_k_norm_g, mla_w_uk, mla_w_uv, hgrn_lower_bounds, hgrn_norm_g, na_q_norm_g, na_k_norm_g, na_rpb, w_branch, w_out, ffn_w_up, ffn_conv_w, ffn_conv_b, ffn_w_down):
    n_ctx, s_len, d = x_prompt.shape
    n_lat, t_len, _ = x_sample.shape
    depth = w_in.shape[0]
    p_len = cache_mla_ckv.shape[2]
    ffn = ffn_w_down.shape[1]
    m_ctx, m_lat = n_ctx * s_len, n_lat * t_len
    m = m_ctx + m_lat
    assert m_ctx % t_len == 0, "context rows must be a whole number of latent sequences"

    tm = _pow2_tile(1024, m_ctx, t_len)
    tn = _pow2_tile(1024, d)
    tp = _pow2_tile(512, m_ctx, t_len)
    lay = _ZLayout(d, tn)

    def group_fn(rows):
        return lambda i: jnp.where(i < m_ctx // rows, 0, 1 + (i - m_ctx // rows) // (t_len // rows))

    group_of_tile = group_fn(tm)

    n_groups = -(-(1 + n_lat) // SUBLANES) * SUBLANES
    cond = jnp.concatenate([c_ctx[None], c, jnp.zeros((n_groups - 1 - n_lat, d), f32)], axis=0)
    mods = _modulation(cond, w_mod, b_mod).reshape(depth, n_groups, 6, d).transpose(0, 2, 1, 3)[:, :, :, None, :]

    sm = jax.nn.softmax(hgrn_lower_bounds.astype(f32), axis=1)
    csum = jnp.cumsum(sm, axis=1)
    lower = csum - csum[:, :1]
    hg_tabs = tuple(jnp.asarray(t, dt) for t, dt in zip(_hgrn_tables(), (bf16, f32, bf16)))

    cos, sin = _rope_tables(t_len, tp)
    n_ctx_tp = m_ctx // tp

    def rope_block(i):
        return jnp.where(i < n_ctx_tp, 0, 1 + (i - n_ctx_tp) % (t_len // tp))

    x = jnp.concatenate([x_prompt.reshape(m_ctx, d), x_sample.reshape(m_lat, d)], axis=0)
    st_mla = st_na = st_hg = None
    for l in range(depth):
        sh1, sc1, g1, sh2, sc2, g2 = (mods[l, k] for k in range(6))
        w_in_l, w_kpe_l = _permute_w_in(w_in[l], lay)
        z, kpe = _normmod_matmul(x, norm1_g[l], sc1, sh1, w_in_l, group_of_tile, tm, tn, "in_proj", w_side=w_kpe_l)

        w_uk = mla_w_uk[l].reshape(MLA_KV_RANK, MLA_HEADS * MLA_NOPE).astype(bf16)
        w_uv = mla_w_uv[l].reshape(MLA_KV_RANK, MLA_HEADS * MLA_V).astype(bf16)
        q_mla = _mla_q(z, lay, mla_q_norm_g[l], cos, sin, rope_block, tp)
        k_mla, v_mla, *st_mla = _mla_kv(
            z, lay.ckv, kpe, 0, mla_kv_norm_g[l], mla_k_norm_g[l], w_uk, w_uv, cos, sin, rope_block, tp, True, True,
            "mla_kv", state=_CtxState(st_mla, (MLA_KV_RANK, MLA_ROPE), l, depth, n_ctx, s_len, tp))
        ckv_c = cache_mla_ckv[:, l].reshape(n_lat * p_len, MLA_KV_RANK)
        kpe_c = jnp.pad(cache_mla_kpe[:, l].reshape(n_lat * p_len, MLA_ROPE), ((0, 0), (0, LANES - MLA_ROPE)))
        tc = _pow2_tile(512, n_lat * p_len)
        k_c, v_c = _mla_kv(ckv_c, 0, kpe_c, 0, mla_kv_norm_g[l], mla_k_norm_g[l], w_uk, w_uv,
                              cos, sin, lambda i: 0, tc, False, False, "mla_kv_cache")
        tq_c = _pow2_tile(256, s_len)
        o_mla_c = _flash(q_mla, k_mla, v_mla, batch=n_ctx, heads=MLA_HEADS, t_len=s_len, s_len=s_len, row0=0,
                         dq=MLA_QPAD, dv=MLA_V, tq=tq_c, tk=tq_c, heads_per_step=MLA_HEADS, name="mla_attn_ctx")
        tq_l = _pow2_tile(512, t_len, p_len)
        o_mla_l = _flash(q_mla, k_mla, v_mla, batch=n_lat, heads=MLA_HEADS, t_len=t_len, s_len=t_len, row0=m_ctx,
                         dq=MLA_QPAD, dv=MLA_V, tq=tq_l, tk=tq_l, heads_per_step=2, extra=(k_c, v_c, p_len),
                         name="mla_attn_latent")

        lb = lower[:, l]
        lb_tabs = (jnp.log(lb) * LOG2E, jnp.log1p(-lb) * LOG2E, 1.0 - lb)
        o_hg_c, st_hg = _hgrn(z, lay, lb_tabs, hgrn_norm_g[l], hg_tabs, batch=n_ctx, t_len=s_len, row0=0,
                              s0=None, state_out=(st_hg, l, depth))
        o_hg_l, _ = _hgrn(z, lay, lb_tabs, hgrn_norm_g[l], hg_tabs, batch=n_lat, t_len=t_len, row0=m_ctx,
                          s0=state_hgrn[:, l])

        q_na, k_na, v_na, *st_na = _na_prep(
            z, lay, na_q_norm_g[l], na_k_norm_g[l], tp,
            _CtxState(st_na, (NA_HEADS * NA_DH, NA_HEADS * NA_DH), l, depth, n_ctx, s_len, tp))
        o_na_c = _flash(q_na, k_na, v_na, batch=n_ctx, heads=NA_HEADS, t_len=s_len, s_len=s_len, row0=0,
                        dq=NA_DH, dv=NA_DH, tq=tq_c, tk=tq_c, heads_per_step=NA_HEADS, name="na_attn_ctx")
        kc_na = cache_na_k[:, l].reshape(n_lat * p_len, NA_HEADS * NA_DH).astype(bf16)
        vc_na = cache_na_v[:, l].reshape(n_lat * p_len, NA_HEADS * NA_DH).astype(bf16)
        bias = _na_bias_table(na_rpb[l], t_len // GRID_W)
        o_na_l = _na_latent(q_na, k_na, v_na, kc_na, vc_na, bias, batch=n_lat, t_len=t_len, p_len=p_len, row0=m_ctx)

        merged = _merge((o_mla_c, o_mla_l), (o_hg_c, o_hg_l), (o_na_c, o_na_l), z, lay, w_branch[l].astype(bf16), tp, tn)
        x = _proj_residual(merged, w_out[l].astype(bf16), x, g1, group_fn(tp), tp, d)

        x = _ffn(x, norm2_g[l], sc2, sh2, ffn_w_up[l].astype(bf16), ffn_conv_w[l], ffn_conv_b[l],
                 ffn_w_down[l].astype(bf16), g2, group_of_tile, tm, _pow2_tile(512, ffn), m_ctx, s_len, t_len)


    y_prompt = x[:m_ctx].reshape(n_ctx, s_len, d)
    y_sample = x[m_ctx:].reshape(n_lat, t_len, d)
    na_shape = (n_ctx, depth, s_len, NA_HEADS, NA_DH)
    return (y_prompt, y_sample, st_mla[0], st_mla[1], st_na[0].reshape(na_shape), st_na[1].reshape(na_shape), st_hg)
```

```python
import functools
import math

import numpy as np
import jax
import jax.numpy as jnp
from jax import lax
from jax.experimental import pallas as pl
from jax.experimental.pallas import tpu as pltpu

f32 = jnp.float32
bf16 = jnp.bfloat16

GRID_W = 64
MLA_HEADS, MLA_NOPE, MLA_ROPE, MLA_V, MLA_KV_RANK = 8, 128, 64, 128, 512
MLA_QK = MLA_NOPE + MLA_ROPE
HG_HEADS, HG_K, HG_V = 8, 128, 128
NA_HEADS, NA_DH, NA_WIN_R, NA_WIN_C = 8, 128, 8, 16
ROPE_BASE = 10000.0
EPS = 1e-6
NEG = -1e30
LOG2E = 1.4426950408889634

LANES = 128
SUBLANES = 8
VMEM_LIMIT_BYTES = 56 * 1024 * 1024

HEAD_W = 128
MLA_QPAD = 256
HG_CHUNK = 128
HG_LEVELS = int(math.log2(HG_CHUNK))
NA_QROWS = 8
NA_KROWS = 16


def _params(semantics):
    return pltpu.CompilerParams(dimension_semantics=semantics, vmem_limit_bytes=VMEM_LIMIT_BYTES)


def _pow2_tile(pref, *dims):
    t = pref
    while any(d % t for d in dims):
        t //= 2
    assert t >= SUBLANES, (pref, dims)
    return t


def _dot(a, b):
    return jnp.dot(a, b, preferred_element_type=f32)


def _dot_nt(a, b):
    return lax.dot_general(a, b, (((1,), (1,)), ((), ())), preferred_element_type=f32)


def _dot_tn(a, b):
    return lax.dot_general(a, b, (((0,), (0,)), ((), ())), preferred_element_type=f32)


def _silu(x):
    return x * jax.nn.sigmoid(x)


def _mod_const(x, n):
    return jnp.bitwise_and(x, n - 1) if n & (n - 1) == 0 else lax.rem(x, n)


class _ZLayout:
    def __init__(self, d_model, tn):
        self.mq = 0
        self.ckv = self.mq + MLA_HEADS * MLA_QK
        self.hq = self.ckv + MLA_KV_RANK
        self.hff = self.hq + HG_HEADS * HG_K
        self.hfb = self.hff + HG_HEADS * HG_K
        self.hi = self.hfb + HG_HEADS * HG_K
        self.hg = self.hi + HG_HEADS * HG_V
        self.nq = self.hg + HG_HEADS * HG_V
        self.nk = self.nq + NA_HEADS * NA_DH
        self.nv = self.nk + NA_HEADS * NA_DH
        self.ga = self.nv + NA_HEADS * NA_DH
        self.gb = self.ga + d_model
        self.gc = self.gb + d_model
        self.width = self.gc + d_model
        assert self.width % tn == 0


def _permute_w_in(w, lay):
    n_front = MLA_HEADS * MLA_QK + MLA_KV_RANK
    kpe = w[..., n_front:n_front + MLA_ROPE]
    main = jnp.concatenate([w[..., :n_front], w[..., n_front + MLA_ROPE:]], axis=-1).astype(bf16)
    assert main.shape[-1] == lay.width
    pad = [(0, 0)] * (w.ndim - 1) + [(0, LANES - MLA_ROPE)]
    return main, jnp.pad(kpe, pad).astype(bf16)


def _mod_kernel(c_ref, w_ref, b_ref, o_ref):
    a = _silu(c_ref[...]).astype(bf16)
    o_ref[...] = _dot(a, w_ref[...].astype(bf16)) + b_ref[...]


def _modulation(cond, w_mod, b_mod):
    depth, d, n = w_mod.shape
    g = cond.shape[0]
    tn = _pow2_tile(1024, n)
    return pl.pallas_call(
        _mod_kernel,
        grid=(depth, n // tn),
        in_specs=[
            pl.BlockSpec((g, d), lambda l, j: (0, 0)),
            pl.BlockSpec((None, d, tn), lambda l, j: (l, 0, j)),
            pl.BlockSpec((None, 1, tn), lambda l, j: (l, 0, j)),
        ],
        out_specs=pl.BlockSpec((None, g, tn), lambda l, j: (l, 0, j)),
        out_shape=jax.ShapeDtypeStruct((depth, g, n), f32),
        compiler_params=_params(("arbitrary", "arbitrary")),
        name="modulation",
    )(cond, w_mod, b_mod.reshape(depth, 1, n))


def _normmod_matmul_kernel(x_ref, g_ref, sc_ref, sh_ref, w_ref, *rest):
    if len(rest) == 4:
        ws_ref, o_ref, os_ref, h_scr = rest
    else:
        (o_ref, h_scr), ws_ref, os_ref = rest, None, None

    @pl.when(pl.program_id(1) == 0)
    def _():
        x = x_ref[...]
        ms = jnp.mean(x * x, axis=-1, keepdims=True)
        y = x * lax.rsqrt(ms + EPS) * g_ref[...]
        h_scr[...] = (y * (1.0 + sc_ref[...]) + sh_ref[...]).astype(bf16)
        if ws_ref is not None:
            os_ref[...] = _dot(h_scr[...], ws_ref[...])

    o_ref[...] = _dot(h_scr[...], w_ref[...])


def _normmod_matmul(x, g, sc, sh, w, layer, group_of_tile, tm, tn, name, w_side=None):
    m, d = x.shape
    n = w.shape[2]
    in_specs = [
        pl.BlockSpec((tm, d), lambda i, j: (i, 0)),
        pl.BlockSpec((1, d), lambda i, j: (0, 0)),
        pl.BlockSpec((None, 1, d), lambda i, j: (group_of_tile(i), 0, 0)),
        pl.BlockSpec((None, 1, d), lambda i, j: (group_of_tile(i), 0, 0)),
        pl.BlockSpec((None, d, tn), lambda i, j: (layer, 0, j)),
    ]
    args = [x, g.reshape(1, d), sc, sh, w]
    out_specs = [pl.BlockSpec((tm, tn), lambda i, j: (i, j))]
    out_shape = [jax.ShapeDtypeStruct((m, n), f32)]
    if w_side is not None:
        ns = w_side.shape[2]
        in_specs.append(pl.BlockSpec((None, d, ns), lambda i, j: (layer, 0, 0)))
        args.append(w_side)
        out_specs.append(pl.BlockSpec((tm, ns), lambda i, j: (i, 0)))
        out_shape.append(jax.ShapeDtypeStruct((m, ns), f32))
    res = pl.pallas_call(
        _normmod_matmul_kernel,
        grid=(m // tm, n // tn),
        in_specs=in_specs,
        out_specs=out_specs,
        out_shape=out_shape,
        scratch_shapes=[pltpu.VMEM((tm, d), bf16)],
        compiler_params=_params(("arbitrary", "arbitrary")),
        name=name,
    )(*args)
    return res if w_side is not None else res[0]


def _rope(x, cos, sin_signed):
    n = x.shape[-1]
    half = MLA_ROPE // 4
    lane = lax.broadcasted_iota(jnp.int32, x.shape, 1)
    partner = jnp.where((lane % (2 * half)) < half, pltpu.roll(x, n - half, 1), pltpu.roll(x, half, 1))
    return x * cos + partner * sin_signed


def _mla_q_kernel(mq_ref, gn_ref, gp_ref, cos_ref, sin_ref, o_ref, *, scale):
    tm = mq_ref.shape[0]
    lane = lax.broadcasted_iota(jnp.int32, (tm, LANES), 1)
    low = lane < MLA_ROPE
    cos = cos_ref[...]
    sin = sin_ref[...]
    gn = gn_ref[...]
    gp = gp_ref[...]
    for pair in range(MLA_HEADS // 2):
        t0, t1, t2 = (mq_ref[:, (3 * pair + t) * LANES:(3 * pair + t + 1) * LANES] for t in range(3))
        nopes = (t0, jnp.where(low, pltpu.roll(t1, MLA_ROPE, 1), pltpu.roll(t2, MLA_ROPE, 1)))
        pcol = jnp.where(low, t1, t2)
        sq = pcol * pcol
        ss_all = jnp.sum(sq, axis=-1, keepdims=True)
        ss_low = jnp.sum(jnp.where(low, sq, 0.0), axis=-1, keepdims=True)
        rot = _rope(pcol * gp, cos, sin)
        for k, ss_pe in enumerate((ss_low, ss_all - ss_low)):
            h = 2 * pair + k
            nope = nopes[k]
            ss = jnp.sum(nope * nope, axis=-1, keepdims=True) + ss_pe
            r = lax.rsqrt(ss * (1.0 / MLA_QK) + EPS) * scale
            pe = rot if k == 0 else pltpu.roll(rot, MLA_ROPE, 1)
            o_ref[:, h * MLA_QPAD:h * MLA_QPAD + MLA_NOPE] = (nope * r * gn).astype(bf16)
            o_ref[:, h * MLA_QPAD + MLA_NOPE:(h + 1) * MLA_QPAD] = jnp.where(low, pe * r, 0.0).astype(bf16)


class _CtxState:
    def __init__(self, prevs, widths, layer, depth, n_ctx, s_len, tm):
        assert tm % s_len == 0
        self.n_seq = tm // s_len
        self.s_len = s_len
        self.n_tiles = n_ctx * s_len // tm
        self.prevs = [] if prevs is None else list(prevs)
        n_tiles = self.n_tiles
        self.out_specs = [pl.BlockSpec((self.n_seq, None, s_len, w), lambda i: (jnp.minimum(i, n_tiles - 1), layer, 0, 0))
                          for w in widths]
        self.out_shape = [jax.ShapeDtypeStruct((n_ctx, depth, s_len, w), f32) for w in widths]
        self.in_specs = [pl.BlockSpec(memory_space=pl.ANY)] * len(self.prevs)

    def aliases(self, n_inputs_before, n_outputs_before):
        return {n_inputs_before + k: n_outputs_before + k for k in range(len(self.prevs))}

    def kernel_kwargs(self):
        return dict(n_prev=len(self.prevs), n_ctx_tiles=self.n_tiles, s_len=self.s_len)


def _mla_kv_kernel(*refs, norm_ckv, rope, n_prev=0, n_ctx_tiles=0, s_len=0):
    ckv_ref, kpe_ref, kvg_ref, gn_ref, gp_ref, wuk_ref, wuv_ref, cos_ref, sin_ref = refs[:9]
    k_ref, v_ref, *state_refs = refs[9 + n_prev:]
    ckv = ckv_ref[...]
    if norm_ckv:
        ms = jnp.mean(ckv * ckv, axis=-1, keepdims=True)
        ckv = ckv * lax.rsqrt(ms + EPS) * kvg_ref[...]
    if state_refs:
        ckv_state_ref, kpe_state_ref = state_refs

        @pl.when(pl.program_id(0) < n_ctx_tiles)
        def _():
            for n in range(ckv.shape[0] // s_len):
                ckv_state_ref[n] = ckv[n * s_len:(n + 1) * s_len]
                kpe_state_ref[n] = kpe_ref[n * s_len:(n + 1) * s_len, :MLA_ROPE]

    cb = ckv.astype(bf16)
    kn = _dot(cb, wuk_ref[...])
    v_ref[...] = _dot(cb, wuv_ref[...]).astype(bf16)
    tm = ckv.shape[0]
    lane = lax.broadcasted_iota(jnp.int32, (tm, LANES), 1)
    low = lane < MLA_ROPE
    kpe = jnp.where(low, kpe_ref[...], 0.0)
    ss_pe = jnp.sum(kpe * kpe, axis=-1, keepdims=True)
    pe = kpe * gp_ref[...]
    if rope:
        pe = _rope(pe, cos_ref[...], sin_ref[...])
    gn = gn_ref[...]
    for h in range(MLA_HEADS):
        nope = kn[:, h * MLA_NOPE:(h + 1) * MLA_NOPE]
        ss = jnp.sum(nope * nope, axis=-1, keepdims=True) + ss_pe
        r = lax.rsqrt(ss * (1.0 / MLA_QK) + EPS)
        k_ref[:, h * MLA_QPAD:h * MLA_QPAD + MLA_NOPE] = (nope * r * gn).astype(bf16)
        k_ref[:, h * MLA_QPAD + MLA_NOPE:(h + 1) * MLA_QPAD] = jnp.where(low, pe * r, 0.0).astype(bf16)


def _rope_tables(t_len, tm):
    n_freq = MLA_ROPE // 4
    t = jnp.arange(t_len, dtype=jnp.int32)
    inv = ROPE_BASE ** (-jnp.arange(n_freq, dtype=f32) / n_freq)
    ang_r = (t // GRID_W).astype(f32)[:, None] * inv
    ang_c = (t % GRID_W).astype(f32)[:, None] * inv
    cos64 = jnp.concatenate([jnp.cos(ang_r), jnp.cos(ang_r), jnp.cos(ang_c), jnp.cos(ang_c)], axis=1)
    sin64 = jnp.concatenate([-jnp.sin(ang_r), jnp.sin(ang_r), -jnp.sin(ang_c), jnp.sin(ang_c)], axis=1)
    cos = jnp.concatenate([jnp.ones((tm, LANES), f32), jnp.tile(cos64, (1, 2))], axis=0)
    sin = jnp.concatenate([jnp.zeros((tm, LANES), f32), jnp.tile(sin64, (1, 2))], axis=0)
    return cos, sin


def _mla_q(z, lay, q_g, cos, sin, rope_block, tm):
    m = z.shape[0]
    gn = q_g[:MLA_NOPE].reshape(1, MLA_NOPE)
    gp = jnp.tile(q_g[MLA_NOPE:], 2).reshape(1, LANES)
    wq = MLA_HEADS * MLA_QK
    assert MLA_HEADS % 2 == 0 and 2 * MLA_QK == 3 * LANES and 2 * MLA_ROPE == LANES and lay.mq % wq == 0
    return pl.pallas_call(
        functools.partial(_mla_q_kernel, scale=MLA_QK ** -0.5 * LOG2E),
        grid=(m // tm,),
        in_specs=[
            pl.BlockSpec((tm, wq), lambda i: (i, lay.mq // wq)),
            pl.BlockSpec((1, MLA_NOPE), lambda i: (0, 0)),
            pl.BlockSpec((1, LANES), lambda i: (0, 0)),
            pl.BlockSpec((tm, LANES), lambda i: (rope_block(i), 0)),
            pl.BlockSpec((tm, LANES), lambda i: (rope_block(i), 0)),
        ],
        out_specs=pl.BlockSpec((tm, MLA_HEADS * MLA_QPAD), lambda i: (i, 0)),
        out_shape=jax.ShapeDtypeStruct((m, MLA_HEADS * MLA_QPAD), bf16),
        compiler_params=_params(("arbitrary",)),
        name="mla_q",
    )(z, gn, gp, cos, sin)


def _mla_kv(ckv_src, ckv_col, kpe_src, kpe_col, kv_g, k_g, w_uk, w_uv, layer, cos, sin, rope_block, tm, norm_ckv, rope,
            name, state=None):
    m = ckv_src.shape[0]
    gn = k_g[:MLA_NOPE].reshape(1, MLA_NOPE)
    gp = jnp.concatenate([k_g[MLA_NOPE:], jnp.zeros((LANES - MLA_ROPE,), f32)]).reshape(1, LANES)
    hw = MLA_HEADS * MLA_NOPE
    in_specs = [
        pl.BlockSpec((tm, MLA_KV_RANK), lambda i: (i, ckv_col // MLA_KV_RANK)),
        pl.BlockSpec((tm, LANES), lambda i: (i, kpe_col // LANES)),
        pl.BlockSpec((1, MLA_KV_RANK), lambda i: (0, 0)),
        pl.BlockSpec((1, MLA_NOPE), lambda i: (0, 0)),
        pl.BlockSpec((1, LANES), lambda i: (0, 0)),
        pl.BlockSpec((None, MLA_KV_RANK, hw), lambda i: (layer, 0, 0)),
        pl.BlockSpec((None, MLA_KV_RANK, hw), lambda i: (layer, 0, 0)),
        pl.BlockSpec((tm, LANES), lambda i: (rope_block(i), 0)),
        pl.BlockSpec((tm, LANES), lambda i: (rope_block(i), 0)),
    ]
    args = [ckv_src, kpe_src, kv_g.reshape(1, MLA_KV_RANK), gn, gp, w_uk, w_uv, cos, sin]
    out_specs = [pl.BlockSpec((tm, MLA_HEADS * MLA_QPAD), lambda i: (i, 0)), pl.BlockSpec((tm, hw), lambda i: (i, 0))]
    out_shape = [jax.ShapeDtypeStruct((m, MLA_HEADS * MLA_QPAD), bf16), jax.ShapeDtypeStruct((m, hw), bf16)]
    kwargs, aliases = {}, {}
    if state is not None:
        aliases = state.aliases(len(args), len(out_specs))
        kwargs = state.kernel_kwargs()
        in_specs += state.in_specs
        args += state.prevs
        out_specs += state.out_specs
        out_shape += state.out_shape
    return pl.pallas_call(
        functools.partial(_mla_kv_kernel, norm_ckv=norm_ckv, rope=rope, **kwargs),
        grid=(m // tm,),
        in_specs=in_specs,
        out_specs=out_specs,
        out_shape=out_shape,
        input_output_aliases=aliases,
        compiler_params=_params(("arbitrary",)),
        name=name,
    )(*args)


def _flash_kernel(*refs, n_main, tk, n_extra, heads, dq, dv):
    if n_extra:
        q_ref, k_ref, v_ref, kx_ref, vx_ref, o_ref = refs
    else:
        q_ref, k_ref, v_ref, o_ref = refs
        kx_ref = vx_ref = None
    tq = q_ref.shape[0]
    chunks = [(k_ref, v_ref, c) for c in range(n_main)] + [(kx_ref, vx_ref, c) for c in range(n_extra)]
    def scores(i, g):
        kr, _, c = chunks[i]
        return _dot_nt(kr[c * tk:(c + 1) * tk, g * dq:(g + 1) * dq], q_ref[:, g * dq:(g + 1) * dq])

    m = [jnp.full((1, tq), -jnp.inf, f32)] * heads
    l = [jnp.zeros((1, tq), f32)] * heads
    acc = [jnp.zeros((dv, tq), f32)] * heads
    s = [scores(0, g) for g in range(heads)]
    for i, (_, vr, c) in enumerate(chunks):
        for g in range(heads):
            s_next = scores(i + 1, g) if i + 1 < len(chunks) else None
            m_new = jnp.maximum(m[g], jnp.max(s[g], axis=0, keepdims=True))
            alpha = jnp.exp2(m[g] - m_new)
            p = jnp.exp2(s[g] - m_new)
            l[g] = alpha * l[g] + jnp.sum(p, axis=0, keepdims=True)
            acc[g] = alpha * acc[g] + _dot_tn(vr[c * tk:(c + 1) * tk, g * dv:(g + 1) * dv], p.astype(bf16))
            m[g], s[g] = m_new, s_next
    for g in range(heads):
        o_ref[:, g * dv:(g + 1) * dv] = (acc[g] / l[g]).T.astype(o_ref.dtype)


def _flash(q, k, v, *, batch, heads, t_len, s_len, row0, dq, dv, tq, tk, heads_per_step=1, extra=None, name):
    assert row0 % tq == 0 and row0 % s_len == 0 and t_len % tq == 0 and s_len % tk == 0
    assert heads % heads_per_step == 0
    nq = t_len // tq
    g = heads_per_step
    in_specs = [
        pl.BlockSpec((tq, g * dq), lambda b, h, i: (row0 // tq + b * nq + i, h)),
        pl.BlockSpec((s_len, g * dq), lambda b, h, i: (row0 // s_len + b, h)),
        pl.BlockSpec((s_len, g * dv), lambda b, h, i: (row0 // s_len + b, h)),
    ]
    args = [q, k, v]
    n_extra = 0
    if extra is not None:
        kx, vx, p_len = extra
        assert p_len % tk == 0
        n_extra = p_len // tk
        in_specs += [
            pl.BlockSpec((p_len, g * dq), lambda b, h, i: (b, h)),
            pl.BlockSpec((p_len, g * dv), lambda b, h, i: (b, h)),
        ]
        args += [kx, vx]
    return pl.pallas_call(
        functools.partial(_flash_kernel, n_main=s_len // tk, tk=tk, n_extra=n_extra, heads=g, dq=dq, dv=dv),
        grid=(batch, heads // g, nq),
        in_specs=in_specs,
        out_specs=pl.BlockSpec((tq, g * dv), lambda b, h, i: (b * nq + i, h)),
        out_shape=jax.ShapeDtypeStruct((batch * t_len, heads * dv), bf16),
        compiler_params=_params(("arbitrary", "arbitrary", "arbitrary")),
        name=name,
    )(*args)


def _na_prep_kernel(*refs, scale, n_prev, n_ctx_tiles, s_len):
    nq_ref, nk_ref, nv_ref, qg_ref, kg_ref = refs[:5]
    q_ref, k_ref, v_ref, k_state_ref, v_state_ref = refs[5 + n_prev:]
    qg = qg_ref[...]
    kg = kg_ref[...]
    tm = nv_ref.shape[0]
    is_ctx = pl.program_id(0) < n_ctx_tiles
    v_ref[...] = nv_ref[...].astype(bf16)

    @pl.when(is_ctx)
    def _():
        for n in range(tm // s_len):
            v_state_ref[n] = nv_ref[n * s_len:(n + 1) * s_len, :]

    for h in range(NA_HEADS):
        sl = slice(h * NA_DH, (h + 1) * NA_DH)
        q = nq_ref[:, sl]
        k = nk_ref[:, sl]
        rq = lax.rsqrt(jnp.mean(q * q, axis=-1, keepdims=True) + EPS)
        rk = lax.rsqrt(jnp.mean(k * k, axis=-1, keepdims=True) + EPS)
        q_ref[:, sl] = (q * rq * qg * scale).astype(bf16)
        kn = k * rk * kg
        k_ref[:, sl] = kn.astype(bf16)

        @pl.when(is_ctx)
        def _():
            for n in range(tm // s_len):
                k_state_ref[n, :, sl] = kn[n * s_len:(n + 1) * s_len]


def _na_prep(z, lay, q_g, k_g, tm, state):
    m = z.shape[0]
    w = NA_HEADS * NA_DH
    spec = pl.BlockSpec((tm, w), lambda i: (i, 0))
    in_specs = [
        pl.BlockSpec((tm, w), lambda i: (i, lay.nq // w)),
        pl.BlockSpec((tm, w), lambda i: (i, lay.nk // w)),
        pl.BlockSpec((tm, w), lambda i: (i, lay.nv // w)),
        pl.BlockSpec((1, NA_DH), lambda i: (0, 0)),
        pl.BlockSpec((1, NA_DH), lambda i: (0, 0)),
    ]
    args = [z, z, z, q_g.reshape(1, NA_DH), k_g.reshape(1, NA_DH)]
    return pl.pallas_call(
        functools.partial(_na_prep_kernel, scale=NA_DH ** -0.5 * LOG2E, **state.kernel_kwargs()),
        grid=(m // tm,),
        in_specs=in_specs + state.in_specs,
        out_specs=[spec, spec, spec] + state.out_specs,
        out_shape=[jax.ShapeDtypeStruct((m, w), bf16)] * 3 + state.out_shape,
        input_output_aliases=state.aliases(len(args), 3),
        compiler_params=_params(("arbitrary",)),
        name="na_prep",
    )(*args, *state.prevs)


def _na_static_maps(rows):
    nblk = rows // NA_QROWS
    reps = (0, min(1, nblk - 1), nblk - 1)
    dr_map = np.full((3, NA_QROWS, NA_KROWS), 2 * NA_WIN_R - 1, np.int32)
    for v, kb in enumerate(reps):
        ws = min(max(NA_QROWS * kb - NA_WIN_R // 2, 0), rows - NA_KROWS)
        for i in range(NA_QROWS):
            qrow = NA_QROWS * kb + i
            rs = min(max(qrow - NA_WIN_R // 2, 0), rows - NA_WIN_R)
            for j in range(NA_KROWS):
                krow = ws + j
                if rs <= krow < rs + NA_WIN_R:
                    dr_map[v, i, j] = krow - qrow + NA_WIN_R - 1
    qcol = np.arange(GRID_W)
    cs = np.clip(qcol - NA_WIN_C // 2, 0, GRID_W - NA_WIN_C)
    in_win = (qcol[None, :] >= cs[:, None]) & (qcol[None, :] < cs[:, None] + NA_WIN_C)
    dc_idx = np.clip(qcol[None, :] - qcol[:, None], -(NA_WIN_C - 1), NA_WIN_C - 1) + (NA_WIN_C - 1)
    return dr_map, in_win, dc_idx


def _na_bias_table(rpb, rows):
    dr_map, in_win, dc_idx = _na_static_maps(rows)
    h = rpb.shape[0]
    bt = jnp.where(in_win.T[None, None], jnp.take(rpb * LOG2E, jnp.asarray(dc_idx.T), axis=2), NEG)
    bt = jnp.concatenate([bt, jnp.full((h, 1, GRID_W, GRID_W), NEG, f32)], axis=1)
    tab = jnp.take(bt, jnp.asarray(dr_map.transpose(0, 2, 1).reshape(-1)), axis=1)
    tab = tab.reshape(h, 3, NA_KROWS, NA_QROWS, GRID_W, GRID_W).transpose(1, 0, 2, 4, 3, 5)
    return tab.reshape(3, h, NA_KROWS * GRID_W, NA_QROWS * GRID_W)


def _na_kernel(q_ref, k_ref, v_ref, kc_ref, vc_ref, bias_ref, o_ref, *, rows, heads):
    kb = pl.program_id(2)
    ws = jnp.clip(NA_QROWS * kb - NA_WIN_R // 2, 0, rows - NA_KROWS)
    win = pl.ds(pl.multiple_of(ws * GRID_W, (NA_WIN_R // 2) * GRID_W), NA_KROWS * GRID_W)
    cols = [slice(g * NA_DH, (g + 1) * NA_DH) for g in range(heads)]
    s_loc = [_dot_nt(k_ref[win, c], q_ref[:, c]) + bias_ref[g] for g, c in enumerate(cols)]
    s_ctx = [_dot_nt(kc_ref[:, c], q_ref[:, c]) for c in cols]
    p_loc, p_ctx, l = [], [], []
    for g in range(heads):
        m = jnp.maximum(jnp.max(s_loc[g], axis=0, keepdims=True), jnp.max(s_ctx[g], axis=0, keepdims=True))
        pl_g = jnp.exp2(s_loc[g] - m)
        pc_g = jnp.exp2(s_ctx[g] - m)
        l.append(jnp.sum(pl_g, axis=0, keepdims=True) + jnp.sum(pc_g, axis=0, keepdims=True))
        p_loc.append(pl_g.astype(bf16))
        p_ctx.append(pc_g.astype(bf16))
    for g, c in enumerate(cols):
        acc = _dot_tn(v_ref[win, c], p_loc[g]) + _dot_tn(vc_ref[:, c], p_ctx[g])
        o_ref[:, c] = (acc / l[g]).T.astype(o_ref.dtype)


def _na_latent(q, k, v, kc, vc, bias, *, batch, t_len, p_len, row0, heads_per_step=2):
    rows = t_len // GRID_W
    assert t_len % GRID_W == 0 and rows % NA_QROWS == 0 and rows >= NA_KROWS
    tq = NA_QROWS * GRID_W
    assert row0 % tq == 0 and row0 % t_len == 0 and NA_HEADS % heads_per_step == 0
    nblk = rows // NA_QROWS
    g = heads_per_step
    gw = g * NA_DH

    def variant(i):
        return jnp.where(i == 0, 0, jnp.where(i == nblk - 1, 2, 1))

    return pl.pallas_call(
        functools.partial(_na_kernel, rows=rows, heads=g),
        grid=(batch, NA_HEADS // g, nblk),
        in_specs=[
            pl.BlockSpec((tq, gw), lambda b, h, i: (row0 // tq + b * nblk + i, h)),
            pl.BlockSpec((t_len, gw), lambda b, h, i: (row0 // t_len + b, h)),
            pl.BlockSpec((t_len, gw), lambda b, h, i: (row0 // t_len + b, h)),
            pl.BlockSpec((p_len, gw), lambda b, h, i: (b, h)),
            pl.BlockSpec((p_len, gw), lambda b, h, i: (b, h)),
            pl.BlockSpec((None, g, NA_KROWS * GRID_W, tq), lambda b, h, i: (variant(i), h, 0, 0)),
        ],
        out_specs=pl.BlockSpec((tq, gw), lambda b, h, i: (b * nblk + i, h)),
        out_shape=jax.ShapeDtypeStruct((batch * t_len, NA_HEADS * NA_DH), bf16),
        compiler_params=_params(("arbitrary", "arbitrary", "arbitrary")),
        name="na_latent",
    )(q, k, v, kc, vc, bias)


def _hgrn_tables():
    c, lv = HG_CHUNK, HG_LEVELS
    t = np.arange(c)[:, None]
    r = np.arange(c)[None, :]
    wcum = np.stack([r <= t, r >= t]).astype(np.float32)
    qside = np.zeros((2, lv * c, LANES), np.float32)
    mask = np.zeros((2, lv + 1, c, c), np.float32)
    mask[:, 0] = np.eye(c)
    for l in range(lv):
        hs = 1 << l
        blk = t // (2 * hs)
        mid = blk * (2 * hs) + hs
        qf = t >= mid
        qb = t < mid
        qside[0, l * c:(l + 1) * c] = np.where(qf, 1.0, -1.0)
        qside[1, l * c:(l + 1) * c] = np.where(qb, 1.0, -1.0)
        same = blk == blk.T
        mask[0, 1 + l] = same & qf & ~qf.T
        mask[1, 1 + l] = same & qb & ~qb.T
    return wcum, qside, mask


def _hgrn_kernel(*refs, n_chunks, heads, has_state, emit_state, n_prev=0):
    (hq_ref, hff_ref, hfb_ref, hi_ref, hg_ref, la_ref, lc_ref, om_ref, ng_ref,
     wcum_ref, qside_ref, mask_ref) = refs[:12]
    rest = list(refs[12:])
    s0_ref = rest.pop(0) if has_state else None
    del rest[:n_prev]
    o_ref = rest.pop(0)
    st_ref = rest.pop(0) if emit_state else None
    o_scr, s_scr, b_scr = rest
    c, lv = HG_CHUNK, HG_LEVELS

    o_scr[...] = jnp.zeros_like(o_scr)
    for d in range(2):
        for g in range(heads):
            if has_state:
                s_scr[d, g] = s0_ref[d, g].T
            else:
                s_scr[d, g] = jnp.zeros((HG_V, HG_K), f32)

    chains = [(d, g) for g in range(heads) for d in range(2)]
    low_half = lax.broadcasted_iota(jnp.int32, (SUBLANES, HEAD_W), 0) < SUBLANES // 2

    def gates_and_cumsum(ci, row):
        d, g = chains[ci]
        rs = pl.ds(pl.multiple_of(row, c), c)
        cs = slice(g * HEAD_W, (g + 1) * HEAD_W)
        x = (hff_ref if d == 0 else hfb_ref)[rs, cs]
        x2 = x * LOG2E
        u = jnp.exp2(-jnp.abs(x2))
        inv = 1.0 / (1.0 + u)
        pos = x >= 0.0
        log_sig = jnp.minimum(x2, 0.0) - jnp.log2(1.0 + u)
        a = la_ref[d:d + 1, cs]
        bb = lc_ref[d:d + 1, cs] + log_sig
        lf = jnp.maximum(a, bb) + jnp.log2(1.0 + jnp.exp2(-jnp.abs(a - bb)))
        om = om_ref[d:d + 1, cs]
        f = (1.0 - om) + om * (jnp.where(pos, 1.0, u) * inv)
        kin = om * (jnp.where(pos, u, 1.0) * inv)
        q = _silu(hq_ref[rs, cs])
        v = hi_ref[rs, cs].astype(bf16)
        l1 = lf.astype(bf16)
        r1 = lf - l1.astype(f32)
        l2 = r1.astype(bf16)
        l3 = (r1 - l2.astype(f32)).astype(bf16)
        w = wcum_ref[d]
        b = _dot(w, l1) + _dot(w, l2) + _dot(w, l3)
        b_scr[ci] = b
        return dict(rs=rs, cs=cs, f=f, kin=kin, q=q, v=v, b=b)

    def intra_chunk(ci, s):
        d, _ = chains[ci]
        b, q, kin = s["b"], s["q"], s["kin"]

        def row(r):
            return b_scr[ci, r:r + 1, :]

        s["b_end"] = row(c - 1 if d == 0 else 0)
        xs = [jnp.where(qside_ref[d, 0:c, :] > 0.0, q * s["f"], kin).astype(bf16)]
        for l in range(1, lv):
            hs = 1 << l
            blk = 2 * hs
            pick = hs - 1 if d == 0 else hs
            if hs >= SUBLANES:
                pieces = []
                for j in range(c // blk):
                    m = row(blk * j + pick)
                    early = slice(blk * j, blk * j + hs)
                    late = slice(blk * j + hs, blk * (j + 1))
                    q_sl, k_sl = (late, early) if d == 0 else (early, late)
                    xq = q[q_sl] * jnp.exp2(b[q_sl] - m)
                    xk = kin[k_sl] * jnp.exp2(m - b[k_sl])
                    pieces += [xk, xq] if d == 0 else [xq, xk]
                xs.append(jnp.concatenate(pieces, axis=0).astype(bf16))
                continue
            if blk < SUBLANES:
                pieces = [jnp.where(low_half, jnp.broadcast_to(row(SUBLANES * j + pick), (SUBLANES, HEAD_W)),
                                    jnp.broadcast_to(row(SUBLANES * j + blk + pick), (SUBLANES, HEAD_W)))
                          for j in range(c // SUBLANES)]
            else:
                pieces = [jnp.broadcast_to(row(blk * j + pick), (blk, HEAD_W)) for j in range(c // blk)]
            m = pieces[0] if len(pieces) == 1 else jnp.concatenate(pieces, axis=0)
            sign = qside_ref[d, l * c:(l + 1) * c, :]
            xs.append((jnp.where(sign > 0.0, q, kin) * jnp.exp2((b - m) * sign)).astype(bf16))
        att = mask_ref[d, 0] * _dot_nt(q.astype(bf16), kin.astype(bf16)).astype(bf16)
        for l in range(lv):
            att = att + mask_ref[d, 1 + l] * _dot_nt(xs[l], xs[l]).astype(bf16)
        s["att"] = att

    def state_step(ci, s):
        d, g = chains[ci]
        b, q, kin, v = s["b"], s["q"], s["kin"], s["v"]
        st = s_scr[d, g]
        o = _dot(s["att"], v) + _dot_nt((q * jnp.exp2(b)).astype(bf16), st.astype(bf16))
        b_end = s["b_end"]
        s_scr[d, g] = st * jnp.exp2(b_end) + _dot_tn(v, (kin * jnp.exp2(b_end - b)).astype(bf16))
        o_scr[s["rs"], s["cs"]] += o

    def body(i, _):
        rows = (i * c, (n_chunks - 1 - i) * c)
        states = [gates_and_cumsum(ci, rows[chains[ci][0]]) for ci in range(len(chains))]
        for ci, s in enumerate(states):
            intra_chunk(ci, s)
        for ci, s in enumerate(states):
            state_step(ci, s)
        return 0

    lax.fori_loop(0, n_chunks, body, 0)

    ng = ng_ref[...]
    for g in range(heads):
        cs = slice(g * HEAD_W, (g + 1) * HEAD_W)
        o = o_scr[:, cs]
        y = o * lax.rsqrt(jnp.mean(o * o, axis=-1, keepdims=True) + EPS) * ng
        o_ref[:, cs] = (y * _silu(hg_ref[:, cs])).astype(o_ref.dtype)
        if emit_state:
            for d in range(2):
                st_ref[d, g] = s_scr[d, g].T


def _hgrn(z, lay, lb_tabs, norm_g, tables, *, batch, t_len, row0, s0, state_out=None, heads_per_step=2):
    emit_state = state_out is not None
    assert t_len % HG_CHUNK == 0 and row0 % t_len == 0 and HG_K == HEAD_W and HG_V == HEAD_W
    g = heads_per_step
    gw = g * HEAD_W
    la, lc, om = lb_tabs
    wcum, qside, mask = tables
    r0 = row0 // t_len

    def zspec(col):
        return pl.BlockSpec((t_len, gw), lambda b, h: (r0 + b, col // gw + h))

    def const(shape):
        return pl.BlockSpec(shape, lambda b, h: (0,) * len(shape))

    in_specs = [zspec(lay.hq), zspec(lay.hff), zspec(lay.hfb), zspec(lay.hi), zspec(lay.hg),
                pl.BlockSpec((2, gw), lambda b, h: (0, h)), pl.BlockSpec((2, gw), lambda b, h: (0, h)),
                pl.BlockSpec((2, gw), lambda b, h: (0, h)), const((1, HG_V)),
                const(wcum.shape), const(qside.shape), const(mask.shape)]
    args = [z, z, z, z, z, la, lc, om, norm_g.reshape(1, HG_V), wcum, qside, mask]
    state_spec = pl.BlockSpec((None, 2, g, HG_K, HG_V), lambda b, h: (b, 0, h, 0, 0))
    if s0 is not None:
        in_specs.append(state_spec)
        args.append(s0)
    out_specs = [pl.BlockSpec((t_len, gw), lambda b, h: (b, h))]
    out_shape = [jax.ShapeDtypeStruct((batch * t_len, HG_HEADS * HG_V), bf16)]
    aliases, n_prev = {}, 0
    if emit_state:
        prev, layer, depth = state_out
        if prev is not None:
            aliases, n_prev = {len(args): 1}, 1
            in_specs.append(pl.BlockSpec(memory_space=pl.ANY))
            args.append(prev)
        out_specs.append(pl.BlockSpec((None, None, 2, g, HG_K, HG_V), lambda b, h: (b, layer, 0, h, 0, 0)))
        out_shape.append(jax.ShapeDtypeStruct((batch, depth, 2, HG_HEADS, HG_K, HG_V), f32))
    res = pl.pallas_call(
        functools.partial(_hgrn_kernel, n_chunks=t_len // HG_CHUNK, heads=g, has_state=s0 is not None,
                          emit_state=emit_state, n_prev=n_prev),
        grid=(batch, HG_HEADS // g),
        in_specs=in_specs,
        out_specs=out_specs,
        out_shape=out_shape,
        input_output_aliases=aliases,
        scratch_shapes=[pltpu.VMEM((t_len, gw), f32), pltpu.VMEM((2, g, HG_V, HG_K), f32),
                        pltpu.VMEM((2 * g, HG_CHUNK, HEAD_W), f32)],
        compiler_params=_params(("arbitrary", "arbitrary")),
        name="hgrn_ctx" if emit_state else "hgrn_latent",
    )(*args)
    return res if emit_state else (res[0], None)


def _merge_kernel(oac_ref, oal_ref, obc_ref, obl_ref, occ_ref, ocl_ref, ga_ref, gb_ref, gc_ref, w_ref, o_ref,
                  *, n_ctx_tiles):
    def compute(oa_ref, ob_ref, oc_ref):
        m = jax.nn.sigmoid(ga_ref[...]) * _dot(oa_ref[...], w_ref[0])
        m = m + jax.nn.sigmoid(gb_ref[...]) * _dot(ob_ref[...], w_ref[1])
        m = m + jax.nn.sigmoid(gc_ref[...]) * _dot(oc_ref[...], w_ref[2])
        o_ref[...] = m.astype(o_ref.dtype)

    is_ctx = pl.program_id(1) < n_ctx_tiles

    @pl.when(is_ctx)
    def _():
        compute(oac_ref, obc_ref, occ_ref)

    @pl.when(jnp.logical_not(is_ctx))
    def _():
        compute(oal_ref, obl_ref, ocl_ref)


def _merge(o_a, o_b, o_c, z, lay, w_branch, layer, tm, tn):
    bw = o_a[0].shape[1]
    m = z.shape[0]
    d = w_branch.shape[3]
    nct = o_a[0].shape[0] // tm
    cspec = pl.BlockSpec((tm, bw), lambda j, i: (jnp.minimum(i, nct - 1), 0))
    lspec = pl.BlockSpec((tm, bw), lambda j, i: (jnp.maximum(i - nct, 0), 0))

    def gate(col):
        return pl.BlockSpec((tm, tn), lambda j, i: (i, col // tn + j))

    return pl.pallas_call(
        functools.partial(_merge_kernel, n_ctx_tiles=nct),
        grid=(d // tn, m // tm),
        in_specs=[cspec, lspec, cspec, lspec, cspec, lspec, gate(lay.ga), gate(lay.gb), gate(lay.gc),
                  pl.BlockSpec((None, 3, bw, tn), lambda j, i: (layer, 0, 0, j))],
        out_specs=pl.BlockSpec((tm, tn), lambda j, i: (i, j)),
        out_shape=jax.ShapeDtypeStruct((m, d), bf16),
        compiler_params=_params(("arbitrary", "arbitrary")),
        name="merge",
    )(o_a[0], o_a[1], o_b[0], o_b[1], o_c[0], o_c[1], z, z, z, w_branch)


def _proj_residual_kernel(m_ref, w_ref, x_ref, g_ref, o_ref):
    o_ref[...] = x_ref[...] + g_ref[...] * _dot(m_ref[...], w_ref[...])


def _proj_residual(mm, w, layer, x, gate, group_of_tile, tm, tn):
    m, k = mm.shape
    d = w.shape[2]
    return pl.pallas_call(
        _proj_residual_kernel,
        grid=(m // tm, d // tn),
        in_specs=[
            pl.BlockSpec((tm, k), lambda i, j: (i, 0)),
            pl.BlockSpec((None, k, tn), lambda i, j: (layer, 0, j)),
            pl.BlockSpec((tm, tn), lambda i, j: (i, j)),
            pl.BlockSpec((None, 1, tn), lambda i, j: (group_of_tile(i), 0, j)),
        ],
        out_specs=pl.BlockSpec((tm, tn), lambda i, j: (i, j)),
        out_shape=jax.ShapeDtypeStruct((m, d), f32),
        compiler_params=_params(("arbitrary", "arbitrary")),
        name="out_proj",
    )(mm, w, x, gate)


def _ffn_kernel(x_ref, xprev_ref, xnext_ref, ng_ref, sc_ref, sh_ref, wa_ref, wg_ref, cw_ref, cb_ref, wd_ref, gate_ref,
                o_ref, h_scr, *, m_ctx, s_len, t_len, n_k):
    i = pl.program_id(0)
    k = pl.program_id(1)
    tm = x_ref.shape[0]
    tk = wa_ref.shape[1]
    halo = SUBLANES

    @pl.when(k == 0)
    def _():
        def normmod(x):
            ms = jnp.mean(x * x, axis=-1, keepdims=True)
            y = x * lax.rsqrt(ms + EPS) * ng_ref[...]
            return (y * (1.0 + sc_ref[...]) + sh_ref[...]).astype(bf16)

        h_scr[0:halo] = normmod(xprev_ref[...])
        h_scr[halo:halo + tm] = normmod(x_ref[...])
        h_scr[halo + tm:halo + tm + halo] = normmod(xnext_ref[...])
        o_ref[...] = jnp.zeros_like(o_ref)

    row = i * tm + lax.broadcasted_iota(jnp.int32, (tm, LANES), 0)
    pos = jnp.where(row < m_ctx, _mod_const(row, s_len), _mod_const(row - m_ctx, t_len))
    last = jnp.where(row < m_ctx, s_len - 1, t_len - 1)
    keep_prev = jnp.tile(jnp.where(pos == 0, 0.0, 1.0), (1, tk // LANES))
    keep_next = jnp.tile(jnp.where(pos == last, 0.0, 1.0), (1, tk // LANES))

    g_ext = _dot(h_scr[...], wg_ref[...])
    a = _dot(h_scr[halo:halo + tm], wa_ref[...])
    n_ext = tm + 2 * halo
    g = g_ext[halo:halo + tm]
    g_prev = pltpu.roll(g_ext, 1, 0)[halo:halo + tm]
    g_next = pltpu.roll(g_ext, n_ext - 1, 0)[halo:halo + tm]
    conv = (g_prev * keep_prev * cw_ref[0:1, :] + g * cw_ref[1:2, :] + g_next * keep_next * cw_ref[2:3, :]
            + cb_ref[...])
    u = (_silu(conv) * a).astype(bf16)
    o_ref[...] = _dot(u, wd_ref[...]) + o_ref[...]

    @pl.when(k == n_k - 1)
    def _():
        o_ref[...] = x_ref[...] + gate_ref[...] * o_ref[...]


def _ffn(x, norm_g, sc, sh, w_up, conv_w, conv_b, w_down, layer, gate, group_of_tile, tm, tk, m_ctx, s_len, t_len):
    m, d = x.shape
    f = w_down.shape[1]
    nk = f // tk
    nsub = tm // SUBLANES
    last_sub = m // SUBLANES - 1
    return pl.pallas_call(
        functools.partial(_ffn_kernel, m_ctx=m_ctx, s_len=s_len, t_len=t_len, n_k=nk),
        grid=(m // tm, nk),
        in_specs=[
            pl.BlockSpec((tm, d), lambda i, k: (i, 0), pipeline_mode=pl.Buffered(1)),
            pl.BlockSpec((SUBLANES, d), lambda i, k: (jnp.maximum(i * nsub - 1, 0), 0)),
            pl.BlockSpec((SUBLANES, d), lambda i, k: (jnp.minimum((i + 1) * nsub, last_sub), 0)),
            pl.BlockSpec((1, d), lambda i, k: (0, 0)),
            pl.BlockSpec((None, 1, d), lambda i, k: (group_of_tile(i), 0, 0)),
            pl.BlockSpec((None, 1, d), lambda i, k: (group_of_tile(i), 0, 0)),
            pl.BlockSpec((None, d, tk), lambda i, k: (layer, 0, k)),
            pl.BlockSpec((None, d, tk), lambda i, k: (layer, 0, nk + k)),
            pl.BlockSpec((3, tk), lambda i, k: (0, k)),
            pl.BlockSpec((1, tk), lambda i, k: (0, k)),
            pl.BlockSpec((None, tk, d), lambda i, k: (layer, k, 0)),
            pl.BlockSpec((None, 1, d), lambda i, k: (group_of_tile(i), 0, 0)),
        ],
        out_specs=pl.BlockSpec((tm, d), lambda i, k: (i, 0), pipeline_mode=pl.Buffered(1)),
        out_shape=jax.ShapeDtypeStruct((m, d), f32),
        scratch_shapes=[pltpu.VMEM((tm + 2 * SUBLANES, d), bf16)],
        compiler_params=_params(("arbitrary", "arbitrary")),
        name="ffn",
    )(x, x, x, norm_g.reshape(1, d), sc, sh, w_up, w_up, conv_w, conv_b.reshape(1, f), w_down, gate)


def kernel(x_prompt, x_sample, cache_mla_ckv, cache_mla_kpe, cache_na_k, cache_na_v, state_hgrn, c, c_ctx, w_mod, b_mod, norm1_g, norm2_g, w_in, mla_kv_norm_g, mla_q_norm_g, mla_k_norm_g, mla_w_uk, mla_w_uv, hgrn_lower_bounds, hgrn_norm_g, na_q_norm_g, na_k_norm_g, na_rpb, w_branch, w_out, ffn_w_up, ffn_conv_w, ffn_conv_b, ffn_w_down):
    n_ctx, s_len, d = x_prompt.shape
    n_lat, t_len, _ = x_sample.shape
    depth = w_in.shape[0]
    p_len = cache_mla_ckv.shape[2]
    ffn = ffn_w_down.shape[1]
    m_ctx, m_lat = n_ctx * s_len, n_lat * t_len
    m = m_ctx + m_lat
    assert m_ctx % t_len == 0, "context rows must be a whole number of latent sequences"

    tm = _pow2_tile(1024, m_ctx, t_len)
    tn = _pow2_tile(1024, d)
    tp = _pow2_tile(512, m_ctx, t_len)
    lay = _ZLayout(d, tn)

    def group_fn(rows):
        return lambda i: jnp.where(i < m_ctx // rows, 0, 1 + (i - m_ctx // rows) // (t_len // rows))

    group_of_tile = group_fn(tm)

    n_groups = -(-(1 + n_lat) // SUBLANES) * SUBLANES
    cond = jnp.concatenate([c_ctx[None], c, jnp.zeros((n_groups - 1 - n_lat, d), f32)], axis=0)
    mods = _modulation(cond, w_mod, b_mod).reshape(depth, n_groups, 6, d).transpose(0, 2, 1, 3)[:, :, :, None, :]

    sm = jax.nn.softmax(hgrn_lower_bounds.astype(f32), axis=1)
    csum = jnp.cumsum(sm, axis=1)
    lower = csum - csum[:, :1]
    hg_tabs = tuple(jnp.asarray(t, dt) for t, dt in zip(_hgrn_tables(), (bf16, f32, bf16)))

    cos, sin = _rope_tables(t_len, tp)
    n_ctx_tp = m_ctx // tp

    def rope_block(i):
        return jnp.where(i < n_ctx_tp, 0, 1 + (i - n_ctx_tp) % (t_len // tp))

    x = jnp.concatenate([x_prompt.reshape(m_ctx, d), x_sample.reshape(m_lat, d)], axis=0)
    w_in_b, w_kpe_b = _permute_w_in(w_in, lay)
    w_uk = mla_w_uk.reshape(depth, MLA_KV_RANK, MLA_HEADS * MLA_NOPE).astype(bf16)
    w_uv = mla_w_uv.reshape(depth, MLA_KV_RANK, MLA_HEADS * MLA_V).astype(bf16)
    w_branch_b, w_out_b = w_branch.astype(bf16), w_out.astype(bf16)
    w_up_b, w_down_b = ffn_w_up.astype(bf16), ffn_w_down.astype(bf16)

    st_mla = st_na = st_hg = None
    for l in range(depth):
        sh1, sc1, g1, sh2, sc2, g2 = (mods[l, k] for k in range(6))
        z, kpe = _normmod_matmul(x, norm1_g[l], sc1, sh1, w_in_b, l, group_of_tile, tm, tn, "in_proj", w_side=w_kpe_b)

        q_mla = _mla_q(z, lay, mla_q_norm_g[l], cos, sin, rope_block, tp)
        k_mla, v_mla, *st_mla = _mla_kv(
            z, lay.ckv, kpe, 0, mla_kv_norm_g[l], mla_k_norm_g[l], w_uk, w_uv, l, cos, sin, rope_block, tp, True, True,
            "mla_kv", state=_CtxState(st_mla, (MLA_KV_RANK, MLA_ROPE), l, depth, n_ctx, s_len, tp))
        ckv_c = cache_mla_ckv[:, l].reshape(n_lat * p_len, MLA_KV_RANK)
        kpe_c = jnp.pad(cache_mla_kpe[:, l].reshape(n_lat * p_len, MLA_ROPE), ((0, 0), (0, LANES - MLA_ROPE)))
        tc = _pow2_tile(512, n_lat * p_len)
        k_c, v_c = _mla_kv(ckv_c, 0, kpe_c, 0, mla_kv_norm_g[l], mla_k_norm_g[l], w_uk, w_uv, l,
                           cos, sin, lambda i: 0, tc, False, False, "mla_kv_cache")
        tq_c = _pow2_tile(256, s_len)
        o_mla_c = _flash(q_mla, k_mla, v_mla, batch=n_ctx, heads=MLA_HEADS, t_len=s_len, s_len=s_len, row0=0,
                         dq=MLA_QPAD, dv=MLA_V, tq=tq_c, tk=tq_c, heads_per_step=MLA_HEADS, name="mla_attn_ctx")
        tq_l = _pow2_tile(512, t_len, p_len)
        o_mla_l = _flash(q_mla, k_mla, v_mla, batch=n_lat, heads=MLA_HEADS, t_len=t_len, s_len=t_len, row0=m_ctx,
                         dq=MLA_QPAD, dv=MLA_V, tq=tq_l, tk=tq_l, heads_per_step=2, extra=(k_c, v_c, p_len),
                         name="mla_attn_latent")

        lb = lower[:, l]
        lb_tabs = (jnp.log(lb) * LOG2E, jnp.log1p(-lb) * LOG2E, 1.0 - lb)
        o_hg_c, st_hg = _hgrn(z, lay, lb_tabs, hgrn_norm_g[l], hg_tabs, batch=n_ctx, t_len=s_len, row0=0,
                              s0=None, state_out=(st_hg, l, depth))
        o_hg_l, _ = _hgrn(z, lay, lb_tabs, hgrn_norm_g[l], hg_tabs, batch=n_lat, t_len=t_len, row0=m_ctx,
                          s0=state_hgrn[:, l])

        q_na, k_na, v_na, *st_na = _na_prep(
            z, lay, na_q_norm_g[l], na_k_norm_g[l], tp,
            _CtxState(st_na, (NA_HEADS * NA_DH, NA_HEADS * NA_DH), l, depth, n_ctx, s_len, tp))
        o_na_c = _flash(q_na, k_na, v_na, batch=n_ctx, heads=NA_HEADS, t_len=s_len, s_len=s_len, row0=0,
                        dq=NA_DH, dv=NA_DH, tq=tq_c, tk=tq_c, heads_per_step=NA_HEADS, name="na_attn_ctx")
        kc_na = cache_na_k[:, l].reshape(n_lat * p_len, NA_HEADS * NA_DH).astype(bf16)
        vc_na = cache_na_v[:, l].reshape(n_lat * p_len, NA_HEADS * NA_DH).astype(bf16)
        bias = _na_bias_table(na_rpb[l], t_len // GRID_W)
        o_na_l = _na_latent(q_na, k_na, v_na, kc_na, vc_na, bias, batch=n_lat, t_len=t_len, p_len=p_len, row0=m_ctx)

        merged = _merge((o_mla_c, o_mla_l), (o_hg_c, o_hg_l), (o_na_c, o_na_l), z, lay, w_branch_b, l, tp, tn)
        x = _proj_residual(merged, w_out_b, l, x, g1, group_fn(tp), tp, d)

        x = _ffn(x, norm2_g[l], sc2, sh2, w_up_b, ffn_conv_w[l], ffn_conv_b[l], w_down_b, l, g2, group_of_tile, tm,
                 _pow2_tile(512, ffn), m_ctx, s_len, t_len)

    y_prompt = x[:m_ctx].reshape(n_ctx, s_len, d)
    y_sample = x[m_ctx:].reshape(n_lat, t_len, d)
    na_shape = (n_ctx, depth, s_len, NA_HEADS, NA_DH)
    return (y_prompt, y_sample, st_mla[0], st_mla[1], st_na[0].reshape(na_shape), st_na[1].reshape(na_shape), st_hg)
```

```python
import functools
import math

import numpy as np
import jax
import jax.numpy as jnp
from jax import lax
from jax.experimental import pallas as pl
from jax.experimental.pallas import tpu as pltpu

f32 = jnp.float32
bf16 = jnp.bfloat16

GRID_W = 64
MLA_HEADS, MLA_NOPE, MLA_ROPE, MLA_V, MLA_KV_RANK = 8, 128, 64, 128, 512
MLA_QK = MLA_NOPE + MLA_ROPE
HG_HEADS, HG_K, HG_V = 8, 128, 128
NA_HEADS, NA_DH, NA_WIN_R, NA_WIN_C = 8, 128, 8, 16
ROPE_BASE = 10000.0
EPS = 1e-6
NEG = -1e30
LOG2E = 1.4426950408889634

LANES = 128
SUBLANES = 8
VMEM_LIMIT_BYTES = 56 * 1024 * 1024

HEAD_W = 128
MLA_QPAD = 256
HG_CHUNK = 128
HG_LEVELS = int(math.log2(HG_CHUNK))
NA_QROWS = 8
NA_KROWS = 16


def _params(semantics):
    return pltpu.CompilerParams(dimension_semantics=semantics, vmem_limit_bytes=VMEM_LIMIT_BYTES)


def _pow2_tile(pref, *dims):
    t = pref
    while any(d % t for d in dims):
        t //= 2
    assert t >= SUBLANES, (pref, dims)
    return t


def _dot(a, b):
    return jnp.dot(a, b, preferred_element_type=f32)


def _dot_nt(a, b):
    return lax.dot_general(a, b, (((1,), (1,)), ((), ())), preferred_element_type=f32)


def _dot_tn(a, b):
    return lax.dot_general(a, b, (((0,), (0,)), ((), ())), preferred_element_type=f32)


def _silu(x):
    return x * jax.nn.sigmoid(x)


def _mod_const(x, n):
    return jnp.bitwise_and(x, n - 1) if n & (n - 1) == 0 else lax.rem(x, n)


class _ZLayout:
    def __init__(self, d_model, tn):
        self.mq = 0
        self.ckv = self.mq + MLA_HEADS * MLA_QK
        self.hq = self.ckv + MLA_KV_RANK
        self.hff = self.hq + HG_HEADS * HG_K
        self.hfb = self.hff + HG_HEADS * HG_K
        self.hi = self.hfb + HG_HEADS * HG_K
        self.hg = self.hi + HG_HEADS * HG_V
        self.nq = self.hg + HG_HEADS * HG_V
        self.nk = self.nq + NA_HEADS * NA_DH
        self.nv = self.nk + NA_HEADS * NA_DH
        self.ga = self.nv + NA_HEADS * NA_DH
        self.gb = self.ga + d_model
        self.gc = self.gb + d_model
        self.width = self.gc + d_model
        assert self.width % tn == 0


def _permute_w_in(w, lay):
    n_front = MLA_HEADS * MLA_QK + MLA_KV_RANK
    kpe = w[..., n_front:n_front + MLA_ROPE]
    main = jnp.concatenate([w[..., :n_front], w[..., n_front + MLA_ROPE:]], axis=-1).astype(bf16)
    assert main.shape[-1] == lay.width
    pad = [(0, 0)] * (w.ndim - 1) + [(0, LANES - MLA_ROPE)]
    return main, jnp.pad(kpe, pad).astype(bf16)


def _mod_kernel(c_ref, w_ref, b_ref, o_ref):
    a = _silu(c_ref[...]).astype(bf16)
    o_ref[...] = _dot(a, w_ref[...].astype(bf16)) + b_ref[...]


def _modulation(cond, w_mod, b_mod):
    depth, d, n = w_mod.shape
    g = cond.shape[0]
    tn = _pow2_tile(1024, n)
    return pl.pallas_call(
        _mod_kernel,
        grid=(depth, n // tn),
        in_specs=[
            pl.BlockSpec((g, d), lambda l, j: (0, 0)),
            pl.BlockSpec((None, d, tn), lambda l, j: (l, 0, j)),
            pl.BlockSpec((None, 1, tn), lambda l, j: (l, 0, j)),
        ],
        out_specs=pl.BlockSpec((None, g, tn), lambda l, j: (l, 0, j)),
        out_shape=jax.ShapeDtypeStruct((depth, g, n), f32),
        compiler_params=_params(("arbitrary", "arbitrary")),
        name="modulation",
    )(cond, w_mod, b_mod.reshape(depth, 1, n))


def _normmod_matmul_kernel(x_ref, g_ref, sc_ref, sh_ref, w_ref, *rest):
    if len(rest) == 4:
        ws_ref, o_ref, os_ref, h_scr = rest
    else:
        (o_ref, h_scr), ws_ref, os_ref = rest, None, None

    @pl.when(pl.program_id(1) == 0)
    def _():
        x = x_ref[...]
        ms = jnp.mean(x * x, axis=-1, keepdims=True)
        y = x * lax.rsqrt(ms + EPS) * g_ref[...]
        h_scr[...] = (y * (1.0 + sc_ref[...]) + sh_ref[...]).astype(bf16)
        if ws_ref is not None:
            os_ref[...] = _dot(h_scr[...], ws_ref[...])

    o_ref[...] = _dot(h_scr[...], w_ref[...])


def _normmod_matmul(x, g, sc, sh, w, layer, group_of_tile, tm, tn, name, w_side=None):
    m, d = x.shape
    n = w.shape[2]
    in_specs = [
        pl.BlockSpec((tm, d), lambda i, j: (i, 0)),
        pl.BlockSpec((1, d), lambda i, j: (0, 0)),
        pl.BlockSpec((None, 1, d), lambda i, j: (group_of_tile(i), 0, 0)),
        pl.BlockSpec((None, 1, d), lambda i, j: (group_of_tile(i), 0, 0)),
        pl.BlockSpec((None, d, tn), lambda i, j: (layer, 0, j)),
    ]
    args = [x, g.reshape(1, d), sc, sh, w]
    out_specs = [pl.BlockSpec((tm, tn), lambda i, j: (i, j))]
    out_shape = [jax.ShapeDtypeStruct((m, n), f32)]
    if w_side is not None:
        ns = w_side.shape[2]
        in_specs.append(pl.BlockSpec((None, d, ns), lambda i, j: (layer, 0, 0)))
        args.append(w_side)
        out_specs.append(pl.BlockSpec((tm, ns), lambda i, j: (i, 0)))
        out_shape.append(jax.ShapeDtypeStruct((m, ns), f32))
    res = pl.pallas_call(
        _normmod_matmul_kernel,
        grid=(m // tm, n // tn),
        in_specs=in_specs,
        out_specs=out_specs,
        out_shape=out_shape,
        scratch_shapes=[pltpu.VMEM((tm, d), bf16)],
        compiler_params=_params(("arbitrary", "arbitrary")),
        name=name,
    )(*args)
    return res if w_side is not None else res[0]


def _rope(x, cos, sin_signed):
    n = x.shape[-1]
    half = MLA_ROPE // 4
    lane = lax.broadcasted_iota(jnp.int32, x.shape, 1)
    partner = jnp.where((lane % (2 * half)) < half, pltpu.roll(x, n - half, 1), pltpu.roll(x, half, 1))
    return x * cos + partner * sin_signed


def _mla_q_kernel(mq_ref, gn_ref, gp_ref, cos_ref, sin_ref, o_ref, *, scale):
    tm = mq_ref.shape[0]
    lane = lax.broadcasted_iota(jnp.int32, (tm, LANES), 1)
    low = lane < MLA_ROPE
    cos = cos_ref[...]
    sin = sin_ref[...]
    gn = gn_ref[...]
    gp = gp_ref[...]
    for pair in range(MLA_HEADS // 2):
        t0, t1, t2 = (mq_ref[:, (3 * pair + t) * LANES:(3 * pair + t + 1) * LANES] for t in range(3))
        nopes = (t0, jnp.where(low, pltpu.roll(t1, MLA_ROPE, 1), pltpu.roll(t2, MLA_ROPE, 1)))
        pcol = jnp.where(low, t1, t2)
        sq = pcol * pcol
        ss_all = jnp.sum(sq, axis=-1, keepdims=True)
        ss_low = jnp.sum(jnp.where(low, sq, 0.0), axis=-1, keepdims=True)
        rot = _rope(pcol * gp, cos, sin)
        for k, ss_pe in enumerate((ss_low, ss_all - ss_low)):
            h = 2 * pair + k
            nope = nopes[k]
            ss = jnp.sum(nope * nope, axis=-1, keepdims=True) + ss_pe
            r = lax.rsqrt(ss * (1.0 / MLA_QK) + EPS) * scale
            pe = rot if k == 0 else pltpu.roll(rot, MLA_ROPE, 1)
            o_ref[:, h * MLA_QPAD:h * MLA_QPAD + MLA_NOPE] = (nope * r * gn).astype(bf16)
            o_ref[:, h * MLA_QPAD + MLA_NOPE:(h + 1) * MLA_QPAD] = jnp.where(low, pe * r, 0.0).astype(bf16)


class _CtxState:
    def __init__(self, prevs, widths, layer, depth, n_ctx, s_len, tm):
        assert tm % s_len == 0
        self.n_seq = tm // s_len
        self.s_len = s_len
        self.n_tiles = n_ctx * s_len // tm
        self.prevs = list(prevs)
        n_tiles = self.n_tiles
        self.out_specs = [pl.BlockSpec((self.n_seq, None, s_len, w), lambda i: (jnp.minimum(i, n_tiles - 1), layer, 0, 0))
                          for w in widths]
        self.out_shape = [jax.ShapeDtypeStruct((n_ctx, depth, s_len, w), f32) for w in widths]
        self.in_specs = [pl.BlockSpec(memory_space=pl.ANY)] * len(self.prevs)

    def aliases(self, n_inputs_before, n_outputs_before):
        return {n_inputs_before + k: n_outputs_before + k for k in range(len(self.prevs))}

    def kernel_kwargs(self):
        return dict(n_prev=len(self.prevs), n_ctx_tiles=self.n_tiles, s_len=self.s_len)


def _mla_kv_kernel(*refs, norm_ckv, rope, n_prev=0, n_ctx_tiles=0, s_len=0):
    ckv_ref, kpe_ref, kvg_ref, gn_ref, gp_ref, wuk_ref, wuv_ref, cos_ref, sin_ref = refs[:9]
    k_ref, v_ref, *state_refs = refs[9 + n_prev:]
    ckv = ckv_ref[...]
    if norm_ckv:
        ms = jnp.mean(ckv * ckv, axis=-1, keepdims=True)
        ckv = ckv * lax.rsqrt(ms + EPS) * kvg_ref[...]
    if state_refs:
        ckv_state_ref, kpe_state_ref = state_refs

        @pl.when(pl.program_id(0) < n_ctx_tiles)
        def _():
            for n in range(ckv.shape[0] // s_len):
                ckv_state_ref[n] = ckv[n * s_len:(n + 1) * s_len]
                kpe_state_ref[n] = kpe_ref[n * s_len:(n + 1) * s_len, :MLA_ROPE]

    cb = ckv.astype(bf16)
    kn = _dot(cb, wuk_ref[...])
    v_ref[...] = _dot(cb, wuv_ref[...]).astype(bf16)
    tm = ckv.shape[0]
    lane = lax.broadcasted_iota(jnp.int32, (tm, LANES), 1)
    low = lane < MLA_ROPE
    kpe = jnp.where(low, kpe_ref[...], 0.0)
    ss_pe = jnp.sum(kpe * kpe, axis=-1, keepdims=True)
    pe = kpe * gp_ref[...]
    if rope:
        pe = _rope(pe, cos_ref[...], sin_ref[...])
    gn = gn_ref[...]
    for h in range(MLA_HEADS):
        nope = kn[:, h * MLA_NOPE:(h + 1) * MLA_NOPE]
        ss = jnp.sum(nope * nope, axis=-1, keepdims=True) + ss_pe
        r = lax.rsqrt(ss * (1.0 / MLA_QK) + EPS)
        k_ref[:, h * MLA_QPAD:h * MLA_QPAD + MLA_NOPE] = (nope * r * gn).astype(bf16)
        k_ref[:, h * MLA_QPAD + MLA_NOPE:(h + 1) * MLA_QPAD] = jnp.where(low, pe * r, 0.0).astype(bf16)


def _rope_tables(t_len, tm):
    n_freq = MLA_ROPE // 4
    t = jnp.arange(t_len, dtype=jnp.int32)
    inv = ROPE_BASE ** (-jnp.arange(n_freq, dtype=f32) / n_freq)
    ang_r = (t // GRID_W).astype(f32)[:, None] * inv
    ang_c = (t % GRID_W).astype(f32)[:, None] * inv
    cos64 = jnp.concatenate([jnp.cos(ang_r), jnp.cos(ang_r), jnp.cos(ang_c), jnp.cos(ang_c)], axis=1)
    sin64 = jnp.concatenate([-jnp.sin(ang_r), jnp.sin(ang_r), -jnp.sin(ang_c), jnp.sin(ang_c)], axis=1)
    cos = jnp.concatenate([jnp.ones((tm, LANES), f32), jnp.tile(cos64, (1, 2))], axis=0)
    sin = jnp.concatenate([jnp.zeros((tm, LANES), f32), jnp.tile(sin64, (1, 2))], axis=0)
    return cos, sin


def _mla_q(z, lay, q_g, cos, sin, rope_block, tm):
    m = z.shape[0]
    gn = q_g[:MLA_NOPE].reshape(1, MLA_NOPE)
    gp = jnp.tile(q_g[MLA_NOPE:], 2).reshape(1, LANES)
    wq = MLA_HEADS * MLA_QK
    assert MLA_HEADS % 2 == 0 and 2 * MLA_QK == 3 * LANES and 2 * MLA_ROPE == LANES and lay.mq % wq == 0
    return pl.pallas_call(
        functools.partial(_mla_q_kernel, scale=MLA_QK ** -0.5 * LOG2E),
        grid=(m // tm,),
        in_specs=[
            pl.BlockSpec((tm, wq), lambda i: (i, lay.mq // wq)),
            pl.BlockSpec((1, MLA_NOPE), lambda i: (0, 0)),
            pl.BlockSpec((1, LANES), lambda i: (0, 0)),
            pl.BlockSpec((tm, LANES), lambda i: (rope_block(i), 0)),
            pl.BlockSpec((tm, LANES), lambda i: (rope_block(i), 0)),
        ],
        out_specs=pl.BlockSpec((tm, MLA_HEADS * MLA_QPAD), lambda i: (i, 0)),
        out_shape=jax.ShapeDtypeStruct((m, MLA_HEADS * MLA_QPAD), bf16),
        compiler_params=_params(("arbitrary",)),
        name="mla_q",
    )(z, gn, gp, cos, sin)


def _mla_kv(ckv_src, ckv_col, kpe_src, kpe_col, kv_g, k_g, w_uk, w_uv, layer, cos, sin, rope_block, tm, norm_ckv, rope,
            name, state=None):
    m = ckv_src.shape[0]
    gn = k_g[:MLA_NOPE].reshape(1, MLA_NOPE)
    gp = jnp.concatenate([k_g[MLA_NOPE:], jnp.zeros((LANES - MLA_ROPE,), f32)]).reshape(1, LANES)
    hw = MLA_HEADS * MLA_NOPE
    in_specs = [
        pl.BlockSpec((tm, MLA_KV_RANK), lambda i: (i, ckv_col // MLA_KV_RANK)),
        pl.BlockSpec((tm, LANES), lambda i: (i, kpe_col // LANES)),
        pl.BlockSpec((1, MLA_KV_RANK), lambda i: (0, 0)),
        pl.BlockSpec((1, MLA_NOPE), lambda i: (0, 0)),
        pl.BlockSpec((1, LANES), lambda i: (0, 0)),
        pl.BlockSpec((None, MLA_KV_RANK, hw), lambda i: (layer, 0, 0)),
        pl.BlockSpec((None, MLA_KV_RANK, hw), lambda i: (layer, 0, 0)),
        pl.BlockSpec((tm, LANES), lambda i: (rope_block(i), 0)),
        pl.BlockSpec((tm, LANES), lambda i: (rope_block(i), 0)),
    ]
    args = [ckv_src, kpe_src, kv_g.reshape(1, MLA_KV_RANK), gn, gp, w_uk, w_uv, cos, sin]
    out_specs = [pl.BlockSpec((tm, MLA_HEADS * MLA_QPAD), lambda i: (i, 0)), pl.BlockSpec((tm, hw), lambda i: (i, 0))]
    out_shape = [jax.ShapeDtypeStruct((m, MLA_HEADS * MLA_QPAD), bf16), jax.ShapeDtypeStruct((m, hw), bf16)]
    kwargs, aliases = {}, {}
    if state is not None:
        aliases = state.aliases(len(args), len(out_specs))
        kwargs = state.kernel_kwargs()
        in_specs += state.in_specs
        args += state.prevs
        out_specs += state.out_specs
        out_shape += state.out_shape
    return pl.pallas_call(
        functools.partial(_mla_kv_kernel, norm_ckv=norm_ckv, rope=rope, **kwargs),
        grid=(m // tm,),
        in_specs=in_specs,
        out_specs=out_specs,
        out_shape=out_shape,
        input_output_aliases=aliases,
        compiler_params=_params(("arbitrary",)),
        name=name,
    )(*args)


def _flash_kernel(*refs, n_main, tk, n_extra, heads, dq, dv):
    if n_extra:
        q_ref, k_ref, v_ref, kx_ref, vx_ref, o_ref = refs
    else:
        q_ref, k_ref, v_ref, o_ref = refs
        kx_ref = vx_ref = None
    tq = q_ref.shape[0]
    chunks = [(k_ref, v_ref, c) for c in range(n_main)] + [(kx_ref, vx_ref, c) for c in range(n_extra)]
    def scores(i, g):
        kr, _, c = chunks[i]
        return _dot_nt(kr[c * tk:(c + 1) * tk, g * dq:(g + 1) * dq], q_ref[:, g * dq:(g + 1) * dq])

    m = [jnp.full((1, tq), -jnp.inf, f32)] * heads
    l = [jnp.zeros((1, tq), f32)] * heads
    acc = [jnp.zeros((dv, tq), f32)] * heads
    s = [scores(0, g) for g in range(heads)]
    for i, (_, vr, c) in enumerate(chunks):
        for g in range(heads):
            s_next = scores(i + 1, g) if i + 1 < len(chunks) else None
            m_new = jnp.maximum(m[g], jnp.max(s[g], axis=0, keepdims=True))
            alpha = jnp.exp2(m[g] - m_new)
            p = jnp.exp2(s[g] - m_new)
            l[g] = alpha * l[g] + jnp.sum(p, axis=0, keepdims=True)
            acc[g] = alpha * acc[g] + _dot_tn(vr[c * tk:(c + 1) * tk, g * dv:(g + 1) * dv], p.astype(bf16))
            m[g], s[g] = m_new, s_next
    for g in range(heads):
        o_ref[:, g * dv:(g + 1) * dv] = (acc[g] / l[g]).T.astype(o_ref.dtype)


def _flash(q, k, v, *, batch, heads, t_len, s_len, row0, dq, dv, tq, tk, heads_per_step=1, extra=None, name):
    assert row0 % tq == 0 and row0 % s_len == 0 and t_len % tq == 0 and s_len % tk == 0
    assert heads % heads_per_step == 0
    nq = t_len // tq
    g = heads_per_step
    in_specs = [
        pl.BlockSpec((tq, g * dq), lambda b, h, i: (row0 // tq + b * nq + i, h)),
        pl.BlockSpec((s_len, g * dq), lambda b, h, i: (row0 // s_len + b, h)),
        pl.BlockSpec((s_len, g * dv), lambda b, h, i: (row0 // s_len + b, h)),
    ]
    args = [q, k, v]
    n_extra = 0
    if extra is not None:
        kx, vx, p_len = extra
        assert p_len % tk == 0
        n_extra = p_len // tk
        in_specs += [
            pl.BlockSpec((p_len, g * dq), lambda b, h, i: (b, h)),
            pl.BlockSpec((p_len, g * dv), lambda b, h, i: (b, h)),
        ]
        args += [kx, vx]
    return pl.pallas_call(
        functools.partial(_flash_kernel, n_main=s_len // tk, tk=tk, n_extra=n_extra, heads=g, dq=dq, dv=dv),
        grid=(batch, heads // g, nq),
        in_specs=in_specs,
        out_specs=pl.BlockSpec((tq, g * dv), lambda b, h, i: (b * nq + i, h)),
        out_shape=jax.ShapeDtypeStruct((batch * t_len, heads * dv), bf16),
        compiler_params=_params(("arbitrary", "arbitrary", "arbitrary")),
        name=name,
    )(*args)


def _na_prep_kernel(*refs, scale, n_prev, n_ctx_tiles, s_len):
    nq_ref, nk_ref, nv_ref, qg_ref, kg_ref = refs[:5]
    q_ref, k_ref, v_ref, k_state_ref, v_state_ref = refs[5 + n_prev:]
    qg = qg_ref[...]
    kg = kg_ref[...]
    tm = nv_ref.shape[0]
    is_ctx = pl.program_id(0) < n_ctx_tiles
    v_ref[...] = nv_ref[...].astype(bf16)

    @pl.when(is_ctx)
    def _():
        for n in range(tm // s_len):
            v_state_ref[n] = nv_ref[n * s_len:(n + 1) * s_len, :]

    for h in range(NA_HEADS):
        sl = slice(h * NA_DH, (h + 1) * NA_DH)
        q = nq_ref[:, sl]
        k = nk_ref[:, sl]
        rq = lax.rsqrt(jnp.mean(q * q, axis=-1, keepdims=True) + EPS)
        rk = lax.rsqrt(jnp.mean(k * k, axis=-1, keepdims=True) + EPS)
        q_ref[:, sl] = (q * rq * qg * scale).astype(bf16)
        kn = k * rk * kg
        k_ref[:, sl] = kn.astype(bf16)

        @pl.when(is_ctx)
        def _():
            for n in range(tm // s_len):
                k_state_ref[n, :, sl] = kn[n * s_len:(n + 1) * s_len]


def _na_prep(z, lay, q_g, k_g, tm, state):
    m = z.shape[0]
    w = NA_HEADS * NA_DH
    spec = pl.BlockSpec((tm, w), lambda i: (i, 0))
    in_specs = [
        pl.BlockSpec((tm, w), lambda i: (i, lay.nq // w)),
        pl.BlockSpec((tm, w), lambda i: (i, lay.nk // w)),
        pl.BlockSpec((tm, w), lambda i: (i, lay.nv // w)),
        pl.BlockSpec((1, NA_DH), lambda i: (0, 0)),
        pl.BlockSpec((1, NA_DH), lambda i: (0, 0)),
    ]
    args = [z, z, z, q_g.reshape(1, NA_DH), k_g.reshape(1, NA_DH)]
    return pl.pallas_call(
        functools.partial(_na_prep_kernel, scale=NA_DH ** -0.5 * LOG2E, **state.kernel_kwargs()),
        grid=(m // tm,),
        in_specs=in_specs + state.in_specs,
        out_specs=[spec, spec, spec] + state.out_specs,
        out_shape=[jax.ShapeDtypeStruct((m, w), bf16)] * 3 + state.out_shape,
        input_output_aliases=state.aliases(len(args), 3),
        compiler_params=_params(("arbitrary",)),
        name="na_prep",
    )(*args, *state.prevs)


def _na_static_maps(rows):
    nblk = rows // NA_QROWS
    reps = (0, min(1, nblk - 1), nblk - 1)
    dr_map = np.full((3, NA_QROWS, NA_KROWS), 2 * NA_WIN_R - 1, np.int32)
    for v, kb in enumerate(reps):
        ws = min(max(NA_QROWS * kb - NA_WIN_R // 2, 0), rows - NA_KROWS)
        for i in range(NA_QROWS):
            qrow = NA_QROWS * kb + i
            rs = min(max(qrow - NA_WIN_R // 2, 0), rows - NA_WIN_R)
            for j in range(NA_KROWS):
                krow = ws + j
                if rs <= krow < rs + NA_WIN_R:
                    dr_map[v, i, j] = krow - qrow + NA_WIN_R - 1
    qcol = np.arange(GRID_W)
    cs = np.clip(qcol - NA_WIN_C // 2, 0, GRID_W - NA_WIN_C)
    in_win = (qcol[None, :] >= cs[:, None]) & (qcol[None, :] < cs[:, None] + NA_WIN_C)
    dc_idx = np.clip(qcol[None, :] - qcol[:, None], -(NA_WIN_C - 1), NA_WIN_C - 1) + (NA_WIN_C - 1)
    return dr_map, in_win, dc_idx


def _na_bias_table(rpb, rows):
    dr_map, in_win, dc_idx = _na_static_maps(rows)
    h = rpb.shape[0]
    bt = jnp.where(in_win.T[None, None], jnp.take(rpb * LOG2E, jnp.asarray(dc_idx.T), axis=2), NEG)
    bt = jnp.concatenate([bt, jnp.full((h, 1, GRID_W, GRID_W), NEG, f32)], axis=1)
    tab = jnp.take(bt, jnp.asarray(dr_map.transpose(0, 2, 1).reshape(-1)), axis=1)
    tab = tab.reshape(h, 3, NA_KROWS, NA_QROWS, GRID_W, GRID_W).transpose(1, 0, 2, 4, 3, 5)
    return tab.reshape(3, h, NA_KROWS * GRID_W, NA_QROWS * GRID_W)


def _na_kernel(q_ref, k_ref, v_ref, kc_ref, vc_ref, bias_ref, o_ref, *, rows, heads):
    kb = pl.program_id(2)
    ws = jnp.clip(NA_QROWS * kb - NA_WIN_R // 2, 0, rows - NA_KROWS)
    win = pl.ds(pl.multiple_of(ws * GRID_W, (NA_WIN_R // 2) * GRID_W), NA_KROWS * GRID_W)
    cols = [slice(g * NA_DH, (g + 1) * NA_DH) for g in range(heads)]
    s_loc = [_dot_nt(k_ref[win, c], q_ref[:, c]) + bias_ref[g] for g, c in enumerate(cols)]
    s_ctx = [_dot_nt(kc_ref[:, c], q_ref[:, c]) for c in cols]
    p_loc, p_ctx, l = [], [], []
    for g in range(heads):
        m = jnp.maximum(jnp.max(s_loc[g], axis=0, keepdims=True), jnp.max(s_ctx[g], axis=0, keepdims=True))
        pl_g = jnp.exp2(s_loc[g] - m)
        pc_g = jnp.exp2(s_ctx[g] - m)
        l.append(jnp.sum(pl_g, axis=0, keepdims=True) + jnp.sum(pc_g, axis=0, keepdims=True))
        p_loc.append(pl_g.astype(bf16))
        p_ctx.append(pc_g.astype(bf16))
    for g, c in enumerate(cols):
        acc = _dot_tn(v_ref[win, c], p_loc[g]) + _dot_tn(vc_ref[:, c], p_ctx[g])
        o_ref[:, c] = (acc / l[g]).T.astype(o_ref.dtype)


def _na_latent(q, k, v, kc, vc, bias, *, batch, t_len, p_len, row0, heads_per_step=2):
    rows = t_len // GRID_W
    assert t_len % GRID_W == 0 and rows % NA_QROWS == 0 and rows >= NA_KROWS
    tq = NA_QROWS * GRID_W
    assert row0 % tq == 0 and row0 % t_len == 0 and NA_HEADS % heads_per_step == 0
    nblk = rows // NA_QROWS
    g = heads_per_step
    gw = g * NA_DH

    def variant(i):
        return jnp.where(i == 0, 0, jnp.where(i == nblk - 1, 2, 1))

    return pl.pallas_call(
        functools.partial(_na_kernel, rows=rows, heads=g),
        grid=(batch, NA_HEADS // g, nblk),
        in_specs=[
            pl.BlockSpec((tq, gw), lambda b, h, i: (row0 // tq + b * nblk + i, h)),
            pl.BlockSpec((t_len, gw), lambda b, h, i: (row0 // t_len + b, h)),
            pl.BlockSpec((t_len, gw), lambda b, h, i: (row0 // t_len + b, h)),
            pl.BlockSpec((p_len, gw), lambda b, h, i: (b, h)),
            pl.BlockSpec((p_len, gw), lambda b, h, i: (b, h)),
            pl.BlockSpec((None, g, NA_KROWS * GRID_W, tq), lambda b, h, i: (variant(i), h, 0, 0)),
        ],
        out_specs=pl.BlockSpec((tq, gw), lambda b, h, i: (b * nblk + i, h)),
        out_shape=jax.ShapeDtypeStruct((batch * t_len, NA_HEADS * NA_DH), bf16),
        compiler_params=_params(("arbitrary", "arbitrary", "arbitrary")),
        name="na_latent",
    )(q, k, v, kc, vc, bias)


def _hgrn_tables():
    c, lv = HG_CHUNK, HG_LEVELS
    t = np.arange(c)[:, None]
    r = np.arange(c)[None, :]
    wcum = np.stack([r <= t, r >= t]).astype(np.float32)
    qside = np.zeros((2, lv * c, LANES), np.float32)
    mask = np.zeros((2, lv + 1, c, c), np.float32)
    mask[:, 0] = np.eye(c)
    for l in range(lv):
        hs = 1 << l
        blk = t // (2 * hs)
        mid = blk * (2 * hs) + hs
        qf = t >= mid
        qb = t < mid
        qside[0, l * c:(l + 1) * c] = np.where(qf, 1.0, -1.0)
        qside[1, l * c:(l + 1) * c] = np.where(qb, 1.0, -1.0)
        same = blk == blk.T
        mask[0, 1 + l] = same & qf & ~qf.T
        mask[1, 1 + l] = same & qb & ~qb.T
    return wcum, qside, mask


def _hgrn_kernel(*refs, n_chunks, heads, has_state, emit_state, n_prev=0):
    (hq_ref, hff_ref, hfb_ref, hi_ref, hg_ref, la_ref, lc_ref, om_ref, ng_ref,
     wcum_ref, qside_ref, mask_ref) = refs[:12]
    rest = list(refs[12:])
    s0_ref = rest.pop(0) if has_state else None
    del rest[:n_prev]
    o_ref = rest.pop(0)
    st_ref = rest.pop(0) if emit_state else None
    o_scr, s_scr, b_scr = rest
    c, lv = HG_CHUNK, HG_LEVELS

    o_scr[...] = jnp.zeros_like(o_scr)
    for d in range(2):
        for g in range(heads):
            if has_state:
                s_scr[d, g] = s0_ref[d, g].T
            else:
                s_scr[d, g] = jnp.zeros((HG_V, HG_K), f32)

    chains = [(d, g) for g in range(heads) for d in range(2)]
    low_half = lax.broadcasted_iota(jnp.int32, (SUBLANES, HEAD_W), 0) < SUBLANES // 2

    def gates_and_cumsum(ci, row):
        d, g = chains[ci]
        rs = pl.ds(pl.multiple_of(row, c), c)
        cs = slice(g * HEAD_W, (g + 1) * HEAD_W)
        x = (hff_ref if d == 0 else hfb_ref)[rs, cs]
        x2 = x * LOG2E
        u = jnp.exp2(-jnp.abs(x2))
        inv = 1.0 / (1.0 + u)
        pos = x >= 0.0
        log_sig = jnp.minimum(x2, 0.0) - jnp.log2(1.0 + u)
        a = la_ref[d:d + 1, cs]
        bb = lc_ref[d:d + 1, cs] + log_sig
        lf = jnp.maximum(a, bb) + jnp.log2(1.0 + jnp.exp2(-jnp.abs(a - bb)))
        om = om_ref[d:d + 1, cs]
        f = (1.0 - om) + om * (jnp.where(pos, 1.0, u) * inv)
        kin = om * (jnp.where(pos, u, 1.0) * inv)
        q = _silu(hq_ref[rs, cs])
        v = hi_ref[rs, cs].astype(bf16)
        l1 = lf.astype(bf16)
        r1 = lf - l1.astype(f32)
        l2 = r1.astype(bf16)
        l3 = (r1 - l2.astype(f32)).astype(bf16)
        w = wcum_ref[d]
        b = _dot(w, l1) + _dot(w, l2) + _dot(w, l3)
        b_scr[ci] = b
        return dict(rs=rs, cs=cs, f=f, kin=kin, q=q, v=v, b=b)

    def intra_chunk(ci, s):
        d, _ = chains[ci]
        b, q, kin = s["b"], s["q"], s["kin"]

        def row(r):
            return b_scr[ci, r:r + 1, :]

        s["b_end"] = row(c - 1 if d == 0 else 0)
        xs = [jnp.where(qside_ref[d, 0:c, :] > 0.0, q * s["f"], kin).astype(bf16)]
        for l in range(1, lv):
            hs = 1 << l
            blk = 2 * hs
            pick = hs - 1 if d == 0 else hs
            if hs >= SUBLANES:
                pieces = []
                for j in range(c // blk):
                    m = row(blk * j + pick)
                    early = slice(blk * j, blk * j + hs)
                    late = slice(blk * j + hs, blk * (j + 1))
                    q_sl, k_sl = (late, early) if d == 0 else (early, late)
                    xq = q[q_sl] * jnp.exp2(b[q_sl] - m)
                    xk = kin[k_sl] * jnp.exp2(m - b[k_sl])
                    pieces += [xk, xq] if d == 0 else [xq, xk]
                xs.append(jnp.concatenate(pieces, axis=0).astype(bf16))
                continue
            if blk < SUBLANES:
                pieces = [jnp.where(low_half, jnp.broadcast_to(row(SUBLANES * j + pick), (SUBLANES, HEAD_W)),
                                    jnp.broadcast_to(row(SUBLANES * j + blk + pick), (SUBLANES, HEAD_W)))
                          for j in range(c // SUBLANES)]
            else:
                pieces = [jnp.broadcast_to(row(blk * j + pick), (blk, HEAD_W)) for j in range(c // blk)]
            m = pieces[0] if len(pieces) == 1 else jnp.concatenate(pieces, axis=0)
            sign = qside_ref[d, l * c:(l + 1) * c, :]
            xs.append((jnp.where(sign > 0.0, q, kin) * jnp.exp2((b - m) * sign)).astype(bf16))
        att = mask_ref[d, 0] * _dot_nt(q.astype(bf16), kin.astype(bf16)).astype(bf16)
        for l in range(lv):
            att = att + mask_ref[d, 1 + l] * _dot_nt(xs[l], xs[l]).astype(bf16)
        s["att"] = att

    def state_step(ci, s):
        d, g = chains[ci]
        b, q, kin, v = s["b"], s["q"], s["kin"], s["v"]
        st = s_scr[d, g]
        o = _dot(s["att"], v) + _dot_nt((q * jnp.exp2(b)).astype(bf16), st.astype(bf16))
        b_end = s["b_end"]
        s_scr[d, g] = st * jnp.exp2(b_end) + _dot_tn(v, (kin * jnp.exp2(b_end - b)).astype(bf16))
        o_scr[s["rs"], s["cs"]] += o

    def body(i, _):
        rows = (i * c, (n_chunks - 1 - i) * c)
        states = [gates_and_cumsum(ci, rows[chains[ci][0]]) for ci in range(len(chains))]
        for ci, s in enumerate(states):
            intra_chunk(ci, s)
        for ci, s in enumerate(states):
            state_step(ci, s)
        return 0

    lax.fori_loop(0, n_chunks, body, 0)

    ng = ng_ref[...]
    for g in range(heads):
        cs = slice(g * HEAD_W, (g + 1) * HEAD_W)
        o = o_scr[:, cs]
        y = o * lax.rsqrt(jnp.mean(o * o, axis=-1, keepdims=True) + EPS) * ng
        o_ref[:, cs] = (y * _silu(hg_ref[:, cs])).astype(o_ref.dtype)
        if emit_state:
            for d in range(2):
                st_ref[d, g] = s_scr[d, g].T


def _hgrn(z, lay, lb_tabs, norm_g, tables, *, batch, t_len, row0, s0, state_out=None, heads_per_step=2):
    emit_state = state_out is not None
    assert t_len % HG_CHUNK == 0 and row0 % t_len == 0 and HG_K == HEAD_W and HG_V == HEAD_W
    g = heads_per_step
    gw = g * HEAD_W
    la, lc, om = lb_tabs
    wcum, qside, mask = tables
    r0 = row0 // t_len

    def zspec(col):
        return pl.BlockSpec((t_len, gw), lambda b, h: (r0 + b, col // gw + h))

    def const(shape):
        return pl.BlockSpec(shape, lambda b, h: (0,) * len(shape))

    in_specs = [zspec(lay.hq), zspec(lay.hff), zspec(lay.hfb), zspec(lay.hi), zspec(lay.hg),
                pl.BlockSpec((2, gw), lambda b, h: (0, h)), pl.BlockSpec((2, gw), lambda b, h: (0, h)),
                pl.BlockSpec((2, gw), lambda b, h: (0, h)), const((1, HG_V)),
                const(wcum.shape), const(qside.shape), const(mask.shape)]
    args = [z, z, z, z, z, la, lc, om, norm_g.reshape(1, HG_V), wcum, qside, mask]
    state_spec = pl.BlockSpec((None, 2, g, HG_K, HG_V), lambda b, h: (b, 0, h, 0, 0))
    if s0 is not None:
        in_specs.append(state_spec)
        args.append(s0)
    out_specs = [pl.BlockSpec((t_len, gw), lambda b, h: (b, h))]
    out_shape = [jax.ShapeDtypeStruct((batch * t_len, HG_HEADS * HG_V), bf16)]
    aliases, n_prev = {}, 0
    if emit_state:
        prev, layer, depth = state_out
        aliases, n_prev = {len(args): 1}, 1
        in_specs.append(pl.BlockSpec(memory_space=pl.ANY))
        args.append(prev)
        out_specs.append(pl.BlockSpec((None, None, 2, g, HG_K, HG_V), lambda b, h: (b, layer, 0, h, 0, 0)))
        out_shape.append(jax.ShapeDtypeStruct((batch, depth, 2, HG_HEADS, HG_K, HG_V), f32))
    res = pl.pallas_call(
        functools.partial(_hgrn_kernel, n_chunks=t_len // HG_CHUNK, heads=g, has_state=s0 is not None,
                          emit_state=emit_state, n_prev=n_prev),
        grid=(batch, HG_HEADS // g),
        in_specs=in_specs,
        out_specs=out_specs,
        out_shape=out_shape,
        input_output_aliases=aliases,
        scratch_shapes=[pltpu.VMEM((t_len, gw), f32), pltpu.VMEM((2, g, HG_V, HG_K), f32),
                        pltpu.VMEM((2 * g, HG_CHUNK, HEAD_W), f32)],
        compiler_params=_params(("arbitrary", "arbitrary")),
        name="hgrn_ctx" if emit_state else "hgrn_latent",
    )(*args)
    return res if emit_state else (res[0], None)


def _merge_kernel(oac_ref, oal_ref, obc_ref, obl_ref, occ_ref, ocl_ref, ga_ref, gb_ref, gc_ref, w_ref, o_ref,
                  *, n_ctx_tiles):
    def compute(oa_ref, ob_ref, oc_ref):
        m = jax.nn.sigmoid(ga_ref[...]) * _dot(oa_ref[...], w_ref[0])
        m = m + jax.nn.sigmoid(gb_ref[...]) * _dot(ob_ref[...], w_ref[1])
        m = m + jax.nn.sigmoid(gc_ref[...]) * _dot(oc_ref[...], w_ref[2])
        o_ref[...] = m.astype(o_ref.dtype)

    is_ctx = pl.program_id(1) < n_ctx_tiles

    @pl.when(is_ctx)
    def _():
        compute(oac_ref, obc_ref, occ_ref)

    @pl.when(jnp.logical_not(is_ctx))
    def _():
        compute(oal_ref, obl_ref, ocl_ref)


def _merge(o_a, o_b, o_c, z, lay, w_branch, layer, tm, tn):
    bw = o_a[0].shape[1]
    m = z.shape[0]
    d = w_branch.shape[3]
    nct = o_a[0].shape[0] // tm
    cspec = pl.BlockSpec((tm, bw), lambda j, i: (jnp.minimum(i, nct - 1), 0))
    lspec = pl.BlockSpec((tm, bw), lambda j, i: (jnp.maximum(i - nct, 0), 0))

    def gate(col):
        return pl.BlockSpec((tm, tn), lambda j, i: (i, col // tn + j))

    return pl.pallas_call(
        functools.partial(_merge_kernel, n_ctx_tiles=nct),
        grid=(d // tn, m // tm),
        in_specs=[cspec, lspec, cspec, lspec, cspec, lspec, gate(lay.ga), gate(lay.gb), gate(lay.gc),
                  pl.BlockSpec((None, 3, bw, tn), lambda j, i: (layer, 0, 0, j))],
        out_specs=pl.BlockSpec((tm, tn), lambda j, i: (i, j)),
        out_shape=jax.ShapeDtypeStruct((m, d), bf16),
        compiler_params=_params(("arbitrary", "arbitrary")),
        name="merge",
    )(o_a[0], o_a[1], o_b[0], o_b[1], o_c[0], o_c[1], z, z, z, w_branch)


def _proj_residual_kernel(m_ref, w_ref, x_ref, g_ref, o_ref):
    o_ref[...] = x_ref[...] + g_ref[...] * _dot(m_ref[...], w_ref[...])


def _proj_residual(mm, w, layer, x, gate, group_of_tile, tm, tn):
    m, k = mm.shape
    d = w.shape[2]
    return pl.pallas_call(
        _proj_residual_kernel,
        grid=(m // tm, d // tn),
        in_specs=[
            pl.BlockSpec((tm, k), lambda i, j: (i, 0)),
            pl.BlockSpec((None, k, tn), lambda i, j: (layer, 0, j)),
            pl.BlockSpec((tm, tn), lambda i, j: (i, j)),
            pl.BlockSpec((None, 1, tn), lambda i, j: (group_of_tile(i), 0, j)),
        ],
        out_specs=pl.BlockSpec((tm, tn), lambda i, j: (i, j)),
        out_shape=jax.ShapeDtypeStruct((m, d), f32),
        compiler_params=_params(("arbitrary", "arbitrary")),
        name="out_proj",
    )(mm, w, x, gate)


def _ffn_kernel(x_ref, xprev_ref, xnext_ref, ng_ref, sc_ref, sh_ref, wa_ref, wg_ref, cw_ref, cb_ref, wd_ref, gate_ref,
                o_ref, h_scr, *, m_ctx, s_len, t_len, n_k):
    i = pl.program_id(0)
    k = pl.program_id(1)
    tm = x_ref.shape[0]
    tk = wa_ref.shape[1]
    halo = SUBLANES

    @pl.when(k == 0)
    def _():
        def normmod(x):
            ms = jnp.mean(x * x, axis=-1, keepdims=True)
            y = x * lax.rsqrt(ms + EPS) * ng_ref[...]
            return (y * (1.0 + sc_ref[...]) + sh_ref[...]).astype(bf16)

        h_scr[0:halo] = normmod(xprev_ref[...])
        h_scr[halo:halo + tm] = normmod(x_ref[...])
        h_scr[halo + tm:halo + tm + halo] = normmod(xnext_ref[...])
        o_ref[...] = jnp.zeros_like(o_ref)

    row = i * tm + lax.broadcasted_iota(jnp.int32, (tm, LANES), 0)
    pos = jnp.where(row < m_ctx, _mod_const(row, s_len), _mod_const(row - m_ctx, t_len))
    last = jnp.where(row < m_ctx, s_len - 1, t_len - 1)
    keep_prev = jnp.tile(jnp.where(pos == 0, 0.0, 1.0), (1, tk // LANES))
    keep_next = jnp.tile(jnp.where(pos == last, 0.0, 1.0), (1, tk // LANES))

    g_ext = _dot(h_scr[...], wg_ref[...])
    a = _dot(h_scr[halo:halo + tm], wa_ref[...])
    n_ext = tm + 2 * halo
    g = g_ext[halo:halo + tm]
    g_prev = pltpu.roll(g_ext, 1, 0)[halo:halo + tm]
    g_next = pltpu.roll(g_ext, n_ext - 1, 0)[halo:halo + tm]
    conv = (g_prev * keep_prev * cw_ref[0:1, :] + g * cw_ref[1:2, :] + g_next * keep_next * cw_ref[2:3, :]
            + cb_ref[...])
    u = (_silu(conv) * a).astype(bf16)
    o_ref[...] = _dot(u, wd_ref[...]) + o_ref[...]

    @pl.when(k == n_k - 1)
    def _():
        o_ref[...] = x_ref[...] + gate_ref[...] * o_ref[...]


def _ffn(x, norm_g, sc, sh, w_up, conv_w, conv_b, w_down, layer, gate, group_of_tile, tm, tk, m_ctx, s_len, t_len):
    m, d = x.shape
    f = w_down.shape[1]
    nk = f // tk
    nsub = tm // SUBLANES
    last_sub = m // SUBLANES - 1
    return pl.pallas_call(
        functools.partial(_ffn_kernel, m_ctx=m_ctx, s_len=s_len, t_len=t_len, n_k=nk),
        grid=(m // tm, nk),
        in_specs=[
            pl.BlockSpec((tm, d), lambda i, k: (i, 0), pipeline_mode=pl.Buffered(1)),
            pl.BlockSpec((SUBLANES, d), lambda i, k: (jnp.maximum(i * nsub - 1, 0), 0)),
            pl.BlockSpec((SUBLANES, d), lambda i, k: (jnp.minimum((i + 1) * nsub, last_sub), 0)),
            pl.BlockSpec((1, d), lambda i, k: (0, 0)),
            pl.BlockSpec((None, 1, d), lambda i, k: (group_of_tile(i), 0, 0)),
            pl.BlockSpec((None, 1, d), lambda i, k: (group_of_tile(i), 0, 0)),
            pl.BlockSpec((None, d, tk), lambda i, k: (layer, 0, k)),
            pl.BlockSpec((None, d, tk), lambda i, k: (layer, 0, nk + k)),
            pl.BlockSpec((3, tk), lambda i, k: (0, k)),
            pl.BlockSpec((1, tk), lambda i, k: (0, k)),
            pl.BlockSpec((None, tk, d), lambda i, k: (layer, k, 0)),
            pl.BlockSpec((None, 1, d), lambda i, k: (group_of_tile(i), 0, 0)),
        ],
        out_specs=pl.BlockSpec((tm, d), lambda i, k: (i, 0), pipeline_mode=pl.Buffered(1)),
        out_shape=jax.ShapeDtypeStruct((m, d), f32),
        scratch_shapes=[pltpu.VMEM((tm + 2 * SUBLANES, d), bf16)],
        compiler_params=_params(("arbitrary", "arbitrary")),
        name="ffn",
    )(x, x, x, norm_g.reshape(1, d), sc, sh, w_up, w_up, conv_w, conv_b.reshape(1, f), w_down, gate)


def kernel(x_prompt, x_sample, cache_mla_ckv, cache_mla_kpe, cache_na_k, cache_na_v, state_hgrn, c, c_ctx, w_mod, b_mod, norm1_g, norm2_g, w_in, mla_kv_norm_g, mla_q_norm_g, mla_k_norm_g, mla_w_uk, mla_w_uv, hgrn_lower_bounds, hgrn_norm_g, na_q_norm_g, na_k_norm_g, na_rpb, w_branch, w_out, ffn_w_up, ffn_conv_w, ffn_conv_b, ffn_w_down):
    n_ctx, s_len, d = x_prompt.shape
    n_lat, t_len, _ = x_sample.shape
    depth = w_in.shape[0]
    p_len = cache_mla_ckv.shape[2]
    ffn = ffn_w_down.shape[1]
    m_ctx, m_lat = n_ctx * s_len, n_lat * t_len
    m = m_ctx + m_lat
    assert m_ctx % t_len == 0, "context rows must be a whole number of latent sequences"

    tm = _pow2_tile(1024, m_ctx, t_len)
    tn = _pow2_tile(1024, d)
    tp = _pow2_tile(512, m_ctx, t_len)
    lay = _ZLayout(d, tn)

    def group_fn(rows):
        return lambda i: jnp.where(i < m_ctx // rows, 0, 1 + (i - m_ctx // rows) // (t_len // rows))

    group_of_tile = group_fn(tm)

    n_groups = -(-(1 + n_lat) // SUBLANES) * SUBLANES
    cond = jnp.concatenate([c_ctx[None], c, jnp.zeros((n_groups - 1 - n_lat, d), f32)], axis=0)
    mods = _modulation(cond, w_mod, b_mod).reshape(depth, n_groups, 6, d).transpose(0, 2, 1, 3)[:, :, :, None, :]

    sm = jax.nn.softmax(hgrn_lower_bounds.astype(f32), axis=1)
    csum = jnp.cumsum(sm, axis=1)
    lower = csum - csum[:, :1]
    hg_tabs = tuple(jnp.asarray(t, dt) for t, dt in zip(_hgrn_tables(), (bf16, f32, bf16)))

    cos, sin = _rope_tables(t_len, tp)
    n_ctx_tp = m_ctx // tp

    def rope_block(i):
        return jnp.where(i < n_ctx_tp, 0, 1 + (i - n_ctx_tp) % (t_len // tp))

    x = jnp.concatenate([x_prompt.reshape(m_ctx, d), x_sample.reshape(m_lat, d)], axis=0)
    w_in_b, w_kpe_b = _permute_w_in(w_in, lay)
    w_uk = mla_w_uk.reshape(depth, MLA_KV_RANK, MLA_HEADS * MLA_NOPE).astype(bf16)
    w_uv = mla_w_uv.reshape(depth, MLA_KV_RANK, MLA_HEADS * MLA_V).astype(bf16)
    w_branch_b, w_out_b = w_branch.astype(bf16), w_out.astype(bf16)
    w_up_b, w_down_b = ffn_w_up.astype(bf16), ffn_w_down.astype(bf16)

    def state_init(*trailing):
        return jnp.zeros((n_ctx, depth) + trailing, f32)

    st_mla = [state_init(s_len, MLA_KV_RANK), state_init(s_len, MLA_ROPE)]
    st_na = [state_init(s_len, NA_HEADS * NA_DH), state_init(s_len, NA_HEADS * NA_DH)]
    st_hg = state_init(2, HG_HEADS, HG_K, HG_V)
    for l in range(depth):
        sh1, sc1, g1, sh2, sc2, g2 = (mods[l, k] for k in range(6))
        z, kpe = _normmod_matmul(x, norm1_g[l], sc1, sh1, w_in_b, l, group_of_tile, tm, tn, "in_proj", w_side=w_kpe_b)

        q_mla = _mla_q(z, lay, mla_q_norm_g[l], cos, sin, rope_block, tp)
        k_mla, v_mla, *st_mla = _mla_kv(
            z, lay.ckv, kpe, 0, mla_kv_norm_g[l], mla_k_norm_g[l], w_uk, w_uv, l, cos, sin, rope_block, tp, True, True,
            "mla_kv", state=_CtxState(st_mla, (MLA_KV_RANK, MLA_ROPE), l, depth, n_ctx, s_len, tp))
        ckv_c = cache_mla_ckv[:, l].reshape(n_lat * p_len, MLA_KV_RANK)
        kpe_c = jnp.pad(cache_mla_kpe[:, l].reshape(n_lat * p_len, MLA_ROPE), ((0, 0), (0, LANES - MLA_ROPE)))
        tc = _pow2_tile(512, n_lat * p_len)
        k_c, v_c = _mla_kv(ckv_c, 0, kpe_c, 0, mla_kv_norm_g[l], mla_k_norm_g[l], w_uk, w_uv, l,
                           cos, sin, lambda i: 0, tc, False, False, "mla_kv_cache")
        tq_c = _pow2_tile(256, s_len)
        o_mla_c = _flash(q_mla, k_mla, v_mla, batch=n_ctx, heads=MLA_HEADS, t_len=s_len, s_len=s_len, row0=0,
                         dq=MLA_QPAD, dv=MLA_V, tq=tq_c, tk=tq_c, heads_per_step=MLA_HEADS, name="mla_attn_ctx")
        tq_l = _pow2_tile(512, t_len, p_len)
        o_mla_l = _flash(q_mla, k_mla, v_mla, batch=n_lat, heads=MLA_HEADS, t_len=t_len, s_len=t_len, row0=m_ctx,
                         dq=MLA_QPAD, dv=MLA_V, tq=tq_l, tk=tq_l, heads_per_step=2, extra=(k_c, v_c, p_len),
                         name="mla_attn_latent")

        lb = lower[:, l]
        lb_tabs = (jnp.log(lb) * LOG2E, jnp.log1p(-lb) * LOG2E, 1.0 - lb)
        o_hg_c, st_hg = _hgrn(z, lay, lb_tabs, hgrn_norm_g[l], hg_tabs, batch=n_ctx, t_len=s_len, row0=0,
                              s0=None, state_out=(st_hg, l, depth))
        o_hg_l, _ = _hgrn(z, lay, lb_tabs, hgrn_norm_g[l], hg_tabs, batch=n_lat, t_len=t_len, row0=m_ctx,
                          s0=state_hgrn[:, l])

        q_na, k_na, v_na, *st_na = _na_prep(
            z, lay, na_q_norm_g[l], na_k_norm_g[l], tp,
            _CtxState(st_na, (NA_HEADS * NA_DH, NA_HEADS * NA_DH), l, depth, n_ctx, s_len, tp))
        o_na_c = _flash(q_na, k_na, v_na, batch=n_ctx, heads=NA_HEADS, t_len=s_len, s_len=s_len, row0=0,
                        dq=NA_DH, dv=NA_DH, tq=tq_c, tk=tq_c, heads_per_step=NA_HEADS, name="na_attn_ctx")
        kc_na = cache_na_k[:, l].reshape(n_lat * p_len, NA_HEADS * NA_DH).astype(bf16)
        vc_na = cache_na_v[:, l].reshape(n_lat * p_len, NA_HEADS * NA_DH).astype(bf16)
        bias = _na_bias_table(na_rpb[l], t_len // GRID_W)
        o_na_l = _na_latent(q_na, k_na, v_na, kc_na, vc_na, bias, batch=n_lat, t_len=t_len, p_len=p_len, row0=m_ctx)

        merged = _merge((o_mla_c, o_mla_l), (o_hg_c, o_hg_l), (o_na_c, o_na_l), z, lay, w_branch_b, l, tp, tn)
        x = _proj_residual(merged, w_out_b, l, x, g1, group_fn(tp), tp, d)

        x = _ffn(x, norm2_g[l], sc2, sh2, w_up_b, ffn_conv_w[l], ffn_conv_b[l], w_down_b, l, g2, group_of_tile, tm,
                 _pow2_tile(512, ffn), m_ctx, s_len, t_len)

    y_prompt = x[:m_ctx].reshape(n_ctx, s_len, d)
    y_sample = x[m_ctx:].reshape(n_lat, t_len, d)
    na_shape = (n_ctx, depth, s_len, NA_HEADS, NA_DH)
    return (y_prompt, y_sample, st_mla[0], st_mla[1], st_na[0].reshape(na_shape), st_na[1].reshape(na_shape), st_hg)
```

```python
import functools
import math

import numpy as np
import jax
import jax.numpy as jnp
from jax import lax
from jax.experimental import pallas as pl
from jax.experimental.pallas import tpu as pltpu

f32 = jnp.float32
bf16 = jnp.bfloat16

GRID_W = 64
MLA_HEADS, MLA_NOPE, MLA_ROPE, MLA_V, MLA_KV_RANK = 8, 128, 64, 128, 512
MLA_QK = MLA_NOPE + MLA_ROPE
HG_HEADS, HG_K, HG_V = 8, 128, 128
NA_HEADS, NA_DH, NA_WIN_R, NA_WIN_C = 8, 128, 8, 16
ROPE_BASE = 10000.0
EPS = 1e-6
NEG = -1e30
LOG2E = 1.4426950408889634

LANES = 128
SUBLANES = 8
VMEM_LIMIT_BYTES = 56 * 1024 * 1024

HEAD_W = 128
MLA_QPAD = 256
HG_CHUNK = 128
HG_LEVELS = int(math.log2(HG_CHUNK))
NA_QROWS = 8
NA_KROWS = 16


def _params(semantics):
    return pltpu.CompilerParams(dimension_semantics=semantics, vmem_limit_bytes=VMEM_LIMIT_BYTES)


def _pow2_tile(pref, *dims):
    t = pref
    while any(d % t for d in dims):
        t //= 2
    assert t >= SUBLANES, (pref, dims)
    return t


def _dot(a, b):
    return jnp.dot(a, b, preferred_element_type=f32)


def _dot_nt(a, b):
    return lax.dot_general(a, b, (((1,), (1,)), ((), ())), preferred_element_type=f32)


def _dot_tn(a, b):
    return lax.dot_general(a, b, (((0,), (0,)), ((), ())), preferred_element_type=f32)


def _silu(x):
    return x * jax.nn.sigmoid(x)


def _mod_const(x, n):
    return jnp.bitwise_and(x, n - 1) if n & (n - 1) == 0 else lax.rem(x, n)


class _ZLayout:
    def __init__(self, d_model, tn):
        self.mq = 0
        self.ckv = self.mq + MLA_HEADS * MLA_QK
        self.hq = self.ckv + MLA_KV_RANK
        self.hff = self.hq + HG_HEADS * HG_K
        self.hfb = self.hff + HG_HEADS * HG_K
        self.hi = self.hfb + HG_HEADS * HG_K
        self.hg = self.hi + HG_HEADS * HG_V
        self.nq = self.hg + HG_HEADS * HG_V
        self.nk = self.nq + NA_HEADS * NA_DH
        self.nv = self.nk + NA_HEADS * NA_DH
        self.ga = self.nv + NA_HEADS * NA_DH
        self.gb = self.ga + d_model
        self.gc = self.gb + d_model
        self.width = self.gc + d_model
        assert self.width % tn == 0


def _permute_w_in(w, lay):
    n_front = MLA_HEADS * MLA_QK + MLA_KV_RANK
    kpe = w[..., n_front:n_front + MLA_ROPE]
    main = jnp.concatenate([w[..., :n_front], w[..., n_front + MLA_ROPE:]], axis=-1).astype(bf16)
    assert main.shape[-1] == lay.width
    pad = [(0, 0)] * (w.ndim - 1) + [(0, LANES - MLA_ROPE)]
    return main, jnp.pad(kpe, pad).astype(bf16)


def _mod_kernel(c_ref, w_ref, b_ref, o_ref):
    a = _silu(c_ref[...]).astype(bf16)
    o_ref[...] = _dot(a, w_ref[...].astype(bf16)) + b_ref[...]


def _modulation(cond, w_mod, b_mod):
    depth, d, n = w_mod.shape
    g = cond.shape[0]
    tn = _pow2_tile(1024, n)
    return pl.pallas_call(
        _mod_kernel,
        grid=(depth, n // tn),
        in_specs=[
            pl.BlockSpec((g, d), lambda l, j: (0, 0)),
            pl.BlockSpec((None, d, tn), lambda l, j: (l, 0, j)),
            pl.BlockSpec((None, 1, tn), lambda l, j: (l, 0, j)),
        ],
        out_specs=pl.BlockSpec((None, g, tn), lambda l, j: (l, 0, j)),
        out_shape=jax.ShapeDtypeStruct((depth, g, n), f32),
        compiler_params=_params(("arbitrary", "arbitrary")),
        name="modulation",
    )(cond, w_mod, b_mod.reshape(depth, 1, n))


def _normmod_matmul_kernel(x_ref, g_ref, sc_ref, sh_ref, w_ref, *rest):
    if len(rest) == 4:
        ws_ref, o_ref, os_ref, h_scr = rest
    else:
        (o_ref, h_scr), ws_ref, os_ref = rest, None, None

    @pl.when(pl.program_id(1) == 0)
    def _():
        x = x_ref[...]
        ms = jnp.mean(x * x, axis=-1, keepdims=True)
        y = x * lax.rsqrt(ms + EPS) * g_ref[...]
        h_scr[...] = (y * (1.0 + sc_ref[...]) + sh_ref[...]).astype(bf16)
        if ws_ref is not None:
            os_ref[...] = _dot(h_scr[...], ws_ref[...])

    o_ref[...] = _dot(h_scr[...], w_ref[...])


def _normmod_matmul(x, g, sc, sh, w, layer, group_of_tile, tm, tn, name, w_side=None):
    m, d = x.shape
    n = w.shape[2]
    in_specs = [
        pl.BlockSpec((tm, d), lambda i, j: (i, 0)),
        pl.BlockSpec((1, d), lambda i, j: (0, 0)),
        pl.BlockSpec((None, 1, d), lambda i, j: (group_of_tile(i), 0, 0)),
        pl.BlockSpec((None, 1, d), lambda i, j: (group_of_tile(i), 0, 0)),
        pl.BlockSpec((None, d, tn), lambda i, j: (layer, 0, j)),
    ]
    args = [x, g.reshape(1, d), sc, sh, w]
    out_specs = [pl.BlockSpec((tm, tn), lambda i, j: (i, j))]
    out_shape = [jax.ShapeDtypeStruct((m, n), f32)]
    if w_side is not None:
        ns = w_side.shape[2]
        in_specs.append(pl.BlockSpec((None, d, ns), lambda i, j: (layer, 0, 0)))
        args.append(w_side)
        out_specs.append(pl.BlockSpec((tm, ns), lambda i, j: (i, 0)))
        out_shape.append(jax.ShapeDtypeStruct((m, ns), f32))
    res = pl.pallas_call(
        _normmod_matmul_kernel,
        grid=(m // tm, n // tn),
        in_specs=in_specs,
        out_specs=out_specs,
        out_shape=out_shape,
        scratch_shapes=[pltpu.VMEM((tm, d), bf16)],
        compiler_params=_params(("arbitrary", "arbitrary")),
        name=name,
    )(*args)
    return res if w_side is not None else res[0]


def _rope(x, cos, sin_signed):
    n = x.shape[-1]
    half = MLA_ROPE // 4
    lane = lax.broadcasted_iota(jnp.int32, x.shape, 1)
    partner = jnp.where((lane % (2 * half)) < half, pltpu.roll(x, n - half, 1), pltpu.roll(x, half, 1))
    return x * cos + partner * sin_signed


def _mla_q_kernel(mq_ref, gn_ref, gp_ref, cos_ref, sin_ref, o_ref, *, scale):
    tm = mq_ref.shape[0]
    lane = lax.broadcasted_iota(jnp.int32, (tm, LANES), 1)
    low = lane < MLA_ROPE
    cos = cos_ref[...]
    sin = sin_ref[...]
    gn = gn_ref[...]
    gp = gp_ref[...]
    for pair in range(MLA_HEADS // 2):
        t0, t1, t2 = (mq_ref[:, (3 * pair + t) * LANES:(3 * pair + t + 1) * LANES] for t in range(3))
        nopes = (t0, jnp.where(low, pltpu.roll(t1, MLA_ROPE, 1), pltpu.roll(t2, MLA_ROPE, 1)))
        pcol = jnp.where(low, t1, t2)
        sq = pcol * pcol
        ss_all = jnp.sum(sq, axis=-1, keepdims=True)
        ss_low = jnp.sum(jnp.where(low, sq, 0.0), axis=-1, keepdims=True)
        rot = _rope(pcol * gp, cos, sin)
        for k, ss_pe in enumerate((ss_low, ss_all - ss_low)):
            h = 2 * pair + k
            nope = nopes[k]
            ss = jnp.sum(nope * nope, axis=-1, keepdims=True) + ss_pe
            r = lax.rsqrt(ss * (1.0 / MLA_QK) + EPS) * scale
            pe = rot if k == 0 else pltpu.roll(rot, MLA_ROPE, 1)
            o_ref[:, h * MLA_QPAD:h * MLA_QPAD + MLA_NOPE] = (nope * r * gn).astype(bf16)
            o_ref[:, h * MLA_QPAD + MLA_NOPE:(h + 1) * MLA_QPAD] = jnp.where(low, pe * r, 0.0).astype(bf16)


class _CtxState:
    def __init__(self, prevs, widths, layer, depth, n_ctx, s_len, tm):
        assert tm % s_len == 0
        self.n_seq = tm // s_len
        self.s_len = s_len
        self.n_tiles = n_ctx * s_len // tm
        self.prevs = list(prevs)
        n_tiles = self.n_tiles
        self.out_specs = [pl.BlockSpec((self.n_seq, None, s_len, w), lambda i: (jnp.minimum(i, n_tiles - 1), layer, 0, 0))
                          for w in widths]
        self.out_shape = [jax.ShapeDtypeStruct((n_ctx, depth, s_len, w), f32) for w in widths]
        self.in_specs = [pl.BlockSpec(memory_space=pl.ANY)] * len(self.prevs)

    def aliases(self, n_inputs_before, n_outputs_before):
        return {n_inputs_before + k: n_outputs_before + k for k in range(len(self.prevs))}

    def kernel_kwargs(self):
        return dict(n_prev=len(self.prevs), n_ctx_tiles=self.n_tiles, s_len=self.s_len)


def _mla_kv_kernel(*refs, norm_ckv, rope, n_prev=0, n_ctx_tiles=0, s_len=0):
    ckv_ref, kpe_ref, kvg_ref, gn_ref, gp_ref, wuk_ref, wuv_ref, cos_ref, sin_ref = refs[:9]
    k_ref, v_ref, *state_refs = refs[9 + n_prev:]
    ckv = ckv_ref[...]
    if norm_ckv:
        ms = jnp.mean(ckv * ckv, axis=-1, keepdims=True)
        ckv = ckv * lax.rsqrt(ms + EPS) * kvg_ref[...]
    if state_refs:
        ckv_state_ref, kpe_state_ref = state_refs

        @pl.when(pl.program_id(0) < n_ctx_tiles)
        def _():
            for n in range(ckv.shape[0] // s_len):
                ckv_state_ref[n] = ckv[n * s_len:(n + 1) * s_len]
                kpe_state_ref[n] = kpe_ref[n * s_len:(n + 1) * s_len, :MLA_ROPE]

    cb = ckv.astype(bf16)
    kn = _dot(cb, wuk_ref[...])
    v_ref[...] = _dot(cb, wuv_ref[...]).astype(bf16)
    tm = ckv.shape[0]
    lane = lax.broadcasted_iota(jnp.int32, (tm, LANES), 1)
    low = lane < MLA_ROPE
    kpe = jnp.where(low, kpe_ref[...], 0.0)
    ss_pe = jnp.sum(kpe * kpe, axis=-1, keepdims=True)
    pe = kpe * gp_ref[...]
    if rope:
        pe = _rope(pe, cos_ref[...], sin_ref[...])
    gn = gn_ref[...]
    for h in range(MLA_HEADS):
        nope = kn[:, h * MLA_NOPE:(h + 1) * MLA_NOPE]
        ss = jnp.sum(nope * nope, axis=-1, keepdims=True) + ss_pe
        r = lax.rsqrt(ss * (1.0 / MLA_QK) + EPS)
        k_ref[:, h * MLA_QPAD:h * MLA_QPAD + MLA_NOPE] = (nope * r * gn).astype(bf16)
        k_ref[:, h * MLA_QPAD + MLA_NOPE:(h + 1) * MLA_QPAD] = jnp.where(low, pe * r, 0.0).astype(bf16)


def _rope_tables(t_len, tm):
    n_freq = MLA_ROPE // 4
    t = jnp.arange(t_len, dtype=jnp.int32)
    inv = ROPE_BASE ** (-jnp.arange(n_freq, dtype=f32) / n_freq)
    ang_r = (t // GRID_W).astype(f32)[:, None] * inv
    ang_c = (t % GRID_W).astype(f32)[:, None] * inv
    cos64 = jnp.concatenate([jnp.cos(ang_r), jnp.cos(ang_r), jnp.cos(ang_c), jnp.cos(ang_c)], axis=1)
    sin64 = jnp.concatenate([-jnp.sin(ang_r), jnp.sin(ang_r), -jnp.sin(ang_c), jnp.sin(ang_c)], axis=1)
    cos = jnp.concatenate([jnp.ones((tm, LANES), f32), jnp.tile(cos64, (1, 2))], axis=0)
    sin = jnp.concatenate([jnp.zeros((tm, LANES), f32), jnp.tile(sin64, (1, 2))], axis=0)
    return cos, sin


def _mla_q(z, lay, q_g, cos, sin, rope_block, tm):
    m = z.shape[0]
    gn = q_g[:MLA_NOPE].reshape(1, MLA_NOPE)
    gp = jnp.tile(q_g[MLA_NOPE:], 2).reshape(1, LANES)
    wq = MLA_HEADS * MLA_QK
    assert MLA_HEADS % 2 == 0 and 2 * MLA_QK == 3 * LANES and 2 * MLA_ROPE == LANES and lay.mq % wq == 0
    return pl.pallas_call(
        functools.partial(_mla_q_kernel, scale=MLA_QK ** -0.5 * LOG2E),
        grid=(m // tm,),
        in_specs=[
            pl.BlockSpec((tm, wq), lambda i: (i, lay.mq // wq)),
            pl.BlockSpec((1, MLA_NOPE), lambda i: (0, 0)),
            pl.BlockSpec((1, LANES), lambda i: (0, 0)),
            pl.BlockSpec((tm, LANES), lambda i: (rope_block(i), 0)),
            pl.BlockSpec((tm, LANES), lambda i: (rope_block(i), 0)),
        ],
        out_specs=pl.BlockSpec((tm, MLA_HEADS * MLA_QPAD), lambda i: (i, 0)),
        out_shape=jax.ShapeDtypeStruct((m, MLA_HEADS * MLA_QPAD), bf16),
        compiler_params=_params(("arbitrary",)),
        name="mla_q",
    )(z, gn, gp, cos, sin)


def _mla_kv(ckv_src, ckv_col, kpe_src, kpe_col, kv_g, k_g, w_uk, w_uv, layer, cos, sin, rope_block, tm, norm_ckv, rope,
            name, state=None):
    m = ckv_src.shape[0]
    gn = k_g[:MLA_NOPE].reshape(1, MLA_NOPE)
    gp = jnp.concatenate([k_g[MLA_NOPE:], jnp.zeros((LANES - MLA_ROPE,), f32)]).reshape(1, LANES)
    hw = MLA_HEADS * MLA_NOPE
    in_specs = [
        pl.BlockSpec((tm, MLA_KV_RANK), lambda i: (i, ckv_col // MLA_KV_RANK)),
        pl.BlockSpec((tm, LANES), lambda i: (i, kpe_col // LANES)),
        pl.BlockSpec((1, MLA_KV_RANK), lambda i: (0, 0)),
        pl.BlockSpec((1, MLA_NOPE), lambda i: (0, 0)),
        pl.BlockSpec((1, LANES), lambda i: (0, 0)),
        pl.BlockSpec((None, MLA_KV_RANK, hw), lambda i: (layer, 0, 0)),
        pl.BlockSpec((None, MLA_KV_RANK, hw), lambda i: (layer, 0, 0)),
        pl.BlockSpec((tm, LANES), lambda i: (rope_block(i), 0)),
        pl.BlockSpec((tm, LANES), lambda i: (rope_block(i), 0)),
    ]
    args = [ckv_src, kpe_src, kv_g.reshape(1, MLA_KV_RANK), gn, gp, w_uk, w_uv, cos, sin]
    out_specs = [pl.BlockSpec((tm, MLA_HEADS * MLA_QPAD), lambda i: (i, 0)), pl.BlockSpec((tm, hw), lambda i: (i, 0))]
    out_shape = [jax.ShapeDtypeStruct((m, MLA_HEADS * MLA_QPAD), bf16), jax.ShapeDtypeStruct((m, hw), bf16)]
    kwargs, aliases = {}, {}
    if state is not None:
        aliases = state.aliases(len(args), len(out_specs))
        kwargs = state.kernel_kwargs()
        in_specs += state.in_specs
        args += state.prevs
        out_specs += state.out_specs
        out_shape += state.out_shape
    return pl.pallas_call(
        functools.partial(_mla_kv_kernel, norm_ckv=norm_ckv, rope=rope, **kwargs),
        grid=(m // tm,),
        in_specs=in_specs,
        out_specs=out_specs,
        out_shape=out_shape,
        input_output_aliases=aliases,
        compiler_params=_params(("arbitrary",)),
        name=name,
    )(*args)


def _flash_kernel(*refs, n_main, tk, n_extra, heads, dq, dv):
    if n_extra:
        q_ref, k_ref, v_ref, kx_ref, vx_ref, o_ref = refs
    else:
        q_ref, k_ref, v_ref, o_ref = refs
        kx_ref = vx_ref = None
    tq = q_ref.shape[0]
    chunks = [(k_ref, v_ref, c) for c in range(n_main)] + [(kx_ref, vx_ref, c) for c in range(n_extra)]
    def scores(i, g):
        kr, _, c = chunks[i]
        return _dot_nt(kr[c * tk:(c + 1) * tk, g * dq:(g + 1) * dq], q_ref[:, g * dq:(g + 1) * dq])

    m = [jnp.full((1, tq), -jnp.inf, f32)] * heads
    l = [jnp.zeros((1, tq), f32)] * heads
    acc = [jnp.zeros((dv, tq), f32)] * heads
    s = [scores(0, g) for g in range(heads)]
    for i, (_, vr, c) in enumerate(chunks):
        for g in range(heads):
            s_next = scores(i + 1, g) if i + 1 < len(chunks) else None
            m_new = jnp.maximum(m[g], jnp.max(s[g], axis=0, keepdims=True))
            alpha = jnp.exp2(m[g] - m_new)
            p = jnp.exp2(s[g] - m_new)
            l[g] = alpha * l[g] + jnp.sum(p, axis=0, keepdims=True)
            acc[g] = alpha * acc[g] + _dot_tn(vr[c * tk:(c + 1) * tk, g * dv:(g + 1) * dv], p.astype(bf16))
            m[g], s[g] = m_new, s_next
    for g in range(heads):
        o_ref[:, g * dv:(g + 1) * dv] = (acc[g] / l[g]).T.astype(o_ref.dtype)


def _flash(q, k, v, *, batch, heads, t_len, s_len, row0, dq, dv, tq, tk, heads_per_step=1, extra=None, name):
    assert row0 % tq == 0 and row0 % s_len == 0 and t_len % tq == 0 and s_len % tk == 0
    assert heads % heads_per_step == 0
    nq = t_len // tq
    g = heads_per_step
    in_specs = [
        pl.BlockSpec((tq, g * dq), lambda b, h, i: (row0 // tq + b * nq + i, h)),
        pl.BlockSpec((s_len, g * dq), lambda b, h, i: (row0 // s_len + b, h)),
        pl.BlockSpec((s_len, g * dv), lambda b, h, i: (row0 // s_len + b, h)),
    ]
    args = [q, k, v]
    n_extra = 0
    if extra is not None:
        kx, vx, p_len = extra
        assert p_len % tk == 0
        n_extra = p_len // tk
        in_specs += [
            pl.BlockSpec((p_len, g * dq), lambda b, h, i: (b, h)),
            pl.BlockSpec((p_len, g * dv), lambda b, h, i: (b, h)),
        ]
        args += [kx, vx]
    return pl.pallas_call(
        functools.partial(_flash_kernel, n_main=s_len // tk, tk=tk, n_extra=n_extra, heads=g, dq=dq, dv=dv),
        grid=(batch, heads // g, nq),
        in_specs=in_specs,
        out_specs=pl.BlockSpec((tq, g * dv), lambda b, h, i: (b * nq + i, h)),
        out_shape=jax.ShapeDtypeStruct((batch * t_len, heads * dv), bf16),
        compiler_params=_params(("arbitrary", "arbitrary", "arbitrary")),
        name=name,
    )(*args)


def _na_prep_kernel(*refs, scale, n_prev, n_ctx_tiles, s_len):
    nq_ref, nk_ref, nv_ref, qg_ref, kg_ref = refs[:5]
    q_ref, k_ref, v_ref, k_state_ref, v_state_ref = refs[5 + n_prev:]
    qg = qg_ref[...]
    kg = kg_ref[...]
    tm = nv_ref.shape[0]
    is_ctx = pl.program_id(0) < n_ctx_tiles
    v_ref[...] = nv_ref[...].astype(bf16)

    @pl.when(is_ctx)
    def _():
        for n in range(tm // s_len):
            v_state_ref[n] = nv_ref[n * s_len:(n + 1) * s_len, :]

    for h in range(NA_HEADS):
        sl = slice(h * NA_DH, (h + 1) * NA_DH)
        q = nq_ref[:, sl]
        k = nk_ref[:, sl]
        rq = lax.rsqrt(jnp.mean(q * q, axis=-1, keepdims=True) + EPS)
        rk = lax.rsqrt(jnp.mean(k * k, axis=-1, keepdims=True) + EPS)
        q_ref[:, sl] = (q * rq * qg * scale).astype(bf16)
        kn = k * rk * kg
        k_ref[:, sl] = kn.astype(bf16)

        @pl.when(is_ctx)
        def _():
            for n in range(tm // s_len):
                k_state_ref[n, :, sl] = kn[n * s_len:(n + 1) * s_len]


def _na_prep(z, lay, q_g, k_g, tm, state):
    m = z.shape[0]
    w = NA_HEADS * NA_DH
    spec = pl.BlockSpec((tm, w), lambda i: (i, 0))
    in_specs = [
        pl.BlockSpec((tm, w), lambda i: (i, lay.nq // w)),
        pl.BlockSpec((tm, w), lambda i: (i, lay.nk // w)),
        pl.BlockSpec((tm, w), lambda i: (i, lay.nv // w)),
        pl.BlockSpec((1, NA_DH), lambda i: (0, 0)),
        pl.BlockSpec((1, NA_DH), lambda i: (0, 0)),
    ]
    args = [z, z, z, q_g.reshape(1, NA_DH), k_g.reshape(1, NA_DH)]
    return pl.pallas_call(
        functools.partial(_na_prep_kernel, scale=NA_DH ** -0.5 * LOG2E, **state.kernel_kwargs()),
        grid=(m // tm,),
        in_specs=in_specs + state.in_specs,
        out_specs=[spec, spec, spec] + state.out_specs,
        out_shape=[jax.ShapeDtypeStruct((m, w), bf16)] * 3 + state.out_shape,
        input_output_aliases=state.aliases(len(args), 3),
        compiler_params=_params(("arbitrary",)),
        name="na_prep",
    )(*args, *state.prevs)


def _na_static_maps(rows):
    nblk = rows // NA_QROWS
    reps = (0, min(1, nblk - 1), nblk - 1)
    dr_map = np.full((3, NA_QROWS, NA_KROWS), 2 * NA_WIN_R - 1, np.int32)
    for v, kb in enumerate(reps):
        ws = min(max(NA_QROWS * kb - NA_WIN_R // 2, 0), rows - NA_KROWS)
        for i in range(NA_QROWS):
            qrow = NA_QROWS * kb + i
            rs = min(max(qrow - NA_WIN_R // 2, 0), rows - NA_WIN_R)
            for j in range(NA_KROWS):
                krow = ws + j
                if rs <= krow < rs + NA_WIN_R:
                    dr_map[v, i, j] = krow - qrow + NA_WIN_R - 1
    qcol = np.arange(GRID_W)
    cs = np.clip(qcol - NA_WIN_C // 2, 0, GRID_W - NA_WIN_C)
    in_win = (qcol[None, :] >= cs[:, None]) & (qcol[None, :] < cs[:, None] + NA_WIN_C)
    dc_idx = np.clip(qcol[None, :] - qcol[:, None], -(NA_WIN_C - 1), NA_WIN_C - 1) + (NA_WIN_C - 1)
    return dr_map, in_win, dc_idx


def _na_bias_table(rpb, rows):
    dr_map, in_win, dc_idx = _na_static_maps(rows)
    depth, h = rpb.shape[:2]
    bt = jnp.where(in_win.T, jnp.take(rpb * LOG2E, jnp.asarray(dc_idx.T), axis=3), NEG)
    bt = jnp.concatenate([bt, jnp.full((depth, h, 1, GRID_W, GRID_W), NEG, f32)], axis=2)
    tab = jnp.take(bt, jnp.asarray(dr_map.transpose(0, 2, 1).reshape(-1)), axis=2)
    tab = tab.reshape(depth, h, 3, NA_KROWS, NA_QROWS, GRID_W, GRID_W).transpose(0, 2, 1, 3, 5, 4, 6)
    return tab.reshape(depth, 3, h, NA_KROWS * GRID_W, NA_QROWS * GRID_W)


def _na_kernel(q_ref, k_ref, v_ref, kc_ref, vc_ref, bias_ref, o_ref, *, rows, heads):
    kb = pl.program_id(2)
    ws = jnp.clip(NA_QROWS * kb - NA_WIN_R // 2, 0, rows - NA_KROWS)
    win = pl.ds(pl.multiple_of(ws * GRID_W, (NA_WIN_R // 2) * GRID_W), NA_KROWS * GRID_W)
    cols = [slice(g * NA_DH, (g + 1) * NA_DH) for g in range(heads)]
    s_loc = [_dot_nt(k_ref[win, c], q_ref[:, c]) + bias_ref[g] for g, c in enumerate(cols)]
    s_ctx = [_dot_nt(kc_ref[:, c], q_ref[:, c]) for c in cols]
    p_loc, p_ctx, l = [], [], []
    for g in range(heads):
        m = jnp.maximum(jnp.max(s_loc[g], axis=0, keepdims=True), jnp.max(s_ctx[g], axis=0, keepdims=True))
        pl_g = jnp.exp2(s_loc[g] - m)
        pc_g = jnp.exp2(s_ctx[g] - m)
        l.append(jnp.sum(pl_g, axis=0, keepdims=True) + jnp.sum(pc_g, axis=0, keepdims=True))
        p_loc.append(pl_g.astype(bf16))
        p_ctx.append(pc_g.astype(bf16))
    for g, c in enumerate(cols):
        acc = _dot_tn(v_ref[win, c], p_loc[g]) + _dot_tn(vc_ref[:, c], p_ctx[g])
        o_ref[:, c] = (acc / l[g]).T.astype(o_ref.dtype)


def _na_latent(q, k, v, kc, vc, bias, layer, *, batch, t_len, p_len, row0, heads_per_step=2):
    rows = t_len // GRID_W
    assert t_len % GRID_W == 0 and rows % NA_QROWS == 0 and rows >= NA_KROWS
    tq = NA_QROWS * GRID_W
    assert row0 % tq == 0 and row0 % t_len == 0 and NA_HEADS % heads_per_step == 0
    nblk = rows // NA_QROWS
    g = heads_per_step
    gw = g * NA_DH

    def variant(i):
        return jnp.where(i == 0, 0, jnp.where(i == nblk - 1, 2, 1))

    return pl.pallas_call(
        functools.partial(_na_kernel, rows=rows, heads=g),
        grid=(batch, NA_HEADS // g, nblk),
        in_specs=[
            pl.BlockSpec((tq, gw), lambda b, h, i: (row0 // tq + b * nblk + i, h)),
            pl.BlockSpec((t_len, gw), lambda b, h, i: (row0 // t_len + b, h)),
            pl.BlockSpec((t_len, gw), lambda b, h, i: (row0 // t_len + b, h)),
            pl.BlockSpec((None, None, p_len, gw), lambda b, h, i: (b, layer, 0, h)),
            pl.BlockSpec((None, None, p_len, gw), lambda b, h, i: (b, layer, 0, h)),
            pl.BlockSpec((None, None, g, NA_KROWS * GRID_W, tq), lambda b, h, i: (layer, variant(i), h, 0, 0)),
        ],
        out_specs=pl.BlockSpec((tq, gw), lambda b, h, i: (b * nblk + i, h)),
        out_shape=jax.ShapeDtypeStruct((batch * t_len, NA_HEADS * NA_DH), bf16),
        compiler_params=_params(("arbitrary", "arbitrary", "arbitrary")),
        name="na_latent",
    )(q, k, v, kc, vc, bias)


def _hgrn_tables():
    c, lv = HG_CHUNK, HG_LEVELS
    t = np.arange(c)[:, None]
    r = np.arange(c)[None, :]
    wcum = np.stack([r <= t, r >= t]).astype(np.float32)
    qside = np.zeros((2, lv * c, LANES), np.float32)
    mask = np.zeros((2, lv + 1, c, c), np.float32)
    mask[:, 0] = np.eye(c)
    for l in range(lv):
        hs = 1 << l
        blk = t // (2 * hs)
        mid = blk * (2 * hs) + hs
        qf = t >= mid
        qb = t < mid
        qside[0, l * c:(l + 1) * c] = np.where(qf, 1.0, -1.0)
        qside[1, l * c:(l + 1) * c] = np.where(qb, 1.0, -1.0)
        same = blk == blk.T
        mask[0, 1 + l] = same & qf & ~qf.T
        mask[1, 1 + l] = same & qb & ~qb.T
    return wcum, qside, mask


def _hgrn_kernel(*refs, n_chunks, heads, has_state, emit_state, n_prev=0):
    (hq_ref, hff_ref, hfb_ref, hi_ref, hg_ref, la_ref, lc_ref, om_ref, ng_ref,
     wcum_ref, qside_ref, mask_ref) = refs[:12]
    rest = list(refs[12:])
    s0_ref = rest.pop(0) if has_state else None
    del rest[:n_prev]
    o_ref = rest.pop(0)
    st_ref = rest.pop(0) if emit_state else None
    o_scr, s_scr, b_scr = rest
    c, lv = HG_CHUNK, HG_LEVELS

    o_scr[...] = jnp.zeros_like(o_scr)
    for d in range(2):
        for g in range(heads):
            if has_state:
                s_scr[d, g] = s0_ref[d, g].T
            else:
                s_scr[d, g] = jnp.zeros((HG_V, HG_K), f32)

    chains = [(d, g) for g in range(heads) for d in range(2)]
    low_half = lax.broadcasted_iota(jnp.int32, (SUBLANES, HEAD_W), 0) < SUBLANES // 2

    def gates_and_cumsum(ci, row):
        d, g = chains[ci]
        rs = pl.ds(pl.multiple_of(row, c), c)
        cs = slice(g * HEAD_W, (g + 1) * HEAD_W)
        x = (hff_ref if d == 0 else hfb_ref)[rs, cs]
        x2 = x * LOG2E
        u = jnp.exp2(-jnp.abs(x2))
        inv = 1.0 / (1.0 + u)
        pos = x >= 0.0
        log_sig = jnp.minimum(x2, 0.0) - jnp.log2(1.0 + u)
        a = la_ref[d:d + 1, cs]
        bb = lc_ref[d:d + 1, cs] + log_sig
        lf = jnp.maximum(a, bb) + jnp.log2(1.0 + jnp.exp2(-jnp.abs(a - bb)))
        om = om_ref[d:d + 1, cs]
        f = (1.0 - om) + om * (jnp.where(pos, 1.0, u) * inv)
        kin = om * (jnp.where(pos, u, 1.0) * inv)
        q = _silu(hq_ref[rs, cs])
        v = hi_ref[rs, cs].astype(bf16)
        l1 = lf.astype(bf16)
        r1 = lf - l1.astype(f32)
        l2 = r1.astype(bf16)
        l3 = (r1 - l2.astype(f32)).astype(bf16)
        w = wcum_ref[d]
        b = _dot(w, l1) + _dot(w, l2) + _dot(w, l3)
        b_scr[ci] = b
        return dict(rs=rs, cs=cs, f=f, kin=kin, q=q, v=v, b=b)

    def intra_chunk(ci, s):
        d, _ = chains[ci]
        b, q, kin = s["b"], s["q"], s["kin"]

        def row(r):
            return b_scr[ci, r:r + 1, :]

        s["b_end"] = row(c - 1 if d == 0 else 0)
        xs = [jnp.where(qside_ref[d, 0:c, :] > 0.0, q * s["f"], kin).astype(bf16)]
        for l in range(1, lv):
            hs = 1 << l
            blk = 2 * hs
            pick = hs - 1 if d == 0 else hs
            if hs >= SUBLANES:
                pieces = []
                for j in range(c // blk):
                    m = row(blk * j + pick)
                    early = slice(blk * j, blk * j + hs)
                    late = slice(blk * j + hs, blk * (j + 1))
                    q_sl, k_sl = (late, early) if d == 0 else (early, late)
                    xq = q[q_sl] * jnp.exp2(b[q_sl] - m)
                    xk = kin[k_sl] * jnp.exp2(m - b[k_sl])
                    pieces += [xk, xq] if d == 0 else [xq, xk]
                xs.append(jnp.concatenate(pieces, axis=0).astype(bf16))
                continue
            if blk < SUBLANES:
                pieces = [jnp.where(low_half, jnp.broadcast_to(row(SUBLANES * j + pick), (SUBLANES, HEAD_W)),
                                    jnp.broadcast_to(row(SUBLANES * j + blk + pick), (SUBLANES, HEAD_W)))
                          for j in range(c // SUBLANES)]
            else:
                pieces = [jnp.broadcast_to(row(blk * j + pick), (blk, HEAD_W)) for j in range(c // blk)]
            m = pieces[0] if len(pieces) == 1 else jnp.concatenate(pieces, axis=0)
            sign = qside_ref[d, l * c:(l + 1) * c, :]
            xs.append((jnp.where(sign > 0.0, q, kin) * jnp.exp2((b - m) * sign)).astype(bf16))
        att = mask_ref[d, 0] * _dot_nt(q.astype(bf16), kin.astype(bf16)).astype(bf16)
        for l in range(lv):
            att = att + mask_ref[d, 1 + l] * _dot_nt(xs[l], xs[l]).astype(bf16)
        s["att"] = att

    def state_step(ci, s):
        d, g = chains[ci]
        b, q, kin, v = s["b"], s["q"], s["kin"], s["v"]
        st = s_scr[d, g]
        o = _dot(s["att"], v) + _dot_nt((q * jnp.exp2(b)).astype(bf16), st.astype(bf16))
        b_end = s["b_end"]
        s_scr[d, g] = st * jnp.exp2(b_end) + _dot_tn(v, (kin * jnp.exp2(b_end - b)).astype(bf16))
        o_scr[s["rs"], s["cs"]] += o

    def body(i, _):
        rows = (i * c, (n_chunks - 1 - i) * c)
        states = [gates_and_cumsum(ci, rows[chains[ci][0]]) for ci in range(len(chains))]
        for ci, s in enumerate(states):
            intra_chunk(ci, s)
        for ci, s in enumerate(states):
            state_step(ci, s)
        return 0

    lax.fori_loop(0, n_chunks, body, 0)

    ng = ng_ref[...]
    for g in range(heads):
        cs = slice(g * HEAD_W, (g + 1) * HEAD_W)
        o = o_scr[:, cs]
        y = o * lax.rsqrt(jnp.mean(o * o, axis=-1, keepdims=True) + EPS) * ng
        o_ref[:, cs] = (y * _silu(hg_ref[:, cs])).astype(o_ref.dtype)
        if emit_state:
            for d in range(2):
                st_ref[d, g] = s_scr[d, g].T


def _hgrn(z, lay, lb_tabs, norm_g, tables, *, batch, t_len, row0, s0, state_out=None, heads_per_step=2):
    emit_state = state_out is not None
    assert t_len % HG_CHUNK == 0 and row0 % t_len == 0 and HG_K == HEAD_W and HG_V == HEAD_W
    g = heads_per_step
    gw = g * HEAD_W
    la, lc, om = lb_tabs
    wcum, qside, mask = tables
    r0 = row0 // t_len

    def zspec(col):
        return pl.BlockSpec((t_len, gw), lambda b, h: (r0 + b, col // gw + h))

    def const(shape):
        return pl.BlockSpec(shape, lambda b, h: (0,) * len(shape))

    in_specs = [zspec(lay.hq), zspec(lay.hff), zspec(lay.hfb), zspec(lay.hi), zspec(lay.hg),
                pl.BlockSpec((2, gw), lambda b, h: (0, h)), pl.BlockSpec((2, gw), lambda b, h: (0, h)),
                pl.BlockSpec((2, gw), lambda b, h: (0, h)), const((1, HG_V)),
                const(wcum.shape), const(qside.shape), const(mask.shape)]
    args = [z, z, z, z, z, la, lc, om, norm_g.reshape(1, HG_V), wcum, qside, mask]
    state_spec = pl.BlockSpec((None, 2, g, HG_K, HG_V), lambda b, h: (b, 0, h, 0, 0))
    if s0 is not None:
        in_specs.append(state_spec)
        args.append(s0)
    out_specs = [pl.BlockSpec((t_len, gw), lambda b, h: (b, h))]
    out_shape = [jax.ShapeDtypeStruct((batch * t_len, HG_HEADS * HG_V), bf16)]
    aliases, n_prev = {}, 0
    if emit_state:
        prev, layer, depth = state_out
        aliases, n_prev = {len(args): 1}, 1
        in_specs.append(pl.BlockSpec(memory_space=pl.ANY))
        args.append(prev)
        out_specs.append(pl.BlockSpec((None, None, 2, g, HG_K, HG_V), lambda b, h: (b, layer, 0, h, 0, 0)))
        out_shape.append(jax.ShapeDtypeStruct((batch, depth, 2, HG_HEADS, HG_K, HG_V), f32))
    res = pl.pallas_call(
        functools.partial(_hgrn_kernel, n_chunks=t_len // HG_CHUNK, heads=g, has_state=s0 is not None,
                          emit_state=emit_state, n_prev=n_prev),
        grid=(batch, HG_HEADS // g),
        in_specs=in_specs,
        out_specs=out_specs,
        out_shape=out_shape,
        input_output_aliases=aliases,
        scratch_shapes=[pltpu.VMEM((t_len, gw), f32), pltpu.VMEM((2, g, HG_V, HG_K), f32),
                        pltpu.VMEM((2 * g, HG_CHUNK, HEAD_W), f32)],
        compiler_params=_params(("arbitrary", "arbitrary")),
        name="hgrn_ctx" if emit_state else "hgrn_latent",
    )(*args)
    return res if emit_state else (res[0], None)


def _merge_kernel(oac_ref, oal_ref, obc_ref, obl_ref, occ_ref, ocl_ref, ga_ref, gb_ref, gc_ref, w_ref, o_ref,
                  *, n_ctx_tiles):
    def compute(oa_ref, ob_ref, oc_ref):
        m = jax.nn.sigmoid(ga_ref[...]) * _dot(oa_ref[...], w_ref[0])
        m = m + jax.nn.sigmoid(gb_ref[...]) * _dot(ob_ref[...], w_ref[1])
        m = m + jax.nn.sigmoid(gc_ref[...]) * _dot(oc_ref[...], w_ref[2])
        o_ref[...] = m.astype(o_ref.dtype)

    is_ctx = pl.program_id(1) < n_ctx_tiles

    @pl.when(is_ctx)
    def _():
        compute(oac_ref, obc_ref, occ_ref)

    @pl.when(jnp.logical_not(is_ctx))
    def _():
        compute(oal_ref, obl_ref, ocl_ref)


def _merge(o_a, o_b, o_c, z, lay, w_branch, layer, tm, tn):
    bw = o_a[0].shape[1]
    m = z.shape[0]
    d = w_branch.shape[3]
    nct = o_a[0].shape[0] // tm
    cspec = pl.BlockSpec((tm, bw), lambda j, i: (jnp.minimum(i, nct - 1), 0))
    lspec = pl.BlockSpec((tm, bw), lambda j, i: (jnp.maximum(i - nct, 0), 0))

    def gate(col):
        return pl.BlockSpec((tm, tn), lambda j, i: (i, col // tn + j))

    return pl.pallas_call(
        functools.partial(_merge_kernel, n_ctx_tiles=nct),
        grid=(d // tn, m // tm),
        in_specs=[cspec, lspec, cspec, lspec, cspec, lspec, gate(lay.ga), gate(lay.gb), gate(lay.gc),
                  pl.BlockSpec((None, 3, bw, tn), lambda j, i: (layer, 0, 0, j))],
        out_specs=pl.BlockSpec((tm, tn), lambda j, i: (i, j)),
        out_shape=jax.ShapeDtypeStruct((m, d), bf16),
        compiler_params=_params(("arbitrary", "arbitrary")),
        name="merge",
    )(o_a[0], o_a[1], o_b[0], o_b[1], o_c[0], o_c[1], z, z, z, w_branch)


def _proj_residual_kernel(m_ref, w_ref, x_ref, g_ref, o_ref):
    o_ref[...] = x_ref[...] + g_ref[...] * _dot(m_ref[...], w_ref[...])


def _proj_residual(mm, w, layer, x, gate, group_of_tile, tm, tn):
    m, k = mm.shape
    d = w.shape[2]
    return pl.pallas_call(
        _proj_residual_kernel,
        grid=(m // tm, d // tn),
        in_specs=[
            pl.BlockSpec((tm, k), lambda i, j: (i, 0)),
            pl.BlockSpec((None, k, tn), lambda i, j: (layer, 0, j)),
            pl.BlockSpec((tm, tn), lambda i, j: (i, j)),
            pl.BlockSpec((None, 1, tn), lambda i, j: (group_of_tile(i), 0, j)),
        ],
        out_specs=pl.BlockSpec((tm, tn), lambda i, j: (i, j)),
        out_shape=jax.ShapeDtypeStruct((m, d), f32),
        compiler_params=_params(("arbitrary", "arbitrary")),
        name="out_proj",
    )(mm, w, x, gate)


def _ffn_kernel(x_ref, xprev_ref, xnext_ref, ng_ref, sc_ref, sh_ref, wa_ref, wg_ref, cw_ref, cb_ref, wd_ref, gate_ref,
                o_ref, h_scr, *, m_ctx, s_len, t_len, n_k):
    i = pl.program_id(0)
    k = pl.program_id(1)
    tm = x_ref.shape[0]
    tk = wa_ref.shape[1]
    halo = SUBLANES

    @pl.when(k == 0)
    def _():
        def normmod(x):
            ms = jnp.mean(x * x, axis=-1, keepdims=True)
            y = x * lax.rsqrt(ms + EPS) * ng_ref[...]
            return (y * (1.0 + sc_ref[...]) + sh_ref[...]).astype(bf16)

        h_scr[0:halo] = normmod(xprev_ref[...])
        h_scr[halo:halo + tm] = normmod(x_ref[...])
        h_scr[halo + tm:halo + tm + halo] = normmod(xnext_ref[...])
        o_ref[...] = jnp.zeros_like(o_ref)

    row = i * tm + lax.broadcasted_iota(jnp.int32, (tm, LANES), 0)
    pos = jnp.where(row < m_ctx, _mod_const(row, s_len), _mod_const(row - m_ctx, t_len))
    last = jnp.where(row < m_ctx, s_len - 1, t_len - 1)
    keep_prev = jnp.tile(jnp.where(pos == 0, 0.0, 1.0), (1, tk // LANES))
    keep_next = jnp.tile(jnp.where(pos == last, 0.0, 1.0), (1, tk // LANES))

    g_ext = _dot(h_scr[...], wg_ref[...])
    a = _dot(h_scr[halo:halo + tm], wa_ref[...])
    n_ext = tm + 2 * halo
    g = g_ext[halo:halo + tm]
    g_prev = pltpu.roll(g_ext, 1, 0)[halo:halo + tm]
    g_next = pltpu.roll(g_ext, n_ext - 1, 0)[halo:halo + tm]
    conv = (g_prev * keep_prev * cw_ref[0:1, :] + g * cw_ref[1:2, :] + g_next * keep_next * cw_ref[2:3, :]
            + cb_ref[...])
    u = (_silu(conv) * a).astype(bf16)
    o_ref[...] = _dot(u, wd_ref[...]) + o_ref[...]

    @pl.when(k == n_k - 1)
    def _():
        o_ref[...] = x_ref[...] + gate_ref[...] * o_ref[...]


def _ffn(x, norm_g, sc, sh, w_up, conv_w, conv_b, w_down, layer, gate, group_of_tile, tm, tk, m_ctx, s_len, t_len):
    m, d = x.shape
    f = w_down.shape[1]
    nk = f // tk
    nsub = tm // SUBLANES
    last_sub = m // SUBLANES - 1
    return pl.pallas_call(
        functools.partial(_ffn_kernel, m_ctx=m_ctx, s_len=s_len, t_len=t_len, n_k=nk),
        grid=(m // tm, nk),
        in_specs=[
            pl.BlockSpec((tm, d), lambda i, k: (i, 0), pipeline_mode=pl.Buffered(1)),
            pl.BlockSpec((SUBLANES, d), lambda i, k: (jnp.maximum(i * nsub - 1, 0), 0)),
            pl.BlockSpec((SUBLANES, d), lambda i, k: (jnp.minimum((i + 1) * nsub, last_sub), 0)),
            pl.BlockSpec((1, d), lambda i, k: (0, 0)),
            pl.BlockSpec((None, 1, d), lambda i, k: (group_of_tile(i), 0, 0)),
            pl.BlockSpec((None, 1, d), lambda i, k: (group_of_tile(i), 0, 0)),
            pl.BlockSpec((None, d, tk), lambda i, k: (layer, 0, k)),
            pl.BlockSpec((None, d, tk), lambda i, k: (layer, 0, nk + k)),
            pl.BlockSpec((3, tk), lambda i, k: (0, k)),
            pl.BlockSpec((1, tk), lambda i, k: (0, k)),
            pl.BlockSpec((None, tk, d), lambda i, k: (layer, k, 0)),
            pl.BlockSpec((None, 1, d), lambda i, k: (group_of_tile(i), 0, 0)),
        ],
        out_specs=pl.BlockSpec((tm, d), lambda i, k: (i, 0), pipeline_mode=pl.Buffered(1)),
        out_shape=jax.ShapeDtypeStruct((m, d), f32),
        scratch_shapes=[pltpu.VMEM((tm + 2 * SUBLANES, d), bf16)],
        compiler_params=_params(("arbitrary", "arbitrary")),
        name="ffn",
    )(x, x, x, norm_g.reshape(1, d), sc, sh, w_up, w_up, conv_w, conv_b.reshape(1, f), w_down, gate)


def kernel(x_prompt, x_sample, cache_mla_ckv, cache_mla_kpe, cache_na_k, cache_na_v, state_hgrn, c, c_ctx, w_mod, b_mod, norm1_g, norm2_g, w_in, mla_kv_norm_g, mla_q_norm_g, mla_k_norm_g, mla_w_uk, mla_w_uv, hgrn_lower_bounds, hgrn_norm_g, na_q_norm_g, na_k_norm_g, na_rpb, w_branch, w_out, ffn_w_up, ffn_conv_w, ffn_conv_b, ffn_w_down):
    n_ctx, s_len, d = x_prompt.shape
    n_lat, t_len, _ = x_sample.shape
    depth = w_in.shape[0]
    p_len = cache_mla_ckv.shape[2]
    ffn = ffn_w_down.shape[1]
    m_ctx, m_lat = n_ctx * s_len, n_lat * t_len
    m = m_ctx + m_lat
    assert m_ctx % t_len == 0, "context rows must be a whole number of latent sequences"

    tm = _pow2_tile(1024, m_ctx, t_len)
    tn = _pow2_tile(1024, d)
    tp = _pow2_tile(512, m_ctx, t_len)
    lay = _ZLayout(d, tn)

    def group_fn(rows):
        return lambda i: jnp.where(i < m_ctx // rows, 0, 1 + (i - m_ctx // rows) // (t_len // rows))

    group_of_tile = group_fn(tm)

    n_groups = -(-(1 + n_lat) // SUBLANES) * SUBLANES
    cond = jnp.concatenate([c_ctx[None], c, jnp.zeros((n_groups - 1 - n_lat, d), f32)], axis=0)
    mods = _modulation(cond, w_mod, b_mod).reshape(depth, n_groups, 6, d).transpose(0, 2, 1, 3)[:, :, :, None, :]

    sm = jax.nn.softmax(hgrn_lower_bounds.astype(f32), axis=1)
    csum = jnp.cumsum(sm, axis=1)
    lower = csum - csum[:, :1]
    hg_tabs = tuple(jnp.asarray(t, dt) for t, dt in zip(_hgrn_tables(), (bf16, f32, bf16)))

    cos, sin = _rope_tables(t_len, tp)
    n_ctx_tp = m_ctx // tp

    def rope_block(i):
        return jnp.where(i < n_ctx_tp, 0, 1 + (i - n_ctx_tp) % (t_len // tp))

    x = jnp.concatenate([x_prompt.reshape(m_ctx, d), x_sample.reshape(m_lat, d)], axis=0)
    w_in_b, w_kpe_b = _permute_w_in(w_in, lay)
    w_uk = mla_w_uk.reshape(depth, MLA_KV_RANK, MLA_HEADS * MLA_NOPE).astype(bf16)
    w_uv = mla_w_uv.reshape(depth, MLA_KV_RANK, MLA_HEADS * MLA_V).astype(bf16)
    w_branch_b, w_out_b = w_branch.astype(bf16), w_out.astype(bf16)
    w_up_b, w_down_b = ffn_w_up.astype(bf16), ffn_w_down.astype(bf16)
    na_bias = _na_bias_table(na_rpb, t_len // GRID_W)
    kc_na = cache_na_k.reshape(n_lat, depth, p_len, NA_HEADS * NA_DH).astype(bf16)
    vc_na = cache_na_v.reshape(n_lat, depth, p_len, NA_HEADS * NA_DH).astype(bf16)

    def state_init(*trailing):
        return jnp.zeros((n_ctx, depth) + trailing, f32)

    st_mla = [state_init(s_len, MLA_KV_RANK), state_init(s_len, MLA_ROPE)]
    st_na = [state_init(s_len, NA_HEADS * NA_DH), state_init(s_len, NA_HEADS * NA_DH)]
    st_hg = state_init(2, HG_HEADS, HG_K, HG_V)
    for l in range(depth):
        sh1, sc1, g1, sh2, sc2, g2 = (mods[l, k] for k in range(6))
        z, kpe = _normmod_matmul(x, norm1_g[l], sc1, sh1, w_in_b, l, group_of_tile, tm, tn, "in_proj", w_side=w_kpe_b)

        q_mla = _mla_q(z, lay, mla_q_norm_g[l], cos, sin, rope_block, tp)
        k_mla, v_mla, *st_mla = _mla_kv(
            z, lay.ckv, kpe, 0, mla_kv_norm_g[l], mla_k_norm_g[l], w_uk, w_uv, l, cos, sin, rope_block, tp, True, True,
            "mla_kv", state=_CtxState(st_mla, (MLA_KV_RANK, MLA_ROPE), l, depth, n_ctx, s_len, tp))
        ckv_c = cache_mla_ckv[:, l].reshape(n_lat * p_len, MLA_KV_RANK)
        kpe_c = jnp.pad(cache_mla_kpe[:, l].reshape(n_lat * p_len, MLA_ROPE), ((0, 0), (0, LANES - MLA_ROPE)))
        tc = _pow2_tile(512, n_lat * p_len)
        k_c, v_c = _mla_kv(ckv_c, 0, kpe_c, 0, mla_kv_norm_g[l], mla_k_norm_g[l], w_uk, w_uv, l,
                           cos, sin, lambda i: 0, tc, False, False, "mla_kv_cache")
        tq_c = _pow2_tile(256, s_len)
        o_mla_c = _flash(q_mla, k_mla, v_mla, batch=n_ctx, heads=MLA_HEADS, t_len=s_len, s_len=s_len, row0=0,
                         dq=MLA_QPAD, dv=MLA_V, tq=tq_c, tk=tq_c, heads_per_step=MLA_HEADS, name="mla_attn_ctx")
        tq_l = _pow2_tile(512, t_len, p_len)
        o_mla_l = _flash(q_mla, k_mla, v_mla, batch=n_lat, heads=MLA_HEADS, t_len=t_len, s_len=t_len, row0=m_ctx,
                         dq=MLA_QPAD, dv=MLA_V, tq=tq_l, tk=tq_l, heads_per_step=2, extra=(k_c, v_c, p_len),
                         name="mla_attn_latent")

        lb = lower[:, l]
        lb_tabs = (jnp.log(lb) * LOG2E, jnp.log1p(-lb) * LOG2E, 1.0 - lb)
        o_hg_c, st_hg = _hgrn(z, lay, lb_tabs, hgrn_norm_g[l], hg_tabs, batch=n_ctx, t_len=s_len, row0=0,
                              s0=None, state_out=(st_hg, l, depth), heads_per_step=4)
        o_hg_l, _ = _hgrn(z, lay, lb_tabs, hgrn_norm_g[l], hg_tabs, batch=n_lat, t_len=t_len, row0=m_ctx,
                          s0=state_hgrn[:, l])

        q_na, k_na, v_na, *st_na = _na_prep(
            z, lay, na_q_norm_g[l], na_k_norm_g[l], tp,
            _CtxState(st_na, (NA_HEADS * NA_DH, NA_HEADS * NA_DH), l, depth, n_ctx, s_len, tp))
        o_na_c = _flash(q_na, k_na, v_na, batch=n_ctx, heads=NA_HEADS, t_len=s_len, s_len=s_len, row0=0,
                        dq=NA_DH, dv=NA_DH, tq=tq_c, tk=tq_c, heads_per_step=NA_HEADS, name="na_attn_ctx")
        o_na_l = _na_latent(q_na, k_na, v_na, kc_na, vc_na, na_bias, l, batch=n_lat, t_len=t_len, p_len=p_len, row0=m_ctx)

        merged = _merge((o_mla_c, o_mla_l), (o_hg_c, o_hg_l), (o_na_c, o_na_l), z, lay, w_branch_b, l, tp, tn)
        x = _proj_residual(merged, w_out_b, l, x, g1, group_fn(tp), tp, d)

        x = _ffn(x, norm2_g[l], sc2, sh2, w_up_b, ffn_conv_w[l], ffn_conv_b[l], w_down_b, l, g2, group_of_tile, tm,
                 _pow2_tile(512, ffn), m_ctx, s_len, t_len)

    y_prompt = x[:m_ctx].reshape(n_ctx, s_len, d)
    y_sample = x[m_ctx:].reshape(n_lat, t_len, d)
    na_shape = (n_ctx, depth, s_len, NA_HEADS, NA_DH)
    return (y_prompt, y_sample, st_mla[0], st_mla[1], st_na[0].reshape(na_shape), st_na[1].reshape(na_shape), st_hg)
```

```python
import functools
import math

import numpy as np
import jax
import jax.numpy as jnp
from jax import lax
from jax.experimental import pallas as pl
from jax.experimental.pallas import tpu as pltpu

f32 = jnp.float32
bf16 = jnp.bfloat16

GRID_W = 64
MLA_HEADS, MLA_NOPE, MLA_ROPE, MLA_V, MLA_KV_RANK = 8, 128, 64, 128, 512
MLA_QK = MLA_NOPE + MLA_ROPE
HG_HEADS, HG_K, HG_V = 8, 128, 128
NA_HEADS, NA_DH, NA_WIN_R, NA_WIN_C = 8, 128, 8, 16
ROPE_BASE = 10000.0
EPS = 1e-6
NEG = -1e30
LOG2E = 1.4426950408889634

LANES = 128
SUBLANES = 8
VMEM_LIMIT_BYTES = 56 * 1024 * 1024

HEAD_W = 128
MLA_QPAD = 256
HG_CHUNK = 128
HG_LEVELS = int(math.log2(HG_CHUNK))
NA_QROWS = 8
NA_KROWS = 16


def _params(semantics):
    return pltpu.CompilerParams(dimension_semantics=semantics, vmem_limit_bytes=VMEM_LIMIT_BYTES)


def _pow2_tile(pref, *dims):
    t = pref
    while any(d % t for d in dims):
        t //= 2
    assert t >= SUBLANES, (pref, dims)
    return t


def _dot(a, b):
    return jnp.dot(a, b, preferred_element_type=f32)


def _dot_nt(a, b):
    return lax.dot_general(a, b, (((1,), (1,)), ((), ())), preferred_element_type=f32)


def _dot_tn(a, b):
    return lax.dot_general(a, b, (((0,), (0,)), ((), ())), preferred_element_type=f32)


def _silu(x):
    return x * jax.nn.sigmoid(x)


def _mod_const(x, n):
    return jnp.bitwise_and(x, n - 1) if n & (n - 1) == 0 else lax.rem(x, n)


class _ZLayout:
    def __init__(self, d_model, tn):
        self.mq = 0
        self.ckv = self.mq + MLA_HEADS * MLA_QK
        self.hq = self.ckv + MLA_KV_RANK
        self.hff = self.hq + HG_HEADS * HG_K
        self.hfb = self.hff + HG_HEADS * HG_K
        self.hi = self.hfb + HG_HEADS * HG_K
        self.hg = self.hi + HG_HEADS * HG_V
        self.nq = self.hg + HG_HEADS * HG_V
        self.nk = self.nq + NA_HEADS * NA_DH
        self.nv = self.nk + NA_HEADS * NA_DH
        self.ga = self.nv + NA_HEADS * NA_DH
        self.gb = self.ga + d_model
        self.gc = self.gb + d_model
        self.width = self.gc + d_model
        assert self.width % tn == 0


def _split_w_in(w, lay):
    n_front = MLA_HEADS * MLA_QK + MLA_KV_RANK
    front, kpe, back = w[..., :n_front], w[..., n_front:n_front + MLA_ROPE], w[..., n_front + MLA_ROPE:]
    assert front.shape[-1] + back.shape[-1] == lay.width
    pad = [(0, 0)] * (w.ndim - 1) + [(0, LANES - MLA_ROPE)]
    return (front.astype(bf16), back.astype(bf16)), jnp.pad(kpe, pad).astype(bf16)


def _mod_kernel(c_ref, w_ref, b_ref, o_ref):
    a = _silu(c_ref[...]).astype(bf16)
    o_ref[...] = _dot(a, w_ref[...].astype(bf16)) + b_ref[...]


def _modulation(cond, w_mod, b_mod):
    depth, d, n = w_mod.shape
    g = cond.shape[0]
    tn = _pow2_tile(1024, n)
    return pl.pallas_call(
        _mod_kernel,
        grid=(depth, n // tn),
        in_specs=[
            pl.BlockSpec((g, d), lambda l, j: (0, 0)),
            pl.BlockSpec((None, d, tn), lambda l, j: (l, 0, j)),
            pl.BlockSpec((None, 1, tn), lambda l, j: (l, 0, j)),
        ],
        out_specs=pl.BlockSpec((None, g, tn), lambda l, j: (l, 0, j)),
        out_shape=jax.ShapeDtypeStruct((depth, g, n), f32),
        compiler_params=_params(("arbitrary", "arbitrary")),
        name="modulation",
    )(cond, w_mod, b_mod.reshape(depth, 1, n))


def _row_group_specs(xs, tm, width, col=lambda j: 0):
    if len(xs) == 1:
        return [pl.BlockSpec((tm, width), lambda i, j: (i, col(j)))], 0
    n0 = xs[0].shape[0] // tm
    assert len(xs) == 2 and xs[0].shape[0] % tm == 0 and xs[1].shape[0] % tm == 0
    return [pl.BlockSpec((tm, width), lambda i, j: (jnp.minimum(i, n0 - 1), col(j)), pipeline_mode=pl.Buffered(1)),
            pl.BlockSpec((tm, width), lambda i, j: (jnp.maximum(i - n0, 0), col(j)), pipeline_mode=pl.Buffered(1))], n0


def _for_owner(x_refs, n_first, fn):
    if len(x_refs) == 1:
        fn(x_refs[0])
        return
    first = pl.program_id(0) < n_first
    pl.when(first)(lambda: fn(x_refs[0]))
    pl.when(jnp.logical_not(first))(lambda: fn(x_refs[1]))


def _normmod_matmul_kernel(*refs, n_x, n_first, n_w, n_front, has_side):
    x_refs = refs[:n_x]
    g_ref, sc_ref, sh_ref = refs[n_x:n_x + 3]
    w_refs = refs[n_x + 3:n_x + 3 + n_w]
    rest = refs[n_x + 3 + n_w:]
    if has_side:
        ws_ref, o_ref, os_ref, h_scr = rest
    else:
        o_ref, h_scr = rest
    j = pl.program_id(1)

    def prologue(x_ref):
        x = x_ref[...]
        ms = jnp.mean(x * x, axis=-1, keepdims=True)
        y = x * lax.rsqrt(ms + EPS) * g_ref[...]
        h_scr[...] = (y * (1.0 + sc_ref[...]) + sh_ref[...]).astype(bf16)
        if has_side:
            os_ref[...] = _dot(h_scr[...], ws_ref[...])

    @pl.when(j == 0)
    def _():
        _for_owner(x_refs, n_first, prologue)

    if n_w == 1:
        o_ref[...] = _dot(h_scr[...], w_refs[0][...])
    else:
        @pl.when(j < n_front)
        def _():
            o_ref[...] = _dot(h_scr[...], w_refs[0][...])

        @pl.when(j >= n_front)
        def _():
            o_ref[...] = _dot(h_scr[...], w_refs[1][...])


def _normmod_matmul(xs, g, sc, sh, ws, layer, group_of_tile, tm, tn, name, w_side=None):
    m = sum(x.shape[0] for x in xs)
    d = xs[0].shape[1]
    n = sum(w.shape[2] for w in ws)
    x_specs, n_first = _row_group_specs(xs, tm, d)
    n_front = ws[0].shape[2] // tn
    assert all(w.shape[2] % tn == 0 for w in ws) and len(ws) in (1, 2)
    if len(ws) == 1:
        w_specs = [pl.BlockSpec((None, d, tn), lambda i, j: (layer, 0, j))]
    else:
        w_specs = [pl.BlockSpec((None, d, tn), lambda i, j: (layer, 0, jnp.minimum(j, n_front - 1))),
                   pl.BlockSpec((None, d, tn), lambda i, j: (layer, 0, jnp.maximum(j - n_front, 0)))]
    in_specs = x_specs + [
        pl.BlockSpec((1, d), lambda i, j: (0, 0)),
        pl.BlockSpec((None, 1, d), lambda i, j: (group_of_tile(i), 0, 0)),
        pl.BlockSpec((None, 1, d), lambda i, j: (group_of_tile(i), 0, 0)),
    ] + w_specs
    args = list(xs) + [g.reshape(1, d), sc, sh] + list(ws)
    out_specs = [pl.BlockSpec((tm, tn), lambda i, j: (i, j))]
    out_shape = [jax.ShapeDtypeStruct((m, n), f32)]
    if w_side is not None:
        ns = w_side.shape[2]
        in_specs.append(pl.BlockSpec((None, d, ns), lambda i, j: (layer, 0, 0)))
        args.append(w_side)
        out_specs.append(pl.BlockSpec((tm, ns), lambda i, j: (i, 0)))
        out_shape.append(jax.ShapeDtypeStruct((m, ns), f32))
    res = pl.pallas_call(
        functools.partial(_normmod_matmul_kernel, n_x=len(xs), n_first=n_first, n_w=len(ws), n_front=n_front,
                          has_side=w_side is not None),
        grid=(m // tm, n // tn),
        in_specs=in_specs,
        out_specs=out_specs,
        out_shape=out_shape,
        scratch_shapes=[pltpu.VMEM((tm, d), bf16)],
        compiler_params=_params(("arbitrary", "arbitrary")),
        name=name,
    )(*args)
    return res if w_side is not None else res[0]


def _rope(x, cos, sin_signed):
    n = x.shape[-1]
    half = MLA_ROPE // 4
    lane = lax.broadcasted_iota(jnp.int32, x.shape, 1)
    partner = jnp.where((lane % (2 * half)) < half, pltpu.roll(x, n - half, 1), pltpu.roll(x, half, 1))
    return x * cos + partner * sin_signed


def _mla_q_kernel(mq_ref, gn_ref, gp_ref, cos_ref, sin_ref, o_ref, *, scale):
    tm = mq_ref.shape[0]
    lane = lax.broadcasted_iota(jnp.int32, (tm, LANES), 1)
    low = lane < MLA_ROPE
    cos = cos_ref[...]
    sin = sin_ref[...]
    gn = gn_ref[...]
    gp = gp_ref[...]
    for pair in range(MLA_HEADS // 2):
        t0, t1, t2 = (mq_ref[:, (3 * pair + t) * LANES:(3 * pair + t + 1) * LANES] for t in range(3))
        nopes = (t0, jnp.where(low, pltpu.roll(t1, MLA_ROPE, 1), pltpu.roll(t2, MLA_ROPE, 1)))
        pcol = jnp.where(low, t1, t2)
        sq = pcol * pcol
        ss_all = jnp.sum(sq, axis=-1, keepdims=True)
        ss_low = jnp.sum(jnp.where(low, sq, 0.0), axis=-1, keepdims=True)
        rot = _rope(pcol * gp, cos, sin)
        for k, ss_pe in enumerate((ss_low, ss_all - ss_low)):
            h = 2 * pair + k
            nope = nopes[k]
            ss = jnp.sum(nope * nope, axis=-1, keepdims=True) + ss_pe
            r = lax.rsqrt(ss * (1.0 / MLA_QK) + EPS) * scale
            pe = rot if k == 0 else pltpu.roll(rot, MLA_ROPE, 1)
            o_ref[:, h * MLA_QPAD:h * MLA_QPAD + MLA_NOPE] = (nope * r * gn).astype(bf16)
            o_ref[:, h * MLA_QPAD + MLA_NOPE:(h + 1) * MLA_QPAD] = jnp.where(low, pe * r, 0.0).astype(bf16)


class _CtxState:
    def __init__(self, prevs, widths, layer, depth, n_ctx, s_len, tm):
        assert tm % s_len == 0
        self.n_seq = tm // s_len
        self.s_len = s_len
        self.n_tiles = n_ctx * s_len // tm
        self.prevs = list(prevs)
        n_tiles = self.n_tiles
        self.out_specs = [pl.BlockSpec((self.n_seq, None, s_len, w), lambda i: (jnp.minimum(i, n_tiles - 1), layer, 0, 0))
                          for w in widths]
        self.out_shape = [jax.ShapeDtypeStruct((n_ctx, depth, s_len, w), f32) for w in widths]
        self.in_specs = [pl.BlockSpec(memory_space=pl.ANY)] * len(self.prevs)

    def aliases(self, n_inputs_before, n_outputs_before):
        return {n_inputs_before + k: n_outputs_before + k for k in range(len(self.prevs))}

    def kernel_kwargs(self):
        return dict(n_prev=len(self.prevs), n_ctx_tiles=self.n_tiles, s_len=self.s_len)


def _mla_kv_kernel(*refs, norm_ckv, rope, n_prev=0, n_ctx_tiles=0, s_len=0):
    ckv_ref, kpe_ref, kvg_ref, gn_ref, gp_ref, wuk_ref, wuv_ref, cos_ref, sin_ref = refs[:9]
    k_ref, v_ref, *state_refs = refs[9 + n_prev:]
    ckv = ckv_ref[...]
    if norm_ckv:
        ms = jnp.mean(ckv * ckv, axis=-1, keepdims=True)
        ckv = ckv * lax.rsqrt(ms + EPS) * kvg_ref[...]
    if state_refs:
        ckv_state_ref, kpe_state_ref = state_refs

        @pl.when(pl.program_id(0) < n_ctx_tiles)
        def _():
            for n in range(ckv.shape[0] // s_len):
                ckv_state_ref[n] = ckv[n * s_len:(n + 1) * s_len]
                kpe_state_ref[n] = kpe_ref[n * s_len:(n + 1) * s_len, :MLA_ROPE]

    cb = ckv.astype(bf16)
    kn = _dot(cb, wuk_ref[...])
    v_ref[...] = _dot(cb, wuv_ref[...]).astype(bf16)
    tm = ckv.shape[0]
    lane = lax.broadcasted_iota(jnp.int32, (tm, LANES), 1)
    low = lane < MLA_ROPE
    kpe = jnp.where(low, kpe_ref[...], 0.0)
    ss_pe = jnp.sum(kpe * kpe, axis=-1, keepdims=True)
    pe = kpe * gp_ref[...]
    if rope:
        pe = _rope(pe, cos_ref[...], sin_ref[...])
    gn = gn_ref[...]
    for h in range(MLA_HEADS):
        nope = kn[:, h * MLA_NOPE:(h + 1) * MLA_NOPE]
        ss = jnp.sum(nope * nope, axis=-1, keepdims=True) + ss_pe
        r = lax.rsqrt(ss * (1.0 / MLA_QK) + EPS)
        k_ref[:, h * MLA_QPAD:h * MLA_QPAD + MLA_NOPE] = (nope * r * gn).astype(bf16)
        k_ref[:, h * MLA_QPAD + MLA_NOPE:(h + 1) * MLA_QPAD] = jnp.where(low, pe * r, 0.0).astype(bf16)


def _rope_tables(t_len, tm):
    n_freq = MLA_ROPE // 4
    t = jnp.arange(t_len, dtype=jnp.int32)
    inv = ROPE_BASE ** (-jnp.arange(n_freq, dtype=f32) / n_freq)
    ang_r = (t // GRID_W).astype(f32)[:, None] * inv
    ang_c = (t % GRID_W).astype(f32)[:, None] * inv
    cos64 = jnp.concatenate([jnp.cos(ang_r), jnp.cos(ang_r), jnp.cos(ang_c), jnp.cos(ang_c)], axis=1)
    sin64 = jnp.concatenate([-jnp.sin(ang_r), jnp.sin(ang_r), -jnp.sin(ang_c), jnp.sin(ang_c)], axis=1)
    cos = jnp.concatenate([jnp.ones((tm, LANES), f32), jnp.tile(cos64, (1, 2))], axis=0)
    sin = jnp.concatenate([jnp.zeros((tm, LANES), f32), jnp.tile(sin64, (1, 2))], axis=0)
    return cos, sin


def _mla_q(z, lay, q_g, cos, sin, rope_block, tm):
    m = z.shape[0]
    gn = q_g[:MLA_NOPE].reshape(1, MLA_NOPE)
    gp = jnp.tile(q_g[MLA_NOPE:], 2).reshape(1, LANES)
    wq = MLA_HEADS * MLA_QK
    assert MLA_HEADS % 2 == 0 and 2 * MLA_QK == 3 * LANES and 2 * MLA_ROPE == LANES and lay.mq % wq == 0
    return pl.pallas_call(
        functools.partial(_mla_q_kernel, scale=MLA_QK ** -0.5 * LOG2E),
        grid=(m // tm,),
        in_specs=[
            pl.BlockSpec((tm, wq), lambda i: (i, lay.mq // wq)),
            pl.BlockSpec((1, MLA_NOPE), lambda i: (0, 0)),
            pl.BlockSpec((1, LANES), lambda i: (0, 0)),
            pl.BlockSpec((tm, LANES), lambda i: (rope_block(i), 0)),
            pl.BlockSpec((tm, LANES), lambda i: (rope_block(i), 0)),
        ],
        out_specs=pl.BlockSpec((tm, MLA_HEADS * MLA_QPAD), lambda i: (i, 0)),
        out_shape=jax.ShapeDtypeStruct((m, MLA_HEADS * MLA_QPAD), bf16),
        compiler_params=_params(("arbitrary",)),
        name="mla_q",
    )(z, gn, gp, cos, sin)


def _mla_kv(ckv_src, ckv_col, kpe_src, kpe_col, kv_g, k_g, w_uk, w_uv, layer, cos, sin, rope_block, tm, norm_ckv, rope,
            name, state=None):
    m = ckv_src.shape[0]
    gn = k_g[:MLA_NOPE].reshape(1, MLA_NOPE)
    gp = jnp.concatenate([k_g[MLA_NOPE:], jnp.zeros((LANES - MLA_ROPE,), f32)]).reshape(1, LANES)
    hw = MLA_HEADS * MLA_NOPE
    in_specs = [
        pl.BlockSpec((tm, MLA_KV_RANK), lambda i: (i, ckv_col // MLA_KV_RANK)),
        pl.BlockSpec((tm, LANES), lambda i: (i, kpe_col // LANES)),
        pl.BlockSpec((1, MLA_KV_RANK), lambda i: (0, 0)),
        pl.BlockSpec((1, MLA_NOPE), lambda i: (0, 0)),
        pl.BlockSpec((1, LANES), lambda i: (0, 0)),
        pl.BlockSpec((None, MLA_KV_RANK, hw), lambda i: (layer, 0, 0)),
        pl.BlockSpec((None, MLA_KV_RANK, hw), lambda i: (layer, 0, 0)),
        pl.BlockSpec((tm, LANES), lambda i: (rope_block(i), 0)),
        pl.BlockSpec((tm, LANES), lambda i: (rope_block(i), 0)),
    ]
    args = [ckv_src, kpe_src, kv_g.reshape(1, MLA_KV_RANK), gn, gp, w_uk, w_uv, cos, sin]
    out_specs = [pl.BlockSpec((tm, MLA_HEADS * MLA_QPAD), lambda i: (i, 0)), pl.BlockSpec((tm, hw), lambda i: (i, 0))]
    out_shape = [jax.ShapeDtypeStruct((m, MLA_HEADS * MLA_QPAD), bf16), jax.ShapeDtypeStruct((m, hw), bf16)]
    kwargs, aliases = {}, {}
    if state is not None:
        aliases = state.aliases(len(args), len(out_specs))
        kwargs = state.kernel_kwargs()
        in_specs += state.in_specs
        args += state.prevs
        out_specs += state.out_specs
        out_shape += state.out_shape
    return pl.pallas_call(
        functools.partial(_mla_kv_kernel, norm_ckv=norm_ckv, rope=rope, **kwargs),
        grid=(m // tm,),
        in_specs=in_specs,
        out_specs=out_specs,
        out_shape=out_shape,
        input_output_aliases=aliases,
        compiler_params=_params(("arbitrary",)),
        name=name,
    )(*args)


def _flash_kernel(*refs, n_main, tk, n_extra, heads, dq, dv):
    if n_extra:
        q_ref, k_ref, v_ref, kx_ref, vx_ref, o_ref = refs
    else:
        q_ref, k_ref, v_ref, o_ref = refs
        kx_ref = vx_ref = None
    tq = q_ref.shape[0]
    chunks = [(k_ref, v_ref, c) for c in range(n_main)] + [(kx_ref, vx_ref, c) for c in range(n_extra)]
    def scores(i, g):
        kr, _, c = chunks[i]
        return _dot_nt(kr[c * tk:(c + 1) * tk, g * dq:(g + 1) * dq], q_ref[:, g * dq:(g + 1) * dq])

    m = [jnp.full((1, tq), -jnp.inf, f32)] * heads
    l = [jnp.zeros((1, tq), f32)] * heads
    acc = [jnp.zeros((dv, tq), f32)] * heads
    s = [scores(0, g) for g in range(heads)]
    for i, (_, vr, c) in enumerate(chunks):
        for g in range(heads):
            s_next = scores(i + 1, g) if i + 1 < len(chunks) else None
            m_new = jnp.maximum(m[g], jnp.max(s[g], axis=0, keepdims=True))
            alpha = jnp.exp2(m[g] - m_new)
            p = jnp.exp2(s[g] - m_new)
            l[g] = alpha * l[g] + jnp.sum(p, axis=0, keepdims=True)
            acc[g] = alpha * acc[g] + _dot_tn(vr[c * tk:(c + 1) * tk, g * dv:(g + 1) * dv], p.astype(bf16))
            m[g], s[g] = m_new, s_next
    for g in range(heads):
        o_ref[:, g * dv:(g + 1) * dv] = (acc[g] / l[g]).T.astype(o_ref.dtype)


def _flash(q, k, v, *, batch, heads, t_len, s_len, row0, dq, dv, tq, tk, heads_per_step=1, extra=None, name):
    assert row0 % tq == 0 and row0 % s_len == 0 and t_len % tq == 0 and s_len % tk == 0
    assert heads % heads_per_step == 0
    nq = t_len // tq
    g = heads_per_step
    in_specs = [
        pl.BlockSpec((tq, g * dq), lambda b, h, i: (row0 // tq + b * nq + i, h)),
        pl.BlockSpec((s_len, g * dq), lambda b, h, i: (row0 // s_len + b, h)),
        pl.BlockSpec((s_len, g * dv), lambda b, h, i: (row0 // s_len + b, h)),
    ]
    args = [q, k, v]
    n_extra = 0
    if extra is not None:
        kx, vx, p_len = extra
        assert p_len % tk == 0
        n_extra = p_len // tk
        in_specs += [
            pl.BlockSpec((p_len, g * dq), lambda b, h, i: (b, h)),
            pl.BlockSpec((p_len, g * dv), lambda b, h, i: (b, h)),
        ]
        args += [kx, vx]
    return pl.pallas_call(
        functools.partial(_flash_kernel, n_main=s_len // tk, tk=tk, n_extra=n_extra, heads=g, dq=dq, dv=dv),
        grid=(batch, heads // g, nq),
        in_specs=in_specs,
        out_specs=pl.BlockSpec((tq, g * dv), lambda b, h, i: (b * nq + i, h)),
        out_shape=jax.ShapeDtypeStruct((batch * t_len, heads * dv), bf16),
        compiler_params=_params(("arbitrary", "arbitrary", "arbitrary")),
        name=name,
    )(*args)


def _na_prep_kernel(*refs, scale, n_prev, n_ctx_tiles, s_len):
    nq_ref, nk_ref, nv_ref, qg_ref, kg_ref = refs[:5]
    q_ref, k_ref, v_ref, k_state_ref, v_state_ref = refs[5 + n_prev:]
    qg = qg_ref[...]
    kg = kg_ref[...]
    tm = nv_ref.shape[0]
    is_ctx = pl.program_id(0) < n_ctx_tiles
    v_ref[...] = nv_ref[...].astype(bf16)

    @pl.when(is_ctx)
    def _():
        for n in range(tm // s_len):
            v_state_ref[n] = nv_ref[n * s_len:(n + 1) * s_len, :]

    for h in range(NA_HEADS):
        sl = slice(h * NA_DH, (h + 1) * NA_DH)
        q = nq_ref[:, sl]
        k = nk_ref[:, sl]
        rq = lax.rsqrt(jnp.mean(q * q, axis=-1, keepdims=True) + EPS)
        rk = lax.rsqrt(jnp.mean(k * k, axis=-1, keepdims=True) + EPS)
        q_ref[:, sl] = (q * rq * qg * scale).astype(bf16)
        kn = k * rk * kg
        k_ref[:, sl] = kn.astype(bf16)

        @pl.when(is_ctx)
        def _():
            for n in range(tm // s_len):
                k_state_ref[n, :, sl] = kn[n * s_len:(n + 1) * s_len]


def _na_prep(z, lay, q_g, k_g, tm, state):
    m = z.shape[0]
    w = NA_HEADS * NA_DH
    spec = pl.BlockSpec((tm, w), lambda i: (i, 0))
    in_specs = [
        pl.BlockSpec((tm, w), lambda i: (i, lay.nq // w)),
        pl.BlockSpec((tm, w), lambda i: (i, lay.nk // w)),
        pl.BlockSpec((tm, w), lambda i: (i, lay.nv // w)),
        pl.BlockSpec((1, NA_DH), lambda i: (0, 0)),
        pl.BlockSpec((1, NA_DH), lambda i: (0, 0)),
    ]
    args = [z, z, z, q_g.reshape(1, NA_DH), k_g.reshape(1, NA_DH)]
    return pl.pallas_call(
        functools.partial(_na_prep_kernel, scale=NA_DH ** -0.5 * LOG2E, **state.kernel_kwargs()),
        grid=(m // tm,),
        in_specs=in_specs + state.in_specs,
        out_specs=[spec, spec, spec] + state.out_specs,
        out_shape=[jax.ShapeDtypeStruct((m, w), bf16)] * 3 + state.out_shape,
        input_output_aliases=state.aliases(len(args), 3),
        compiler_params=_params(("arbitrary",)),
        name="na_prep",
    )(*args, *state.prevs)


def _na_static_maps(rows):
    nblk = rows // NA_QROWS
    reps = (0, min(1, nblk - 1), nblk - 1)
    dr_map = np.full((3, NA_QROWS, NA_KROWS), 2 * NA_WIN_R - 1, np.int32)
    for v, kb in enumerate(reps):
        ws = min(max(NA_QROWS * kb - NA_WIN_R // 2, 0), rows - NA_KROWS)
        for i in range(NA_QROWS):
            qrow = NA_QROWS * kb + i
            rs = min(max(qrow - NA_WIN_R // 2, 0), rows - NA_WIN_R)
            for j in range(NA_KROWS):
                krow = ws + j
                if rs <= krow < rs + NA_WIN_R:
                    dr_map[v, i, j] = krow - qrow + NA_WIN_R - 1
    qcol = np.arange(GRID_W)
    cs = np.clip(qcol - NA_WIN_C // 2, 0, GRID_W - NA_WIN_C)
    in_win = (qcol[None, :] >= cs[:, None]) & (qcol[None, :] < cs[:, None] + NA_WIN_C)
    dc_idx = np.clip(qcol[None, :] - qcol[:, None], -(NA_WIN_C - 1), NA_WIN_C - 1) + (NA_WIN_C - 1)
    return dr_map, in_win, dc_idx


def _na_bias_table(rpb, rows):
    dr_map, in_win, dc_idx = _na_static_maps(rows)
    depth, h = rpb.shape[:2]
    bt = jnp.where(in_win.T, jnp.take(rpb * LOG2E, jnp.asarray(dc_idx.T), axis=3), NEG)
    bt = jnp.concatenate([bt, jnp.full((depth, h, 1, GRID_W, GRID_W), NEG, f32)], axis=2)
    tab = jnp.take(bt, jnp.asarray(dr_map.transpose(0, 2, 1).reshape(-1)), axis=2)
    tab = tab.reshape(depth, h, 3, NA_KROWS, NA_QROWS, GRID_W, GRID_W).transpose(0, 2, 1, 3, 5, 4, 6)
    return tab.reshape(depth, 3, h, NA_KROWS * GRID_W, NA_QROWS * GRID_W)


def _na_kernel(q_ref, k_ref, v_ref, kc_ref, vc_ref, bias_ref, o_ref, *, rows, heads):
    kb = pl.program_id(2)
    ws = jnp.clip(NA_QROWS * kb - NA_WIN_R // 2, 0, rows - NA_KROWS)
    win = pl.ds(pl.multiple_of(ws * GRID_W, (NA_WIN_R // 2) * GRID_W), NA_KROWS * GRID_W)
    cols = [slice(g * NA_DH, (g + 1) * NA_DH) for g in range(heads)]
    s_loc = [_dot_nt(k_ref[win, c], q_ref[:, c]) + bias_ref[g] for g, c in enumerate(cols)]
    s_ctx = [_dot_nt(kc_ref[:, c], q_ref[:, c]) for c in cols]
    p_loc, p_ctx, l = [], [], []
    for g in range(heads):
        m = jnp.maximum(jnp.max(s_loc[g], axis=0, keepdims=True), jnp.max(s_ctx[g], axis=0, keepdims=True))
        pl_g = jnp.exp2(s_loc[g] - m)
        pc_g = jnp.exp2(s_ctx[g] - m)
        l.append(jnp.sum(pl_g, axis=0, keepdims=True) + jnp.sum(pc_g, axis=0, keepdims=True))
        p_loc.append(pl_g.astype(bf16))
        p_ctx.append(pc_g.astype(bf16))
    for g, c in enumerate(cols):
        acc = _dot_tn(v_ref[win, c], p_loc[g]) + _dot_tn(vc_ref[:, c], p_ctx[g])
        o_ref[:, c] = (acc / l[g]).T.astype(o_ref.dtype)


def _na_latent(q, k, v, kc, vc, bias, layer, *, batch, t_len, p_len, row0, heads_per_step=2):
    rows = t_len // GRID_W
    assert t_len % GRID_W == 0 and rows % NA_QROWS == 0 and rows >= NA_KROWS
    tq = NA_QROWS * GRID_W
    assert row0 % tq == 0 and row0 % t_len == 0 and NA_HEADS % heads_per_step == 0
    nblk = rows // NA_QROWS
    g = heads_per_step
    gw = g * NA_DH

    def variant(i):
        return jnp.where(i == 0, 0, jnp.where(i == nblk - 1, 2, 1))

    return pl.pallas_call(
        functools.partial(_na_kernel, rows=rows, heads=g),
        grid=(batch, NA_HEADS // g, nblk),
        in_specs=[
            pl.BlockSpec((tq, gw), lambda b, h, i: (row0 // tq + b * nblk + i, h)),
            pl.BlockSpec((t_len, gw), lambda b, h, i: (row0 // t_len + b, h)),
            pl.BlockSpec((t_len, gw), lambda b, h, i: (row0 // t_len + b, h)),
            pl.BlockSpec((None, None, p_len, gw), lambda b, h, i: (b, layer, 0, h)),
            pl.BlockSpec((None, None, p_len, gw), lambda b, h, i: (b, layer, 0, h)),
            pl.BlockSpec((None, None, g, NA_KROWS * GRID_W, tq), lambda b, h, i: (layer, variant(i), h, 0, 0)),
        ],
        out_specs=pl.BlockSpec((tq, gw), lambda b, h, i: (b * nblk + i, h)),
        out_shape=jax.ShapeDtypeStruct((batch * t_len, NA_HEADS * NA_DH), bf16),
        compiler_params=_params(("arbitrary", "arbitrary", "arbitrary")),
        name="na_latent",
    )(q, k, v, kc, vc, bias)


def _hgrn_tables():
    c, lv = HG_CHUNK, HG_LEVELS
    t = np.arange(c)[:, None]
    r = np.arange(c)[None, :]
    wcum = np.stack([r <= t, r >= t]).astype(np.float32)
    qside = np.zeros((2, lv * c, LANES), np.float32)
    mask = np.zeros((2, lv + 1, c, c), np.float32)
    mask[:, 0] = np.eye(c)
    for l in range(lv):
        hs = 1 << l
        blk = t // (2 * hs)
        mid = blk * (2 * hs) + hs
        qf = t >= mid
        qb = t < mid
        qside[0, l * c:(l + 1) * c] = np.where(qf, 1.0, -1.0)
        qside[1, l * c:(l + 1) * c] = np.where(qb, 1.0, -1.0)
        same = blk == blk.T
        mask[0, 1 + l] = same & qf & ~qf.T
        mask[1, 1 + l] = same & qb & ~qb.T
    return wcum, qside, mask


def _hgrn_kernel(*refs, n_chunks, heads, has_state, emit_state, n_prev=0):
    (hq_ref, hff_ref, hfb_ref, hi_ref, hg_ref, la_ref, lc_ref, om_ref, ng_ref,
     wcum_ref, qside_ref, mask_ref) = refs[:12]
    rest = list(refs[12:])
    s0_ref = rest.pop(0) if has_state else None
    del rest[:n_prev]
    o_ref = rest.pop(0)
    st_ref = rest.pop(0) if emit_state else None
    o_scr, s_scr, b_scr = rest
    c, lv = HG_CHUNK, HG_LEVELS

    o_scr[...] = jnp.zeros_like(o_scr)
    for d in range(2):
        for g in range(heads):
            if has_state:
                s_scr[d, g] = s0_ref[d, g].T
            else:
                s_scr[d, g] = jnp.zeros((HG_V, HG_K), f32)

    chains = [(d, g) for g in range(heads) for d in range(2)]
    low_half = lax.broadcasted_iota(jnp.int32, (SUBLANES, HEAD_W), 0) < SUBLANES // 2

    def gates_and_cumsum(ci, row):
        d, g = chains[ci]
        rs = pl.ds(pl.multiple_of(row, c), c)
        cs = slice(g * HEAD_W, (g + 1) * HEAD_W)
        x = (hff_ref if d == 0 else hfb_ref)[rs, cs]
        x2 = x * LOG2E
        u = jnp.exp2(-jnp.abs(x2))
        inv = 1.0 / (1.0 + u)
        pos = x >= 0.0
        log_sig = jnp.minimum(x2, 0.0) - jnp.log2(1.0 + u)
        a = la_ref[d:d + 1, cs]
        bb = lc_ref[d:d + 1, cs] + log_sig
        lf = jnp.maximum(a, bb) + jnp.log2(1.0 + jnp.exp2(-jnp.abs(a - bb)))
        om = om_ref[d:d + 1, cs]
        f = (1.0 - om) + om * (jnp.where(pos, 1.0, u) * inv)
        kin = om * (jnp.where(pos, u, 1.0) * inv)
        q = _silu(hq_ref[rs, cs])
        v = hi_ref[rs, cs].astype(bf16)
        l1 = lf.astype(bf16)
        r1 = lf - l1.astype(f32)
        l2 = r1.astype(bf16)
        l3 = (r1 - l2.astype(f32)).astype(bf16)
        w = wcum_ref[d]
        b = _dot(w, l1) + _dot(w, l2) + _dot(w, l3)
        b_scr[ci] = b
        return dict(rs=rs, cs=cs, f=f, kin=kin, q=q, v=v, b=b)

    def intra_chunk(ci, s):
        d, _ = chains[ci]
        b, q, kin = s["b"], s["q"], s["kin"]

        def row(r):
            return b_scr[ci, r:r + 1, :]

        s["b_end"] = row(c - 1 if d == 0 else 0)
        xs = [jnp.where(qside_ref[d, 0:c, :] > 0.0, q * s["f"], kin).astype(bf16)]
        for l in range(1, lv):
            hs = 1 << l
            blk = 2 * hs
            pick = hs - 1 if d == 0 else hs
            if hs >= SUBLANES:
                pieces = []
                for j in range(c // blk):
                    m = row(blk * j + pick)
                    early = slice(blk * j, blk * j + hs)
                    late = slice(blk * j + hs, blk * (j + 1))
                    q_sl, k_sl = (late, early) if d == 0 else (early, late)
                    xq = q[q_sl] * jnp.exp2(b[q_sl] - m)
                    xk = kin[k_sl] * jnp.exp2(m - b[k_sl])
                    pieces += [xk, xq] if d == 0 else [xq, xk]
                xs.append(jnp.concatenate(pieces, axis=0).astype(bf16))
                continue
            if blk < SUBLANES:
                pieces = [jnp.where(low_half, jnp.broadcast_to(row(SUBLANES * j + pick), (SUBLANES, HEAD_W)),
                                    jnp.broadcast_to(row(SUBLANES * j + blk + pick), (SUBLANES, HEAD_W)))
                          for j in range(c // SUBLANES)]
            else:
                pieces = [jnp.broadcast_to(row(blk * j + pick), (blk, HEAD_W)) for j in range(c // blk)]
            m = pieces[0] if len(pieces) == 1 else jnp.concatenate(pieces, axis=0)
            sign = qside_ref[d, l * c:(l + 1) * c, :]
            xs.append((jnp.where(sign > 0.0, q, kin) * jnp.exp2((b - m) * sign)).astype(bf16))
        att = mask_ref[d, 0] * _dot_nt(q.astype(bf16), kin.astype(bf16)).astype(bf16)
        for l in range(lv):
            att = att + mask_ref[d, 1 + l] * _dot_nt(xs[l], xs[l]).astype(bf16)
        s["att"] = att

    def state_step(ci, s):
        d, g = chains[ci]
        b, q, kin, v = s["b"], s["q"], s["kin"], s["v"]
        st = s_scr[d, g]
        o = _dot(s["att"], v) + _dot_nt((q * jnp.exp2(b)).astype(bf16), st.astype(bf16))
        b_end = s["b_end"]
        s_scr[d, g] = st * jnp.exp2(b_end) + _dot_tn(v, (kin * jnp.exp2(b_end - b)).astype(bf16))
        o_scr[s["rs"], s["cs"]] += o

    def body(i, _):
        rows = (i * c, (n_chunks - 1 - i) * c)
        states = [gates_and_cumsum(ci, rows[chains[ci][0]]) for ci in range(len(chains))]
        for ci, s in enumerate(states):
            intra_chunk(ci, s)
        for ci, s in enumerate(states):
            state_step(ci, s)
        return 0

    lax.fori_loop(0, n_chunks, body, 0)

    ng = ng_ref[...]
    for g in range(heads):
        cs = slice(g * HEAD_W, (g + 1) * HEAD_W)
        o = o_scr[:, cs]
        y = o * lax.rsqrt(jnp.mean(o * o, axis=-1, keepdims=True) + EPS) * ng
        o_ref[:, cs] = (y * _silu(hg_ref[:, cs])).astype(o_ref.dtype)
        if emit_state:
            for d in range(2):
                st_ref[d, g] = s_scr[d, g].T


def _hgrn(z, lay, lb_tabs, norm_g, tables, *, batch, t_len, row0, s0, state_out=None, heads_per_step=2):
    emit_state = state_out is not None
    assert t_len % HG_CHUNK == 0 and row0 % t_len == 0 and HG_K == HEAD_W and HG_V == HEAD_W
    g = heads_per_step
    gw = g * HEAD_W
    la, lc, om = lb_tabs
    wcum, qside, mask = tables
    r0 = row0 // t_len

    def zspec(col):
        return pl.BlockSpec((t_len, gw), lambda b, h: (r0 + b, col // gw + h))

    def const(shape):
        return pl.BlockSpec(shape, lambda b, h: (0,) * len(shape))

    in_specs = [zspec(lay.hq), zspec(lay.hff), zspec(lay.hfb), zspec(lay.hi), zspec(lay.hg),
                pl.BlockSpec((2, gw), lambda b, h: (0, h)), pl.BlockSpec((2, gw), lambda b, h: (0, h)),
                pl.BlockSpec((2, gw), lambda b, h: (0, h)), const((1, HG_V)),
                const(wcum.shape), const(qside.shape), const(mask.shape)]
    args = [z, z, z, z, z, la, lc, om, norm_g.reshape(1, HG_V), wcum, qside, mask]
    state_spec = pl.BlockSpec((None, 2, g, HG_K, HG_V), lambda b, h: (b, 0, h, 0, 0))
    if s0 is not None:
        in_specs.append(state_spec)
        args.append(s0)
    out_specs = [pl.BlockSpec((t_len, gw), lambda b, h: (b, h))]
    out_shape = [jax.ShapeDtypeStruct((batch * t_len, HG_HEADS * HG_V), bf16)]
    aliases, n_prev = {}, 0
    if emit_state:
        prev, layer, depth = state_out
        aliases, n_prev = {len(args): 1}, 1
        in_specs.append(pl.BlockSpec(memory_space=pl.ANY))
        args.append(prev)
        out_specs.append(pl.BlockSpec((None, None, 2, g, HG_K, HG_V), lambda b, h: (b, layer, 0, h, 0, 0)))
        out_shape.append(jax.ShapeDtypeStruct((batch, depth, 2, HG_HEADS, HG_K, HG_V), f32))
    res = pl.pallas_call(
        functools.partial(_hgrn_kernel, n_chunks=t_len // HG_CHUNK, heads=g, has_state=s0 is not None,
                          emit_state=emit_state, n_prev=n_prev),
        grid=(batch, HG_HEADS // g),
        in_specs=in_specs,
        out_specs=out_specs,
        out_shape=out_shape,
        input_output_aliases=aliases,
        scratch_shapes=[pltpu.VMEM((t_len, gw), f32), pltpu.VMEM((2, g, HG_V, HG_K), f32),
                        pltpu.VMEM((2 * g, HG_CHUNK, HEAD_W), f32)],
        compiler_params=_params(("arbitrary", "arbitrary")),
        name="hgrn_ctx" if emit_state else "hgrn_latent",
    )(*args)
    return res if emit_state else (res[0], None)


def _merge_kernel(oac_ref, oal_ref, obc_ref, obl_ref, occ_ref, ocl_ref, ga_ref, gb_ref, gc_ref, w_ref, o_ref,
                  *, n_ctx_tiles):
    def compute(oa_ref, ob_ref, oc_ref):
        m = jax.nn.sigmoid(ga_ref[...]) * _dot(oa_ref[...], w_ref[0])
        m = m + jax.nn.sigmoid(gb_ref[...]) * _dot(ob_ref[...], w_ref[1])
        m = m + jax.nn.sigmoid(gc_ref[...]) * _dot(oc_ref[...], w_ref[2])
        o_ref[...] = m.astype(o_ref.dtype)

    is_ctx = pl.program_id(1) < n_ctx_tiles

    @pl.when(is_ctx)
    def _():
        compute(oac_ref, obc_ref, occ_ref)

    @pl.when(jnp.logical_not(is_ctx))
    def _():
        compute(oal_ref, obl_ref, ocl_ref)


def _merge(o_a, o_b, o_c, z, lay, w_branch, layer, tm, tn):
    bw = o_a[0].shape[1]
    m = z.shape[0]
    d = w_branch.shape[3]
    nct = o_a[0].shape[0] // tm
    cspec = pl.BlockSpec((tm, bw), lambda j, i: (jnp.minimum(i, nct - 1), 0))
    lspec = pl.BlockSpec((tm, bw), lambda j, i: (jnp.maximum(i - nct, 0), 0))

    def gate(col):
        return pl.BlockSpec((tm, tn), lambda j, i: (i, col // tn + j))

    return pl.pallas_call(
        functools.partial(_merge_kernel, n_ctx_tiles=nct),
        grid=(d // tn, m // tm),
        in_specs=[cspec, lspec, cspec, lspec, cspec, lspec, gate(lay.ga), gate(lay.gb), gate(lay.gc),
                  pl.BlockSpec((None, 3, bw, tn), lambda j, i: (layer, 0, 0, j))],
        out_specs=pl.BlockSpec((tm, tn), lambda j, i: (i, j)),
        out_shape=jax.ShapeDtypeStruct((m, d), bf16),
        compiler_params=_params(("arbitrary", "arbitrary")),
        name="merge",
    )(o_a[0], o_a[1], o_b[0], o_b[1], o_c[0], o_c[1], z, z, z, w_branch)


def _proj_residual_kernel(*refs, n_x, n_first):
    m_ref, w_ref = refs[:2]
    x_refs = refs[2:2 + n_x]
    g_ref, o_ref = refs[2 + n_x:]

    def project(x_ref):
        o_ref[...] = x_ref[...] + g_ref[...] * _dot(m_ref[...], w_ref[...])

    _for_owner(x_refs, n_first, project)


def _proj_residual(mm, w, layer, xs, gate, group_of_tile, tm, tn):
    m, k = mm.shape
    d = w.shape[2]
    x_specs, n_first = _row_group_specs(xs, tm, tn, col=lambda j: j)
    return pl.pallas_call(
        functools.partial(_proj_residual_kernel, n_x=len(xs), n_first=n_first),
        grid=(m // tm, d // tn),
        in_specs=[
            pl.BlockSpec((tm, k), lambda i, j: (i, 0)),
            pl.BlockSpec((None, k, tn), lambda i, j: (layer, 0, j)),
        ] + x_specs + [
            pl.BlockSpec((None, 1, tn), lambda i, j: (group_of_tile(i), 0, j)),
        ],
        out_specs=pl.BlockSpec((tm, tn), lambda i, j: (i, j)),
        out_shape=jax.ShapeDtypeStruct((m, d), f32),
        compiler_params=_params(("arbitrary", "arbitrary")),
        name="out_proj",
    )(mm, w, *xs, gate)


def _ffn_kernel(x_ref, xprev_ref, xnext_ref, ng_ref, sc_ref, sh_ref, wa_ref, wg_ref, cw_ref, cb_ref, wd_ref, gate_ref,
                o_ref, h_scr, *, m_ctx, s_len, t_len, n_k):
    i = pl.program_id(0)
    k = pl.program_id(1)
    tm = x_ref.shape[0]
    tk = wa_ref.shape[1]
    halo = SUBLANES

    @pl.when(k == 0)
    def _():
        def normmod(x):
            ms = jnp.mean(x * x, axis=-1, keepdims=True)
            y = x * lax.rsqrt(ms + EPS) * ng_ref[...]
            return (y * (1.0 + sc_ref[...]) + sh_ref[...]).astype(bf16)

        h_scr[0:halo] = normmod(xprev_ref[...])
        h_scr[halo:halo + tm] = normmod(x_ref[...])
        h_scr[halo + tm:halo + tm + halo] = normmod(xnext_ref[...])
        o_ref[...] = jnp.zeros_like(o_ref)

    row = i * tm + lax.broadcasted_iota(jnp.int32, (tm, LANES), 0)
    pos = jnp.where(row < m_ctx, _mod_const(row, s_len), _mod_const(row - m_ctx, t_len))
    last = jnp.where(row < m_ctx, s_len - 1, t_len - 1)
    keep_prev = jnp.tile(jnp.where(pos == 0, 0.0, 1.0), (1, tk // LANES))
    keep_next = jnp.tile(jnp.where(pos == last, 0.0, 1.0), (1, tk // LANES))

    g_ext = _dot(h_scr[...], wg_ref[...])
    a = _dot(h_scr[halo:halo + tm], wa_ref[...])
    n_ext = tm + 2 * halo
    g = g_ext[halo:halo + tm]
    g_prev = pltpu.roll(g_ext, 1, 0)[halo:halo + tm]
    g_next = pltpu.roll(g_ext, n_ext - 1, 0)[halo:halo + tm]
    conv = (g_prev * keep_prev * cw_ref[0:1, :] + g * cw_ref[1:2, :] + g_next * keep_next * cw_ref[2:3, :]
            + cb_ref[...])
    u = (_silu(conv) * a).astype(bf16)
    o_ref[...] = _dot(u, wd_ref[...]) + o_ref[...]

    @pl.when(k == n_k - 1)
    def _():
        o_ref[...] = x_ref[...] + gate_ref[...] * o_ref[...]


def _ffn(x, norm_g, sc, sh, w_up, conv_w, conv_b, w_down, layer, gate, group_of_tile, tm, tk, m_ctx, s_len, t_len):
    m, d = x.shape
    f = w_down.shape[1]
    nk = f // tk
    nsub = tm // SUBLANES
    last_sub = m // SUBLANES - 1
    return pl.pallas_call(
        functools.partial(_ffn_kernel, m_ctx=m_ctx, s_len=s_len, t_len=t_len, n_k=nk),
        grid=(m // tm, nk),
        in_specs=[
            pl.BlockSpec((tm, d), lambda i, k: (i, 0), pipeline_mode=pl.Buffered(1)),
            pl.BlockSpec((SUBLANES, d), lambda i, k: (jnp.maximum(i * nsub - 1, 0), 0)),
            pl.BlockSpec((SUBLANES, d), lambda i, k: (jnp.minimum((i + 1) * nsub, last_sub), 0)),
            pl.BlockSpec((1, d), lambda i, k: (0, 0)),
            pl.BlockSpec((None, 1, d), lambda i, k: (group_of_tile(i), 0, 0)),
            pl.BlockSpec((None, 1, d), lambda i, k: (group_of_tile(i), 0, 0)),
            pl.BlockSpec((None, d, tk), lambda i, k: (layer, 0, k)),
            pl.BlockSpec((None, d, tk), lambda i, k: (layer, 0, nk + k)),
            pl.BlockSpec((3, tk), lambda i, k: (0, k)),
            pl.BlockSpec((1, tk), lambda i, k: (0, k)),
            pl.BlockSpec((None, tk, d), lambda i, k: (layer, k, 0)),
            pl.BlockSpec((None, 1, d), lambda i, k: (group_of_tile(i), 0, 0)),
        ],
        out_specs=pl.BlockSpec((tm, d), lambda i, k: (i, 0), pipeline_mode=pl.Buffered(1)),
        out_shape=jax.ShapeDtypeStruct((m, d), f32),
        scratch_shapes=[pltpu.VMEM((tm + 2 * SUBLANES, d), bf16)],
        compiler_params=_params(("arbitrary", "arbitrary")),
        name="ffn",
    )(x, x, x, norm_g.reshape(1, d), sc, sh, w_up, w_up, conv_w, conv_b.reshape(1, f), w_down, gate)


def kernel(x_prompt, x_sample, cache_mla_ckv, cache_mla_kpe, cache_na_k, cache_na_v, state_hgrn, c, c_ctx, w_mod, b_mod, norm1_g, norm2_g, w_in, mla_kv_norm_g, mla_q_norm_g, mla_k_norm_g, mla_w_uk, mla_w_uv, hgrn_lower_bounds, hgrn_norm_g, na_q_norm_g, na_k_norm_g, na_rpb, w_branch, w_out, ffn_w_up, ffn_conv_w, ffn_conv_b, ffn_w_down):
    n_ctx, s_len, d = x_prompt.shape
    n_lat, t_len, _ = x_sample.shape
    depth = w_in.shape[0]
    p_len = cache_mla_ckv.shape[2]
    ffn = ffn_w_down.shape[1]
    m_ctx, m_lat = n_ctx * s_len, n_lat * t_len
    m = m_ctx + m_lat
    assert m_ctx % t_len == 0, "context rows must be a whole number of latent sequences"

    tm = _pow2_tile(1024, m_ctx, t_len)
    tn = _pow2_tile(1024, d)
    tp = _pow2_tile(512, m_ctx, t_len)
    lay = _ZLayout(d, tn)

    def group_fn(rows):
        return lambda i: jnp.where(i < m_ctx // rows, 0, 1 + (i - m_ctx // rows) // (t_len // rows))

    group_of_tile = group_fn(tm)

    n_groups = -(-(1 + n_lat) // SUBLANES) * SUBLANES
    cond = jnp.concatenate([c_ctx[None], c, jnp.zeros((n_groups - 1 - n_lat, d), f32)], axis=0)
    mods = _modulation(cond, w_mod, b_mod).reshape(depth, n_groups, 6, d).transpose(0, 2, 1, 3)[:, :, :, None, :]

    sm = jax.nn.softmax(hgrn_lower_bounds.astype(f32), axis=1)
    csum = jnp.cumsum(sm, axis=1)
    lower = csum - csum[:, :1]
    hg_tabs = tuple(jnp.asarray(t, dt) for t, dt in zip(_hgrn_tables(), (bf16, f32, bf16)))

    cos, sin = _rope_tables(t_len, tp)
    n_ctx_tp = m_ctx // tp

    def rope_block(i):
        return jnp.where(i < n_ctx_tp, 0, 1 + (i - n_ctx_tp) % (t_len // tp))

    xs = (x_prompt.reshape(m_ctx, d), x_sample.reshape(m_lat, d))
    w_in_b, w_kpe_b = _split_w_in(w_in, lay)
    w_uk = mla_w_uk.reshape(depth, MLA_KV_RANK, MLA_HEADS * MLA_NOPE).astype(bf16)
    w_uv = mla_w_uv.reshape(depth, MLA_KV_RANK, MLA_HEADS * MLA_V).astype(bf16)
    w_branch_b, w_out_b = w_branch.astype(bf16), w_out.astype(bf16)
    w_up_b, w_down_b = ffn_w_up.astype(bf16), ffn_w_down.astype(bf16)
    na_bias = _na_bias_table(na_rpb, t_len // GRID_W)
    kc_na = cache_na_k.reshape(n_lat, depth, p_len, NA_HEADS * NA_DH).astype(bf16)
    vc_na = cache_na_v.reshape(n_lat, depth, p_len, NA_HEADS * NA_DH).astype(bf16)

    def state_init(*trailing):
        return jnp.zeros((n_ctx, depth) + trailing, f32)

    st_mla = [state_init(s_len, MLA_KV_RANK), state_init(s_len, MLA_ROPE)]
    st_na = [state_init(s_len, NA_HEADS * NA_DH), state_init(s_len, NA_HEADS * NA_DH)]
    st_hg = state_init(2, HG_HEADS, HG_K, HG_V)
    for l in range(depth):
        sh1, sc1, g1, sh2, sc2, g2 = (mods[l, k] for k in range(6))
        z, kpe = _normmod_matmul(xs, norm1_g[l], sc1, sh1, w_in_b, l, group_of_tile, tm, tn, "in_proj", w_side=w_kpe_b)

        q_mla = _mla_q(z, lay, mla_q_norm_g[l], cos, sin, rope_block, tp)
        k_mla, v_mla, *st_mla = _mla_kv(
            z, lay.ckv, kpe, 0, mla_kv_norm_g[l], mla_k_norm_g[l], w_uk, w_uv, l, cos, sin, rope_block, tp, True, True,
            "mla_kv", state=_CtxState(st_mla, (MLA_KV_RANK, MLA_ROPE), l, depth, n_ctx, s_len, tp))
        ckv_c = cache_mla_ckv[:, l].reshape(n_lat * p_len, MLA_KV_RANK)
        kpe_c = jnp.pad(cache_mla_kpe[:, l].reshape(n_lat * p_len, MLA_ROPE), ((0, 0), (0, LANES - MLA_ROPE)))
        tc = _pow2_tile(512, n_lat * p_len)
        k_c, v_c = _mla_kv(ckv_c, 0, kpe_c, 0, mla_kv_norm_g[l], mla_k_norm_g[l], w_uk, w_uv, l,
                           cos, sin, lambda i: 0, tc, False, False, "mla_kv_cache")
        tq_c = _pow2_tile(256, s_len)
        o_mla_c = _flash(q_mla, k_mla, v_mla, batch=n_ctx, heads=MLA_HEADS, t_len=s_len, s_len=s_len, row0=0,
                         dq=MLA_QPAD, dv=MLA_V, tq=tq_c, tk=tq_c, heads_per_step=MLA_HEADS, name="mla_attn_ctx")
        tq_l = _pow2_tile(512, t_len, p_len)
        o_mla_l = _flash(q_mla, k_mla, v_mla, batch=n_lat, heads=MLA_HEADS, t_len=t_len, s_len=t_len, row0=m_ctx,
                         dq=MLA_QPAD, dv=MLA_V, tq=tq_l, tk=tq_l, heads_per_step=2, extra=(k_c, v_c, p_len),
                         name="mla_attn_latent")

        lb = lower[:, l]
        lb_tabs = (jnp.log(lb) * LOG2E, jnp.log1p(-lb) * LOG2E, 1.0 - lb)
        o_hg_c, st_hg = _hgrn(z, lay, lb_tabs, hgrn_norm_g[l], hg_tabs, batch=n_ctx, t_len=s_len, row0=0,
                              s0=None, state_out=(st_hg, l, depth), heads_per_step=4)
        o_hg_l, _ = _hgrn(z, lay, lb_tabs, hgrn_norm_g[l], hg_tabs, batch=n_lat, t_len=t_len, row0=m_ctx,
                          s0=state_hgrn[:, l])

        q_na, k_na, v_na, *st_na = _na_prep(
            z, lay, na_q_norm_g[l], na_k_norm_g[l], tp,
            _CtxState(st_na, (NA_HEADS * NA_DH, NA_HEADS * NA_DH), l, depth, n_ctx, s_len, tp))
        o_na_c = _flash(q_na, k_na, v_na, batch=n_ctx, heads=NA_HEADS, t_len=s_len, s_len=s_len, row0=0,
                        dq=NA_DH, dv=NA_DH, tq=tq_c, tk=tq_c, heads_per_step=NA_HEADS, name="na_attn_ctx")
        o_na_l = _na_latent(q_na, k_na, v_na, kc_na, vc_na, na_bias, l, batch=n_lat, t_len=t_len, p_len=p_len, row0=m_ctx)

        merged = _merge((o_mla_c, o_mla_l), (o_hg_c, o_hg_l), (o_na_c, o_na_l), z, lay, w_branch_b, l, tp, tn)
        x = _proj_residual(merged, w_out_b, l, xs, g1, group_fn(tp), tp, d)

        x = _ffn(x, norm2_g[l], sc2, sh2, w_up_b, ffn_conv_w[l], ffn_conv_b[l], w_down_b, l, g2, group_of_tile, tm,
                 _pow2_tile(512, ffn), m_ctx, s_len, t_len)
        xs = (x,)

    y_prompt = x[:m_ctx].reshape(n_ctx, s_len, d)
    y_sample = x[m_ctx:].reshape(n_lat, t_len, d)
    na_shape = (n_ctx, depth, s_len, NA_HEADS, NA_DH)
    return (y_prompt, y_sample, st_mla[0], st_mla[1], st_na[0].reshape(na_shape), st_na[1].reshape(na_shape), st_hg)
```

```python
import functools
import math

import numpy as np
import jax
import jax.numpy as jnp
from jax import lax
from jax.experimental import pallas as pl
from jax.experimental.pallas import tpu as pltpu

f32 = jnp.float32
bf16 = jnp.bfloat16

GRID_W = 64
MLA_HEADS, MLA_NOPE, MLA_ROPE, MLA_V, MLA_KV_RANK = 8, 128, 64, 128, 512
MLA_QK = MLA_NOPE + MLA_ROPE
HG_HEADS, HG_K, HG_V = 8, 128, 128
NA_HEADS, NA_DH, NA_WIN_R, NA_WIN_C = 8, 128, 8, 16
ROPE_BASE = 10000.0
EPS = 1e-6
NEG = -1e30
LOG2E = 1.4426950408889634

LANES = 128
SUBLANES = 8
VMEM_LIMIT_BYTES = 56 * 1024 * 1024

HEAD_W = 128
MLA_QPAD = 256
HG_CHUNK = 128
HG_LEVELS = int(math.log2(HG_CHUNK))
NA_QROWS = 8
NA_KROWS = 16


def _params(semantics):
    return pltpu.CompilerParams(dimension_semantics=semantics, vmem_limit_bytes=VMEM_LIMIT_BYTES)


def _pow2_tile(pref, *dims):
    t = pref
    while any(d % t for d in dims):
        t //= 2
    assert t >= SUBLANES, (pref, dims)
    return t


def _dot(a, b):
    return jnp.dot(a, b, preferred_element_type=f32)


def _dot_nt(a, b):
    return lax.dot_general(a, b, (((1,), (1,)), ((), ())), preferred_element_type=f32)


def _dot_tn(a, b):
    return lax.dot_general(a, b, (((0,), (0,)), ((), ())), preferred_element_type=f32)


def _silu(x):
    return x * jax.nn.sigmoid(x)


def _mod_const(x, n):
    return jnp.bitwise_and(x, n - 1) if n & (n - 1) == 0 else lax.rem(x, n)


class _ZLayout:
    def __init__(self, d_model, tn):
        self.mq = 0
        self.ckv = self.mq + MLA_HEADS * MLA_QK
        self.hq = self.ckv + MLA_KV_RANK
        self.hff = self.hq + HG_HEADS * HG_K
        self.hfb = self.hff + HG_HEADS * HG_K
        self.hi = self.hfb + HG_HEADS * HG_K
        self.hg = self.hi + HG_HEADS * HG_V
        self.nq = self.hg + HG_HEADS * HG_V
        self.nk = self.nq + NA_HEADS * NA_DH
        self.nv = self.nk + NA_HEADS * NA_DH
        self.ga = self.nv + NA_HEADS * NA_DH
        self.gb = self.ga + d_model
        self.gc = self.gb + d_model
        self.width = self.gc + d_model
        assert self.width % tn == 0


def _permute_w_in(w, lay):
    n_front = MLA_HEADS * MLA_QK + MLA_KV_RANK
    kpe = w[..., n_front:n_front + MLA_ROPE]
    main = jnp.concatenate([w[..., :n_front], w[..., n_front + MLA_ROPE:]], axis=-1).astype(bf16)
    assert main.shape[-1] == lay.width
    pad = [(0, 0)] * (w.ndim - 1) + [(0, LANES - MLA_ROPE)]
    return main, jnp.pad(kpe, pad).astype(bf16)


def _mod_kernel(c_ref, w_ref, b_ref, o_ref):
    a = _silu(c_ref[...]).astype(bf16)
    o_ref[...] = _dot(a, w_ref[...].astype(bf16)) + b_ref[...]


def _modulation(cond, w_mod, b_mod):
    depth, d, n = w_mod.shape
    g = cond.shape[0]
    tn = _pow2_tile(1024, n)
    return pl.pallas_call(
        _mod_kernel,
        grid=(depth, n // tn),
        in_specs=[
            pl.BlockSpec((g, d), lambda l, j: (0, 0)),
            pl.BlockSpec((None, d, tn), lambda l, j: (l, 0, j)),
            pl.BlockSpec((None, 1, tn), lambda l, j: (l, 0, j)),
        ],
        out_specs=pl.BlockSpec((None, g, tn), lambda l, j: (l, 0, j)),
        out_shape=jax.ShapeDtypeStruct((depth, g, n), f32),
        compiler_params=_params(("arbitrary", "arbitrary")),
        name="modulation",
    )(cond, w_mod, b_mod.reshape(depth, 1, n))


def _normmod_matmul_kernel(x_ref, g_ref, sc_ref, sh_ref, w_ref, *rest):
    if len(rest) == 4:
        ws_ref, o_ref, os_ref, h_scr = rest
    else:
        (o_ref, h_scr), ws_ref, os_ref = rest, None, None

    @pl.when(pl.program_id(1) == 0)
    def _():
        x = x_ref[...]
        ms = jnp.mean(x * x, axis=-1, keepdims=True)
        y = x * lax.rsqrt(ms + EPS) * g_ref[...]
        h_scr[...] = (y * (1.0 + sc_ref[...]) + sh_ref[...]).astype(bf16)
        if ws_ref is not None:
            os_ref[...] = _dot(h_scr[...], ws_ref[...])

    o_ref[...] = _dot(h_scr[...], w_ref[...])


def _normmod_matmul(x, g, sc, sh, w, layer, group_of_tile, tm, tn, name, w_side=None):
    m, d = x.shape
    n = w.shape[2]
    in_specs = [
        pl.BlockSpec((tm, d), lambda i, j: (i, 0)),
        pl.BlockSpec((1, d), lambda i, j: (0, 0)),
        pl.BlockSpec((None, 1, d), lambda i, j: (group_of_tile(i), 0, 0)),
        pl.BlockSpec((None, 1, d), lambda i, j: (group_of_tile(i), 0, 0)),
        pl.BlockSpec((None, d, tn), lambda i, j: (layer, 0, j)),
    ]
    args = [x, g.reshape(1, d), sc, sh, w]
    out_specs = [pl.BlockSpec((tm, tn), lambda i, j: (i, j))]
    out_shape = [jax.ShapeDtypeStruct((m, n), f32)]
    if w_side is not None:
        ns = w_side.shape[2]
        in_specs.append(pl.BlockSpec((None, d, ns), lambda i, j: (layer, 0, 0)))
        args.append(w_side)
        out_specs.append(pl.BlockSpec((tm, ns), lambda i, j: (i, 0)))
        out_shape.append(jax.ShapeDtypeStruct((m, ns), f32))
    res = pl.pallas_call(
        _normmod_matmul_kernel,
        grid=(m // tm, n // tn),
        in_specs=in_specs,
        out_specs=out_specs,
        out_shape=out_shape,
        scratch_shapes=[pltpu.VMEM((tm, d), bf16)],
        compiler_params=_params(("arbitrary", "arbitrary")),
        name=name,
    )(*args)
    return res if w_side is not None else res[0]


def _rope(x, cos, sin_signed):
    n = x.shape[-1]
    half = MLA_ROPE // 4
    lane = lax.broadcasted_iota(jnp.int32, x.shape, 1)
    partner = jnp.where((lane % (2 * half)) < half, pltpu.roll(x, n - half, 1), pltpu.roll(x, half, 1))
    return x * cos + partner * sin_signed


def _mla_q_kernel(mq_ref, gn_ref, gp_ref, cos_ref, sin_ref, o_ref, *, scale):
    tm = mq_ref.shape[0]
    lane = lax.broadcasted_iota(jnp.int32, (tm, LANES), 1)
    low = lane < MLA_ROPE
    cos = cos_ref[...]
    sin = sin_ref[...]
    gn = gn_ref[...]
    gp = gp_ref[...]
    for pair in range(MLA_HEADS // 2):
        t0, t1, t2 = (mq_ref[:, (3 * pair + t) * LANES:(3 * pair + t + 1) * LANES] for t in range(3))
        nopes = (t0, jnp.where(low, pltpu.roll(t1, MLA_ROPE, 1), pltpu.roll(t2, MLA_ROPE, 1)))
        pcol = jnp.where(low, t1, t2)
        sq = pcol * pcol
        ss_all = jnp.sum(sq, axis=-1, keepdims=True)
        ss_low = jnp.sum(jnp.where(low, sq, 0.0), axis=-1, keepdims=True)
        rot = _rope(pcol * gp, cos, sin)
        for k, ss_pe in enumerate((ss_low, ss_all - ss_low)):
            h = 2 * pair + k
            nope = nopes[k]
            ss = jnp.sum(nope * nope, axis=-1, keepdims=True) + ss_pe
            r = lax.rsqrt(ss * (1.0 / MLA_QK) + EPS) * scale
            pe = rot if k == 0 else pltpu.roll(rot, MLA_ROPE, 1)
            o_ref[:, h * MLA_QPAD:h * MLA_QPAD + MLA_NOPE] = (nope * r * gn).astype(bf16)
            o_ref[:, h * MLA_QPAD + MLA_NOPE:(h + 1) * MLA_QPAD] = jnp.where(low, pe * r, 0.0).astype(bf16)


class _CtxState:
    def __init__(self, prevs, widths, layer, depth, n_ctx, s_len, tm):
        assert tm % s_len == 0
        self.n_seq = tm // s_len
        self.s_len = s_len
        self.n_tiles = n_ctx * s_len // tm
        self.prevs = list(prevs)
        n_tiles = self.n_tiles
        self.out_specs = [pl.BlockSpec((self.n_seq, None, s_len, w), lambda i: (jnp.minimum(i, n_tiles - 1), layer, 0, 0))
                          for w in widths]
        self.out_shape = [jax.ShapeDtypeStruct((n_ctx, depth, s_len, w), f32) for w in widths]
        self.in_specs = [pl.BlockSpec(memory_space=pl.ANY)] * len(self.prevs)

    def aliases(self, n_inputs_before, n_outputs_before):
        return {n_inputs_before + k: n_outputs_before + k for k in range(len(self.prevs))}

    def kernel_kwargs(self):
        return dict(n_prev=len(self.prevs), n_ctx_tiles=self.n_tiles, s_len=self.s_len)


def _mla_kv_kernel(*refs, norm_ckv, rope, n_prev=0, n_ctx_tiles=0, s_len=0):
    ckv_ref, kpe_ref, kvg_ref, gn_ref, gp_ref, wuk_ref, wuv_ref, cos_ref, sin_ref = refs[:9]
    k_ref, v_ref, *state_refs = refs[9 + n_prev:]
    ckv = ckv_ref[...]
    if norm_ckv:
        ms = jnp.mean(ckv * ckv, axis=-1, keepdims=True)
        ckv = ckv * lax.rsqrt(ms + EPS) * kvg_ref[...]
    if state_refs:
        ckv_state_ref, kpe_state_ref = state_refs

        @pl.when(pl.program_id(0) < n_ctx_tiles)
        def _():
            for n in range(ckv.shape[0] // s_len):
                ckv_state_ref[n] = ckv[n * s_len:(n + 1) * s_len]
                kpe_state_ref[n] = kpe_ref[n * s_len:(n + 1) * s_len, :MLA_ROPE]

    cb = ckv.astype(bf16)
    kn = _dot(cb, wuk_ref[...])
    v_ref[...] = _dot(cb, wuv_ref[...]).astype(bf16)
    tm = ckv.shape[0]
    lane = lax.broadcasted_iota(jnp.int32, (tm, LANES), 1)
    low = lane < MLA_ROPE
    kpe = jnp.where(low, kpe_ref[...], 0.0)
    ss_pe = jnp.sum(kpe * kpe, axis=-1, keepdims=True)
    pe = kpe * gp_ref[...]
    if rope:
        pe = _rope(pe, cos_ref[...], sin_ref[...])
    gn = gn_ref[...]
    for h in range(MLA_HEADS):
        nope = kn[:, h * MLA_NOPE:(h + 1) * MLA_NOPE]
        ss = jnp.sum(nope * nope, axis=-1, keepdims=True) + ss_pe
        r = lax.rsqrt(ss * (1.0 / MLA_QK) + EPS)
        k_ref[:, h * MLA_QPAD:h * MLA_QPAD + MLA_NOPE] = (nope * r * gn).astype(bf16)
        k_ref[:, h * MLA_QPAD + MLA_NOPE:(h + 1) * MLA_QPAD] = jnp.where(low, pe * r, 0.0).astype(bf16)


def _rope_tables(t_len, tm):
    n_freq = MLA_ROPE // 4
    t = jnp.arange(t_len, dtype=jnp.int32)
    inv = ROPE_BASE ** (-jnp.arange(n_freq, dtype=f32) / n_freq)
    ang_r = (t // GRID_W).astype(f32)[:, None] * inv
    ang_c = (t % GRID_W).astype(f32)[:, None] * inv
    cos64 = jnp.concatenate([jnp.cos(ang_r), jnp.cos(ang_r), jnp.cos(ang_c), jnp.cos(ang_c)], axis=1)
    sin64 = jnp.concatenate([-jnp.sin(ang_r), jnp.sin(ang_r), -jnp.sin(ang_c), jnp.sin(ang_c)], axis=1)
    cos = jnp.concatenate([jnp.ones((tm, LANES), f32), jnp.tile(cos64, (1, 2))], axis=0)
    sin = jnp.concatenate([jnp.zeros((tm, LANES), f32), jnp.tile(sin64, (1, 2))], axis=0)
    return cos, sin


def _mla_q(z, lay, q_g, cos, sin, rope_block, tm):
    m = z.shape[0]
    gn = q_g[:MLA_NOPE].reshape(1, MLA_NOPE)
    gp = jnp.tile(q_g[MLA_NOPE:], 2).reshape(1, LANES)
    wq = MLA_HEADS * MLA_QK
    assert MLA_HEADS % 2 == 0 and 2 * MLA_QK == 3 * LANES and 2 * MLA_ROPE == LANES and lay.mq % wq == 0
    return pl.pallas_call(
        functools.partial(_mla_q_kernel, scale=MLA_QK ** -0.5 * LOG2E),
        grid=(m // tm,),
        in_specs=[
            pl.BlockSpec((tm, wq), lambda i: (i, lay.mq // wq)),
            pl.BlockSpec((1, MLA_NOPE), lambda i: (0, 0)),
            pl.BlockSpec((1, LANES), lambda i: (0, 0)),
            pl.BlockSpec((tm, LANES), lambda i: (rope_block(i), 0)),
            pl.BlockSpec((tm, LANES), lambda i: (rope_block(i), 0)),
        ],
        out_specs=pl.BlockSpec((tm, MLA_HEADS * MLA_QPAD), lambda i: (i, 0)),
        out_shape=jax.ShapeDtypeStruct((m, MLA_HEADS * MLA_QPAD), bf16),
        compiler_params=_params(("arbitrary",)),
        name="mla_q",
    )(z, gn, gp, cos, sin)


def _mla_kv(ckv_src, ckv_col, kpe_src, kpe_col, kv_g, k_g, w_uk, w_uv, layer, cos, sin, rope_block, tm, norm_ckv, rope,
            name, state=None):
    m = ckv_src.shape[0]
    gn = k_g[:MLA_NOPE].reshape(1, MLA_NOPE)
    gp = jnp.concatenate([k_g[MLA_NOPE:], jnp.zeros((LANES - MLA_ROPE,), f32)]).reshape(1, LANES)
    hw = MLA_HEADS * MLA_NOPE
    in_specs = [
        pl.BlockSpec((tm, MLA_KV_RANK), lambda i: (i, ckv_col // MLA_KV_RANK)),
        pl.BlockSpec((tm, LANES), lambda i: (i, kpe_col // LANES)),
        pl.BlockSpec((1, MLA_KV_RANK), lambda i: (0, 0)),
        pl.BlockSpec((1, MLA_NOPE), lambda i: (0, 0)),
        pl.BlockSpec((1, LANES), lambda i: (0, 0)),
        pl.BlockSpec((None, MLA_KV_RANK, hw), lambda i: (layer, 0, 0)),
        pl.BlockSpec((None, MLA_KV_RANK, hw), lambda i: (layer, 0, 0)),
        pl.BlockSpec((tm, LANES), lambda i: (rope_block(i), 0)),
        pl.BlockSpec((tm, LANES), lambda i: (rope_block(i), 0)),
    ]
    args = [ckv_src, kpe_src, kv_g.reshape(1, MLA_KV_RANK), gn, gp, w_uk, w_uv, cos, sin]
    out_specs = [pl.BlockSpec((tm, MLA_HEADS * MLA_QPAD), lambda i: (i, 0)), pl.BlockSpec((tm, hw), lambda i: (i, 0))]
    out_shape = [jax.ShapeDtypeStruct((m, MLA_HEADS * MLA_QPAD), bf16), jax.ShapeDtypeStruct((m, hw), bf16)]
    kwargs, aliases = {}, {}
    if state is not None:
        aliases = state.aliases(len(args), len(out_specs))
        kwargs = state.kernel_kwargs()
        in_specs += state.in_specs
        args += state.prevs
        out_specs += state.out_specs
        out_shape += state.out_shape
    return pl.pallas_call(
        functools.partial(_mla_kv_kernel, norm_ckv=norm_ckv, rope=rope, **kwargs),
        grid=(m // tm,),
        in_specs=in_specs,
        out_specs=out_specs,
        out_shape=out_shape,
        input_output_aliases=aliases,
        compiler_params=_params(("arbitrary",)),
        name=name,
    )(*args)


def _flash_kernel(*refs, n_main, tk, n_extra, heads, dq, dv):
    if n_extra:
        q_ref, k_ref, v_ref, kx_ref, vx_ref, o_ref = refs
    else:
        q_ref, k_ref, v_ref, o_ref = refs
        kx_ref = vx_ref = None
    tq = q_ref.shape[0]
    chunks = [(k_ref, v_ref, c) for c in range(n_main)] + [(kx_ref, vx_ref, c) for c in range(n_extra)]
    def scores(i, g):
        kr, _, c = chunks[i]
        return _dot_nt(kr[c * tk:(c + 1) * tk, g * dq:(g + 1) * dq], q_ref[:, g * dq:(g + 1) * dq])

    m = [jnp.full((1, tq), -jnp.inf, f32)] * heads
    l = [jnp.zeros((1, tq), f32)] * heads
    acc = [jnp.zeros((dv, tq), f32)] * heads
    s = [scores(0, g) for g in range(heads)]
    for i, (_, vr, c) in enumerate(chunks):
        for g in range(heads):
            s_next = scores(i + 1, g) if i + 1 < len(chunks) else None
            m_new = jnp.maximum(m[g], jnp.max(s[g], axis=0, keepdims=True))
            alpha = jnp.exp2(m[g] - m_new)
            p = jnp.exp2(s[g] - m_new)
            l[g] = alpha * l[g] + jnp.sum(p, axis=0, keepdims=True)
            acc[g] = alpha * acc[g] + _dot_tn(vr[c * tk:(c + 1) * tk, g * dv:(g + 1) * dv], p.astype(bf16))
            m[g], s[g] = m_new, s_next
    for g in range(heads):
        o_ref[:, g * dv:(g + 1) * dv] = (acc[g] / l[g]).T.astype(o_ref.dtype)


def _flash(q, k, v, *, batch, heads, t_len, s_len, row0, dq, dv, tq, tk, heads_per_step=1, extra=None, name):
    assert row0 % tq == 0 and row0 % s_len == 0 and t_len % tq == 0 and s_len % tk == 0
    assert heads % heads_per_step == 0
    nq = t_len // tq
    g = heads_per_step
    in_specs = [
        pl.BlockSpec((tq, g * dq), lambda b, h, i: (row0 // tq + b * nq + i, h)),
        pl.BlockSpec((s_len, g * dq), lambda b, h, i: (row0 // s_len + b, h)),
        pl.BlockSpec((s_len, g * dv), lambda b, h, i: (row0 // s_len + b, h)),
    ]
    args = [q, k, v]
    n_extra = 0
    if extra is not None:
        kx, vx, p_len = extra
        assert p_len % tk == 0
        n_extra = p_len // tk
        in_specs += [
            pl.BlockSpec((p_len, g * dq), lambda b, h, i: (b, h)),
            pl.BlockSpec((p_len, g * dv), lambda b, h, i: (b, h)),
        ]
        args += [kx, vx]
    return pl.pallas_call(
        functools.partial(_flash_kernel, n_main=s_len // tk, tk=tk, n_extra=n_extra, heads=g, dq=dq, dv=dv),
        grid=(batch, heads // g, nq),
        in_specs=in_specs,
        out_specs=pl.BlockSpec((tq, g * dv), lambda b, h, i: (b * nq + i, h)),
        out_shape=jax.ShapeDtypeStruct((batch * t_len, heads * dv), bf16),
        compiler_params=_params(("arbitrary", "arbitrary", "arbitrary")),
        name=name,
    )(*args)


def _na_prep_kernel(*refs, scale, n_prev, n_ctx_tiles, s_len):
    nq_ref, nk_ref, nv_ref, qg_ref, kg_ref = refs[:5]
    q_ref, k_ref, v_ref, k_state_ref, v_state_ref = refs[5 + n_prev:]
    qg = qg_ref[...]
    kg = kg_ref[...]
    tm = nv_ref.shape[0]
    is_ctx = pl.program_id(0) < n_ctx_tiles
    v_ref[...] = nv_ref[...].astype(bf16)

    @pl.when(is_ctx)
    def _():
        for n in range(tm // s_len):
            v_state_ref[n] = nv_ref[n * s_len:(n + 1) * s_len, :]

    for h in range(NA_HEADS):
        sl = slice(h * NA_DH, (h + 1) * NA_DH)
        q = nq_ref[:, sl]
        k = nk_ref[:, sl]
        rq = lax.rsqrt(jnp.mean(q * q, axis=-1, keepdims=True) + EPS)
        rk = lax.rsqrt(jnp.mean(k * k, axis=-1, keepdims=True) + EPS)
        q_ref[:, sl] = (q * rq * qg * scale).astype(bf16)
        kn = k * rk * kg
        k_ref[:, sl] = kn.astype(bf16)

        @pl.when(is_ctx)
        def _():
            for n in range(tm // s_len):
                k_state_ref[n, :, sl] = kn[n * s_len:(n + 1) * s_len]


def _na_prep(z, lay, q_g, k_g, tm, state):
    m = z.shape[0]
    w = NA_HEADS * NA_DH
    spec = pl.BlockSpec((tm, w), lambda i: (i, 0))
    in_specs = [
        pl.BlockSpec((tm, w), lambda i: (i, lay.nq // w)),
        pl.BlockSpec((tm, w), lambda i: (i, lay.nk // w)),
        pl.BlockSpec((tm, w), lambda i: (i, lay.nv // w)),
        pl.BlockSpec((1, NA_DH), lambda i: (0, 0)),
        pl.BlockSpec((1, NA_DH), lambda i: (0, 0)),
    ]
    args = [z, z, z, q_g.reshape(1, NA_DH), k_g.reshape(1, NA_DH)]
    return pl.pallas_call(
        functools.partial(_na_prep_kernel, scale=NA_DH ** -0.5 * LOG2E, **state.kernel_kwargs()),
        grid=(m // tm,),
        in_specs=in_specs + state.in_specs,
        out_specs=[spec, spec, spec] + state.out_specs,
        out_shape=[jax.ShapeDtypeStruct((m, w), bf16)] * 3 + state.out_shape,
        input_output_aliases=state.aliases(len(args), 3),
        compiler_params=_params(("arbitrary",)),
        name="na_prep",
    )(*args, *state.prevs)


def _na_static_maps(rows):
    nblk = rows // NA_QROWS
    reps = (0, min(1, nblk - 1), nblk - 1)
    dr_map = np.full((3, NA_QROWS, NA_KROWS), 2 * NA_WIN_R - 1, np.int32)
    for v, kb in enumerate(reps):
        ws = min(max(NA_QROWS * kb - NA_WIN_R // 2, 0), rows - NA_KROWS)
        for i in range(NA_QROWS):
            qrow = NA_QROWS * kb + i
            rs = min(max(qrow - NA_WIN_R // 2, 0), rows - NA_WIN_R)
            for j in range(NA_KROWS):
                krow = ws + j
                if rs <= krow < rs + NA_WIN_R:
                    dr_map[v, i, j] = krow - qrow + NA_WIN_R - 1
    qcol = np.arange(GRID_W)
    cs = np.clip(qcol - NA_WIN_C // 2, 0, GRID_W - NA_WIN_C)
    in_win = (qcol[None, :] >= cs[:, None]) & (qcol[None, :] < cs[:, None] + NA_WIN_C)
    dc_idx = np.clip(qcol[None, :] - qcol[:, None], -(NA_WIN_C - 1), NA_WIN_C - 1) + (NA_WIN_C - 1)
    return dr_map, in_win, dc_idx


def _na_bias_table(rpb, rows):
    dr_map, in_win, dc_idx = _na_static_maps(rows)
    depth, h = rpb.shape[:2]
    bt = jnp.where(in_win.T, jnp.take(rpb * LOG2E, jnp.asarray(dc_idx.T), axis=3), NEG)
    bt = jnp.concatenate([bt, jnp.full((depth, h, 1, GRID_W, GRID_W), NEG, f32)], axis=2)
    tab = jnp.take(bt, jnp.asarray(dr_map.transpose(0, 2, 1).reshape(-1)), axis=2)
    tab = tab.reshape(depth, h, 3, NA_KROWS, NA_QROWS, GRID_W, GRID_W).transpose(0, 2, 1, 3, 5, 4, 6)
    return tab.reshape(depth, 3, h, NA_KROWS * GRID_W, NA_QROWS * GRID_W)


def _na_kernel(q_ref, k_ref, v_ref, kc_ref, vc_ref, bias_ref, o_ref, *, rows, heads):
    kb = pl.program_id(2)
    ws = jnp.clip(NA_QROWS * kb - NA_WIN_R // 2, 0, rows - NA_KROWS)
    win = pl.ds(pl.multiple_of(ws * GRID_W, (NA_WIN_R // 2) * GRID_W), NA_KROWS * GRID_W)
    cols = [slice(g * NA_DH, (g + 1) * NA_DH) for g in range(heads)]
    s_loc = [_dot_nt(k_ref[win, c], q_ref[:, c]) + bias_ref[g] for g, c in enumerate(cols)]
    s_ctx = [_dot_nt(kc_ref[:, c], q_ref[:, c]) for c in cols]
    p_loc, p_ctx, l = [], [], []
    for g in range(heads):
        m = jnp.maximum(jnp.max(s_loc[g], axis=0, keepdims=True), jnp.max(s_ctx[g], axis=0, keepdims=True))
        pl_g = jnp.exp2(s_loc[g] - m)
        pc_g = jnp.exp2(s_ctx[g] - m)
        l.append(jnp.sum(pl_g, axis=0, keepdims=True) + jnp.sum(pc_g, axis=0, keepdims=True))
        p_loc.append(pl_g.astype(bf16))
        p_ctx.append(pc_g.astype(bf16))
    for g, c in enumerate(cols):
        acc = _dot_tn(v_ref[win, c], p_loc[g]) + _dot_tn(vc_ref[:, c], p_ctx[g])
        o_ref[:, c] = (acc / l[g]).T.astype(o_ref.dtype)


def _na_latent(q, k, v, kc, vc, bias, layer, *, batch, t_len, p_len, row0, heads_per_step=2):
    rows = t_len // GRID_W
    assert t_len % GRID_W == 0 and rows % NA_QROWS == 0 and rows >= NA_KROWS
    tq = NA_QROWS * GRID_W
    assert row0 % tq == 0 and row0 % t_len == 0 and NA_HEADS % heads_per_step == 0
    nblk = rows // NA_QROWS
    g = heads_per_step
    gw = g * NA_DH

    def variant(i):
        return jnp.where(i == 0, 0, jnp.where(i == nblk - 1, 2, 1))

    return pl.pallas_call(
        functools.partial(_na_kernel, rows=rows, heads=g),
        grid=(batch, NA_HEADS // g, nblk),
        in_specs=[
            pl.BlockSpec((tq, gw), lambda b, h, i: (row0 // tq + b * nblk + i, h)),
            pl.BlockSpec((t_len, gw), lambda b, h, i: (row0 // t_len + b, h)),
            pl.BlockSpec((t_len, gw), lambda b, h, i: (row0 // t_len + b, h)),
            pl.BlockSpec((None, None, p_len, gw), lambda b, h, i: (b, layer, 0, h)),
            pl.BlockSpec((None, None, p_len, gw), lambda b, h, i: (b, layer, 0, h)),
            pl.BlockSpec((None, None, g, NA_KROWS * GRID_W, tq), lambda b, h, i: (layer, variant(i), h, 0, 0)),
        ],
        out_specs=pl.BlockSpec((tq, gw), lambda b, h, i: (b * nblk + i, h)),
        out_shape=jax.ShapeDtypeStruct((batch * t_len, NA_HEADS * NA_DH), bf16),
        compiler_params=_params(("arbitrary", "arbitrary", "arbitrary")),
        name="na_latent",
    )(q, k, v, kc, vc, bias)


def _hgrn_tables():
    c, lv = HG_CHUNK, HG_LEVELS
    t = np.arange(c)[:, None]
    r = np.arange(c)[None, :]
    wcum = np.stack([r <= t, r >= t]).astype(np.float32)
    qside = np.zeros((2, lv * c, LANES), np.float32)
    mask = np.zeros((2, lv + 1, c, c), np.float32)
    mask[:, 0] = np.eye(c)
    for l in range(lv):
        hs = 1 << l
        blk = t // (2 * hs)
        mid = blk * (2 * hs) + hs
        qf = t >= mid
        qb = t < mid
        qside[0, l * c:(l + 1) * c] = np.where(qf, 1.0, -1.0)
        qside[1, l * c:(l + 1) * c] = np.where(qb, 1.0, -1.0)
        same = blk == blk.T
        mask[0, 1 + l] = same & qf & ~qf.T
        mask[1, 1 + l] = same & qb & ~qb.T
    return wcum, qside, mask


def _hgrn_kernel(*refs, n_chunks, heads, has_state, emit_state, n_prev=0):
    (hq_ref, hff_ref, hfb_ref, hi_ref, hg_ref, la_ref, lc_ref, om_ref, ng_ref,
     wcum_ref, qside_ref, mask_ref) = refs[:12]
    rest = list(refs[12:])
    s0_ref = rest.pop(0) if has_state else None
    del rest[:n_prev]
    o_ref = rest.pop(0)
    st_ref = rest.pop(0) if emit_state else None
    o_scr, s_scr, b_scr = rest
    c, lv = HG_CHUNK, HG_LEVELS

    o_scr[...] = jnp.zeros_like(o_scr)
    for d in range(2):
        for g in range(heads):
            if has_state:
                s_scr[d, g] = s0_ref[d, g].T
            else:
                s_scr[d, g] = jnp.zeros((HG_V, HG_K), f32)

    chains = [(d, g) for g in range(heads) for d in range(2)]
    low_half = lax.broadcasted_iota(jnp.int32, (SUBLANES, HEAD_W), 0) < SUBLANES // 2

    def gates_and_cumsum(ci, row):
        d, g = chains[ci]
        rs = pl.ds(pl.multiple_of(row, c), c)
        cs = slice(g * HEAD_W, (g + 1) * HEAD_W)
        x = (hff_ref if d == 0 else hfb_ref)[rs, cs]
        x2 = x * LOG2E
        u = jnp.exp2(-jnp.abs(x2))
        inv = 1.0 / (1.0 + u)
        pos = x >= 0.0
        log_sig = jnp.minimum(x2, 0.0) - jnp.log2(1.0 + u)
        a = la_ref[d:d + 1, cs]
        bb = lc_ref[d:d + 1, cs] + log_sig
        lf = jnp.maximum(a, bb) + jnp.log2(1.0 + jnp.exp2(-jnp.abs(a - bb)))
        om = om_ref[d:d + 1, cs]
        f = (1.0 - om) + om * (jnp.where(pos, 1.0, u) * inv)
        kin = om * (jnp.where(pos, u, 1.0) * inv)
        q = _silu(hq_ref[rs, cs])
        v = hi_ref[rs, cs].astype(bf16)
        l1 = lf.astype(bf16)
        r1 = lf - l1.astype(f32)
        l2 = r1.astype(bf16)
        l3 = (r1 - l2.astype(f32)).astype(bf16)
        w = wcum_ref[d]
        b = _dot(w, l1) + _dot(w, l2) + _dot(w, l3)
        b_scr[ci] = b
        return dict(rs=rs, cs=cs, f=f, kin=kin, q=q, v=v, b=b)

    def intra_chunk(ci, s):
        d, _ = chains[ci]
        b, q, kin = s["b"], s["q"], s["kin"]

        def row(r):
            return b_scr[ci, r:r + 1, :]

        s["b_end"] = row(c - 1 if d == 0 else 0)
        xs = [jnp.where(qside_ref[d, 0:c, :] > 0.0, q * s["f"], kin).astype(bf16)]
        for l in range(1, lv):
            hs = 1 << l
            blk = 2 * hs
            pick = hs - 1 if d == 0 else hs
            if hs >= SUBLANES:
                pieces = []
                for j in range(c // blk):
                    m = row(blk * j + pick)
                    early = slice(blk * j, blk * j + hs)
                    late = slice(blk * j + hs, blk * (j + 1))
                    q_sl, k_sl = (late, early) if d == 0 else (early, late)
                    xq = q[q_sl] * jnp.exp2(b[q_sl] - m)
                    xk = kin[k_sl] * jnp.exp2(m - b[k_sl])
                    pieces += [xk, xq] if d == 0 else [xq, xk]
                xs.append(jnp.concatenate(pieces, axis=0).astype(bf16))
                continue
            if blk < SUBLANES:
                pieces = [jnp.where(low_half, jnp.broadcast_to(row(SUBLANES * j + pick), (SUBLANES, HEAD_W)),
                                    jnp.broadcast_to(row(SUBLANES * j + blk + pick), (SUBLANES, HEAD_W)))
                          for j in range(c // SUBLANES)]
            else:
                pieces = [jnp.broadcast_to(row(blk * j + pick), (blk, HEAD_W)) for j in range(c // blk)]
            m = pieces[0] if len(pieces) == 1 else jnp.concatenate(pieces, axis=0)
            sign = qside_ref[d, l * c:(l + 1) * c, :]
            xs.append((jnp.where(sign > 0.0, q, kin) * jnp.exp2((b - m) * sign)).astype(bf16))
        att = mask_ref[d, 0] * _dot_nt(q.astype(bf16), kin.astype(bf16)).astype(bf16)
        for l in range(lv):
            att = att + mask_ref[d, 1 + l] * _dot_nt(xs[l], xs[l]).astype(bf16)
        s["att"] = att

    def state_step(ci, s):
        d, g = chains[ci]
        b, q, kin, v = s["b"], s["q"], s["kin"], s["v"]
        st = s_scr[d, g]
        o = _dot(s["att"], v) + _dot_nt((q * jnp.exp2(b)).astype(bf16), st.astype(bf16))
        b_end = s["b_end"]
        s_scr[d, g] = st * jnp.exp2(b_end) + _dot_tn(v, (kin * jnp.exp2(b_end - b)).astype(bf16))
        o_scr[s["rs"], s["cs"]] += o

    def body(i, _):
        rows = (i * c, (n_chunks - 1 - i) * c)
        states = [gates_and_cumsum(ci, rows[chains[ci][0]]) for ci in range(len(chains))]
        for ci, s in enumerate(states):
            intra_chunk(ci, s)
        for ci, s in enumerate(states):
            state_step(ci, s)
        return 0

    lax.fori_loop(0, n_chunks, body, 0)

    ng = ng_ref[...]
    for g in range(heads):
        cs = slice(g * HEAD_W, (g + 1) * HEAD_W)
        o = o_scr[:, cs]
        y = o * lax.rsqrt(jnp.mean(o * o, axis=-1, keepdims=True) + EPS) * ng
        o_ref[:, cs] = (y * _silu(hg_ref[:, cs])).astype(o_ref.dtype)
        if emit_state:
            for d in range(2):
                st_ref[d, g] = s_scr[d, g].T


def _hgrn(z, lay, lb_tabs, norm_g, tables, *, batch, t_len, row0, s0, state_out=None, heads_per_step=2):
    emit_state = state_out is not None
    assert t_len % HG_CHUNK == 0 and row0 % t_len == 0 and HG_K == HEAD_W and HG_V == HEAD_W
    g = heads_per_step
    gw = g * HEAD_W
    la, lc, om = lb_tabs
    wcum, qside, mask = tables
    r0 = row0 // t_len

    def zspec(col):
        return pl.BlockSpec((t_len, gw), lambda b, h: (r0 + b, col // gw + h))

    def const(shape):
        return pl.BlockSpec(shape, lambda b, h: (0,) * len(shape))

    in_specs = [zspec(lay.hq), zspec(lay.hff), zspec(lay.hfb), zspec(lay.hi), zspec(lay.hg),
                pl.BlockSpec((2, gw), lambda b, h: (0, h)), pl.BlockSpec((2, gw), lambda b, h: (0, h)),
                pl.BlockSpec((2, gw), lambda b, h: (0, h)), const((1, HG_V)),
                const(wcum.shape), const(qside.shape), const(mask.shape)]
    args = [z, z, z, z, z, la, lc, om, norm_g.reshape(1, HG_V), wcum, qside, mask]
    state_spec = pl.BlockSpec((None, 2, g, HG_K, HG_V), lambda b, h: (b, 0, h, 0, 0))
    if s0 is not None:
        in_specs.append(state_spec)
        args.append(s0)
    out_specs = [pl.BlockSpec((t_len, gw), lambda b, h: (b, h))]
    out_shape = [jax.ShapeDtypeStruct((batch * t_len, HG_HEADS * HG_V), bf16)]
    aliases, n_prev = {}, 0
    if emit_state:
        prev, layer, depth = state_out
        aliases, n_prev = {len(args): 1}, 1
        in_specs.append(pl.BlockSpec(memory_space=pl.ANY))
        args.append(prev)
        out_specs.append(pl.BlockSpec((None, None, 2, g, HG_K, HG_V), lambda b, h: (b, layer, 0, h, 0, 0)))
        out_shape.append(jax.ShapeDtypeStruct((batch, depth, 2, HG_HEADS, HG_K, HG_V), f32))
    res = pl.pallas_call(
        functools.partial(_hgrn_kernel, n_chunks=t_len // HG_CHUNK, heads=g, has_state=s0 is not None,
                          emit_state=emit_state, n_prev=n_prev),
        grid=(batch, HG_HEADS // g),
        in_specs=in_specs,
        out_specs=out_specs,
        out_shape=out_shape,
        input_output_aliases=aliases,
        scratch_shapes=[pltpu.VMEM((t_len, gw), f32), pltpu.VMEM((2, g, HG_V, HG_K), f32),
                        pltpu.VMEM((2 * g, HG_CHUNK, HEAD_W), f32)],
        compiler_params=_params(("arbitrary", "arbitrary")),
        name="hgrn_ctx" if emit_state else "hgrn_latent",
    )(*args)
    return res if emit_state else (res[0], None)


def _merge_kernel(oac_ref, oal_ref, obc_ref, obl_ref, occ_ref, ocl_ref, ga_ref, gb_ref, gc_ref, w_ref, o_ref,
                  *, n_ctx_tiles):
    def compute(oa_ref, ob_ref, oc_ref):
        m = jax.nn.sigmoid(ga_ref[...]) * _dot(oa_ref[...], w_ref[0])
        m = m + jax.nn.sigmoid(gb_ref[...]) * _dot(ob_ref[...], w_ref[1])
        m = m + jax.nn.sigmoid(gc_ref[...]) * _dot(oc_ref[...], w_ref[2])
        o_ref[...] = m.astype(o_ref.dtype)

    is_ctx = pl.program_id(1) < n_ctx_tiles

    @pl.when(is_ctx)
    def _():
        compute(oac_ref, obc_ref, occ_ref)

    @pl.when(jnp.logical_not(is_ctx))
    def _():
        compute(oal_ref, obl_ref, ocl_ref)


def _merge(o_a, o_b, o_c, z, lay, w_branch, layer, tm, tn):
    bw = o_a[0].shape[1]
    m = z.shape[0]
    d = w_branch.shape[3]
    nct = o_a[0].shape[0] // tm
    cspec = pl.BlockSpec((tm, bw), lambda j, i: (jnp.minimum(i, nct - 1), 0))
    lspec = pl.BlockSpec((tm, bw), lambda j, i: (jnp.maximum(i - nct, 0), 0))

    def gate(col):
        return pl.BlockSpec((tm, tn), lambda j, i: (i, col // tn + j))

    return pl.pallas_call(
        functools.partial(_merge_kernel, n_ctx_tiles=nct),
        grid=(d // tn, m // tm),
        in_specs=[cspec, lspec, cspec, lspec, cspec, lspec, gate(lay.ga), gate(lay.gb), gate(lay.gc),
                  pl.BlockSpec((None, 3, bw, tn), lambda j, i: (layer, 0, 0, j))],
        out_specs=pl.BlockSpec((tm, tn), lambda j, i: (i, j)),
        out_shape=jax.ShapeDtypeStruct((m, d), bf16),
        compiler_params=_params(("arbitrary", "arbitrary")),
        name="merge",
    )(o_a[0], o_a[1], o_b[0], o_b[1], o_c[0], o_c[1], z, z, z, w_branch)


def _proj_residual_kernel(m_ref, w_ref, x_ref, g_ref, o_ref):
    o_ref[...] = x_ref[...] + g_ref[...] * _dot(m_ref[...], w_ref[...])


def _proj_residual(mm, w, layer, x, gate, group_of_tile, tm, tn):
    m, k = mm.shape
    d = w.shape[2]
    return pl.pallas_call(
        _proj_residual_kernel,
        grid=(m // tm, d // tn),
        in_specs=[
            pl.BlockSpec((tm, k), lambda i, j: (i, 0)),
            pl.BlockSpec((None, k, tn), lambda i, j: (layer, 0, j)),
            pl.BlockSpec((tm, tn), lambda i, j: (i, j)),
            pl.BlockSpec((None, 1, tn), lambda i, j: (group_of_tile(i), 0, j)),
        ],
        out_specs=pl.BlockSpec((tm, tn), lambda i, j: (i, j)),
        out_shape=jax.ShapeDtypeStruct((m, d), f32),
        compiler_params=_params(("arbitrary", "arbitrary")),
        name="out_proj",
    )(mm, w, x, gate)


def _ffn_kernel(x_ref, xprev_ref, xnext_ref, ng_ref, sc_ref, sh_ref, wa_ref, wg_ref, cw_ref, cb_ref, wd_ref, gate_ref,
                o_ref, h_scr, *, m_ctx, s_len, t_len, n_k):
    i = pl.program_id(0)
    k = pl.program_id(1)
    tm = x_ref.shape[0]
    tk = wa_ref.shape[1]
    halo = SUBLANES

    @pl.when(k == 0)
    def _():
        def normmod(x):
            ms = jnp.mean(x * x, axis=-1, keepdims=True)
            y = x * lax.rsqrt(ms + EPS) * ng_ref[...]
            return (y * (1.0 + sc_ref[...]) + sh_ref[...]).astype(bf16)

        h_scr[0:halo] = normmod(xprev_ref[...])
        h_scr[halo:halo + tm] = normmod(x_ref[...])
        h_scr[halo + tm:halo + tm + halo] = normmod(xnext_ref[...])
        o_ref[...] = jnp.zeros_like(o_ref)

    row = i * tm + lax.broadcasted_iota(jnp.int32, (tm, LANES), 0)
    pos = jnp.where(row < m_ctx, _mod_const(row, s_len), _mod_const(row - m_ctx, t_len))
    last = jnp.where(row < m_ctx, s_len - 1, t_len - 1)
    keep_prev = jnp.tile(jnp.where(pos == 0, 0.0, 1.0), (1, tk // LANES))
    keep_next = jnp.tile(jnp.where(pos == last, 0.0, 1.0), (1, tk // LANES))

    g_ext = _dot(h_scr[...], wg_ref[...])
    a = _dot(h_scr[halo:halo + tm], wa_ref[...])
    n_ext = tm + 2 * halo
    g = g_ext[halo:halo + tm]
    g_prev = pltpu.roll(g_ext, 1, 0)[halo:halo + tm]
    g_next = pltpu.roll(g_ext, n_ext - 1, 0)[halo:halo + tm]
    conv = (g_prev * keep_prev * cw_ref[0:1, :] + g * cw_ref[1:2, :] + g_next * keep_next * cw_ref[2:3, :]
            + cb_ref[...])
    u = (_silu(conv) * a).astype(bf16)
    o_ref[...] = _dot(u, wd_ref[...]) + o_ref[...]

    @pl.when(k == n_k - 1)
    def _():
        o_ref[...] = x_ref[...] + gate_ref[...] * o_ref[...]


def _ffn(x, norm_g, sc, sh, w_up, conv_w, conv_b, w_down, layer, gate, group_of_tile, tm, tk, m_ctx, s_len, t_len):
    m, d = x.shape
    f = w_down.shape[1]
    nk = f // tk
    nsub = tm // SUBLANES
    last_sub = m // SUBLANES - 1
    return pl.pallas_call(
        functools.partial(_ffn_kernel, m_ctx=m_ctx, s_len=s_len, t_len=t_len, n_k=nk),
        grid=(m // tm, nk),
        in_specs=[
            pl.BlockSpec((tm, d), lambda i, k: (i, 0)),
            pl.BlockSpec((SUBLANES, d), lambda i, k: (jnp.maximum(i * nsub - 1, 0), 0)),
            pl.BlockSpec((SUBLANES, d), lambda i, k: (jnp.minimum((i + 1) * nsub, last_sub), 0)),
            pl.BlockSpec((1, d), lambda i, k: (0, 0)),
            pl.BlockSpec((None, 1, d), lambda i, k: (group_of_tile(i), 0, 0)),
            pl.BlockSpec((None, 1, d), lambda i, k: (group_of_tile(i), 0, 0)),
            pl.BlockSpec((None, d, tk), lambda i, k: (layer, 0, k)),
            pl.BlockSpec((None, d, tk), lambda i, k: (layer, 0, nk + k)),
            pl.BlockSpec((3, tk), lambda i, k: (0, k)),
            pl.BlockSpec((1, tk), lambda i, k: (0, k)),
            pl.BlockSpec((None, tk, d), lambda i, k: (layer, k, 0)),
            pl.BlockSpec((None, 1, d), lambda i, k: (group_of_tile(i), 0, 0)),
        ],
        out_specs=pl.BlockSpec((tm, d), lambda i, k: (i, 0), pipeline_mode=pl.Buffered(1)),
        out_shape=jax.ShapeDtypeStruct((m, d), f32),
        scratch_shapes=[pltpu.VMEM((tm + 2 * SUBLANES, d), bf16)],
        compiler_params=_params(("arbitrary", "arbitrary")),
        name="ffn",
    )(x, x, x, norm_g.reshape(1, d), sc, sh, w_up, w_up, conv_w, conv_b.reshape(1, f), w_down, gate)


def kernel(x_prompt, x_sample, cache_mla_ckv, cache_mla_kpe, cache_na_k, cache_na_v, state_hgrn, c, c_ctx, w_mod, b_mod, norm1_g, norm2_g, w_in, mla_kv_norm_g, mla_q_norm_g, mla_k_norm_g, mla_w_uk, mla_w_uv, hgrn_lower_bounds, hgrn_norm_g, na_q_norm_g, na_k_norm_g, na_rpb, w_branch, w_out, ffn_w_up, ffn_conv_w, ffn_conv_b, ffn_w_down):
    n_ctx, s_len, d = x_prompt.shape
    n_lat, t_len, _ = x_sample.shape
    depth = w_in.shape[0]
    p_len = cache_mla_ckv.shape[2]
    ffn = ffn_w_down.shape[1]
    m_ctx, m_lat = n_ctx * s_len, n_lat * t_len
    m = m_ctx + m_lat
    assert m_ctx % t_len == 0, "context rows must be a whole number of latent sequences"

    tm = _pow2_tile(1024, m_ctx, t_len)
    tn = _pow2_tile(1024, d)
    tp = _pow2_tile(512, m_ctx, t_len)
    lay = _ZLayout(d, tn)

    def group_fn(rows):
        return lambda i: jnp.where(i < m_ctx // rows, 0, 1 + (i - m_ctx // rows) // (t_len // rows))

    group_of_tile = group_fn(tm)

    n_groups = -(-(1 + n_lat) // SUBLANES) * SUBLANES
    cond = jnp.concatenate([c_ctx[None], c, jnp.zeros((n_groups - 1 - n_lat, d), f32)], axis=0)
    mods = _modulation(cond, w_mod, b_mod).reshape(depth, n_groups, 6, d).transpose(0, 2, 1, 3)[:, :, :, None, :]

    sm = jax.nn.softmax(hgrn_lower_bounds.astype(f32), axis=1)
    csum = jnp.cumsum(sm, axis=1)
    lower = csum - csum[:, :1]
    hg_tabs = tuple(jnp.asarray(t, dt) for t, dt in zip(_hgrn_tables(), (bf16, f32, bf16)))

    cos, sin = _rope_tables(t_len, tp)
    n_ctx_tp = m_ctx // tp

    def rope_block(i):
        return jnp.where(i < n_ctx_tp, 0, 1 + (i - n_ctx_tp) % (t_len // tp))

    x = jnp.concatenate([x_prompt.reshape(m_ctx, d), x_sample.reshape(m_lat, d)], axis=0)
    w_in_b, w_kpe_b = _permute_w_in(w_in, lay)
    w_uk = mla_w_uk.reshape(depth, MLA_KV_RANK, MLA_HEADS * MLA_NOPE).astype(bf16)
    w_uv = mla_w_uv.reshape(depth, MLA_KV_RANK, MLA_HEADS * MLA_V).astype(bf16)
    w_branch_b, w_out_b = w_branch.astype(bf16), w_out.astype(bf16)
    w_up_b, w_down_b = ffn_w_up.astype(bf16), ffn_w_down.astype(bf16)
    na_bias = _na_bias_table(na_rpb, t_len // GRID_W)
    kc_na = cache_na_k.reshape(n_lat, depth, p_len, NA_HEADS * NA_DH).astype(bf16)
    vc_na = cache_na_v.reshape(n_lat, depth, p_len, NA_HEADS * NA_DH).astype(bf16)

    def state_init(*trailing):
        return jnp.zeros((n_ctx, depth) + trailing, f32)

    st_mla = [state_init(s_len, MLA_KV_RANK), state_init(s_len, MLA_ROPE)]
    st_na = [state_init(s_len, NA_HEADS * NA_DH), state_init(s_len, NA_HEADS * NA_DH)]
    st_hg = state_init(2, HG_HEADS, HG_K, HG_V)
    for l in range(depth):
        sh1, sc1, g1, sh2, sc2, g2 = (mods[l, k] for k in range(6))
        z, kpe = _normmod_matmul(x, norm1_g[l], sc1, sh1, w_in_b, l, group_of_tile, tm, tn, "in_proj", w_side=w_kpe_b)

        q_mla = _mla_q(z, lay, mla_q_norm_g[l], cos, sin, rope_block, tp)
        k_mla, v_mla, *st_mla = _mla_kv(
            z, lay.ckv, kpe, 0, mla_kv_norm_g[l], mla_k_norm_g[l], w_uk, w_uv, l, cos, sin, rope_block, tp, True, True,
            "mla_kv", state=_CtxState(st_mla, (MLA_KV_RANK, MLA_ROPE), l, depth, n_ctx, s_len, tp))
        ckv_c = cache_mla_ckv[:, l].reshape(n_lat * p_len, MLA_KV_RANK)
        kpe_c = jnp.pad(cache_mla_kpe[:, l].reshape(n_lat * p_len, MLA_ROPE), ((0, 0), (0, LANES - MLA_ROPE)))
        tc = _pow2_tile(512, n_lat * p_len)
        k_c, v_c = _mla_kv(ckv_c, 0, kpe_c, 0, mla_kv_norm_g[l], mla_k_norm_g[l], w_uk, w_uv, l,
                           cos, sin, lambda i: 0, tc, False, False, "mla_kv_cache")
        tq_c = _pow2_tile(256, s_len)
        o_mla_c = _flash(q_mla, k_mla, v_mla, batch=n_ctx, heads=MLA_HEADS, t_len=s_len, s_len=s_len, row0=0,
                         dq=MLA_QPAD, dv=MLA_V, tq=tq_c, tk=tq_c, heads_per_step=MLA_HEADS, name="mla_attn_ctx")
        tq_l = _pow2_tile(512, t_len, p_len)
        o_mla_l = _flash(q_mla, k_mla, v_mla, batch=n_lat, heads=MLA_HEADS, t_len=t_len, s_len=t_len, row0=m_ctx,
                         dq=MLA_QPAD, dv=MLA_V, tq=tq_l, tk=tq_l, heads_per_step=2, extra=(k_c, v_c, p_len),
                         name="mla_attn_latent")

        lb = lower[:, l]
        lb_tabs = (jnp.log(lb) * LOG2E, jnp.log1p(-lb) * LOG2E, 1.0 - lb)
        o_hg_c, st_hg = _hgrn(z, lay, lb_tabs, hgrn_norm_g[l], hg_tabs, batch=n_ctx, t_len=s_len, row0=0,
                              s0=None, state_out=(st_hg, l, depth), heads_per_step=4)
        o_hg_l, _ = _hgrn(z, lay, lb_tabs, hgrn_norm_g[l], hg_tabs, batch=n_lat, t_len=t_len, row0=m_ctx,
                          s0=state_hgrn[:, l])

        q_na, k_na, v_na, *st_na = _na_prep(
            z, lay, na_q_norm_g[l], na_k_norm_g[l], tp,
            _CtxState(st_na, (NA_HEADS * NA_DH, NA_HEADS * NA_DH), l, depth, n_ctx, s_len, tp))
        o_na_c = _flash(q_na, k_na, v_na, batch=n_ctx, heads=NA_HEADS, t_len=s_len, s_len=s_len, row0=0,
                        dq=NA_DH, dv=NA_DH, tq=tq_c, tk=tq_c, heads_per_step=NA_HEADS, name="na_attn_ctx")
        o_na_l = _na_latent(q_na, k_na, v_na, kc_na, vc_na, na_bias, l, batch=n_lat, t_len=t_len, p_len=p_len, row0=m_ctx)

        merged = _merge((o_mla_c, o_mla_l), (o_hg_c, o_hg_l), (o_na_c, o_na_l), z, lay, w_branch_b, l, tp, tn)
        x = _proj_residual(merged, w_out_b, l, x, g1, group_fn(tp), tp, d)

        x = _ffn(x, norm2_g[l], sc2, sh2, w_up_b, ffn_conv_w[l], ffn_conv_b[l], w_down_b, l, g2, group_of_tile, tm,
                 _pow2_tile(512, ffn), m_ctx, s_len, t_len)

    y_prompt = x[:m_ctx].reshape(n_ctx, s_len, d)
    y_sample = x[m_ctx:].reshape(n_lat, t_len, d)
    na_shape = (n_ctx, depth, s_len, NA_HEADS, NA_DH)
    return (y_prompt, y_sample, st_mla[0], st_mla[1], st_na[0].reshape(na_shape), st_na[1].reshape(na_shape), st_hg)
```

```python
import functools
import math

import numpy as np
import jax
import jax.numpy as jnp
from jax import lax
from jax.experimental import pallas as pl
from jax.experimental.pallas import tpu as pltpu

f32 = jnp.float32
bf16 = jnp.bfloat16

GRID_W = 64
MLA_HEADS, MLA_NOPE, MLA_ROPE, MLA_V, MLA_KV_RANK = 8, 128, 64, 128, 512
MLA_QK = MLA_NOPE + MLA_ROPE
HG_HEADS, HG_K, HG_V = 8, 128, 128
NA_HEADS, NA_DH, NA_WIN_R, NA_WIN_C = 8, 128, 8, 16
ROPE_BASE = 10000.0
EPS = 1e-6
NEG = -1e30
LOG2E = 1.4426950408889634

LANES = 128
SUBLANES = 8
VMEM_LIMIT_BYTES = 60 * 1024 * 1024

HEAD_W = 128
MLA_QPAD = 256
HG_CHUNK = 128
HG_LEVELS = int(math.log2(HG_CHUNK))
NA_QROWS = 8
NA_KROWS = 16


def _params(semantics):
    return pltpu.CompilerParams(dimension_semantics=semantics, vmem_limit_bytes=VMEM_LIMIT_BYTES)


def _pow2_tile(pref, *dims):
    t = pref
    while any(d % t for d in dims):
        t //= 2
    assert t >= SUBLANES, (pref, dims)
    return t


def _dot(a, b):
    return jnp.dot(a, b, preferred_element_type=f32)


def _dot_nt(a, b):
    return lax.dot_general(a, b, (((1,), (1,)), ((), ())), preferred_element_type=f32)


def _dot_tn(a, b):
    return lax.dot_general(a, b, (((0,), (0,)), ((), ())), preferred_element_type=f32)


def _silu(x):
    return x * jax.nn.sigmoid(x)


def _mod_const(x, n):
    return jnp.bitwise_and(x, n - 1) if n & (n - 1) == 0 else lax.rem(x, n)


class _ZLayout:
    def __init__(self, d_model, tn):
        self.mq = 0
        self.ckv = self.mq + MLA_HEADS * MLA_QK
        self.hq = self.ckv + MLA_KV_RANK
        self.hff = self.hq + HG_HEADS * HG_K
        self.hfb = self.hff + HG_HEADS * HG_K
        self.hi = self.hfb + HG_HEADS * HG_K
        self.hg = self.hi + HG_HEADS * HG_V
        self.nq = self.hg + HG_HEADS * HG_V
        self.nk = self.nq + NA_HEADS * NA_DH
        self.nv = self.nk + NA_HEADS * NA_DH
        self.ga = self.nv + NA_HEADS * NA_DH
        self.gb = self.ga + d_model
        self.gc = self.gb + d_model
        self.width = self.gc + d_model
        assert self.width % tn == 0


def _permute_w_in(w, lay):
    n_front = MLA_HEADS * MLA_QK + MLA_KV_RANK
    kpe = w[..., n_front:n_front + MLA_ROPE]
    main = jnp.concatenate([w[..., :n_front], w[..., n_front + MLA_ROPE:]], axis=-1).astype(bf16)
    assert main.shape[-1] == lay.width
    pad = [(0, 0)] * (w.ndim - 1) + [(0, LANES - MLA_ROPE)]
    return main, jnp.pad(kpe, pad).astype(bf16)


def _mod_kernel(c_ref, w_ref, b_ref, o_ref):
    a = _silu(c_ref[...]).astype(bf16)
    o_ref[...] = _dot(a, w_ref[...].astype(bf16)) + b_ref[...]


def _modulation(cond, w_mod, b_mod):
    depth, d, n = w_mod.shape
    g = cond.shape[0]
    tn = _pow2_tile(1024, n)
    return pl.pallas_call(
        _mod_kernel,
        grid=(depth, n // tn),
        in_specs=[
            pl.BlockSpec((g, d), lambda l, j: (0, 0)),
            pl.BlockSpec((None, d, tn), lambda l, j: (l, 0, j)),
            pl.BlockSpec((None, 1, tn), lambda l, j: (l, 0, j)),
        ],
        out_specs=pl.BlockSpec((None, g, tn), lambda l, j: (l, 0, j)),
        out_shape=jax.ShapeDtypeStruct((depth, g, n), f32),
        compiler_params=_params(("arbitrary", "arbitrary")),
        name="modulation",
    )(cond, w_mod, b_mod.reshape(depth, 1, n))


def _normmod_matmul_kernel(x_ref, g_ref, sc_ref, sh_ref, w_ref, *rest):
    if len(rest) == 4:
        ws_ref, o_ref, os_ref, h_scr = rest
    else:
        (o_ref, h_scr), ws_ref, os_ref = rest, None, None

    @pl.when(pl.program_id(1) == 0)
    def _():
        x = x_ref[...]
        ms = jnp.mean(x * x, axis=-1, keepdims=True)
        y = x * lax.rsqrt(ms + EPS) * g_ref[...]
        h_scr[...] = (y * (1.0 + sc_ref[...]) + sh_ref[...]).astype(bf16)
        if ws_ref is not None:
            os_ref[...] = _dot(h_scr[...], ws_ref[...])

    o_ref[...] = _dot(h_scr[...], w_ref[...])


def _normmod_matmul(x, g, sc, sh, w, layer, group_of_tile, tm, tn, name, w_side=None):
    m, d = x.shape
    n = w.shape[2]
    in_specs = [
        pl.BlockSpec((tm, d), lambda i, j: (i, 0)),
        pl.BlockSpec((1, d), lambda i, j: (0, 0)),
        pl.BlockSpec((None, 1, d), lambda i, j: (group_of_tile(i), 0, 0)),
        pl.BlockSpec((None, 1, d), lambda i, j: (group_of_tile(i), 0, 0)),
        pl.BlockSpec((None, d, tn), lambda i, j: (layer, 0, j)),
    ]
    args = [x, g.reshape(1, d), sc, sh, w]
    out_specs = [pl.BlockSpec((tm, tn), lambda i, j: (i, j))]
    out_shape = [jax.ShapeDtypeStruct((m, n), f32)]
    if w_side is not None:
        ns = w_side.shape[2]
        in_specs.append(pl.BlockSpec((None, d, ns), lambda i, j: (layer, 0, 0)))
        args.append(w_side)
        out_specs.append(pl.BlockSpec((tm, ns), lambda i, j: (i, 0)))
        out_shape.append(jax.ShapeDtypeStruct((m, ns), f32))
    res = pl.pallas_call(
        _normmod_matmul_kernel,
        grid=(m // tm, n // tn),
        in_specs=in_specs,
        out_specs=out_specs,
        out_shape=out_shape,
        scratch_shapes=[pltpu.VMEM((tm, d), bf16)],
        compiler_params=_params(("arbitrary", "arbitrary")),
        name=name,
    )(*args)
    return res if w_side is not None else res[0]


def _rope(x, cos, sin_signed):
    n = x.shape[-1]
    half = MLA_ROPE // 4
    lane = lax.broadcasted_iota(jnp.int32, x.shape, 1)
    partner = jnp.where((lane % (2 * half)) < half, pltpu.roll(x, n - half, 1), pltpu.roll(x, half, 1))
    return x * cos + partner * sin_signed


def _mla_q_kernel(mq_ref, gn_ref, gp_ref, cos_ref, sin_ref, o_ref, *, scale):
    tm = mq_ref.shape[0]
    lane = lax.broadcasted_iota(jnp.int32, (tm, LANES), 1)
    low = lane < MLA_ROPE
    cos = cos_ref[...]
    sin = sin_ref[...]
    gn = gn_ref[...]
    gp = gp_ref[...]
    for pair in range(MLA_HEADS // 2):
        t0, t1, t2 = (mq_ref[:, (3 * pair + t) * LANES:(3 * pair + t + 1) * LANES] for t in range(3))
        nopes = (t0, jnp.where(low, pltpu.roll(t1, MLA_ROPE, 1), pltpu.roll(t2, MLA_ROPE, 1)))
        pcol = jnp.where(low, t1, t2)
        sq = pcol * pcol
        ss_all = jnp.sum(sq, axis=-1, keepdims=True)
        ss_low = jnp.sum(jnp.where(low, sq, 0.0), axis=-1, keepdims=True)
        rot = _rope(pcol * gp, cos, sin)
        for k, ss_pe in enumerate((ss_low, ss_all - ss_low)):
            h = 2 * pair + k
            nope = nopes[k]
            ss = jnp.sum(nope * nope, axis=-1, keepdims=True) + ss_pe
            r = lax.rsqrt(ss * (1.0 / MLA_QK) + EPS) * scale
            pe = rot if k == 0 else pltpu.roll(rot, MLA_ROPE, 1)
            o_ref[:, h * MLA_QPAD:h * MLA_QPAD + MLA_NOPE] = (nope * r * gn).astype(bf16)
            o_ref[:, h * MLA_QPAD + MLA_NOPE:(h + 1) * MLA_QPAD] = jnp.where(low, pe * r, 0.0).astype(bf16)


class _CtxState:
    def __init__(self, prevs, widths, layer, depth, n_ctx, s_len, tm):
        assert tm % s_len == 0
        self.n_seq = tm // s_len
        self.s_len = s_len
        self.n_tiles = n_ctx * s_len // tm
        self.prevs = list(prevs)
        n_tiles = self.n_tiles
        self.out_specs = [pl.BlockSpec((self.n_seq, None, s_len, w), lambda i: (jnp.minimum(i, n_tiles - 1), layer, 0, 0))
                          for w in widths]
        self.out_shape = [jax.ShapeDtypeStruct((n_ctx, depth, s_len, w), f32) for w in widths]
        self.in_specs = [pl.BlockSpec(memory_space=pl.ANY)] * len(self.prevs)

    def aliases(self, n_inputs_before, n_outputs_before):
        return {n_inputs_before + k: n_outputs_before + k for k in range(len(self.prevs))}

    def kernel_kwargs(self):
        return dict(n_prev=len(self.prevs), n_ctx_tiles=self.n_tiles, s_len=self.s_len)


def _mla_kv_kernel(*refs, norm_ckv, rope, n_prev=0, n_ctx_tiles=0, s_len=0):
    ckv_ref, kpe_ref, kvg_ref, gn_ref, gp_ref, wuk_ref, wuv_ref, cos_ref, sin_ref = refs[:9]
    k_ref, v_ref, *state_refs = refs[9 + n_prev:]
    ckv = ckv_ref[...]
    if norm_ckv:
        ms = jnp.mean(ckv * ckv, axis=-1, keepdims=True)
        ckv = ckv * lax.rsqrt(ms + EPS) * kvg_ref[...]
    if state_refs:
        ckv_state_ref, kpe_state_ref = state_refs

        @pl.when(pl.program_id(0) < n_ctx_tiles)
        def _():
            for n in range(ckv.shape[0] // s_len):
                ckv_state_ref[n] = ckv[n * s_len:(n + 1) * s_len]
                kpe_state_ref[n] = kpe_ref[n * s_len:(n + 1) * s_len, :MLA_ROPE]

    cb = ckv.astype(bf16)
    kn = _dot(cb, wuk_ref[...])
    v_ref[...] = _dot(cb, wuv_ref[...]).astype(bf16)
    tm = ckv.shape[0]
    lane = lax.broadcasted_iota(jnp.int32, (tm, LANES), 1)
    low = lane < MLA_ROPE
    kpe = jnp.where(low, kpe_ref[...], 0.0)
    ss_pe = jnp.sum(kpe * kpe, axis=-1, keepdims=True)
    pe = kpe * gp_ref[...]
    if rope:
        pe = _rope(pe, cos_ref[...], sin_ref[...])
    gn = gn_ref[...]
    for h in range(MLA_HEADS):
        nope = kn[:, h * MLA_NOPE:(h + 1) * MLA_NOPE]
        ss = jnp.sum(nope * nope, axis=-1, keepdims=True) + ss_pe
        r = lax.rsqrt(ss * (1.0 / MLA_QK) + EPS)
        k_ref[:, h * MLA_QPAD:h * MLA_QPAD + MLA_NOPE] = (nope * r * gn).astype(bf16)
        k_ref[:, h * MLA_QPAD + MLA_NOPE:(h + 1) * MLA_QPAD] = jnp.where(low, pe * r, 0.0).astype(bf16)


def _rope_tables(t_len, tm):
    n_freq = MLA_ROPE // 4
    t = jnp.arange(t_len, dtype=jnp.int32)
    inv = ROPE_BASE ** (-jnp.arange(n_freq, dtype=f32) / n_freq)
    ang_r = (t // GRID_W).astype(f32)[:, None] * inv
    ang_c = (t % GRID_W).astype(f32)[:, None] * inv
    cos64 = jnp.concatenate([jnp.cos(ang_r), jnp.cos(ang_r), jnp.cos(ang_c), jnp.cos(ang_c)], axis=1)
    sin64 = jnp.concatenate([-jnp.sin(ang_r), jnp.sin(ang_r), -jnp.sin(ang_c), jnp.sin(ang_c)], axis=1)
    cos = jnp.concatenate([jnp.ones((tm, LANES), f32), jnp.tile(cos64, (1, 2))], axis=0)
    sin = jnp.concatenate([jnp.zeros((tm, LANES), f32), jnp.tile(sin64, (1, 2))], axis=0)
    return cos, sin


def _mla_q(z, lay, q_g, cos, sin, rope_block, tm):
    m = z.shape[0]
    gn = q_g[:MLA_NOPE].reshape(1, MLA_NOPE)
    gp = jnp.tile(q_g[MLA_NOPE:], 2).reshape(1, LANES)
    wq = MLA_HEADS * MLA_QK
    assert MLA_HEADS % 2 == 0 and 2 * MLA_QK == 3 * LANES and 2 * MLA_ROPE == LANES and lay.mq % wq == 0
    return pl.pallas_call(
        functools.partial(_mla_q_kernel, scale=MLA_QK ** -0.5 * LOG2E),
        grid=(m // tm,),
        in_specs=[
            pl.BlockSpec((tm, wq), lambda i: (i, lay.mq // wq)),
            pl.BlockSpec((1, MLA_NOPE), lambda i: (0, 0)),
            pl.BlockSpec((1, LANES), lambda i: (0, 0)),
            pl.BlockSpec((tm, LANES), lambda i: (rope_block(i), 0)),
            pl.BlockSpec((tm, LANES), lambda i: (rope_block(i), 0)),
        ],
        out_specs=pl.BlockSpec((tm, MLA_HEADS * MLA_QPAD), lambda i: (i, 0)),
        out_shape=jax.ShapeDtypeStruct((m, MLA_HEADS * MLA_QPAD), bf16),
        compiler_params=_params(("arbitrary",)),
        name="mla_q",
    )(z, gn, gp, cos, sin)


def _mla_kv(ckv_src, ckv_col, kpe_src, kpe_col, kv_g, k_g, w_uk, w_uv, layer, cos, sin, rope_block, tm, norm_ckv, rope,
            name, state=None):
    m = ckv_src.shape[0]
    gn = k_g[:MLA_NOPE].reshape(1, MLA_NOPE)
    gp = jnp.concatenate([k_g[MLA_NOPE:], jnp.zeros((LANES - MLA_ROPE,), f32)]).reshape(1, LANES)
    hw = MLA_HEADS * MLA_NOPE
    in_specs = [
        pl.BlockSpec((tm, MLA_KV_RANK), lambda i: (i, ckv_col // MLA_KV_RANK)),
        pl.BlockSpec((tm, LANES), lambda i: (i, kpe_col // LANES)),
        pl.BlockSpec((1, MLA_KV_RANK), lambda i: (0, 0)),
        pl.BlockSpec((1, MLA_NOPE), lambda i: (0, 0)),
        pl.BlockSpec((1, LANES), lambda i: (0, 0)),
        pl.BlockSpec((None, MLA_KV_RANK, hw), lambda i: (layer, 0, 0)),
        pl.BlockSpec((None, MLA_KV_RANK, hw), lambda i: (layer, 0, 0)),
        pl.BlockSpec((tm, LANES), lambda i: (rope_block(i), 0)),
        pl.BlockSpec((tm, LANES), lambda i: (rope_block(i), 0)),
    ]
    args = [ckv_src, kpe_src, kv_g.reshape(1, MLA_KV_RANK), gn, gp, w_uk, w_uv, cos, sin]
    out_specs = [pl.BlockSpec((tm, MLA_HEADS * MLA_QPAD), lambda i: (i, 0)), pl.BlockSpec((tm, hw), lambda i: (i, 0))]
    out_shape = [jax.ShapeDtypeStruct((m, MLA_HEADS * MLA_QPAD), bf16), jax.ShapeDtypeStruct((m, hw), bf16)]
    kwargs, aliases = {}, {}
    if state is not None:
        aliases = state.aliases(len(args), len(out_specs))
        kwargs = state.kernel_kwargs()
        in_specs += state.in_specs
        args += state.prevs
        out_specs += state.out_specs
        out_shape += state.out_shape
    return pl.pallas_call(
        functools.partial(_mla_kv_kernel, norm_ckv=norm_ckv, rope=rope, **kwargs),
        grid=(m // tm,),
        in_specs=in_specs,
        out_specs=out_specs,
        out_shape=out_shape,
        input_output_aliases=aliases,
        compiler_params=_params(("arbitrary",)),
        name=name,
    )(*args)


def _flash_kernel(*refs, n_main, tk, n_extra, heads, dq, dv):
    if n_extra:
        q_ref, k_ref, v_ref, kx_ref, vx_ref, o_ref = refs
    else:
        q_ref, k_ref, v_ref, o_ref = refs
        kx_ref = vx_ref = None
    tq = q_ref.shape[0]
    chunks = [(k_ref, v_ref, c) for c in range(n_main)] + [(kx_ref, vx_ref, c) for c in range(n_extra)]
    def scores(i, g):
        kr, _, c = chunks[i]
        return _dot_nt(kr[c * tk:(c + 1) * tk, g * dq:(g + 1) * dq], q_ref[:, g * dq:(g + 1) * dq])

    m = [jnp.full((1, tq), -jnp.inf, f32)] * heads
    l = [jnp.zeros((1, tq), f32)] * heads
    acc = [jnp.zeros((dv, tq), f32)] * heads
    s = [scores(0, g) for g in range(heads)]
    for i, (_, vr, c) in enumerate(chunks):
        for g in range(heads):
            s_next = scores(i + 1, g) if i + 1 < len(chunks) else None
            m_new = jnp.maximum(m[g], jnp.max(s[g], axis=0, keepdims=True))
            alpha = jnp.exp2(m[g] - m_new)
            p = jnp.exp2(s[g] - m_new)
            l[g] = alpha * l[g] + jnp.sum(p, axis=0, keepdims=True)
            acc[g] = alpha * acc[g] + _dot_tn(vr[c * tk:(c + 1) * tk, g * dv:(g + 1) * dv], p.astype(bf16))
            m[g], s[g] = m_new, s_next
    for g in range(heads):
        o_ref[:, g * dv:(g + 1) * dv] = (acc[g] / l[g]).T.astype(o_ref.dtype)


def _flash(q, k, v, *, batch, heads, t_len, s_len, row0, dq, dv, tq, tk, heads_per_step=1, extra=None, name):
    assert row0 % tq == 0 and row0 % s_len == 0 and t_len % tq == 0 and s_len % tk == 0
    assert heads % heads_per_step == 0
    nq = t_len // tq
    g = heads_per_step
    in_specs = [
        pl.BlockSpec((tq, g * dq), lambda b, h, i: (row0 // tq + b * nq + i, h)),
        pl.BlockSpec((s_len, g * dq), lambda b, h, i: (row0 // s_len + b, h)),
        pl.BlockSpec((s_len, g * dv), lambda b, h, i: (row0 // s_len + b, h)),
    ]
    args = [q, k, v]
    n_extra = 0
    if extra is not None:
        kx, vx, p_len = extra
        assert p_len % tk == 0
        n_extra = p_len // tk
        in_specs += [
            pl.BlockSpec((p_len, g * dq), lambda b, h, i: (b, h)),
            pl.BlockSpec((p_len, g * dv), lambda b, h, i: (b, h)),
        ]
        args += [kx, vx]
    return pl.pallas_call(
        functools.partial(_flash_kernel, n_main=s_len // tk, tk=tk, n_extra=n_extra, heads=g, dq=dq, dv=dv),
        grid=(batch, heads // g, nq),
        in_specs=in_specs,
        out_specs=pl.BlockSpec((tq, g * dv), lambda b, h, i: (b * nq + i, h)),
        out_shape=jax.ShapeDtypeStruct((batch * t_len, heads * dv), bf16),
        compiler_params=_params(("arbitrary", "arbitrary", "arbitrary")),
        name=name,
    )(*args)


def _na_prep_kernel(*refs, scale, n_prev, n_ctx_tiles, s_len):
    nq_ref, nk_ref, nv_ref, qg_ref, kg_ref = refs[:5]
    q_ref, k_ref, v_ref, k_state_ref, v_state_ref = refs[5 + n_prev:]
    qg = qg_ref[...]
    kg = kg_ref[...]
    tm = nv_ref.shape[0]
    is_ctx = pl.program_id(0) < n_ctx_tiles
    v_ref[...] = nv_ref[...].astype(bf16)

    @pl.when(is_ctx)
    def _():
        for n in range(tm // s_len):
            v_state_ref[n] = nv_ref[n * s_len:(n + 1) * s_len, :]

    for h in range(NA_HEADS):
        sl = slice(h * NA_DH, (h + 1) * NA_DH)
        q = nq_ref[:, sl]
        k = nk_ref[:, sl]
        rq = lax.rsqrt(jnp.mean(q * q, axis=-1, keepdims=True) + EPS)
        rk = lax.rsqrt(jnp.mean(k * k, axis=-1, keepdims=True) + EPS)
        q_ref[:, sl] = (q * rq * qg * scale).astype(bf16)
        kn = k * rk * kg
        k_ref[:, sl] = kn.astype(bf16)

        @pl.when(is_ctx)
        def _():
            for n in range(tm // s_len):
                k_state_ref[n, :, sl] = kn[n * s_len:(n + 1) * s_len]


def _na_prep(z, lay, q_g, k_g, tm, state):
    m = z.shape[0]
    w = NA_HEADS * NA_DH
    spec = pl.BlockSpec((tm, w), lambda i: (i, 0))
    in_specs = [
        pl.BlockSpec((tm, w), lambda i: (i, lay.nq // w)),
        pl.BlockSpec((tm, w), lambda i: (i, lay.nk // w)),
        pl.BlockSpec((tm, w), lambda i: (i, lay.nv // w)),
        pl.BlockSpec((1, NA_DH), lambda i: (0, 0)),
        pl.BlockSpec((1, NA_DH), lambda i: (0, 0)),
    ]
    args = [z, z, z, q_g.reshape(1, NA_DH), k_g.reshape(1, NA_DH)]
    return pl.pallas_call(
        functools.partial(_na_prep_kernel, scale=NA_DH ** -0.5 * LOG2E, **state.kernel_kwargs()),
        grid=(m // tm,),
        in_specs=in_specs + state.in_specs,
        out_specs=[spec, spec, spec] + state.out_specs,
        out_shape=[jax.ShapeDtypeStruct((m, w), bf16)] * 3 + state.out_shape,
        input_output_aliases=state.aliases(len(args), 3),
        compiler_params=_params(("arbitrary",)),
        name="na_prep",
    )(*args, *state.prevs)


def _na_static_maps(rows):
    nblk = rows // NA_QROWS
    reps = (0, min(1, nblk - 1), nblk - 1)
    dr_map = np.full((3, NA_QROWS, NA_KROWS), 2 * NA_WIN_R - 1, np.int32)
    for v, kb in enumerate(reps):
        ws = min(max(NA_QROWS * kb - NA_WIN_R // 2, 0), rows - NA_KROWS)
        for i in range(NA_QROWS):
            qrow = NA_QROWS * kb + i
            rs = min(max(qrow - NA_WIN_R // 2, 0), rows - NA_WIN_R)
            for j in range(NA_KROWS):
                krow = ws + j
                if rs <= krow < rs + NA_WIN_R:
                    dr_map[v, i, j] = krow - qrow + NA_WIN_R - 1
    qcol = np.arange(GRID_W)
    cs = np.clip(qcol - NA_WIN_C // 2, 0, GRID_W - NA_WIN_C)
    in_win = (qcol[None, :] >= cs[:, None]) & (qcol[None, :] < cs[:, None] + NA_WIN_C)
    dc_idx = np.clip(qcol[None, :] - qcol[:, None], -(NA_WIN_C - 1), NA_WIN_C - 1) + (NA_WIN_C - 1)
    return dr_map, in_win, dc_idx


def _na_bias_table(rpb, rows):
    dr_map, in_win, dc_idx = _na_static_maps(rows)
    depth, h = rpb.shape[:2]
    bt = jnp.where(in_win.T, jnp.take(rpb * LOG2E, jnp.asarray(dc_idx.T), axis=3), NEG)
    bt = jnp.concatenate([bt, jnp.full((depth, h, 1, GRID_W, GRID_W), NEG, f32)], axis=2)
    tab = jnp.take(bt, jnp.asarray(dr_map.transpose(0, 2, 1).reshape(-1)), axis=2)
    tab = tab.reshape(depth, h, 3, NA_KROWS, NA_QROWS, GRID_W, GRID_W).transpose(0, 2, 1, 3, 5, 4, 6)
    return tab.reshape(depth, 3, h, NA_KROWS * GRID_W, NA_QROWS * GRID_W)


def _na_kernel(q_ref, k_ref, v_ref, kc_ref, vc_ref, bias_ref, o_ref, *, rows, heads):
    kb = pl.program_id(2)
    ws = jnp.clip(NA_QROWS * kb - NA_WIN_R // 2, 0, rows - NA_KROWS)
    win = pl.ds(pl.multiple_of(ws * GRID_W, (NA_WIN_R // 2) * GRID_W), NA_KROWS * GRID_W)
    cols = [slice(g * NA_DH, (g + 1) * NA_DH) for g in range(heads)]
    s_loc = [_dot_nt(k_ref[win, c], q_ref[:, c]) + bias_ref[g] for g, c in enumerate(cols)]
    s_ctx = [_dot_nt(kc_ref[:, c], q_ref[:, c]) for c in cols]
    p_loc, p_ctx, l = [], [], []
    for g in range(heads):
        m = jnp.maximum(jnp.max(s_loc[g], axis=0, keepdims=True), jnp.max(s_ctx[g], axis=0, keepdims=True))
        pl_g = jnp.exp2(s_loc[g] - m)
        pc_g = jnp.exp2(s_ctx[g] - m)
        l.append(jnp.sum(pl_g, axis=0, keepdims=True) + jnp.sum(pc_g, axis=0, keepdims=True))
        p_loc.append(pl_g.astype(bf16))
        p_ctx.append(pc_g.astype(bf16))
    for g, c in enumerate(cols):
        acc = _dot_tn(v_ref[win, c], p_loc[g]) + _dot_tn(vc_ref[:, c], p_ctx[g])
        o_ref[:, c] = (acc / l[g]).T.astype(o_ref.dtype)


def _na_latent(q, k, v, kc, vc, bias, layer, *, batch, t_len, p_len, row0, heads_per_step=2):
    rows = t_len // GRID_W
    assert t_len % GRID_W == 0 and rows % NA_QROWS == 0 and rows >= NA_KROWS
    tq = NA_QROWS * GRID_W
    assert row0 % tq == 0 and row0 % t_len == 0 and NA_HEADS % heads_per_step == 0
    nblk = rows // NA_QROWS
    g = heads_per_step
    gw = g * NA_DH

    def variant(i):
        return jnp.where(i == 0, 0, jnp.where(i == nblk - 1, 2, 1))

    return pl.pallas_call(
        functools.partial(_na_kernel, rows=rows, heads=g),
        grid=(batch, NA_HEADS // g, nblk),
        in_specs=[
            pl.BlockSpec((tq, gw), lambda b, h, i: (row0 // tq + b * nblk + i, h)),
            pl.BlockSpec((t_len, gw), lambda b, h, i: (row0 // t_len + b, h)),
            pl.BlockSpec((t_len, gw), lambda b, h, i: (row0 // t_len + b, h)),
            pl.BlockSpec((None, None, p_len, gw), lambda b, h, i: (b, layer, 0, h)),
            pl.BlockSpec((None, None, p_len, gw), lambda b, h, i: (b, layer, 0, h)),
            pl.BlockSpec((None, None, g, NA_KROWS * GRID_W, tq), lambda b, h, i: (layer, variant(i), h, 0, 0)),
        ],
        out_specs=pl.BlockSpec((tq, gw), lambda b, h, i: (b * nblk + i, h)),
        out_shape=jax.ShapeDtypeStruct((batch * t_len, NA_HEADS * NA_DH), bf16),
        compiler_params=_params(("arbitrary", "arbitrary", "arbitrary")),
        name="na_latent",
    )(q, k, v, kc, vc, bias)


def _hgrn_tables():
    c, lv = HG_CHUNK, HG_LEVELS
    t = np.arange(c)[:, None]
    r = np.arange(c)[None, :]
    wcum = np.stack([r <= t, r >= t]).astype(np.float32)
    qside = np.zeros((2, lv * c, LANES), np.float32)
    mask = np.zeros((2, lv + 1, c, c), np.float32)
    mask[:, 0] = np.eye(c)
    for l in range(lv):
        hs = 1 << l
        blk = t // (2 * hs)
        mid = blk * (2 * hs) + hs
        qf = t >= mid
        qb = t < mid
        qside[0, l * c:(l + 1) * c] = np.where(qf, 1.0, -1.0)
        qside[1, l * c:(l + 1) * c] = np.where(qb, 1.0, -1.0)
        same = blk == blk.T
        mask[0, 1 + l] = same & qf & ~qf.T
        mask[1, 1 + l] = same & qb & ~qb.T
    return wcum, qside, mask


def _hgrn_kernel(*refs, n_chunks, heads, has_state, emit_state, n_prev=0):
    (hq_ref, hff_ref, hfb_ref, hi_ref, hg_ref, la_ref, lc_ref, om_ref, ng_ref,
     wcum_ref, qside_ref, mask_ref) = refs[:12]
    rest = list(refs[12:])
    s0_ref = rest.pop(0) if has_state else None
    del rest[:n_prev]
    o_ref = rest.pop(0)
    st_ref = rest.pop(0) if emit_state else None
    o_scr, s_scr, b_scr = rest
    c, lv = HG_CHUNK, HG_LEVELS

    o_scr[...] = jnp.zeros_like(o_scr)
    for d in range(2):
        for g in range(heads):
            if has_state:
                s_scr[d, g] = s0_ref[d, g].T
            else:
                s_scr[d, g] = jnp.zeros((HG_V, HG_K), f32)

    chains = [(d, g) for g in range(heads) for d in range(2)]
    low_half = lax.broadcasted_iota(jnp.int32, (SUBLANES, HEAD_W), 0) < SUBLANES // 2

    def gates_and_cumsum(ci, row):
        d, g = chains[ci]
        rs = pl.ds(pl.multiple_of(row, c), c)
        cs = slice(g * HEAD_W, (g + 1) * HEAD_W)
        x = (hff_ref if d == 0 else hfb_ref)[rs, cs]
        x2 = x * LOG2E
        u = jnp.exp2(-jnp.abs(x2))
        inv = 1.0 / (1.0 + u)
        pos = x >= 0.0
        log_sig = jnp.minimum(x2, 0.0) - jnp.log2(1.0 + u)
        a = la_ref[d:d + 1, cs]
        bb = lc_ref[d:d + 1, cs] + log_sig
        lf = jnp.maximum(a, bb) + jnp.log2(1.0 + jnp.exp2(-jnp.abs(a - bb)))
        om = om_ref[d:d + 1, cs]
        f = (1.0 - om) + om * (jnp.where(pos, 1.0, u) * inv)
        kin = om * (jnp.where(pos, u, 1.0) * inv)
        q = _silu(hq_ref[rs, cs])
        v = hi_ref[rs, cs].astype(bf16)
        l1 = lf.astype(bf16)
        r1 = lf - l1.astype(f32)
        l2 = r1.astype(bf16)
        l3 = (r1 - l2.astype(f32)).astype(bf16)
        w = wcum_ref[d]
        b = _dot(w, l1) + _dot(w, l2) + _dot(w, l3)
        b_scr[ci] = b
        return dict(rs=rs, cs=cs, f=f, kin=kin, q=q, v=v, b=b)

    def intra_chunk(ci, s):
        d, _ = chains[ci]
        b, q, kin = s["b"], s["q"], s["kin"]

        def row(r):
            return b_scr[ci, r:r + 1, :]

        s["b_end"] = row(c - 1 if d == 0 else 0)
        xs = [jnp.where(qside_ref[d, 0:c, :] > 0.0, q * s["f"], kin).astype(bf16)]
        for l in range(1, lv):
            hs = 1 << l
            blk = 2 * hs
            pick = hs - 1 if d == 0 else hs
            if hs >= SUBLANES:
                pieces = []
                for j in range(c // blk):
                    m = row(blk * j + pick)
                    early = slice(blk * j, blk * j + hs)
                    late = slice(blk * j + hs, blk * (j + 1))
                    q_sl, k_sl = (late, early) if d == 0 else (early, late)
                    xq = q[q_sl] * jnp.exp2(b[q_sl] - m)
                    xk = kin[k_sl] * jnp.exp2(m - b[k_sl])
                    pieces += [xk, xq] if d == 0 else [xq, xk]
                xs.append(jnp.concatenate(pieces, axis=0).astype(bf16))
                continue
            if blk < SUBLANES:
                pieces = [jnp.where(low_half, jnp.broadcast_to(row(SUBLANES * j + pick), (SUBLANES, HEAD_W)),
                                    jnp.broadcast_to(row(SUBLANES * j + blk + pick), (SUBLANES, HEAD_W)))
                          for j in range(c // SUBLANES)]
            else:
                pieces = [jnp.broadcast_to(row(blk * j + pick), (blk, HEAD_W)) for j in range(c // blk)]
            m = pieces[0] if len(pieces) == 1 else jnp.concatenate(pieces, axis=0)
            sign = qside_ref[d, l * c:(l + 1) * c, :]
            xs.append((jnp.where(sign > 0.0, q, kin) * jnp.exp2((b - m) * sign)).astype(bf16))
        att = mask_ref[d, 0] * _dot_nt(q.astype(bf16), kin.astype(bf16)).astype(bf16)
        for l in range(lv):
            att = att + mask_ref[d, 1 + l] * _dot_nt(xs[l], xs[l]).astype(bf16)
        s["att"] = att

    def state_step(ci, s):
        d, g = chains[ci]
        b, q, kin, v = s["b"], s["q"], s["kin"], s["v"]
        st = s_scr[d, g]
        o = _dot(s["att"], v) + _dot_nt((q * jnp.exp2(b)).astype(bf16), st.astype(bf16))
        b_end = s["b_end"]
        s_scr[d, g] = st * jnp.exp2(b_end) + _dot_tn(v, (kin * jnp.exp2(b_end - b)).astype(bf16))
        o_scr[s["rs"], s["cs"]] += o

    def body(i, _):
        rows = (i * c, (n_chunks - 1 - i) * c)
        states = [gates_and_cumsum(ci, rows[chains[ci][0]]) for ci in range(len(chains))]
        for ci, s in enumerate(states):
            intra_chunk(ci, s)
        for ci, s in enumerate(states):
            state_step(ci, s)
        return 0

    lax.fori_loop(0, n_chunks, body, 0)

    ng = ng_ref[...]
    for g in range(heads):
        cs = slice(g * HEAD_W, (g + 1) * HEAD_W)
        o = o_scr[:, cs]
        y = o * lax.rsqrt(jnp.mean(o * o, axis=-1, keepdims=True) + EPS) * ng
        o_ref[:, cs] = (y * _silu(hg_ref[:, cs])).astype(o_ref.dtype)
        if emit_state:
            for d in range(2):
                st_ref[d, g] = s_scr[d, g].T


def _hgrn(z, lay, lb_tabs, norm_g, tables, *, batch, t_len, row0, s0, state_out=None, heads_per_step=2):
    emit_state = state_out is not None
    assert t_len % HG_CHUNK == 0 and row0 % t_len == 0 and HG_K == HEAD_W and HG_V == HEAD_W
    g = heads_per_step
    gw = g * HEAD_W
    la, lc, om = lb_tabs
    wcum, qside, mask = tables
    r0 = row0 // t_len

    def zspec(col):
        return pl.BlockSpec((t_len, gw), lambda b, h: (r0 + b, col // gw + h))

    def const(shape):
        return pl.BlockSpec(shape, lambda b, h: (0,) * len(shape))

    in_specs = [zspec(lay.hq), zspec(lay.hff), zspec(lay.hfb), zspec(lay.hi), zspec(lay.hg),
                pl.BlockSpec((2, gw), lambda b, h: (0, h)), pl.BlockSpec((2, gw), lambda b, h: (0, h)),
                pl.BlockSpec((2, gw), lambda b, h: (0, h)), const((1, HG_V)),
                const(wcum.shape), const(qside.shape), const(mask.shape)]
    args = [z, z, z, z, z, la, lc, om, norm_g.reshape(1, HG_V), wcum, qside, mask]
    state_spec = pl.BlockSpec((None, 2, g, HG_K, HG_V), lambda b, h: (b, 0, h, 0, 0))
    if s0 is not None:
        in_specs.append(state_spec)
        args.append(s0)
    out_specs = [pl.BlockSpec((t_len, gw), lambda b, h: (b, h))]
    out_shape = [jax.ShapeDtypeStruct((batch * t_len, HG_HEADS * HG_V), bf16)]
    aliases, n_prev = {}, 0
    if emit_state:
        prev, layer, depth = state_out
        aliases, n_prev = {len(args): 1}, 1
        in_specs.append(pl.BlockSpec(memory_space=pl.ANY))
        args.append(prev)
        out_specs.append(pl.BlockSpec((None, None, 2, g, HG_K, HG_V), lambda b, h: (b, layer, 0, h, 0, 0)))
        out_shape.append(jax.ShapeDtypeStruct((batch, depth, 2, HG_HEADS, HG_K, HG_V), f32))
    res = pl.pallas_call(
        functools.partial(_hgrn_kernel, n_chunks=t_len // HG_CHUNK, heads=g, has_state=s0 is not None,
                          emit_state=emit_state, n_prev=n_prev),
        grid=(batch, HG_HEADS // g),
        in_specs=in_specs,
        out_specs=out_specs,
        out_shape=out_shape,
        input_output_aliases=aliases,
        scratch_shapes=[pltpu.VMEM((t_len, gw), f32), pltpu.VMEM((2, g, HG_V, HG_K), f32),
                        pltpu.VMEM((2 * g, HG_CHUNK, HEAD_W), f32)],
        compiler_params=_params(("arbitrary", "arbitrary")),
        name="hgrn_ctx" if emit_state else "hgrn_latent",
    )(*args)
    return res if emit_state else (res[0], None)


def _merge_kernel(oac_ref, oal_ref, obc_ref, obl_ref, occ_ref, ocl_ref, ga_ref, gb_ref, gc_ref, w_ref, o_ref,
                  *, n_ctx_tiles):
    def compute(oa_ref, ob_ref, oc_ref):
        m = jax.nn.sigmoid(ga_ref[...]) * _dot(oa_ref[...], w_ref[0])
        m = m + jax.nn.sigmoid(gb_ref[...]) * _dot(ob_ref[...], w_ref[1])
        m = m + jax.nn.sigmoid(gc_ref[...]) * _dot(oc_ref[...], w_ref[2])
        o_ref[...] = m.astype(o_ref.dtype)

    is_ctx = pl.program_id(1) < n_ctx_tiles

    @pl.when(is_ctx)
    def _():
        compute(oac_ref, obc_ref, occ_ref)

    @pl.when(jnp.logical_not(is_ctx))
    def _():
        compute(oal_ref, obl_ref, ocl_ref)


def _merge(o_a, o_b, o_c, z, lay, w_branch, layer, tm, tn):
    bw = o_a[0].shape[1]
    m = z.shape[0]
    d = w_branch.shape[3]
    nct = o_a[0].shape[0] // tm
    cspec = pl.BlockSpec((tm, bw), lambda j, i: (jnp.minimum(i, nct - 1), 0))
    lspec = pl.BlockSpec((tm, bw), lambda j, i: (jnp.maximum(i - nct, 0), 0))

    def gate(col):
        return pl.BlockSpec((tm, tn), lambda j, i: (i, col // tn + j))

    return pl.pallas_call(
        functools.partial(_merge_kernel, n_ctx_tiles=nct),
        grid=(d // tn, m // tm),
        in_specs=[cspec, lspec, cspec, lspec, cspec, lspec, gate(lay.ga), gate(lay.gb), gate(lay.gc),
                  pl.BlockSpec((None, 3, bw, tn), lambda j, i: (layer, 0, 0, j))],
        out_specs=pl.BlockSpec((tm, tn), lambda j, i: (i, j)),
        out_shape=jax.ShapeDtypeStruct((m, d), bf16),
        compiler_params=_params(("arbitrary", "arbitrary")),
        name="merge",
    )(o_a[0], o_a[1], o_b[0], o_b[1], o_c[0], o_c[1], z, z, z, w_branch)


def _proj_residual_kernel(m_ref, w_ref, x_ref, g_ref, o_ref):
    o_ref[...] = x_ref[...] + g_ref[...] * _dot(m_ref[...], w_ref[...])


def _proj_residual(mm, w, layer, x, gate, group_of_tile, tm, tn):
    m, k = mm.shape
    d = w.shape[2]
    return pl.pallas_call(
        _proj_residual_kernel,
        grid=(m // tm, d // tn),
        in_specs=[
            pl.BlockSpec((tm, k), lambda i, j: (i, 0)),
            pl.BlockSpec((None, k, tn), lambda i, j: (layer, 0, j)),
            pl.BlockSpec((tm, tn), lambda i, j: (i, j)),
            pl.BlockSpec((None, 1, tn), lambda i, j: (group_of_tile(i), 0, j)),
        ],
        out_specs=pl.BlockSpec((tm, tn), lambda i, j: (i, j)),
        out_shape=jax.ShapeDtypeStruct((m, d), f32),
        compiler_params=_params(("arbitrary", "arbitrary")),
        name="out_proj",
    )(mm, w, x, gate)


def _ffn_kernel(x_ref, xprev_ref, xnext_ref, ng_ref, sc_ref, sh_ref, wa_ref, wg_ref, cw_ref, cb_ref, wd_ref, gate_ref,
                o_ref, h_scr, *, m_ctx, s_len, t_len, n_k):
    i = pl.program_id(0)
    k = pl.program_id(1)
    tm = x_ref.shape[0]
    tk = wa_ref.shape[1]
    halo = SUBLANES

    @pl.when(k == 0)
    def _():
        def normmod(x):
            ms = jnp.mean(x * x, axis=-1, keepdims=True)
            y = x * lax.rsqrt(ms + EPS) * ng_ref[...]
            return (y * (1.0 + sc_ref[...]) + sh_ref[...]).astype(bf16)

        h_scr[0:halo] = normmod(xprev_ref[...])
        h_scr[halo:halo + tm] = normmod(x_ref[...])
        h_scr[halo + tm:halo + tm + halo] = normmod(xnext_ref[...])
        o_ref[...] = jnp.zeros_like(o_ref)

    row = i * tm + lax.broadcasted_iota(jnp.int32, (tm, LANES), 0)
    pos = jnp.where(row < m_ctx, _mod_const(row, s_len), _mod_const(row - m_ctx, t_len))
    last = jnp.where(row < m_ctx, s_len - 1, t_len - 1)
    keep_prev = jnp.tile(jnp.where(pos == 0, 0.0, 1.0), (1, tk // LANES))
    keep_next = jnp.tile(jnp.where(pos == last, 0.0, 1.0), (1, tk // LANES))

    g_ext = _dot(h_scr[...], wg_ref[...])
    a = _dot(h_scr[halo:halo + tm], wa_ref[...])
    n_ext = tm + 2 * halo
    g = g_ext[halo:halo + tm]
    g_prev = pltpu.roll(g_ext, 1, 0)[halo:halo + tm]
    g_next = pltpu.roll(g_ext, n_ext - 1, 0)[halo:halo + tm]
    conv = (g_prev * keep_prev * cw_ref[0:1, :] + g * cw_ref[1:2, :] + g_next * keep_next * cw_ref[2:3, :]
            + cb_ref[...])
    u = (_silu(conv) * a).astype(bf16)
    o_ref[...] = _dot(u, wd_ref[...]) + o_ref[...]

    @pl.when(k == n_k - 1)
    def _():
        o_ref[...] = x_ref[...] + gate_ref[...] * o_ref[...]


def _ffn(x, norm_g, sc, sh, w_up, conv_w, conv_b, w_down, layer, gate, group_of_tile, tm, tk, m_ctx, s_len, t_len):
    m, d = x.shape
    f = w_down.shape[1]
    nk = f // tk
    nsub = tm // SUBLANES
    last_sub = m // SUBLANES - 1
    return pl.pallas_call(
        functools.partial(_ffn_kernel, m_ctx=m_ctx, s_len=s_len, t_len=t_len, n_k=nk),
        grid=(m // tm, nk),
        in_specs=[
            pl.BlockSpec((tm, d), lambda i, k: (i, 0)),
            pl.BlockSpec((SUBLANES, d), lambda i, k: (jnp.maximum(i * nsub - 1, 0), 0)),
            pl.BlockSpec((SUBLANES, d), lambda i, k: (jnp.minimum((i + 1) * nsub, last_sub), 0)),
            pl.BlockSpec((1, d), lambda i, k: (0, 0)),
            pl.BlockSpec((None, 1, d), lambda i, k: (group_of_tile(i), 0, 0)),
            pl.BlockSpec((None, 1, d), lambda i, k: (group_of_tile(i), 0, 0)),
            pl.BlockSpec((None, d, tk), lambda i, k: (layer, 0, k)),
            pl.BlockSpec((None, d, tk), lambda i, k: (layer, 0, nk + k)),
            pl.BlockSpec((3, tk), lambda i, k: (0, k)),
            pl.BlockSpec((1, tk), lambda i, k: (0, k)),
            pl.BlockSpec((None, tk, d), lambda i, k: (layer, k, 0)),
            pl.BlockSpec((None, 1, d), lambda i, k: (group_of_tile(i), 0, 0)),
        ],
        out_specs=pl.BlockSpec((tm, d), lambda i, k: (i, 0)),
        out_shape=jax.ShapeDtypeStruct((m, d), f32),
        scratch_shapes=[pltpu.VMEM((tm + 2 * SUBLANES, d), bf16)],
        compiler_params=_params(("arbitrary", "arbitrary")),
        name="ffn",
    )(x, x, x, norm_g.reshape(1, d), sc, sh, w_up, w_up, conv_w, conv_b.reshape(1, f), w_down, gate)


def kernel(x_prompt, x_sample, cache_mla_ckv, cache_mla_kpe, cache_na_k, cache_na_v, state_hgrn, c, c_ctx, w_mod, b_mod, norm1_g, norm2_g, w_in, mla_kv_norm_g, mla_q_norm_g, mla_k_norm_g, mla_w_uk, mla_w_uv, hgrn_lower_bounds, hgrn_norm_g, na_q_norm_g, na_k_norm_g, na_rpb, w_branch, w_out, ffn_w_up, ffn_conv_w, ffn_conv_b, ffn_w_down):
    n_ctx, s_len, d = x_prompt.shape
    n_lat, t_len, _ = x_sample.shape
    depth = w_in.shape[0]
    p_len = cache_mla_ckv.shape[2]
    ffn = ffn_w_down.shape[1]
    m_ctx, m_lat = n_ctx * s_len, n_lat * t_len
    m = m_ctx + m_lat
    assert m_ctx % t_len == 0, "context rows must be a whole number of latent sequences"

    tm = _pow2_tile(1024, m_ctx, t_len)
    tn = _pow2_tile(1024, d)
    tp = _pow2_tile(512, m_ctx, t_len)
    lay = _ZLayout(d, tn)

    def group_fn(rows):
        return lambda i: jnp.where(i < m_ctx // rows, 0, 1 + (i - m_ctx // rows) // (t_len // rows))

    group_of_tile = group_fn(tm)

    n_groups = -(-(1 + n_lat) // SUBLANES) * SUBLANES
    cond = jnp.concatenate([c_ctx[None], c, jnp.zeros((n_groups - 1 - n_lat, d), f32)], axis=0)
    mods = _modulation(cond, w_mod, b_mod).reshape(depth, n_groups, 6, d).transpose(0, 2, 1, 3)[:, :, :, None, :]

    sm = jax.nn.softmax(hgrn_lower_bounds.astype(f32), axis=1)
    csum = jnp.cumsum(sm, axis=1)
    lower = csum - csum[:, :1]
    hg_tabs = tuple(jnp.asarray(t, dt) for t, dt in zip(_hgrn_tables(), (bf16, f32, bf16)))

    cos, sin = _rope_tables(t_len, tp)
    n_ctx_tp = m_ctx // tp

    def rope_block(i):
        return jnp.where(i < n_ctx_tp, 0, 1 + (i - n_ctx_tp) % (t_len // tp))

    x = jnp.concatenate([x_prompt.reshape(m_ctx, d), x_sample.reshape(m_lat, d)], axis=0)
    w_in_b, w_kpe_b = _permute_w_in(w_in, lay)
    w_uk = mla_w_uk.reshape(depth, MLA_KV_RANK, MLA_HEADS * MLA_NOPE).astype(bf16)
    w_uv = mla_w_uv.reshape(depth, MLA_KV_RANK, MLA_HEADS * MLA_V).astype(bf16)
    w_branch_b, w_out_b = w_branch.astype(bf16), w_out.astype(bf16)
    w_up_b, w_down_b = ffn_w_up.astype(bf16), ffn_w_down.astype(bf16)
    na_bias = _na_bias_table(na_rpb, t_len // GRID_W)
    kc_na = cache_na_k.reshape(n_lat, depth, p_len, NA_HEADS * NA_DH).astype(bf16)
    vc_na = cache_na_v.reshape(n_lat, depth, p_len, NA_HEADS * NA_DH).astype(bf16)

    def state_init(*trailing):
        return jnp.zeros((n_ctx, depth) + trailing, f32)

    st_mla = [state_init(s_len, MLA_KV_RANK), state_init(s_len, MLA_ROPE)]
    st_na = [state_init(s_len, NA_HEADS * NA_DH), state_init(s_len, NA_HEADS * NA_DH)]
    st_hg = state_init(2, HG_HEADS, HG_K, HG_V)
    for l in range(depth):
        sh1, sc1, g1, sh2, sc2, g2 = (mods[l, k] for k in range(6))
        z, kpe = _normmod_matmul(x, norm1_g[l], sc1, sh1, w_in_b, l, group_of_tile, tm, tn, "in_proj", w_side=w_kpe_b)

        q_mla = _mla_q(z, lay, mla_q_norm_g[l], cos, sin, rope_block, tp)
        k_mla, v_mla, *st_mla = _mla_kv(
            z, lay.ckv, kpe, 0, mla_kv_norm_g[l], mla_k_norm_g[l], w_uk, w_uv, l, cos, sin, rope_block, tp, True, True,
            "mla_kv", state=_CtxState(st_mla, (MLA_KV_RANK, MLA_ROPE), l, depth, n_ctx, s_len, tp))
        ckv_c = cache_mla_ckv[:, l].reshape(n_lat * p_len, MLA_KV_RANK)
        kpe_c = jnp.pad(cache_mla_kpe[:, l].reshape(n_lat * p_len, MLA_ROPE), ((0, 0), (0, LANES - MLA_ROPE)))
        tc = _pow2_tile(512, n_lat * p_len)
        k_c, v_c = _mla_kv(ckv_c, 0, kpe_c, 0, mla_kv_norm_g[l], mla_k_norm_g[l], w_uk, w_uv, l,
                           cos, sin, lambda i: 0, tc, False, False, "mla_kv_cache")
        tq_c = _pow2_tile(256, s_len)
        o_mla_c = _flash(q_mla, k_mla, v_mla, batch=n_ctx, heads=MLA_HEADS, t_len=s_len, s_len=s_len, row0=0,
                         dq=MLA_QPAD, dv=MLA_V, tq=tq_c, tk=tq_c, heads_per_step=MLA_HEADS, name="mla_attn_ctx")
        tq_l = _pow2_tile(512, t_len, p_len)
        o_mla_l = _flash(q_mla, k_mla, v_mla, batch=n_lat, heads=MLA_HEADS, t_len=t_len, s_len=t_len, row0=m_ctx,
                         dq=MLA_QPAD, dv=MLA_V, tq=tq_l, tk=tq_l, heads_per_step=2, extra=(k_c, v_c, p_len),
                         name="mla_attn_latent")

        lb = lower[:, l]
        lb_tabs = (jnp.log(lb) * LOG2E, jnp.log1p(-lb) * LOG2E, 1.0 - lb)
        o_hg_c, st_hg = _hgrn(z, lay, lb_tabs, hgrn_norm_g[l], hg_tabs, batch=n_ctx, t_len=s_len, row0=0,
                              s0=None, state_out=(st_hg, l, depth), heads_per_step=4)
        o_hg_l, _ = _hgrn(z, lay, lb_tabs, hgrn_norm_g[l], hg_tabs, batch=n_lat, t_len=t_len, row0=m_ctx,
                          s0=state_hgrn[:, l])

        q_na, k_na, v_na, *st_na = _na_prep(
            z, lay, na_q_norm_g[l], na_k_norm_g[l], tp,
            _CtxState(st_na, (NA_HEADS * NA_DH, NA_HEADS * NA_DH), l, depth, n_ctx, s_len, tp))
        o_na_c = _flash(q_na, k_na, v_na, batch=n_ctx, heads=NA_HEADS, t_len=s_len, s_len=s_len, row0=0,
                        dq=NA_DH, dv=NA_DH, tq=tq_c, tk=tq_c, heads_per_step=NA_HEADS, name="na_attn_ctx")
        o_na_l = _na_latent(q_na, k_na, v_na, kc_na, vc_na, na_bias, l, batch=n_lat, t_len=t_len, p_len=p_len, row0=m_ctx)

        merged = _merge((o_mla_c, o_mla_l), (o_hg_c, o_hg_l), (o_na_c, o_na_l), z, lay, w_branch_b, l, tp, tn)
        x = _proj_residual(merged, w_out_b, l, x, g1, group_fn(tp), tp, d)

        x = _ffn(x, norm2_g[l], sc2, sh2, w_up_b, ffn_conv_w[l], ffn_conv_b[l], w_down_b, l, g2, group_of_tile, tm,
                 _pow2_tile(512, ffn), m_ctx, s_len, t_len)

    y_prompt = x[:m_ctx].reshape(n_ctx, s_len, d)
    y_sample = x[m_ctx:].reshape(n_lat, t_len, d)
    na_shape = (n_ctx, depth, s_len, NA_HEADS, NA_DH)
    return (y_prompt, y_sample, st_mla[0], st_mla[1], st_na[0].reshape(na_shape), st_na[1].reshape(na_shape), st_hg)
```

```python
import functools
import math

import numpy as np
import jax
import jax.numpy as jnp
from jax import lax
from jax.experimental import pallas as pl
from jax.experimental.pallas import tpu as pltpu

f32 = jnp.float32
bf16 = jnp.bfloat16

GRID_W = 64
MLA_HEADS, MLA_NOPE, MLA_ROPE, MLA_V, MLA_KV_RANK = 8, 128, 64, 128, 512
MLA_QK = MLA_NOPE + MLA_ROPE
HG_HEADS, HG_K, HG_V = 8, 128, 128
NA_HEADS, NA_DH, NA_WIN_R, NA_WIN_C = 8, 128, 8, 16
ROPE_BASE = 10000.0
EPS = 1e-6
NEG = -1e30
LOG2E = 1.4426950408889634

LANES = 128
SUBLANES = 8
VMEM_LIMIT_BYTES = 60 * 1024 * 1024

HEAD_W = 128
MLA_QPAD = 256
HG_CHUNK = 128
HG_LEVELS = int(math.log2(HG_CHUNK))
NA_QROWS = 8
NA_KROWS = 16


def _params(semantics):
    return pltpu.CompilerParams(dimension_semantics=semantics, vmem_limit_bytes=VMEM_LIMIT_BYTES)


def _pow2_tile(pref, *dims):
    t = pref
    while any(d % t for d in dims):
        t //= 2
    assert t >= SUBLANES, (pref, dims)
    return t


def _dot(a, b):
    return jnp.dot(a, b, preferred_element_type=f32)


def _dot_nt(a, b):
    return lax.dot_general(a, b, (((1,), (1,)), ((), ())), preferred_element_type=f32)


def _dot_tn(a, b):
    return lax.dot_general(a, b, (((0,), (0,)), ((), ())), preferred_element_type=f32)


def _silu(x):
    return x * jax.nn.sigmoid(x)


def _mod_const(x, n):
    return jnp.bitwise_and(x, n - 1) if n & (n - 1) == 0 else lax.rem(x, n)


class _ZLayout:
    def __init__(self, d_model, tn):
        self.mq = 0
        self.ckv = self.mq + MLA_HEADS * MLA_QK
        self.hq = self.ckv + MLA_KV_RANK
        self.hff = self.hq + HG_HEADS * HG_K
        self.hfb = self.hff + HG_HEADS * HG_K
        self.hi = self.hfb + HG_HEADS * HG_K
        self.hg = self.hi + HG_HEADS * HG_V
        self.nq = self.hg + HG_HEADS * HG_V
        self.nk = self.nq + NA_HEADS * NA_DH
        self.nv = self.nk + NA_HEADS * NA_DH
        self.ga = self.nv + NA_HEADS * NA_DH
        self.gb = self.ga + d_model
        self.gc = self.gb + d_model
        self.width = self.gc + d_model
        assert self.width % tn == 0


def _permute_w_in(w, lay):
    n_front = MLA_HEADS * MLA_QK + MLA_KV_RANK
    kpe = w[..., n_front:n_front + MLA_ROPE]
    main = jnp.concatenate([w[..., :n_front], w[..., n_front + MLA_ROPE:]], axis=-1).astype(bf16)
    assert main.shape[-1] == lay.width
    pad = [(0, 0)] * (w.ndim - 1) + [(0, LANES - MLA_ROPE)]
    return main, jnp.pad(kpe, pad).astype(bf16)


def _mod_kernel(c_ref, w_ref, b_ref, o_ref):
    a = _silu(c_ref[...]).astype(bf16)
    o_ref[...] = _dot(a, w_ref[...].astype(bf16)) + b_ref[...]


def _modulation(cond, w_mod, b_mod):
    depth, d, n = w_mod.shape
    g = cond.shape[0]
    tn = _pow2_tile(1024, n)
    return pl.pallas_call(
        _mod_kernel,
        grid=(depth, n // tn),
        in_specs=[
            pl.BlockSpec((g, d), lambda l, j: (0, 0)),
            pl.BlockSpec((None, d, tn), lambda l, j: (l, 0, j)),
            pl.BlockSpec((None, 1, tn), lambda l, j: (l, 0, j)),
        ],
        out_specs=pl.BlockSpec((None, g, tn), lambda l, j: (l, 0, j)),
        out_shape=jax.ShapeDtypeStruct((depth, g, n), f32),
        compiler_params=_params(("arbitrary", "arbitrary")),
        name="modulation",
    )(cond, w_mod, b_mod.reshape(depth, 1, n))


def _normmod_matmul_kernel(x_ref, g_ref, sc_ref, sh_ref, w_ref, *rest):
    if len(rest) == 4:
        ws_ref, o_ref, os_ref, h_scr = rest
    else:
        (o_ref, h_scr), ws_ref, os_ref = rest, None, None

    @pl.when(pl.program_id(1) == 0)
    def _():
        x = x_ref[...]
        ms = jnp.mean(x * x, axis=-1, keepdims=True)
        y = x * lax.rsqrt(ms + EPS) * g_ref[...]
        h_scr[...] = (y * (1.0 + sc_ref[...]) + sh_ref[...]).astype(bf16)
        if ws_ref is not None:
            os_ref[...] = _dot(h_scr[...], ws_ref[...])

    o_ref[...] = _dot(h_scr[...], w_ref[...])


def _normmod_matmul(x, g, sc, sh, w, layer, group_of_tile, tm, tn, name, w_side=None):
    m, d = x.shape
    n = w.shape[2]
    in_specs = [
        pl.BlockSpec((tm, d), lambda i, j: (i, 0)),
        pl.BlockSpec((1, d), lambda i, j: (0, 0)),
        pl.BlockSpec((None, 1, d), lambda i, j: (group_of_tile(i), 0, 0)),
        pl.BlockSpec((None, 1, d), lambda i, j: (group_of_tile(i), 0, 0)),
        pl.BlockSpec((None, d, tn), lambda i, j: (layer, 0, j)),
    ]
    args = [x, g.reshape(1, d), sc, sh, w]
    out_specs = [pl.BlockSpec((tm, tn), lambda i, j: (i, j))]
    out_shape = [jax.ShapeDtypeStruct((m, n), f32)]
    if w_side is not None:
        ns = w_side.shape[2]
        in_specs.append(pl.BlockSpec((None, d, ns), lambda i, j: (layer, 0, 0)))
        args.append(w_side)
        out_specs.append(pl.BlockSpec((tm, ns), lambda i, j: (i, 0)))
        out_shape.append(jax.ShapeDtypeStruct((m, ns), f32))
    res = pl.pallas_call(
        _normmod_matmul_kernel,
        grid=(m // tm, n // tn),
        in_specs=in_specs,
        out_specs=out_specs,
        out_shape=out_shape,
        scratch_shapes=[pltpu.VMEM((tm, d), bf16)],
        compiler_params=_params(("arbitrary", "arbitrary")),
        name=name,
    )(*args)
    return res if w_side is not None else res[0]


def _rope(x, cos, sin_signed):
    n = x.shape[-1]
    half = MLA_ROPE // 4
    lane = lax.broadcasted_iota(jnp.int32, x.shape, 1)
    partner = jnp.where((lane % (2 * half)) < half, pltpu.roll(x, n - half, 1), pltpu.roll(x, half, 1))
    return x * cos + partner * sin_signed


def _mla_q_kernel(mq_ref, gn_ref, gp_ref, cos_ref, sin_ref, o_ref, *, scale):
    tm = mq_ref.shape[0]
    lane = lax.broadcasted_iota(jnp.int32, (tm, LANES), 1)
    low = lane < MLA_ROPE
    cos = cos_ref[...]
    sin = sin_ref[...]
    gn = gn_ref[...]
    gp = gp_ref[...]
    for pair in range(MLA_HEADS // 2):
        t0, t1, t2 = (mq_ref[:, (3 * pair + t) * LANES:(3 * pair + t + 1) * LANES] for t in range(3))
        nopes = (t0, jnp.where(low, pltpu.roll(t1, MLA_ROPE, 1), pltpu.roll(t2, MLA_ROPE, 1)))
        pcol = jnp.where(low, t1, t2)
        sq = pcol * pcol
        ss_all = jnp.sum(sq, axis=-1, keepdims=True)
        ss_low = jnp.sum(jnp.where(low, sq, 0.0), axis=-1, keepdims=True)
        rot = _rope(pcol * gp, cos, sin)
        for k, ss_pe in enumerate((ss_low, ss_all - ss_low)):
            h = 2 * pair + k
            nope = nopes[k]
            ss = jnp.sum(nope * nope, axis=-1, keepdims=True) + ss_pe
            r = lax.rsqrt(ss * (1.0 / MLA_QK) + EPS) * scale
            pe = rot if k == 0 else pltpu.roll(rot, MLA_ROPE, 1)
            o_ref[:, h * MLA_QPAD:h * MLA_QPAD + MLA_NOPE] = (nope * r * gn).astype(bf16)
            o_ref[:, h * MLA_QPAD + MLA_NOPE:(h + 1) * MLA_QPAD] = jnp.where(low, pe * r, 0.0).astype(bf16)


class _CtxState:
    def __init__(self, prevs, widths, layer, depth, n_ctx, s_len, tm):
        assert tm % s_len == 0
        self.n_seq = tm // s_len
        self.s_len = s_len
        self.n_tiles = n_ctx * s_len // tm
        self.prevs = list(prevs)
        n_tiles = self.n_tiles
        self.out_specs = [pl.BlockSpec((self.n_seq, None, s_len, w), lambda i: (jnp.minimum(i, n_tiles - 1), layer, 0, 0))
                          for w in widths]
        self.out_shape = [jax.ShapeDtypeStruct((n_ctx, depth, s_len, w), f32) for w in widths]
        self.in_specs = [pl.BlockSpec(memory_space=pl.ANY)] * len(self.prevs)

    def aliases(self, n_inputs_before, n_outputs_before):
        return {n_inputs_before + k: n_outputs_before + k for k in range(len(self.prevs))}

    def kernel_kwargs(self):
        return dict(n_prev=len(self.prevs), n_ctx_tiles=self.n_tiles, s_len=self.s_len)


def _mla_kv_kernel(*refs, norm_ckv, rope, n_prev=0, n_ctx_tiles=0, s_len=0):
    ckv_ref, kpe_ref, kvg_ref, gn_ref, gp_ref, wuk_ref, wuv_ref, cos_ref, sin_ref = refs[:9]
    k_ref, v_ref, *state_refs = refs[9 + n_prev:]
    ckv = ckv_ref[...]
    if norm_ckv:
        ms = jnp.mean(ckv * ckv, axis=-1, keepdims=True)
        ckv = ckv * lax.rsqrt(ms + EPS) * kvg_ref[...]
    if state_refs:
        ckv_state_ref, kpe_state_ref = state_refs

        @pl.when(pl.program_id(0) < n_ctx_tiles)
        def _():
            for n in range(ckv.shape[0] // s_len):
                ckv_state_ref[n] = ckv[n * s_len:(n + 1) * s_len]
                kpe_state_ref[n] = kpe_ref[n * s_len:(n + 1) * s_len, :MLA_ROPE]

    cb = ckv.astype(bf16)
    kn = _dot(cb, wuk_ref[...])
    v_ref[...] = _dot(cb, wuv_ref[...]).astype(bf16)
    tm = ckv.shape[0]
    lane = lax.broadcasted_iota(jnp.int32, (tm, LANES), 1)
    low = lane < MLA_ROPE
    kpe = jnp.where(low, kpe_ref[...], 0.0)
    ss_pe = jnp.sum(kpe * kpe, axis=-1, keepdims=True)
    pe = kpe * gp_ref[...]
    if rope:
        pe = _rope(pe, cos_ref[...], sin_ref[...])
    gn = gn_ref[...]
    for h in range(MLA_HEADS):
        nope = kn[:, h * MLA_NOPE:(h + 1) * MLA_NOPE]
        ss = jnp.sum(nope * nope, axis=-1, keepdims=True) + ss_pe
        r = lax.rsqrt(ss * (1.0 / MLA_QK) + EPS)
        k_ref[:, h * MLA_QPAD:h * MLA_QPAD + MLA_NOPE] = (nope * r * gn).astype(bf16)
        k_ref[:, h * MLA_QPAD + MLA_NOPE:(h + 1) * MLA_QPAD] = jnp.where(low, pe * r, 0.0).astype(bf16)


def _rope_tables(t_len, tm):
    n_freq = MLA_ROPE // 4
    t = jnp.arange(t_len, dtype=jnp.int32)
    inv = ROPE_BASE ** (-jnp.arange(n_freq, dtype=f32) / n_freq)
    ang_r = (t // GRID_W).astype(f32)[:, None] * inv
    ang_c = (t % GRID_W).astype(f32)[:, None] * inv
    cos64 = jnp.concatenate([jnp.cos(ang_r), jnp.cos(ang_r), jnp.cos(ang_c), jnp.cos(ang_c)], axis=1)
    sin64 = jnp.concatenate([-jnp.sin(ang_r), jnp.sin(ang_r), -jnp.sin(ang_c), jnp.sin(ang_c)], axis=1)
    cos = jnp.concatenate([jnp.ones((tm, LANES), f32), jnp.tile(cos64, (1, 2))], axis=0)
    sin = jnp.concatenate([jnp.zeros((tm, LANES), f32), jnp.tile(sin64, (1, 2))], axis=0)
    return cos, sin


def _mla_q(z, lay, q_g, cos, sin, rope_block, tm):
    m = z.shape[0]
    gn = q_g[:MLA_NOPE].reshape(1, MLA_NOPE)
    gp = jnp.tile(q_g[MLA_NOPE:], 2).reshape(1, LANES)
    wq = MLA_HEADS * MLA_QK
    assert MLA_HEADS % 2 == 0 and 2 * MLA_QK == 3 * LANES and 2 * MLA_ROPE == LANES and lay.mq % wq == 0
    return pl.pallas_call(
        functools.partial(_mla_q_kernel, scale=MLA_QK ** -0.5 * LOG2E),
        grid=(m // tm,),
        in_specs=[
            pl.BlockSpec((tm, wq), lambda i: (i, lay.mq // wq)),
            pl.BlockSpec((1, MLA_NOPE), lambda i: (0, 0)),
            pl.BlockSpec((1, LANES), lambda i: (0, 0)),
            pl.BlockSpec((tm, LANES), lambda i: (rope_block(i), 0)),
            pl.BlockSpec((tm, LANES), lambda i: (rope_block(i), 0)),
        ],
        out_specs=pl.BlockSpec((tm, MLA_HEADS * MLA_QPAD), lambda i: (i, 0)),
        out_shape=jax.ShapeDtypeStruct((m, MLA_HEADS * MLA_QPAD), bf16),
        compiler_params=_params(("arbitrary",)),
        name="mla_q",
    )(z, gn, gp, cos, sin)


def _mla_kv(ckv_src, ckv_col, kpe_src, kpe_col, kv_g, k_g, w_uk, w_uv, layer, cos, sin, rope_block, tm, norm_ckv, rope,
            name, state=None):
    m = ckv_src.shape[0]
    gn = k_g[:MLA_NOPE].reshape(1, MLA_NOPE)
    gp = jnp.concatenate([k_g[MLA_NOPE:], jnp.zeros((LANES - MLA_ROPE,), f32)]).reshape(1, LANES)
    hw = MLA_HEADS * MLA_NOPE
    in_specs = [
        pl.BlockSpec((tm, MLA_KV_RANK), lambda i: (i, ckv_col // MLA_KV_RANK)),
        pl.BlockSpec((tm, LANES), lambda i: (i, kpe_col // LANES)),
        pl.BlockSpec((1, MLA_KV_RANK), lambda i: (0, 0)),
        pl.BlockSpec((1, MLA_NOPE), lambda i: (0, 0)),
        pl.BlockSpec((1, LANES), lambda i: (0, 0)),
        pl.BlockSpec((None, MLA_KV_RANK, hw), lambda i: (layer, 0, 0)),
        pl.BlockSpec((None, MLA_KV_RANK, hw), lambda i: (layer, 0, 0)),
        pl.BlockSpec((tm, LANES), lambda i: (rope_block(i), 0)),
        pl.BlockSpec((tm, LANES), lambda i: (rope_block(i), 0)),
    ]
    args = [ckv_src, kpe_src, kv_g.reshape(1, MLA_KV_RANK), gn, gp, w_uk, w_uv, cos, sin]
    out_specs = [pl.BlockSpec((tm, MLA_HEADS * MLA_QPAD), lambda i: (i, 0)), pl.BlockSpec((tm, hw), lambda i: (i, 0))]
    out_shape = [jax.ShapeDtypeStruct((m, MLA_HEADS * MLA_QPAD), bf16), jax.ShapeDtypeStruct((m, hw), bf16)]
    kwargs, aliases = {}, {}
    if state is not None:
        aliases = state.aliases(len(args), len(out_specs))
        kwargs = state.kernel_kwargs()
        in_specs += state.in_specs
        args += state.prevs
        out_specs += state.out_specs
        out_shape += state.out_shape
    return pl.pallas_call(
        functools.partial(_mla_kv_kernel, norm_ckv=norm_ckv, rope=rope, **kwargs),
        grid=(m // tm,),
        in_specs=in_specs,
        out_specs=out_specs,
        out_shape=out_shape,
        input_output_aliases=aliases,
        compiler_params=_params(("arbitrary",)),
        name=name,
    )(*args)


def _flash_kernel(*refs, n_main, tk, n_extra, heads, dq, dv):
    if n_extra:
        q_ref, k_ref, v_ref, kx_ref, vx_ref, o_ref = refs
    else:
        q_ref, k_ref, v_ref, o_ref = refs
        kx_ref = vx_ref = None
    tq = q_ref.shape[0]
    chunks = [(k_ref, v_ref, c) for c in range(n_main)] + [(kx_ref, vx_ref, c) for c in range(n_extra)]
    def scores(i, g):
        kr, _, c = chunks[i]
        return _dot_nt(kr[c * tk:(c + 1) * tk, g * dq:(g + 1) * dq], q_ref[:, g * dq:(g + 1) * dq])

    m = [jnp.full((1, tq), -jnp.inf, f32)] * heads
    l = [jnp.zeros((1, tq), f32)] * heads
    acc = [jnp.zeros((dv, tq), f32)] * heads
    s = [scores(0, g) for g in range(heads)]
    for i, (_, vr, c) in enumerate(chunks):
        for g in range(heads):
            s_next = scores(i + 1, g) if i + 1 < len(chunks) else None
            m_new = jnp.maximum(m[g], jnp.max(s[g], axis=0, keepdims=True))
            alpha = jnp.exp2(m[g] - m_new)
            p = jnp.exp2(s[g] - m_new)
            l[g] = alpha * l[g] + jnp.sum(p, axis=0, keepdims=True)
            acc[g] = alpha * acc[g] + _dot_tn(vr[c * tk:(c + 1) * tk, g * dv:(g + 1) * dv], p.astype(bf16))
            m[g], s[g] = m_new, s_next
    for g in range(heads):
        o_ref[:, g * dv:(g + 1) * dv] = (acc[g] / l[g]).T.astype(o_ref.dtype)


def _flash(q, k, v, *, batch, heads, t_len, s_len, row0, dq, dv, tq, tk, heads_per_step=1, extra=None, name):
    assert row0 % tq == 0 and row0 % s_len == 0 and t_len % tq == 0 and s_len % tk == 0
    assert heads % heads_per_step == 0
    nq = t_len // tq
    g = heads_per_step
    in_specs = [
        pl.BlockSpec((tq, g * dq), lambda b, h, i: (row0 // tq + b * nq + i, h)),
        pl.BlockSpec((s_len, g * dq), lambda b, h, i: (row0 // s_len + b, h)),
        pl.BlockSpec((s_len, g * dv), lambda b, h, i: (row0 // s_len + b, h)),
    ]
    args = [q, k, v]
    n_extra = 0
    if extra is not None:
        kx, vx, p_len = extra
        assert p_len % tk == 0
        n_extra = p_len // tk
        in_specs += [
            pl.BlockSpec((p_len, g * dq), lambda b, h, i: (b, h)),
            pl.BlockSpec((p_len, g * dv), lambda b, h, i: (b, h)),
        ]
        args += [kx, vx]
    return pl.pallas_call(
        functools.partial(_flash_kernel, n_main=s_len // tk, tk=tk, n_extra=n_extra, heads=g, dq=dq, dv=dv),
        grid=(batch, heads // g, nq),
        in_specs=in_specs,
        out_specs=pl.BlockSpec((tq, g * dv), lambda b, h, i: (b * nq + i, h)),
        out_shape=jax.ShapeDtypeStruct((batch * t_len, heads * dv), bf16),
        compiler_params=_params(("arbitrary", "arbitrary", "arbitrary")),
        name=name,
    )(*args)


def _na_prep_kernel(*refs, scale, n_prev, n_ctx_tiles, s_len):
    nq_ref, nk_ref, nv_ref, qg_ref, kg_ref = refs[:5]
    q_ref, k_ref, v_ref, k_state_ref, v_state_ref = refs[5 + n_prev:]
    qg = qg_ref[...]
    kg = kg_ref[...]
    tm = nv_ref.shape[0]
    is_ctx = pl.program_id(0) < n_ctx_tiles
    v_ref[...] = nv_ref[...].astype(bf16)

    @pl.when(is_ctx)
    def _():
        for n in range(tm // s_len):
            v_state_ref[n] = nv_ref[n * s_len:(n + 1) * s_len, :]

    for h in range(NA_HEADS):
        sl = slice(h * NA_DH, (h + 1) * NA_DH)
        q = nq_ref[:, sl]
        k = nk_ref[:, sl]
        rq = lax.rsqrt(jnp.mean(q * q, axis=-1, keepdims=True) + EPS)
        rk = lax.rsqrt(jnp.mean(k * k, axis=-1, keepdims=True) + EPS)
        q_ref[:, sl] = (q * rq * qg * scale).astype(bf16)
        kn = k * rk * kg
        k_ref[:, sl] = kn.astype(bf16)

        @pl.when(is_ctx)
        def _():
            for n in range(tm // s_len):
                k_state_ref[n, :, sl] = kn[n * s_len:(n + 1) * s_len]


def _na_prep(z, lay, q_g, k_g, tm, state):
    m = z.shape[0]
    w = NA_HEADS * NA_DH
    spec = pl.BlockSpec((tm, w), lambda i: (i, 0))
    in_specs = [
        pl.BlockSpec((tm, w), lambda i: (i, lay.nq // w)),
        pl.BlockSpec((tm, w), lambda i: (i, lay.nk // w)),
        pl.BlockSpec((tm, w), lambda i: (i, lay.nv // w)),
        pl.BlockSpec((1, NA_DH), lambda i: (0, 0)),
        pl.BlockSpec((1, NA_DH), lambda i: (0, 0)),
    ]
    args = [z, z, z, q_g.reshape(1, NA_DH), k_g.reshape(1, NA_DH)]
    return pl.pallas_call(
        functools.partial(_na_prep_kernel, scale=NA_DH ** -0.5 * LOG2E, **state.kernel_kwargs()),
        grid=(m // tm,),
        in_specs=in_specs + state.in_specs,
        out_specs=[spec, spec, spec] + state.out_specs,
        out_shape=[jax.ShapeDtypeStruct((m, w), bf16)] * 3 + state.out_shape,
        input_output_aliases=state.aliases(len(args), 3),
        compiler_params=_params(("arbitrary",)),
        name="na_prep",
    )(*args, *state.prevs)


def _na_static_maps(rows):
    nblk = rows // NA_QROWS
    reps = (0, min(1, nblk - 1), nblk - 1)
    dr_map = np.full((3, NA_QROWS, NA_KROWS), 2 * NA_WIN_R - 1, np.int32)
    for v, kb in enumerate(reps):
        ws = min(max(NA_QROWS * kb - NA_WIN_R // 2, 0), rows - NA_KROWS)
        for i in range(NA_QROWS):
            qrow = NA_QROWS * kb + i
            rs = min(max(qrow - NA_WIN_R // 2, 0), rows - NA_WIN_R)
            for j in range(NA_KROWS):
                krow = ws + j
                if rs <= krow < rs + NA_WIN_R:
                    dr_map[v, i, j] = krow - qrow + NA_WIN_R - 1
    qcol = np.arange(GRID_W)
    cs = np.clip(qcol - NA_WIN_C // 2, 0, GRID_W - NA_WIN_C)
    in_win = (qcol[None, :] >= cs[:, None]) & (qcol[None, :] < cs[:, None] + NA_WIN_C)
    dc_idx = np.clip(qcol[None, :] - qcol[:, None], -(NA_WIN_C - 1), NA_WIN_C - 1) + (NA_WIN_C - 1)
    return dr_map, in_win, dc_idx


def _na_bias_table(rpb, rows):
    dr_map, in_win, dc_idx = _na_static_maps(rows)
    depth, h = rpb.shape[:2]
    bt = jnp.where(in_win.T, jnp.take(rpb * LOG2E, jnp.asarray(dc_idx.T), axis=3), NEG)
    bt = jnp.concatenate([bt, jnp.full((depth, h, 1, GRID_W, GRID_W), NEG, f32)], axis=2)
    tab = jnp.take(bt, jnp.asarray(dr_map.transpose(0, 2, 1).reshape(-1)), axis=2)
    tab = tab.reshape(depth, h, 3, NA_KROWS, NA_QROWS, GRID_W, GRID_W).transpose(0, 2, 1, 3, 5, 4, 6)
    return tab.reshape(depth, 3, h, NA_KROWS * GRID_W, NA_QROWS * GRID_W)


def _na_kernel(q_ref, k_ref, v_ref, kc_ref, vc_ref, bias_ref, o_ref, *, rows, heads):
    kb = pl.program_id(2)
    ws = jnp.clip(NA_QROWS * kb - NA_WIN_R // 2, 0, rows - NA_KROWS)
    win = pl.ds(pl.multiple_of(ws * GRID_W, (NA_WIN_R // 2) * GRID_W), NA_KROWS * GRID_W)
    cols = [slice(g * NA_DH, (g + 1) * NA_DH) for g in range(heads)]
    s_loc = [_dot_nt(k_ref[win, c], q_ref[:, c]) + bias_ref[g] for g, c in enumerate(cols)]
    s_ctx = [_dot_nt(kc_ref[:, c], q_ref[:, c]) for c in cols]
    p_loc, p_ctx, l = [], [], []
    for g in range(heads):
        m = jnp.maximum(jnp.max(s_loc[g], axis=0, keepdims=True), jnp.max(s_ctx[g], axis=0, keepdims=True))
        pl_g = jnp.exp2(s_loc[g] - m)
        pc_g = jnp.exp2(s_ctx[g] - m)
        l.append(jnp.sum(pl_g, axis=0, keepdims=True) + jnp.sum(pc_g, axis=0, keepdims=True))
        p_loc.append(pl_g.astype(bf16))
        p_ctx.append(pc_g.astype(bf16))
    for g, c in enumerate(cols):
        acc = _dot_tn(v_ref[win, c], p_loc[g]) + _dot_tn(vc_ref[:, c], p_ctx[g])
        o_ref[:, c] = (acc / l[g]).T.astype(o_ref.dtype)


def _na_latent(q, k, v, kc, vc, bias, layer, *, batch, t_len, p_len, row0, heads_per_step=2):
    rows = t_len // GRID_W
    assert t_len % GRID_W == 0 and rows % NA_QROWS == 0 and rows >= NA_KROWS
    tq = NA_QROWS * GRID_W
    assert row0 % tq == 0 and row0 % t_len == 0 and NA_HEADS % heads_per_step == 0
    nblk = rows // NA_QROWS
    g = heads_per_step
    gw = g * NA_DH

    def variant(i):
        return jnp.where(i == 0, 0, jnp.where(i == nblk - 1, 2, 1))

    return pl.pallas_call(
        functools.partial(_na_kernel, rows=rows, heads=g),
        grid=(batch, NA_HEADS // g, nblk),
        in_specs=[
            pl.BlockSpec((tq, gw), lambda b, h, i: (row0 // tq + b * nblk + i, h)),
            pl.BlockSpec((t_len, gw), lambda b, h, i: (row0 // t_len + b, h)),
            pl.BlockSpec((t_len, gw), lambda b, h, i: (row0 // t_len + b, h)),
            pl.BlockSpec((None, None, p_len, gw), lambda b, h, i: (b, layer, 0, h)),
            pl.BlockSpec((None, None, p_len, gw), lambda b, h, i: (b, layer, 0, h)),
            pl.BlockSpec((None, None, g, NA_KROWS * GRID_W, tq), lambda b, h, i: (layer, variant(i), h, 0, 0)),
        ],
        out_specs=pl.BlockSpec((tq, gw), lambda b, h, i: (b * nblk + i, h)),
        out_shape=jax.ShapeDtypeStruct((batch * t_len, NA_HEADS * NA_DH), bf16),
        compiler_params=_params(("arbitrary", "arbitrary", "arbitrary")),
        name="na_latent",
    )(q, k, v, kc, vc, bias)


def _hgrn_tables():
    c, lv = HG_CHUNK, HG_LEVELS
    t = np.arange(c)[:, None]
    r = np.arange(c)[None, :]
    wcum = np.stack([r <= t, r >= t]).astype(np.float32)
    qside = np.zeros((2, lv * c, LANES), np.float32)
    mask = np.zeros((2, lv + 1, c, c), np.float32)
    mask[:, 0] = np.eye(c)
    for l in range(lv):
        hs = 1 << l
        blk = t // (2 * hs)
        mid = blk * (2 * hs) + hs
        qf = t >= mid
        qb = t < mid
        qside[0, l * c:(l + 1) * c] = np.where(qf, 1.0, -1.0)
        qside[1, l * c:(l + 1) * c] = np.where(qb, 1.0, -1.0)
        same = blk == blk.T
        mask[0, 1 + l] = same & qf & ~qf.T
        mask[1, 1 + l] = same & qb & ~qb.T
    return wcum, qside, mask


def _hgrn_kernel(*refs, n_chunks, heads, has_state, emit_state, n_prev=0):
    (hq_ref, hff_ref, hfb_ref, hi_ref, hg_ref, la_ref, lc_ref, om_ref, ng_ref,
     wcum_ref, qside_ref, mask_ref) = refs[:12]
    rest = list(refs[12:])
    s0_ref = rest.pop(0) if has_state else None
    del rest[:n_prev]
    o_ref = rest.pop(0)
    st_ref = rest.pop(0) if emit_state else None
    o_scr, s_scr, b_scr = rest
    c, lv = HG_CHUNK, HG_LEVELS

    o_scr[...] = jnp.zeros_like(o_scr)
    for d in range(2):
        for g in range(heads):
            if has_state:
                s_scr[d, g] = s0_ref[d, g].T
            else:
                s_scr[d, g] = jnp.zeros((HG_V, HG_K), f32)

    chains = [(d, g) for g in range(heads) for d in range(2)]
    low_half = lax.broadcasted_iota(jnp.int32, (SUBLANES, HEAD_W), 0) < SUBLANES // 2

    def gates_and_cumsum(ci, row):
        d, g = chains[ci]
        rs = pl.ds(pl.multiple_of(row, c), c)
        cs = slice(g * HEAD_W, (g + 1) * HEAD_W)
        x = (hff_ref if d == 0 else hfb_ref)[rs, cs]
        x2 = x * LOG2E
        u = jnp.exp2(-jnp.abs(x2))
        inv = 1.0 / (1.0 + u)
        pos = x >= 0.0
        log_sig = jnp.minimum(x2, 0.0) - jnp.log2(1.0 + u)
        a = la_ref[d:d + 1, cs]
        bb = lc_ref[d:d + 1, cs] + log_sig
        lf = jnp.maximum(a, bb) + jnp.log2(1.0 + jnp.exp2(-jnp.abs(a - bb)))
        om = om_ref[d:d + 1, cs]
        f = (1.0 - om) + om * (jnp.where(pos, 1.0, u) * inv)
        kin = om * (jnp.where(pos, u, 1.0) * inv)
        q = _silu(hq_ref[rs, cs])
        v = hi_ref[rs, cs].astype(bf16)
        l1 = lf.astype(bf16)
        r1 = lf - l1.astype(f32)
        l2 = r1.astype(bf16)
        l3 = (r1 - l2.astype(f32)).astype(bf16)
        w = wcum_ref[d]
        b = _dot(w, l1) + _dot(w, l2) + _dot(w, l3)
        b_scr[ci] = b
        return dict(rs=rs, cs=cs, f=f, kin=kin, q=q, v=v, b=b)

    def intra_chunk(ci, s):
        d, _ = chains[ci]
        b, q, kin = s["b"], s["q"], s["kin"]

        def row(r):
            return b_scr[ci, r:r + 1, :]

        s["b_end"] = row(c - 1 if d == 0 else 0)
        xs = [jnp.where(qside_ref[d, 0:c, :] > 0.0, q * s["f"], kin).astype(bf16)]
        for l in range(1, lv):
            hs = 1 << l
            blk = 2 * hs
            pick = hs - 1 if d == 0 else hs
            if hs >= SUBLANES:
                pieces = []
                for j in range(c // blk):
                    m = row(blk * j + pick)
                    early = slice(blk * j, blk * j + hs)
                    late = slice(blk * j + hs, blk * (j + 1))
                    q_sl, k_sl = (late, early) if d == 0 else (early, late)
                    xq = q[q_sl] * jnp.exp2(b[q_sl] - m)
                    xk = kin[k_sl] * jnp.exp2(m - b[k_sl])
                    pieces += [xk, xq] if d == 0 else [xq, xk]
                xs.append(jnp.concatenate(pieces, axis=0).astype(bf16))
                continue
            if blk < SUBLANES:
                pieces = [jnp.where(low_half, jnp.broadcast_to(row(SUBLANES * j + pick), (SUBLANES, HEAD_W)),
                                    jnp.broadcast_to(row(SUBLANES * j + blk + pick), (SUBLANES, HEAD_W)))
                          for j in range(c // SUBLANES)]
            else:
                pieces = [jnp.broadcast_to(row(blk * j + pick), (blk, HEAD_W)) for j in range(c // blk)]
            m = pieces[0] if len(pieces) == 1 else jnp.concatenate(pieces, axis=0)
            sign = qside_ref[d, l * c:(l + 1) * c, :]
            xs.append((jnp.where(sign > 0.0, q, kin) * jnp.exp2((b - m) * sign)).astype(bf16))
        att = mask_ref[d, 0] * _dot_nt(q.astype(bf16), kin.astype(bf16)).astype(bf16)
        for l in range(lv):
            att = att + mask_ref[d, 1 + l] * _dot_nt(xs[l], xs[l]).astype(bf16)
        s["att"] = att

    def state_step(ci, s):
        d, g = chains[ci]
        b, q, kin, v = s["b"], s["q"], s["kin"], s["v"]
        st = s_scr[d, g]
        o = _dot(s["att"], v) + _dot_nt((q * jnp.exp2(b)).astype(bf16), st.astype(bf16))
        b_end = s["b_end"]
        s_scr[d, g] = st * jnp.exp2(b_end) + _dot_tn(v, (kin * jnp.exp2(b_end - b)).astype(bf16))
        o_scr[s["rs"], s["cs"]] += o

    def body(i, _):
        rows = (i * c, (n_chunks - 1 - i) * c)
        states = [gates_and_cumsum(ci, rows[chains[ci][0]]) for ci in range(len(chains))]
        for ci, s in enumerate(states):
            intra_chunk(ci, s)
        for ci, s in enumerate(states):
            state_step(ci, s)
        return 0

    lax.fori_loop(0, n_chunks, body, 0)

    ng = ng_ref[...]
    for g in range(heads):
        cs = slice(g * HEAD_W, (g + 1) * HEAD_W)
        o = o_scr[:, cs]
        y = o * lax.rsqrt(jnp.mean(o * o, axis=-1, keepdims=True) + EPS) * ng
        o_ref[:, cs] = (y * _silu(hg_ref[:, cs])).astype(o_ref.dtype)
        if emit_state:
            for d in range(2):
                st_ref[d, g] = s_scr[d, g].T


def _hgrn(z, lay, lb_tabs, norm_g, tables, *, batch, t_len, row0, s0, state_out=None, heads_per_step=2):
    emit_state = state_out is not None
    assert t_len % HG_CHUNK == 0 and row0 % t_len == 0 and HG_K == HEAD_W and HG_V == HEAD_W
    g = heads_per_step
    gw = g * HEAD_W
    la, lc, om = lb_tabs
    wcum, qside, mask = tables
    r0 = row0 // t_len

    def zspec(col):
        return pl.BlockSpec((t_len, gw), lambda b, h: (r0 + b, col // gw + h))

    def const(shape):
        return pl.BlockSpec(shape, lambda b, h: (0,) * len(shape))

    in_specs = [zspec(lay.hq), zspec(lay.hff), zspec(lay.hfb), zspec(lay.hi), zspec(lay.hg),
                pl.BlockSpec((2, gw), lambda b, h: (0, h)), pl.BlockSpec((2, gw), lambda b, h: (0, h)),
                pl.BlockSpec((2, gw), lambda b, h: (0, h)), const((1, HG_V)),
                const(wcum.shape), const(qside.shape), const(mask.shape)]
    args = [z, z, z, z, z, la, lc, om, norm_g.reshape(1, HG_V), wcum, qside, mask]
    state_spec = pl.BlockSpec((None, 2, g, HG_K, HG_V), lambda b, h: (b, 0, h, 0, 0))
    if s0 is not None:
        in_specs.append(state_spec)
        args.append(s0)
    out_specs = [pl.BlockSpec((t_len, gw), lambda b, h: (b, h))]
    out_shape = [jax.ShapeDtypeStruct((batch * t_len, HG_HEADS * HG_V), bf16)]
    aliases, n_prev = {}, 0
    if emit_state:
        prev, layer, depth = state_out
        aliases, n_prev = {len(args): 1}, 1
        in_specs.append(pl.BlockSpec(memory_space=pl.ANY))
        args.append(prev)
        out_specs.append(pl.BlockSpec((None, None, 2, g, HG_K, HG_V), lambda b, h: (b, layer, 0, h, 0, 0)))
        out_shape.append(jax.ShapeDtypeStruct((batch, depth, 2, HG_HEADS, HG_K, HG_V), f32))
    res = pl.pallas_call(
        functools.partial(_hgrn_kernel, n_chunks=t_len // HG_CHUNK, heads=g, has_state=s0 is not None,
                          emit_state=emit_state, n_prev=n_prev),
        grid=(batch, HG_HEADS // g),
        in_specs=in_specs,
        out_specs=out_specs,
        out_shape=out_shape,
        input_output_aliases=aliases,
        scratch_shapes=[pltpu.VMEM((t_len, gw), f32), pltpu.VMEM((2, g, HG_V, HG_K), f32),
                        pltpu.VMEM((2 * g, HG_CHUNK, HEAD_W), f32)],
        compiler_params=_params(("arbitrary", "arbitrary")),
        name="hgrn_ctx" if emit_state else "hgrn_latent",
    )(*args)
    return res if emit_state else (res[0], None)


def _merge_kernel(oac_ref, oal_ref, obc_ref, obl_ref, occ_ref, ocl_ref, ga_ref, gb_ref, gc_ref, w_ref, o_ref,
                  *, n_ctx_tiles):
    def compute(oa_ref, ob_ref, oc_ref):
        m = jax.nn.sigmoid(ga_ref[...]) * _dot(oa_ref[...], w_ref[0])
        m = m + jax.nn.sigmoid(gb_ref[...]) * _dot(ob_ref[...], w_ref[1])
        m = m + jax.nn.sigmoid(gc_ref[...]) * _dot(oc_ref[...], w_ref[2])
        o_ref[...] = m.astype(o_ref.dtype)

    is_ctx = pl.program_id(1) < n_ctx_tiles

    @pl.when(is_ctx)
    def _():
        compute(oac_ref, obc_ref, occ_ref)

    @pl.when(jnp.logical_not(is_ctx))
    def _():
        compute(oal_ref, obl_ref, ocl_ref)


def _merge(o_a, o_b, o_c, z, lay, w_branch, layer, tm, tn):
    bw = o_a[0].shape[1]
    m = z.shape[0]
    d = w_branch.shape[3]
    nct = o_a[0].shape[0] // tm
    cspec = pl.BlockSpec((tm, bw), lambda j, i: (jnp.minimum(i, nct - 1), 0))
    lspec = pl.BlockSpec((tm, bw), lambda j, i: (jnp.maximum(i - nct, 0), 0))

    def gate(col):
        return pl.BlockSpec((tm, tn), lambda j, i: (i, col // tn + j))

    return pl.pallas_call(
        functools.partial(_merge_kernel, n_ctx_tiles=nct),
        grid=(d // tn, m // tm),
        in_specs=[cspec, lspec, cspec, lspec, cspec, lspec, gate(lay.ga), gate(lay.gb), gate(lay.gc),
                  pl.BlockSpec((None, 3, bw, tn), lambda j, i: (layer, 0, 0, j))],
        out_specs=pl.BlockSpec((tm, tn), lambda j, i: (i, j)),
        out_shape=jax.ShapeDtypeStruct((m, d), bf16),
        compiler_params=_params(("arbitrary", "arbitrary")),
        name="merge",
    )(o_a[0], o_a[1], o_b[0], o_b[1], o_c[0], o_c[1], z, z, z, w_branch)


def _proj_residual_kernel(m_ref, w_ref, x_ref, g_ref, o_ref):
    o_ref[...] = x_ref[...] + g_ref[...] * _dot(m_ref[...], w_ref[...])


def _proj_residual(mm, w, layer, x, gate, group_of_tile, tm, tn):
    m, k = mm.shape
    d = w.shape[2]
    return pl.pallas_call(
        _proj_residual_kernel,
        grid=(m // tm, d // tn),
        in_specs=[
            pl.BlockSpec((tm, k), lambda i, j: (i, 0)),
            pl.BlockSpec((None, k, tn), lambda i, j: (layer, 0, j)),
            pl.BlockSpec((tm, tn), lambda i, j: (i, j)),
            pl.BlockSpec((None, 1, tn), lambda i, j: (group_of_tile(i), 0, j)),
        ],
        out_specs=pl.BlockSpec((tm, tn), lambda i, j: (i, j)),
        out_shape=jax.ShapeDtypeStruct((m, d), f32),
        compiler_params=_params(("arbitrary", "arbitrary")),
        name="out_proj",
    )(mm, w, x, gate)


def _ffn_kernel(x_ref, xprev_ref, xnext_ref, ng_ref, sc_ref, sh_ref, wa_ref, wg_ref, cw_ref, cb_ref, wd_ref, gate_ref,
                o_ref, h_scr, *, m_ctx, s_len, t_len, n_k):
    i = pl.program_id(0)
    k = pl.program_id(1)
    tm = x_ref.shape[0]
    tk = wa_ref.shape[1]
    halo = SUBLANES

    @pl.when(k == 0)
    def _():
        def normmod(x):
            ms = jnp.mean(x * x, axis=-1, keepdims=True)
            y = x * lax.rsqrt(ms + EPS) * ng_ref[...]
            return (y * (1.0 + sc_ref[...]) + sh_ref[...]).astype(bf16)

        h_scr[0:halo] = normmod(xprev_ref[...])
        h_scr[halo:halo + tm] = normmod(x_ref[...])
        h_scr[halo + tm:halo + tm + halo] = normmod(xnext_ref[...])
        o_ref[...] = jnp.zeros_like(o_ref)

    row = i * tm + lax.broadcasted_iota(jnp.int32, (tm, LANES), 0)
    pos = jnp.where(row < m_ctx, _mod_const(row, s_len), _mod_const(row - m_ctx, t_len))
    last = jnp.where(row < m_ctx, s_len - 1, t_len - 1)
    keep_prev = jnp.tile(jnp.where(pos == 0, 0.0, 1.0), (1, tk // LANES))
    keep_next = jnp.tile(jnp.where(pos == last, 0.0, 1.0), (1, tk // LANES))

    g_ext = _dot(h_scr[...], wg_ref[...])
    a = _dot(h_scr[halo:halo + tm], wa_ref[...])
    n_ext = tm + 2 * halo
    g = g_ext[halo:halo + tm]
    g_prev = pltpu.roll(g_ext, 1, 0)[halo:halo + tm]
    g_next = pltpu.roll(g_ext, n_ext - 1, 0)[halo:halo + tm]
    conv = (g_prev * keep_prev * cw_ref[0:1, :] + g * cw_ref[1:2, :] + g_next * keep_next * cw_ref[2:3, :]
            + cb_ref[...])
    u = (_silu(conv) * a).astype(bf16)
    o_ref[...] = _dot(u, wd_ref[...]) + o_ref[...]

    @pl.when(k == n_k - 1)
    def _():
        o_ref[...] = x_ref[...] + gate_ref[...] * o_ref[...]


def _ffn(x, norm_g, sc, sh, w_up, conv_w, conv_b, w_down, layer, gate, group_of_tile, tm, tk, m_ctx, s_len, t_len):
    m, d = x.shape
    f = w_down.shape[1]
    nk = f // tk
    nsub = tm // SUBLANES
    last_sub = m // SUBLANES - 1
    return pl.pallas_call(
        functools.partial(_ffn_kernel, m_ctx=m_ctx, s_len=s_len, t_len=t_len, n_k=nk),
        grid=(m // tm, nk),
        in_specs=[
            pl.BlockSpec((tm, d), lambda i, k: (i, 0)),
            pl.BlockSpec((SUBLANES, d), lambda i, k: (jnp.maximum(i * nsub - 1, 0), 0)),
            pl.BlockSpec((SUBLANES, d), lambda i, k: (jnp.minimum((i + 1) * nsub, last_sub), 0)),
            pl.BlockSpec((1, d), lambda i, k: (0, 0)),
            pl.BlockSpec((None, 1, d), lambda i, k: (group_of_tile(i), 0, 0)),
            pl.BlockSpec((None, 1, d), lambda i, k: (group_of_tile(i), 0, 0)),
            pl.BlockSpec((None, d, tk), lambda i, k: (layer, 0, k)),
            pl.BlockSpec((None, d, tk), lambda i, k: (layer, 0, nk + k)),
            pl.BlockSpec((3, tk), lambda i, k: (0, k)),
            pl.BlockSpec((1, tk), lambda i, k: (0, k)),
            pl.BlockSpec((None, tk, d), lambda i, k: (layer, k, 0)),
            pl.BlockSpec((None, 1, d), lambda i, k: (group_of_tile(i), 0, 0)),
        ],
        out_specs=pl.BlockSpec((tm, d), lambda i, k: (i, 0)),
        out_shape=jax.ShapeDtypeStruct((m, d), f32),
        scratch_shapes=[pltpu.VMEM((tm + 2 * SUBLANES, d), bf16)],
        compiler_params=_params(("arbitrary", "arbitrary")),
        name="ffn",
    )(x, x, x, norm_g.reshape(1, d), sc, sh, w_up, w_up, conv_w, conv_b.reshape(1, f), w_down, gate)


def kernel(x_prompt, x_sample, cache_mla_ckv, cache_mla_kpe, cache_na_k, cache_na_v, state_hgrn, c, c_ctx, w_mod, b_mod, norm1_g, norm2_g, w_in, mla_kv_norm_g, mla_q_norm_g, mla_k_norm_g, mla_w_uk, mla_w_uv, hgrn_lower_bounds, hgrn_norm_g, na_q_norm_g, na_k_norm_g, na_rpb, w_branch, w_out, ffn_w_up, ffn_conv_w, ffn_conv_b, ffn_w_down):
    n_ctx, s_len, d = x_prompt.shape
    n_lat, t_len, _ = x_sample.shape
    depth = w_in.shape[0]
    p_len = cache_mla_ckv.shape[2]
    ffn = ffn_w_down.shape[1]
    m_ctx, m_lat = n_ctx * s_len, n_lat * t_len
    m = m_ctx + m_lat
    assert m_ctx % t_len == 0, "context rows must be a whole number of latent sequences"

    tm = _pow2_tile(1024, m_ctx, t_len)
    tn = _pow2_tile(1024, d)
    tp = _pow2_tile(512, m_ctx, t_len)
    lay = _ZLayout(d, tn)

    def group_fn(rows):
        return lambda i: jnp.where(i < m_ctx // rows, 0, 1 + (i - m_ctx // rows) // (t_len // rows))

    group_of_tile = group_fn(tm)

    n_groups = -(-(1 + n_lat) // SUBLANES) * SUBLANES
    cond = jnp.concatenate([c_ctx[None], c, jnp.zeros((n_groups - 1 - n_lat, d), f32)], axis=0)
    mods = _modulation(cond, w_mod, b_mod).reshape(depth, n_groups, 6, d).transpose(0, 2, 1, 3)[:, :, :, None, :]

    sm = jax.nn.softmax(hgrn_lower_bounds.astype(f32), axis=1)
    csum = jnp.cumsum(sm, axis=1)
    lower = csum - csum[:, :1]
    hg_tabs = tuple(jnp.asarray(t, dt) for t, dt in zip(_hgrn_tables(), (bf16, f32, bf16)))

    cos, sin = _rope_tables(t_len, tp)
    n_ctx_tp = m_ctx // tp

    def rope_block(i):
        return jnp.where(i < n_ctx_tp, 0, 1 + (i - n_ctx_tp) % (t_len // tp))

    x = jnp.concatenate([x_prompt.reshape(m_ctx, d), x_sample.reshape(m_lat, d)], axis=0)
    w_in_b, w_kpe_b = _permute_w_in(w_in, lay)
    w_uk = mla_w_uk.reshape(depth, MLA_KV_RANK, MLA_HEADS * MLA_NOPE).astype(bf16)
    w_uv = mla_w_uv.reshape(depth, MLA_KV_RANK, MLA_HEADS * MLA_V).astype(bf16)
    w_branch_b, w_out_b = w_branch.astype(bf16), w_out.astype(bf16)
    w_up_b, w_down_b = ffn_w_up.astype(bf16), ffn_w_down.astype(bf16)
    na_bias = _na_bias_table(na_rpb, t_len // GRID_W)
    kc_na = cache_na_k.reshape(n_lat, depth, p_len, NA_HEADS * NA_DH).astype(bf16)
    vc_na = cache_na_v.reshape(n_lat, depth, p_len, NA_HEADS * NA_DH).astype(bf16)

    def state_init(*trailing):
        return jnp.zeros((n_ctx, depth) + trailing, f32)

    st_mla = [state_init(s_len, MLA_KV_RANK), state_init(s_len, MLA_ROPE)]
    st_na = [state_init(s_len, NA_HEADS * NA_DH), state_init(s_len, NA_HEADS * NA_DH)]
    st_hg = state_init(2, HG_HEADS, HG_K, HG_V)
    for l in range(depth):
        sh1, sc1, g1, sh2, sc2, g2 = (mods[l, k] for k in range(6))
        z, kpe = _normmod_matmul(x, norm1_g[l], sc1, sh1, w_in_b, l, group_of_tile, tm, _pow2_tile(2048, lay.width),
                                 "in_proj", w_side=w_kpe_b)

        q_mla = _mla_q(z, lay, mla_q_norm_g[l], cos, sin, rope_block, tp)
        k_mla, v_mla, *st_mla = _mla_kv(
            z, lay.ckv, kpe, 0, mla_kv_norm_g[l], mla_k_norm_g[l], w_uk, w_uv, l, cos, sin, rope_block, tp, True, True,
            "mla_kv", state=_CtxState(st_mla, (MLA_KV_RANK, MLA_ROPE), l, depth, n_ctx, s_len, tp))
        ckv_c = cache_mla_ckv[:, l].reshape(n_lat * p_len, MLA_KV_RANK)
        kpe_c = jnp.pad(cache_mla_kpe[:, l].reshape(n_lat * p_len, MLA_ROPE), ((0, 0), (0, LANES - MLA_ROPE)))
        tc = _pow2_tile(512, n_lat * p_len)
        k_c, v_c = _mla_kv(ckv_c, 0, kpe_c, 0, mla_kv_norm_g[l], mla_k_norm_g[l], w_uk, w_uv, l,
                           cos, sin, lambda i: 0, tc, False, False, "mla_kv_cache")
        tq_c = _pow2_tile(256, s_len)
        o_mla_c = _flash(q_mla, k_mla, v_mla, batch=n_ctx, heads=MLA_HEADS, t_len=s_len, s_len=s_len, row0=0,
                         dq=MLA_QPAD, dv=MLA_V, tq=tq_c, tk=tq_c, heads_per_step=MLA_HEADS, name="mla_attn_ctx")
        tq_l = _pow2_tile(512, t_len, p_len)
        o_mla_l = _flash(q_mla, k_mla, v_mla, batch=n_lat, heads=MLA_HEADS, t_len=t_len, s_len=t_len, row0=m_ctx,
                         dq=MLA_QPAD, dv=MLA_V, tq=tq_l, tk=tq_l, heads_per_step=4, extra=(k_c, v_c, p_len),
                         name="mla_attn_latent")

        lb = lower[:, l]
        lb_tabs = (jnp.log(lb) * LOG2E, jnp.log1p(-lb) * LOG2E, 1.0 - lb)
        o_hg_c, st_hg = _hgrn(z, lay, lb_tabs, hgrn_norm_g[l], hg_tabs, batch=n_ctx, t_len=s_len, row0=0,
                              s0=None, state_out=(st_hg, l, depth), heads_per_step=4)
        o_hg_l, _ = _hgrn(z, lay, lb_tabs, hgrn_norm_g[l], hg_tabs, batch=n_lat, t_len=t_len, row0=m_ctx,
                          s0=state_hgrn[:, l])

        q_na, k_na, v_na, *st_na = _na_prep(
            z, lay, na_q_norm_g[l], na_k_norm_g[l], tp,
            _CtxState(st_na, (NA_HEADS * NA_DH, NA_HEADS * NA_DH), l, depth, n_ctx, s_len, tp))
        o_na_c = _flash(q_na, k_na, v_na, batch=n_ctx, heads=NA_HEADS, t_len=s_len, s_len=s_len, row0=0,
                        dq=NA_DH, dv=NA_DH, tq=tq_c, tk=tq_c, heads_per_step=NA_HEADS, name="na_attn_ctx")
        o_na_l = _na_latent(q_na, k_na, v_na, kc_na, vc_na, na_bias, l, batch=n_lat, t_len=t_len, p_len=p_len, row0=m_ctx)

        merged = _merge((o_mla_c, o_mla_l), (o_hg_c, o_hg_l), (o_na_c, o_na_l), z, lay, w_branch_b, l, tp, tn)
        x = _proj_residual(merged, w_out_b, l, x, g1, group_fn(tp), tp, d)

        x = _ffn(x, norm2_g[l], sc2, sh2, w_up_b, ffn_conv_w[l], ffn_conv_b[l], w_down_b, l, g2, group_of_tile, tm,
                 _pow2_tile(512, ffn), m_ctx, s_len, t_len)

    y_prompt = x[:m_ctx].reshape(n_ctx, s_len, d)
    y_sample = x[m_ctx:].reshape(n_lat, t_len, d)
    na_shape = (n_ctx, depth, s_len, NA_HEADS, NA_DH)
    return (y_prompt, y_sample, st_mla[0], st_mla[1], st_na[0].reshape(na_shape), st_na[1].reshape(na_shape), st_hg)
```

```python
import functools
import math

import numpy as np
import jax
import jax.numpy as jnp
from jax import lax
from jax.experimental import pallas as pl
from jax.experimental.pallas import tpu as pltpu

f32 = jnp.float32
bf16 = jnp.bfloat16

GRID_W = 64
MLA_HEADS, MLA_NOPE, MLA_ROPE, MLA_V, MLA_KV_RANK = 8, 128, 64, 128, 512
MLA_QK = MLA_NOPE + MLA_ROPE
HG_HEADS, HG_K, HG_V = 8, 128, 128
NA_HEADS, NA_DH, NA_WIN_R, NA_WIN_C = 8, 128, 8, 16
ROPE_BASE = 10000.0
EPS = 1e-6
NEG = -1e30
LOG2E = 1.4426950408889634

LANES = 128
SUBLANES = 8
VMEM_LIMIT_BYTES = 60 * 1024 * 1024

HEAD_W = 128
MLA_QPAD = 256
HG_CHUNK = 128
HG_LEVELS = int(math.log2(HG_CHUNK))
NA_QROWS = 8
NA_KROWS = 16


def _params(semantics):
    return pltpu.CompilerParams(dimension_semantics=semantics, vmem_limit_bytes=VMEM_LIMIT_BYTES)


def _pow2_tile(pref, *dims):
    t = pref
    while any(d % t for d in dims):
        t //= 2
    assert t >= SUBLANES, (pref, dims)
    return t


def _dot(a, b):
    return jnp.dot(a, b, preferred_element_type=f32)


def _dot_nt(a, b):
    return lax.dot_general(a, b, (((1,), (1,)), ((), ())), preferred_element_type=f32)


def _dot_tn(a, b):
    return lax.dot_general(a, b, (((0,), (0,)), ((), ())), preferred_element_type=f32)


def _silu(x):
    return x * jax.nn.sigmoid(x)


def _mod_const(x, n):
    return jnp.bitwise_and(x, n - 1) if n & (n - 1) == 0 else lax.rem(x, n)


class _ZLayout:
    def __init__(self, d_model, tn):
        self.mq = 0
        self.ckv = self.mq + MLA_HEADS * MLA_QK
        self.hq = self.ckv + MLA_KV_RANK
        self.hff = self.hq + HG_HEADS * HG_K
        self.hfb = self.hff + HG_HEADS * HG_K
        self.hi = self.hfb + HG_HEADS * HG_K
        self.hg = self.hi + HG_HEADS * HG_V
        self.nq = self.hg + HG_HEADS * HG_V
        self.nk = self.nq + NA_HEADS * NA_DH
        self.nv = self.nk + NA_HEADS * NA_DH
        self.ga = self.nv + NA_HEADS * NA_DH
        self.gb = self.ga + d_model
        self.gc = self.gb + d_model
        self.width = self.gc + d_model
        assert self.width % tn == 0


def _permute_w_in(w, lay):
    n_front = MLA_HEADS * MLA_QK + MLA_KV_RANK
    kpe = w[..., n_front:n_front + MLA_ROPE]
    main = jnp.concatenate([w[..., :n_front], w[..., n_front + MLA_ROPE:]], axis=-1).astype(bf16)
    assert main.shape[-1] == lay.width
    pad = [(0, 0)] * (w.ndim - 1) + [(0, LANES - MLA_ROPE)]
    return main, jnp.pad(kpe, pad).astype(bf16)


def _mod_kernel(c_ref, w_ref, b_ref, o_ref):
    a = _silu(c_ref[...]).astype(bf16)
    o_ref[...] = _dot(a, w_ref[...].astype(bf16)) + b_ref[...]


def _modulation(cond, w_mod, b_mod):
    depth, d, n = w_mod.shape
    g = cond.shape[0]
    tn = _pow2_tile(1024, n)
    return pl.pallas_call(
        _mod_kernel,
        grid=(depth, n // tn),
        in_specs=[
            pl.BlockSpec((g, d), lambda l, j: (0, 0)),
            pl.BlockSpec((None, d, tn), lambda l, j: (l, 0, j)),
            pl.BlockSpec((None, 1, tn), lambda l, j: (l, 0, j)),
        ],
        out_specs=pl.BlockSpec((None, g, tn), lambda l, j: (l, 0, j)),
        out_shape=jax.ShapeDtypeStruct((depth, g, n), f32),
        compiler_params=_params(("arbitrary", "arbitrary")),
        name="modulation",
    )(cond, w_mod, b_mod.reshape(depth, 1, n))


def _normmod_matmul_kernel(x_ref, g_ref, sc_ref, sh_ref, w_ref, *rest):
    if len(rest) == 4:
        ws_ref, o_ref, os_ref, h_scr = rest
    else:
        (o_ref, h_scr), ws_ref, os_ref = rest, None, None

    @pl.when(pl.program_id(1) == 0)
    def _():
        x = x_ref[...]
        ms = jnp.mean(x * x, axis=-1, keepdims=True)
        y = x * lax.rsqrt(ms + EPS) * g_ref[...]
        h_scr[...] = (y * (1.0 + sc_ref[...]) + sh_ref[...]).astype(bf16)
        if ws_ref is not None:
            os_ref[...] = _dot(h_scr[...], ws_ref[...])

    o_ref[...] = _dot(h_scr[...], w_ref[...])


def _normmod_matmul(x, g, sc, sh, w, layer, group_of_tile, tm, tn, name, w_side=None):
    m, d = x.shape
    n = w.shape[2]
    in_specs = [
        pl.BlockSpec((tm, d), lambda i, j: (i, 0)),
        pl.BlockSpec((1, d), lambda i, j: (0, 0)),
        pl.BlockSpec((None, 1, d), lambda i, j: (group_of_tile(i), 0, 0)),
        pl.BlockSpec((None, 1, d), lambda i, j: (group_of_tile(i), 0, 0)),
        pl.BlockSpec((None, d, tn), lambda i, j: (layer, 0, j)),
    ]
    args = [x, g.reshape(1, d), sc, sh, w]
    out_specs = [pl.BlockSpec((tm, tn), lambda i, j: (i, j))]
    out_shape = [jax.ShapeDtypeStruct((m, n), f32)]
    if w_side is not None:
        ns = w_side.shape[2]
        in_specs.append(pl.BlockSpec((None, d, ns), lambda i, j: (layer, 0, 0)))
        args.append(w_side)
        out_specs.append(pl.BlockSpec((tm, ns), lambda i, j: (i, 0)))
        out_shape.append(jax.ShapeDtypeStruct((m, ns), f32))
    res = pl.pallas_call(
        _normmod_matmul_kernel,
        grid=(m // tm, n // tn),
        in_specs=in_specs,
        out_specs=out_specs,
        out_shape=out_shape,
        scratch_shapes=[pltpu.VMEM((tm, d), bf16)],
        compiler_params=_params(("arbitrary", "arbitrary")),
        name=name,
    )(*args)
    return res if w_side is not None else res[0]


def _rope(x, cos, sin_signed):
    n = x.shape[-1]
    half = MLA_ROPE // 4
    lane = lax.broadcasted_iota(jnp.int32, x.shape, 1)
    partner = jnp.where((lane % (2 * half)) < half, pltpu.roll(x, n - half, 1), pltpu.roll(x, half, 1))
    return x * cos + partner * sin_signed


def _mla_q_kernel(mq_ref, gn_ref, gp_ref, cos_ref, sin_ref, o_ref, *, scale):
    tm = mq_ref.shape[0]
    lane = lax.broadcasted_iota(jnp.int32, (tm, LANES), 1)
    low = lane < MLA_ROPE
    cos = cos_ref[...]
    sin = sin_ref[...]
    gn = gn_ref[...]
    gp = gp_ref[...]
    for pair in range(MLA_HEADS // 2):
        t0, t1, t2 = (mq_ref[:, (3 * pair + t) * LANES:(3 * pair + t + 1) * LANES] for t in range(3))
        nopes = (t0, jnp.where(low, pltpu.roll(t1, MLA_ROPE, 1), pltpu.roll(t2, MLA_ROPE, 1)))
        pcol = jnp.where(low, t1, t2)
        sq = pcol * pcol
        ss_all = jnp.sum(sq, axis=-1, keepdims=True)
        ss_low = jnp.sum(jnp.where(low, sq, 0.0), axis=-1, keepdims=True)
        rot = _rope(pcol * gp, cos, sin)
        for k, ss_pe in enumerate((ss_low, ss_all - ss_low)):
            h = 2 * pair + k
            nope = nopes[k]
            ss = jnp.sum(nope * nope, axis=-1, keepdims=True) + ss_pe
            r = lax.rsqrt(ss * (1.0 / MLA_QK) + EPS) * scale
            pe = rot if k == 0 else pltpu.roll(rot, MLA_ROPE, 1)
            o_ref[:, h * MLA_QPAD:h * MLA_QPAD + MLA_NOPE] = (nope * r * gn).astype(bf16)
            o_ref[:, h * MLA_QPAD + MLA_NOPE:(h + 1) * MLA_QPAD] = jnp.where(low, pe * r, 0.0).astype(bf16)


class _CtxState:
    def __init__(self, prevs, widths, layer, depth, n_ctx, s_len, tm):
        assert tm % s_len == 0
        self.n_seq = tm // s_len
        self.s_len = s_len
        self.n_tiles = n_ctx * s_len // tm
        self.prevs = list(prevs)
        n_tiles = self.n_tiles
        self.out_specs = [pl.BlockSpec((self.n_seq, None, s_len, w), lambda i: (jnp.minimum(i, n_tiles - 1), layer, 0, 0))
                          for w in widths]
        self.out_shape = [jax.ShapeDtypeStruct((n_ctx, depth, s_len, w), f32) for w in widths]
        self.in_specs = [pl.BlockSpec(memory_space=pl.ANY)] * len(self.prevs)

    def aliases(self, n_inputs_before, n_outputs_before):
        return {n_inputs_before + k: n_outputs_before + k for k in range(len(self.prevs))}

    def kernel_kwargs(self):
        return dict(n_prev=len(self.prevs), n_ctx_tiles=self.n_tiles, s_len=self.s_len)


def _mla_kv_kernel(*refs, norm_ckv, rope, n_prev=0, n_ctx_tiles=0, s_len=0):
    ckv_ref, kpe_ref, kvg_ref, gn_ref, gp_ref, wuk_ref, wuv_ref, cos_ref, sin_ref = refs[:9]
    k_ref, v_ref, *state_refs = refs[9 + n_prev:]
    ckv = ckv_ref[...]
    if norm_ckv:
        ms = jnp.mean(ckv * ckv, axis=-1, keepdims=True)
        ckv = ckv * lax.rsqrt(ms + EPS) * kvg_ref[...]
    if state_refs:
        ckv_state_ref, kpe_state_ref = state_refs

        @pl.when(pl.program_id(0) < n_ctx_tiles)
        def _():
            for n in range(ckv.shape[0] // s_len):
                ckv_state_ref[n] = ckv[n * s_len:(n + 1) * s_len]
                kpe_state_ref[n] = kpe_ref[n * s_len:(n + 1) * s_len, :MLA_ROPE]

    cb = ckv.astype(bf16)
    kn = _dot(cb, wuk_ref[...])
    v_ref[...] = _dot(cb, wuv_ref[...]).astype(bf16)
    tm = ckv.shape[0]
    lane = lax.broadcasted_iota(jnp.int32, (tm, LANES), 1)
    low = lane < MLA_ROPE
    kpe = jnp.where(low, kpe_ref[...], 0.0)
    ss_pe = jnp.sum(kpe * kpe, axis=-1, keepdims=True)
    pe = kpe * gp_ref[...]
    if rope:
        pe = _rope(pe, cos_ref[...], sin_ref[...])
    gn = gn_ref[...]
    for h in range(MLA_HEADS):
        nope = kn[:, h * MLA_NOPE:(h + 1) * MLA_NOPE]
        ss = jnp.sum(nope * nope, axis=-1, keepdims=True) + ss_pe
        r = lax.rsqrt(ss * (1.0 / MLA_QK) + EPS)
        k_ref[:, h * MLA_QPAD:h * MLA_QPAD + MLA_NOPE] = (nope * r * gn).astype(bf16)
        k_ref[:, h * MLA_QPAD + MLA_NOPE:(h + 1) * MLA_QPAD] = jnp.where(low, pe * r, 0.0).astype(bf16)


def _rope_tables(t_len, tm):
    n_freq = MLA_ROPE // 4
    t = jnp.arange(t_len, dtype=jnp.int32)
    inv = ROPE_BASE ** (-jnp.arange(n_freq, dtype=f32) / n_freq)
    ang_r = (t // GRID_W).astype(f32)[:, None] * inv
    ang_c = (t % GRID_W).astype(f32)[:, None] * inv
    cos64 = jnp.concatenate([jnp.cos(ang_r), jnp.cos(ang_r), jnp.cos(ang_c), jnp.cos(ang_c)], axis=1)
    sin64 = jnp.concatenate([-jnp.sin(ang_r), jnp.sin(ang_r), -jnp.sin(ang_c), jnp.sin(ang_c)], axis=1)
    cos = jnp.concatenate([jnp.ones((tm, LANES), f32), jnp.tile(cos64, (1, 2))], axis=0)
    sin = jnp.concatenate([jnp.zeros((tm, LANES), f32), jnp.tile(sin64, (1, 2))], axis=0)
    return cos, sin


def _mla_q(z, lay, q_g, cos, sin, rope_block, tm):
    m = z.shape[0]
    gn = q_g[:MLA_NOPE].reshape(1, MLA_NOPE)
    gp = jnp.tile(q_g[MLA_NOPE:], 2).reshape(1, LANES)
    wq = MLA_HEADS * MLA_QK
    assert MLA_HEADS % 2 == 0 and 2 * MLA_QK == 3 * LANES and 2 * MLA_ROPE == LANES and lay.mq % wq == 0
    return pl.pallas_call(
        functools.partial(_mla_q_kernel, scale=MLA_QK ** -0.5 * LOG2E),
        grid=(m // tm,),
        in_specs=[
            pl.BlockSpec((tm, wq), lambda i: (i, lay.mq // wq)),
            pl.BlockSpec((1, MLA_NOPE), lambda i: (0, 0)),
            pl.BlockSpec((1, LANES), lambda i: (0, 0)),
            pl.BlockSpec((tm, LANES), lambda i: (rope_block(i), 0)),
            pl.BlockSpec((tm, LANES), lambda i: (rope_block(i), 0)),
        ],
        out_specs=pl.BlockSpec((tm, MLA_HEADS * MLA_QPAD), lambda i: (i, 0)),
        out_shape=jax.ShapeDtypeStruct((m, MLA_HEADS * MLA_QPAD), bf16),
        compiler_params=_params(("arbitrary",)),
        name="mla_q",
    )(z, gn, gp, cos, sin)


def _mla_kv(ckv_src, ckv_col, kpe_src, kpe_col, kv_g, k_g, w_uk, w_uv, layer, cos, sin, rope_block, tm, norm_ckv, rope,
            name, state=None):
    m = ckv_src.shape[0]
    gn = k_g[:MLA_NOPE].reshape(1, MLA_NOPE)
    gp = jnp.concatenate([k_g[MLA_NOPE:], jnp.zeros((LANES - MLA_ROPE,), f32)]).reshape(1, LANES)
    hw = MLA_HEADS * MLA_NOPE
    in_specs = [
        pl.BlockSpec((tm, MLA_KV_RANK), lambda i: (i, ckv_col // MLA_KV_RANK)),
        pl.BlockSpec((tm, LANES), lambda i: (i, kpe_col // LANES)),
        pl.BlockSpec((1, MLA_KV_RANK), lambda i: (0, 0)),
        pl.BlockSpec((1, MLA_NOPE), lambda i: (0, 0)),
        pl.BlockSpec((1, LANES), lambda i: (0, 0)),
        pl.BlockSpec((None, MLA_KV_RANK, hw), lambda i: (layer, 0, 0)),
        pl.BlockSpec((None, MLA_KV_RANK, hw), lambda i: (layer, 0, 0)),
        pl.BlockSpec((tm, LANES), lambda i: (rope_block(i), 0)),
        pl.BlockSpec((tm, LANES), lambda i: (rope_block(i), 0)),
    ]
    args = [ckv_src, kpe_src, kv_g.reshape(1, MLA_KV_RANK), gn, gp, w_uk, w_uv, cos, sin]
    out_specs = [pl.BlockSpec((tm, MLA_HEADS * MLA_QPAD), lambda i: (i, 0)), pl.BlockSpec((tm, hw), lambda i: (i, 0))]
    out_shape = [jax.ShapeDtypeStruct((m, MLA_HEADS * MLA_QPAD), bf16), jax.ShapeDtypeStruct((m, hw), bf16)]
    kwargs, aliases = {}, {}
    if state is not None:
        aliases = state.aliases(len(args), len(out_specs))
        kwargs = state.kernel_kwargs()
        in_specs += state.in_specs
        args += state.prevs
        out_specs += state.out_specs
        out_shape += state.out_shape
    return pl.pallas_call(
        functools.partial(_mla_kv_kernel, norm_ckv=norm_ckv, rope=rope, **kwargs),
        grid=(m // tm,),
        in_specs=in_specs,
        out_specs=out_specs,
        out_shape=out_shape,
        input_output_aliases=aliases,
        compiler_params=_params(("arbitrary",)),
        name=name,
    )(*args)


def _flash_kernel(*refs, n_main, tk, n_extra, heads, dq, dv):
    if n_extra:
        q_ref, k_ref, v_ref, kx_ref, vx_ref, o_ref = refs
    else:
        q_ref, k_ref, v_ref, o_ref = refs
        kx_ref = vx_ref = None
    tq = q_ref.shape[0]
    chunks = [(k_ref, v_ref, c) for c in range(n_main)] + [(kx_ref, vx_ref, c) for c in range(n_extra)]
    def scores(i, g):
        kr, _, c = chunks[i]
        return _dot_nt(kr[c * tk:(c + 1) * tk, g * dq:(g + 1) * dq], q_ref[:, g * dq:(g + 1) * dq])

    m = [jnp.full((1, tq), -jnp.inf, f32)] * heads
    l = [jnp.zeros((1, tq), f32)] * heads
    acc = [jnp.zeros((dv, tq), f32)] * heads
    s = [scores(0, g) for g in range(heads)]
    for i, (_, vr, c) in enumerate(chunks):
        for g in range(heads):
            s_next = scores(i + 1, g) if i + 1 < len(chunks) else None
            m_new = jnp.maximum(m[g], jnp.max(s[g], axis=0, keepdims=True))
            alpha = jnp.exp2(m[g] - m_new)
            p = jnp.exp2(s[g] - m_new)
            l[g] = alpha * l[g] + jnp.sum(p, axis=0, keepdims=True)
            acc[g] = alpha * acc[g] + _dot_tn(vr[c * tk:(c + 1) * tk, g * dv:(g + 1) * dv], p.astype(bf16))
            m[g], s[g] = m_new, s_next
    for g in range(heads):
        o_ref[:, g * dv:(g + 1) * dv] = (acc[g] / l[g]).T.astype(o_ref.dtype)


def _flash(q, k, v, *, batch, heads, t_len, s_len, row0, dq, dv, tq, tk, heads_per_step=1, extra=None, name):
    assert row0 % tq == 0 and row0 % s_len == 0 and t_len % tq == 0 and s_len % tk == 0
    assert heads % heads_per_step == 0
    nq = t_len // tq
    g = heads_per_step
    in_specs = [
        pl.BlockSpec((tq, g * dq), lambda b, h, i: (row0 // tq + b * nq + i, h)),
        pl.BlockSpec((s_len, g * dq), lambda b, h, i: (row0 // s_len + b, h)),
        pl.BlockSpec((s_len, g * dv), lambda b, h, i: (row0 // s_len + b, h)),
    ]
    args = [q, k, v]
    n_extra = 0
    if extra is not None:
        kx, vx, p_len = extra
        assert p_len % tk == 0
        n_extra = p_len // tk
        in_specs += [
            pl.BlockSpec((p_len, g * dq), lambda b, h, i: (b, h)),
            pl.BlockSpec((p_len, g * dv), lambda b, h, i: (b, h)),
        ]
        args += [kx, vx]
    return pl.pallas_call(
        functools.partial(_flash_kernel, n_main=s_len // tk, tk=tk, n_extra=n_extra, heads=g, dq=dq, dv=dv),
        grid=(batch, heads // g, nq),
        in_specs=in_specs,
        out_specs=pl.BlockSpec((tq, g * dv), lambda b, h, i: (b * nq + i, h)),
        out_shape=jax.ShapeDtypeStruct((batch * t_len, heads * dv), bf16),
        compiler_params=_params(("arbitrary", "arbitrary", "arbitrary")),
        name=name,
    )(*args)


def _na_prep_kernel(*refs, scale, n_prev, n_ctx_tiles, s_len):
    nq_ref, nk_ref, nv_ref, qg_ref, kg_ref = refs[:5]
    q_ref, k_ref, v_ref, k_state_ref, v_state_ref = refs[5 + n_prev:]
    qg = qg_ref[...]
    kg = kg_ref[...]
    tm = nv_ref.shape[0]
    is_ctx = pl.program_id(0) < n_ctx_tiles
    v_ref[...] = nv_ref[...].astype(bf16)

    @pl.when(is_ctx)
    def _():
        for n in range(tm // s_len):
            v_state_ref[n] = nv_ref[n * s_len:(n + 1) * s_len, :]

    for h in range(NA_HEADS):
        sl = slice(h * NA_DH, (h + 1) * NA_DH)
        q = nq_ref[:, sl]
        k = nk_ref[:, sl]
        rq = lax.rsqrt(jnp.mean(q * q, axis=-1, keepdims=True) + EPS)
        rk = lax.rsqrt(jnp.mean(k * k, axis=-1, keepdims=True) + EPS)
        q_ref[:, sl] = (q * rq * qg * scale).astype(bf16)
        kn = k * rk * kg
        k_ref[:, sl] = kn.astype(bf16)

        @pl.when(is_ctx)
        def _():
            for n in range(tm // s_len):
                k_state_ref[n, :, sl] = kn[n * s_len:(n + 1) * s_len]


def _na_prep(z, lay, q_g, k_g, tm, state):
    m = z.shape[0]
    w = NA_HEADS * NA_DH
    spec = pl.BlockSpec((tm, w), lambda i: (i, 0))
    in_specs = [
        pl.BlockSpec((tm, w), lambda i: (i, lay.nq // w)),
        pl.BlockSpec((tm, w), lambda i: (i, lay.nk // w)),
        pl.BlockSpec((tm, w), lambda i: (i, lay.nv // w)),
        pl.BlockSpec((1, NA_DH), lambda i: (0, 0)),
        pl.BlockSpec((1, NA_DH), lambda i: (0, 0)),
    ]
    args = [z, z, z, q_g.reshape(1, NA_DH), k_g.reshape(1, NA_DH)]
    return pl.pallas_call(
        functools.partial(_na_prep_kernel, scale=NA_DH ** -0.5 * LOG2E, **state.kernel_kwargs()),
        grid=(m // tm,),
        in_specs=in_specs + state.in_specs,
        out_specs=[spec, spec, spec] + state.out_specs,
        out_shape=[jax.ShapeDtypeStruct((m, w), bf16)] * 3 + state.out_shape,
        input_output_aliases=state.aliases(len(args), 3),
        compiler_params=_params(("arbitrary",)),
        name="na_prep",
    )(*args, *state.prevs)


def _na_static_maps(rows):
    nblk = rows // NA_QROWS
    reps = (0, min(1, nblk - 1), nblk - 1)
    dr_map = np.full((3, NA_QROWS, NA_KROWS), 2 * NA_WIN_R - 1, np.int32)
    for v, kb in enumerate(reps):
        ws = min(max(NA_QROWS * kb - NA_WIN_R // 2, 0), rows - NA_KROWS)
        for i in range(NA_QROWS):
            qrow = NA_QROWS * kb + i
            rs = min(max(qrow - NA_WIN_R // 2, 0), rows - NA_WIN_R)
            for j in range(NA_KROWS):
                krow = ws + j
                if rs <= krow < rs + NA_WIN_R:
                    dr_map[v, i, j] = krow - qrow + NA_WIN_R - 1
    qcol = np.arange(GRID_W)
    cs = np.clip(qcol - NA_WIN_C // 2, 0, GRID_W - NA_WIN_C)
    in_win = (qcol[None, :] >= cs[:, None]) & (qcol[None, :] < cs[:, None] + NA_WIN_C)
    dc_idx = np.clip(qcol[None, :] - qcol[:, None], -(NA_WIN_C - 1), NA_WIN_C - 1) + (NA_WIN_C - 1)
    return dr_map, in_win, dc_idx


def _na_bias_table(rpb, rows):
    dr_map, in_win, dc_idx = _na_static_maps(rows)
    depth, h = rpb.shape[:2]
    bt = jnp.where(in_win.T, jnp.take(rpb * LOG2E, jnp.asarray(dc_idx.T), axis=3), NEG)
    bt = jnp.concatenate([bt, jnp.full((depth, h, 1, GRID_W, GRID_W), NEG, f32)], axis=2)
    tab = jnp.take(bt, jnp.asarray(dr_map.transpose(0, 2, 1).reshape(-1)), axis=2)
    tab = tab.reshape(depth, h, 3, NA_KROWS, NA_QROWS, GRID_W, GRID_W).transpose(0, 2, 1, 3, 5, 4, 6)
    return tab.reshape(depth, 3, h, NA_KROWS * GRID_W, NA_QROWS * GRID_W)


def _na_kernel(q_ref, k_ref, v_ref, kc_ref, vc_ref, bias_ref, o_ref, *, rows, heads):
    kb = pl.program_id(2)
    ws = jnp.clip(NA_QROWS * kb - NA_WIN_R // 2, 0, rows - NA_KROWS)
    win = pl.ds(pl.multiple_of(ws * GRID_W, (NA_WIN_R // 2) * GRID_W), NA_KROWS * GRID_W)
    cols = [slice(g * NA_DH, (g + 1) * NA_DH) for g in range(heads)]
    s_loc = [_dot_nt(k_ref[win, c], q_ref[:, c]) + bias_ref[g] for g, c in enumerate(cols)]
    s_ctx = [_dot_nt(kc_ref[:, c], q_ref[:, c]) for c in cols]
    p_loc, p_ctx, l = [], [], []
    for g in range(heads):
        m = jnp.maximum(jnp.max(s_loc[g], axis=0, keepdims=True), jnp.max(s_ctx[g], axis=0, keepdims=True))
        pl_g = jnp.exp2(s_loc[g] - m)
        pc_g = jnp.exp2(s_ctx[g] - m)
        l.append(jnp.sum(pl_g, axis=0, keepdims=True) + jnp.sum(pc_g, axis=0, keepdims=True))
        p_loc.append(pl_g.astype(bf16))
        p_ctx.append(pc_g.astype(bf16))
    for g, c in enumerate(cols):
        acc = _dot_tn(v_ref[win, c], p_loc[g]) + _dot_tn(vc_ref[:, c], p_ctx[g])
        o_ref[:, c] = (acc / l[g]).T.astype(o_ref.dtype)


def _na_latent(q, k, v, kc, vc, bias, layer, *, batch, t_len, p_len, row0, heads_per_step=4):
    rows = t_len // GRID_W
    assert t_len % GRID_W == 0 and rows % NA_QROWS == 0 and rows >= NA_KROWS
    tq = NA_QROWS * GRID_W
    assert row0 % tq == 0 and row0 % t_len == 0 and NA_HEADS % heads_per_step == 0
    nblk = rows // NA_QROWS
    g = heads_per_step
    gw = g * NA_DH

    def variant(i):
        return jnp.where(i == 0, 0, jnp.where(i == nblk - 1, 2, 1))

    return pl.pallas_call(
        functools.partial(_na_kernel, rows=rows, heads=g),
        grid=(batch, NA_HEADS // g, nblk),
        in_specs=[
            pl.BlockSpec((tq, gw), lambda b, h, i: (row0 // tq + b * nblk + i, h)),
            pl.BlockSpec((t_len, gw), lambda b, h, i: (row0 // t_len + b, h)),
            pl.BlockSpec((t_len, gw), lambda b, h, i: (row0 // t_len + b, h)),
            pl.BlockSpec((None, None, p_len, gw), lambda b, h, i: (b, layer, 0, h)),
            pl.BlockSpec((None, None, p_len, gw), lambda b, h, i: (b, layer, 0, h)),
            pl.BlockSpec((None, None, g, NA_KROWS * GRID_W, tq), lambda b, h, i: (layer, variant(i), h, 0, 0)),
        ],
        out_specs=pl.BlockSpec((tq, gw), lambda b, h, i: (b * nblk + i, h)),
        out_shape=jax.ShapeDtypeStruct((batch * t_len, NA_HEADS * NA_DH), bf16),
        compiler_params=_params(("arbitrary", "arbitrary", "arbitrary")),
        name="na_latent",
    )(q, k, v, kc, vc, bias)


def _hgrn_tables():
    c, lv = HG_CHUNK, HG_LEVELS
    t = np.arange(c)[:, None]
    r = np.arange(c)[None, :]
    wcum = np.stack([r <= t, r >= t]).astype(np.float32)
    qside = np.zeros((2, lv * c, LANES), np.float32)
    mask = np.zeros((2, lv + 1, c, c), np.float32)
    mask[:, 0] = np.eye(c)
    for l in range(lv):
        hs = 1 << l
        blk = t // (2 * hs)
        mid = blk * (2 * hs) + hs
        qf = t >= mid
        qb = t < mid
        qside[0, l * c:(l + 1) * c] = np.where(qf, 1.0, -1.0)
        qside[1, l * c:(l + 1) * c] = np.where(qb, 1.0, -1.0)
        same = blk == blk.T
        mask[0, 1 + l] = same & qf & ~qf.T
        mask[1, 1 + l] = same & qb & ~qb.T
    return wcum, qside, mask


def _hgrn_kernel(*refs, n_chunks, heads, has_state, emit_state, n_prev=0):
    (hq_ref, hff_ref, hfb_ref, hi_ref, hg_ref, la_ref, lc_ref, om_ref, ng_ref,
     wcum_ref, qside_ref, mask_ref) = refs[:12]
    rest = list(refs[12:])
    s0_ref = rest.pop(0) if has_state else None
    del rest[:n_prev]
    o_ref = rest.pop(0)
    st_ref = rest.pop(0) if emit_state else None
    o_scr, s_scr, b_scr = rest
    c, lv = HG_CHUNK, HG_LEVELS

    o_scr[...] = jnp.zeros_like(o_scr)
    for d in range(2):
        for g in range(heads):
            if has_state:
                s_scr[d, g] = s0_ref[d, g].T
            else:
                s_scr[d, g] = jnp.zeros((HG_V, HG_K), f32)

    chains = [(d, g) for g in range(heads) for d in range(2)]
    low_half = lax.broadcasted_iota(jnp.int32, (SUBLANES, HEAD_W), 0) < SUBLANES // 2

    def gates_and_cumsum(ci, row):
        d, g = chains[ci]
        rs = pl.ds(pl.multiple_of(row, c), c)
        cs = slice(g * HEAD_W, (g + 1) * HEAD_W)
        x = (hff_ref if d == 0 else hfb_ref)[rs, cs]
        x2 = x * LOG2E
        u = jnp.exp2(-jnp.abs(x2))
        inv = 1.0 / (1.0 + u)
        pos = x >= 0.0
        log_sig = jnp.minimum(x2, 0.0) - jnp.log2(1.0 + u)
        a = la_ref[d:d + 1, cs]
        bb = lc_ref[d:d + 1, cs] + log_sig
        lf = jnp.maximum(a, bb) + jnp.log2(1.0 + jnp.exp2(-jnp.abs(a - bb)))
        om = om_ref[d:d + 1, cs]
        f = (1.0 - om) + om * (jnp.where(pos, 1.0, u) * inv)
        kin = om * (jnp.where(pos, u, 1.0) * inv)
        q = _silu(hq_ref[rs, cs])
        v = hi_ref[rs, cs].astype(bf16)
        l1 = lf.astype(bf16)
        r1 = lf - l1.astype(f32)
        l2 = r1.astype(bf16)
        l3 = (r1 - l2.astype(f32)).astype(bf16)
        w = wcum_ref[d]
        b = _dot(w, l1) + _dot(w, l2) + _dot(w, l3)
        b_scr[ci] = b
        return dict(rs=rs, cs=cs, f=f, kin=kin, q=q, v=v, b=b)

    def intra_chunk(ci, s):
        d, _ = chains[ci]
        b, q, kin = s["b"], s["q"], s["kin"]

        def row(r):
            return b_scr[ci, r:r + 1, :]

        s["b_end"] = row(c - 1 if d == 0 else 0)
        xs = [jnp.where(qside_ref[d, 0:c, :] > 0.0, q * s["f"], kin).astype(bf16)]
        for l in range(1, lv):
            hs = 1 << l
            blk = 2 * hs
            pick = hs - 1 if d == 0 else hs
            if hs >= SUBLANES:
                pieces = []
                for j in range(c // blk):
                    m = row(blk * j + pick)
                    early = slice(blk * j, blk * j + hs)
                    late = slice(blk * j + hs, blk * (j + 1))
                    q_sl, k_sl = (late, early) if d == 0 else (early, late)
                    xq = q[q_sl] * jnp.exp2(b[q_sl] - m)
                    xk = kin[k_sl] * jnp.exp2(m - b[k_sl])
                    pieces += [xk, xq] if d == 0 else [xq, xk]
                xs.append(jnp.concatenate(pieces, axis=0).astype(bf16))
                continue
            if blk < SUBLANES:
                pieces = [jnp.where(low_half, jnp.broadcast_to(row(SUBLANES * j + pick), (SUBLANES, HEAD_W)),
                                    jnp.broadcast_to(row(SUBLANES * j + blk + pick), (SUBLANES, HEAD_W)))
                          for j in range(c // SUBLANES)]
            else:
                pieces = [jnp.broadcast_to(row(blk * j + pick), (blk, HEAD_W)) for j in range(c // blk)]
            m = pieces[0] if len(pieces) == 1 else jnp.concatenate(pieces, axis=0)
            sign = qside_ref[d, l * c:(l + 1) * c, :]
            xs.append((jnp.where(sign > 0.0, q, kin) * jnp.exp2((b - m) * sign)).astype(bf16))
        att = mask_ref[d, 0] * _dot_nt(q.astype(bf16), kin.astype(bf16)).astype(bf16)
        for l in range(lv):
            att = att + mask_ref[d, 1 + l] * _dot_nt(xs[l], xs[l]).astype(bf16)
        s["att"] = att

    def state_step(ci, s):
        d, g = chains[ci]
        b, q, kin, v = s["b"], s["q"], s["kin"], s["v"]
        st = s_scr[d, g]
        o = _dot(s["att"], v) + _dot_nt((q * jnp.exp2(b)).astype(bf16), st.astype(bf16))
        b_end = s["b_end"]
        s_scr[d, g] = st * jnp.exp2(b_end) + _dot_tn(v, (kin * jnp.exp2(b_end - b)).astype(bf16))
        o_scr[s["rs"], s["cs"]] += o

    def body(i, _):
        rows = (i * c, (n_chunks - 1 - i) * c)
        states = [gates_and_cumsum(ci, rows[chains[ci][0]]) for ci in range(len(chains))]
        for ci, s in enumerate(states):
            intra_chunk(ci, s)
        for ci, s in enumerate(states):
            state_step(ci, s)
        return 0

    lax.fori_loop(0, n_chunks, body, 0)

    ng = ng_ref[...]
    for g in range(heads):
        cs = slice(g * HEAD_W, (g + 1) * HEAD_W)
        o = o_scr[:, cs]
        y = o * lax.rsqrt(jnp.mean(o * o, axis=-1, keepdims=True) + EPS) * ng
        o_ref[:, cs] = (y * _silu(hg_ref[:, cs])).astype(o_ref.dtype)
        if emit_state:
            for d in range(2):
                st_ref[d, g] = s_scr[d, g].T


def _hgrn(z, lay, lb_tabs, norm_g, tables, *, batch, t_len, row0, s0, state_out=None, heads_per_step=2):
    emit_state = state_out is not None
    assert t_len % HG_CHUNK == 0 and row0 % t_len == 0 and HG_K == HEAD_W and HG_V == HEAD_W
    g = heads_per_step
    gw = g * HEAD_W
    la, lc, om = lb_tabs
    wcum, qside, mask = tables
    r0 = row0 // t_len

    def zspec(col):
        return pl.BlockSpec((t_len, gw), lambda b, h: (r0 + b, col // gw + h))

    def const(shape):
        return pl.BlockSpec(shape, lambda b, h: (0,) * len(shape))

    in_specs = [zspec(lay.hq), zspec(lay.hff), zspec(lay.hfb), zspec(lay.hi), zspec(lay.hg),
                pl.BlockSpec((2, gw), lambda b, h: (0, h)), pl.BlockSpec((2, gw), lambda b, h: (0, h)),
                pl.BlockSpec((2, gw), lambda b, h: (0, h)), const((1, HG_V)),
                const(wcum.shape), const(qside.shape), const(mask.shape)]
    args = [z, z, z, z, z, la, lc, om, norm_g.reshape(1, HG_V), wcum, qside, mask]
    state_spec = pl.BlockSpec((None, 2, g, HG_K, HG_V), lambda b, h: (b, 0, h, 0, 0))
    if s0 is not None:
        in_specs.append(state_spec)
        args.append(s0)
    out_specs = [pl.BlockSpec((t_len, gw), lambda b, h: (b, h))]
    out_shape = [jax.ShapeDtypeStruct((batch * t_len, HG_HEADS * HG_V), bf16)]
    aliases, n_prev = {}, 0
    if emit_state:
        prev, layer, depth = state_out
        aliases, n_prev = {len(args): 1}, 1
        in_specs.append(pl.BlockSpec(memory_space=pl.ANY))
        args.append(prev)
        out_specs.append(pl.BlockSpec((None, None, 2, g, HG_K, HG_V), lambda b, h: (b, layer, 0, h, 0, 0)))
        out_shape.append(jax.ShapeDtypeStruct((batch, depth, 2, HG_HEADS, HG_K, HG_V), f32))
    res = pl.pallas_call(
        functools.partial(_hgrn_kernel, n_chunks=t_len // HG_CHUNK, heads=g, has_state=s0 is not None,
                          emit_state=emit_state, n_prev=n_prev),
        grid=(batch, HG_HEADS // g),
        in_specs=in_specs,
        out_specs=out_specs,
        out_shape=out_shape,
        input_output_aliases=aliases,
        scratch_shapes=[pltpu.VMEM((t_len, gw), f32), pltpu.VMEM((2, g, HG_V, HG_K), f32),
                        pltpu.VMEM((2 * g, HG_CHUNK, HEAD_W), f32)],
        compiler_params=_params(("arbitrary", "arbitrary")),
        name="hgrn_ctx" if emit_state else "hgrn_latent",
    )(*args)
    return res if emit_state else (res[0], None)


def _merge_kernel(oac_ref, oal_ref, obc_ref, obl_ref, occ_ref, ocl_ref, ga_ref, gb_ref, gc_ref, w_ref, o_ref,
                  *, n_ctx_tiles):
    def compute(oa_ref, ob_ref, oc_ref):
        m = jax.nn.sigmoid(ga_ref[...]) * _dot(oa_ref[...], w_ref[0])
        m = m + jax.nn.sigmoid(gb_ref[...]) * _dot(ob_ref[...], w_ref[1])
        m = m + jax.nn.sigmoid(gc_ref[...]) * _dot(oc_ref[...], w_ref[2])
        o_ref[...] = m.astype(o_ref.dtype)

    is_ctx = pl.program_id(1) < n_ctx_tiles

    @pl.when(is_ctx)
    def _():
        compute(oac_ref, obc_ref, occ_ref)

    @pl.when(jnp.logical_not(is_ctx))
    def _():
        compute(oal_ref, obl_ref, ocl_ref)


def _merge(o_a, o_b, o_c, z, lay, w_branch, layer, tm, tn):
    bw = o_a[0].shape[1]
    m = z.shape[0]
    d = w_branch.shape[3]
    nct = o_a[0].shape[0] // tm
    cspec = pl.BlockSpec((tm, bw), lambda j, i: (jnp.minimum(i, nct - 1), 0))
    lspec = pl.BlockSpec((tm, bw), lambda j, i: (jnp.maximum(i - nct, 0), 0))

    def gate(col):
        return pl.BlockSpec((tm, tn), lambda j, i: (i, col // tn + j))

    return pl.pallas_call(
        functools.partial(_merge_kernel, n_ctx_tiles=nct),
        grid=(d // tn, m // tm),
        in_specs=[cspec, lspec, cspec, lspec, cspec, lspec, gate(lay.ga), gate(lay.gb), gate(lay.gc),
                  pl.BlockSpec((None, 3, bw, tn), lambda j, i: (layer, 0, 0, j))],
        out_specs=pl.BlockSpec((tm, tn), lambda j, i: (i, j)),
        out_shape=jax.ShapeDtypeStruct((m, d), bf16),
        compiler_params=_params(("arbitrary", "arbitrary")),
        name="merge",
    )(o_a[0], o_a[1], o_b[0], o_b[1], o_c[0], o_c[1], z, z, z, w_branch)


def _proj_residual_kernel(m_ref, w_ref, x_ref, g_ref, o_ref):
    o_ref[...] = x_ref[...] + g_ref[...] * _dot(m_ref[...], w_ref[...])


def _proj_residual(mm, w, layer, x, gate, group_of_tile, tm, tn):
    m, k = mm.shape
    d = w.shape[2]
    return pl.pallas_call(
        _proj_residual_kernel,
        grid=(m // tm, d // tn),
        in_specs=[
            pl.BlockSpec((tm, k), lambda i, j: (i, 0)),
            pl.BlockSpec((None, k, tn), lambda i, j: (layer, 0, j)),
            pl.BlockSpec((tm, tn), lambda i, j: (i, j)),
            pl.BlockSpec((None, 1, tn), lambda i, j: (group_of_tile(i), 0, j)),
        ],
        out_specs=pl.BlockSpec((tm, tn), lambda i, j: (i, j)),
        out_shape=jax.ShapeDtypeStruct((m, d), f32),
        compiler_params=_params(("arbitrary", "arbitrary")),
        name="out_proj",
    )(mm, w, x, gate)


def _ffn_kernel(x_ref, xprev_ref, xnext_ref, ng_ref, sc_ref, sh_ref, wa_ref, wg_ref, cw_ref, cb_ref, wd_ref, gate_ref,
                o_ref, h_scr, *, m_ctx, s_len, t_len, n_k):
    i = pl.program_id(0)
    k = pl.program_id(1)
    tm = x_ref.shape[0]
    tk = wa_ref.shape[1]
    halo = SUBLANES

    @pl.when(k == 0)
    def _():
        def normmod(x):
            ms = jnp.mean(x * x, axis=-1, keepdims=True)
            y = x * lax.rsqrt(ms + EPS) * ng_ref[...]
            return (y * (1.0 + sc_ref[...]) + sh_ref[...]).astype(bf16)

        h_scr[0:halo] = normmod(xprev_ref[...])
        h_scr[halo:halo + tm] = normmod(x_ref[...])
        h_scr[halo + tm:halo + tm + halo] = normmod(xnext_ref[...])
        o_ref[...] = jnp.zeros_like(o_ref)

    row = i * tm + lax.broadcasted_iota(jnp.int32, (tm, LANES), 0)
    pos = jnp.where(row < m_ctx, _mod_const(row, s_len), _mod_const(row - m_ctx, t_len))
    last = jnp.where(row < m_ctx, s_len - 1, t_len - 1)
    keep_prev = jnp.tile(jnp.where(pos == 0, 0.0, 1.0), (1, tk // LANES))
    keep_next = jnp.tile(jnp.where(pos == last, 0.0, 1.0), (1, tk // LANES))

    g_ext = _dot(h_scr[...], wg_ref[...])
    a = _dot(h_scr[halo:halo + tm], wa_ref[...])
    n_ext = tm + 2 * halo
    g = g_ext[halo:halo + tm]
    g_prev = pltpu.roll(g_ext, 1, 0)[halo:halo + tm]
    g_next = pltpu.roll(g_ext, n_ext - 1, 0)[halo:halo + tm]
    conv = (g_prev * keep_prev * cw_ref[0:1, :] + g * cw_ref[1:2, :] + g_next * keep_next * cw_ref[2:3, :]
            + cb_ref[...])
    u = (_silu(conv) * a).astype(bf16)
    o_ref[...] = _dot(u, wd_ref[...]) + o_ref[...]

    @pl.when(k == n_k - 1)
    def _():
        o_ref[...] = x_ref[...] + gate_ref[...] * o_ref[...]


def _ffn(x, norm_g, sc, sh, w_up, conv_w, conv_b, w_down, layer, gate, group_of_tile, tm, tk, m_ctx, s_len, t_len):
    m, d = x.shape
    f = w_down.shape[1]
    nk = f // tk
    nsub = tm // SUBLANES
    last_sub = m // SUBLANES - 1
    return pl.pallas_call(
        functools.partial(_ffn_kernel, m_ctx=m_ctx, s_len=s_len, t_len=t_len, n_k=nk),
        grid=(m // tm, nk),
        in_specs=[
            pl.BlockSpec((tm, d), lambda i, k: (i, 0)),
            pl.BlockSpec((SUBLANES, d), lambda i, k: (jnp.maximum(i * nsub - 1, 0), 0)),
            pl.BlockSpec((SUBLANES, d), lambda i, k: (jnp.minimum((i + 1) * nsub, last_sub), 0)),
            pl.BlockSpec((1, d), lambda i, k: (0, 0)),
            pl.BlockSpec((None, 1, d), lambda i, k: (group_of_tile(i), 0, 0)),
            pl.BlockSpec((None, 1, d), lambda i, k: (group_of_tile(i), 0, 0)),
            pl.BlockSpec((None, d, tk), lambda i, k: (layer, 0, k)),
            pl.BlockSpec((None, d, tk), lambda i, k: (layer, 0, nk + k)),
            pl.BlockSpec((3, tk), lambda i, k: (0, k)),
            pl.BlockSpec((1, tk), lambda i, k: (0, k)),
            pl.BlockSpec((None, tk, d), lambda i, k: (layer, k, 0)),
            pl.BlockSpec((None, 1, d), lambda i, k: (group_of_tile(i), 0, 0)),
        ],
        out_specs=pl.BlockSpec((tm, d), lambda i, k: (i, 0)),
        out_shape=jax.ShapeDtypeStruct((m, d), f32),
        scratch_shapes=[pltpu.VMEM((tm + 2 * SUBLANES, d), bf16)],
        compiler_params=_params(("arbitrary", "arbitrary")),
        name="ffn",
    )(x, x, x, norm_g.reshape(1, d), sc, sh, w_up, w_up, conv_w, conv_b.reshape(1, f), w_down, gate)


def kernel(x_prompt, x_sample, cache_mla_ckv, cache_mla_kpe, cache_na_k, cache_na_v, state_hgrn, c, c_ctx, w_mod, b_mod, norm1_g, norm2_g, w_in, mla_kv_norm_g, mla_q_norm_g, mla_k_norm_g, mla_w_uk, mla_w_uv, hgrn_lower_bounds, hgrn_norm_g, na_q_norm_g, na_k_norm_g, na_rpb, w_branch, w_out, ffn_w_up, ffn_conv_w, ffn_conv_b, ffn_w_down):
    n_ctx, s_len, d = x_prompt.shape
    n_lat, t_len, _ = x_sample.shape
    depth = w_in.shape[0]
    p_len = cache_mla_ckv.shape[2]
    ffn = ffn_w_down.shape[1]
    m_ctx, m_lat = n_ctx * s_len, n_lat * t_len
    m = m_ctx + m_lat
    assert m_ctx % t_len == 0, "context rows must be a whole number of latent sequences"

    tm = _pow2_tile(1024, m_ctx, t_len)
    tn = _pow2_tile(1024, d)
    tp = _pow2_tile(512, m_ctx, t_len)
    lay = _ZLayout(d, tn)

    def group_fn(rows):
        return lambda i: jnp.where(i < m_ctx // rows, 0, 1 + (i - m_ctx // rows) // (t_len // rows))

    group_of_tile = group_fn(tm)

    n_groups = -(-(1 + n_lat) // SUBLANES) * SUBLANES
    cond = jnp.concatenate([c_ctx[None], c, jnp.zeros((n_groups - 1 - n_lat, d), f32)], axis=0)
    mods = _modulation(cond, w_mod, b_mod).reshape(depth, n_groups, 6, d).transpose(0, 2, 1, 3)[:, :, :, None, :]

    sm = jax.nn.softmax(hgrn_lower_bounds.astype(f32), axis=1)
    csum = jnp.cumsum(sm, axis=1)
    lower = csum - csum[:, :1]
    hg_tabs = tuple(jnp.asarray(t, dt) for t, dt in zip(_hgrn_tables(), (bf16, f32, bf16)))

    cos, sin = _rope_tables(t_len, tp)
    n_ctx_tp = m_ctx // tp

    def rope_block(i):
        return jnp.where(i < n_ctx_tp, 0, 1 + (i - n_ctx_tp) % (t_len // tp))

    x = jnp.concatenate([x_prompt.reshape(m_ctx, d), x_sample.reshape(m_lat, d)], axis=0)
    w_in_b, w_kpe_b = _permute_w_in(w_in, lay)
    w_uk = mla_w_uk.reshape(depth, MLA_KV_RANK, MLA_HEADS * MLA_NOPE).astype(bf16)
    w_uv = mla_w_uv.reshape(depth, MLA_KV_RANK, MLA_HEADS * MLA_V).astype(bf16)
    w_branch_b, w_out_b = w_branch.astype(bf16), w_out.astype(bf16)
    w_up_b, w_down_b = ffn_w_up.astype(bf16), ffn_w_down.astype(bf16)
    na_bias = _na_bias_table(na_rpb, t_len // GRID_W)
    kc_na = cache_na_k.reshape(n_lat, depth, p_len, NA_HEADS * NA_DH).astype(bf16)
    vc_na = cache_na_v.reshape(n_lat, depth, p_len, NA_HEADS * NA_DH).astype(bf16)

    def state_init(*trailing):
        return jnp.zeros((n_ctx, depth) + trailing, f32)

    st_mla = [state_init(s_len, MLA_KV_RANK), state_init(s_len, MLA_ROPE)]
    st_na = [state_init(s_len, NA_HEADS * NA_DH), state_init(s_len, NA_HEADS * NA_DH)]
    st_hg = state_init(2, HG_HEADS, HG_K, HG_V)
    for l in range(depth):
        sh1, sc1, g1, sh2, sc2, g2 = (mods[l, k] for k in range(6))
        z, kpe = _normmod_matmul(x, norm1_g[l], sc1, sh1, w_in_b, l, group_of_tile, tm, _pow2_tile(2048, lay.width),
                                 "in_proj", w_side=w_kpe_b)

        q_mla = _mla_q(z, lay, mla_q_norm_g[l], cos, sin, rope_block, tp)
        k_mla, v_mla, *st_mla = _mla_kv(
            z, lay.ckv, kpe, 0, mla_kv_norm_g[l], mla_k_norm_g[l], w_uk, w_uv, l, cos, sin, rope_block, tp, True, True,
            "mla_kv", state=_CtxState(st_mla, (MLA_KV_RANK, MLA_ROPE), l, depth, n_ctx, s_len, tp))
        ckv_c = cache_mla_ckv[:, l].reshape(n_lat * p_len, MLA_KV_RANK)
        kpe_c = jnp.pad(cache_mla_kpe[:, l].reshape(n_lat * p_len, MLA_ROPE), ((0, 0), (0, LANES - MLA_ROPE)))
        tc = _pow2_tile(512, n_lat * p_len)
        k_c, v_c = _mla_kv(ckv_c, 0, kpe_c, 0, mla_kv_norm_g[l], mla_k_norm_g[l], w_uk, w_uv, l,
                           cos, sin, lambda i: 0, tc, False, False, "mla_kv_cache")
        tq_c = _pow2_tile(256, s_len)
        o_mla_c = _flash(q_mla, k_mla, v_mla, batch=n_ctx, heads=MLA_HEADS, t_len=s_len, s_len=s_len, row0=0,
                         dq=MLA_QPAD, dv=MLA_V, tq=tq_c, tk=tq_c, heads_per_step=MLA_HEADS, name="mla_attn_ctx")
        tq_l = _pow2_tile(512, t_len, p_len)
        o_mla_l = _flash(q_mla, k_mla, v_mla, batch=n_lat, heads=MLA_HEADS, t_len=t_len, s_len=t_len, row0=m_ctx,
                         dq=MLA_QPAD, dv=MLA_V, tq=tq_l, tk=tq_l, heads_per_step=4, extra=(k_c, v_c, p_len),
                         name="mla_attn_latent")

        lb = lower[:, l]
        lb_tabs = (jnp.log(lb) * LOG2E, jnp.log1p(-lb) * LOG2E, 1.0 - lb)
        o_hg_c, st_hg = _hgrn(z, lay, lb_tabs, hgrn_norm_g[l], hg_tabs, batch=n_ctx, t_len=s_len, row0=0,
                              s0=None, state_out=(st_hg, l, depth), heads_per_step=HG_HEADS)
        o_hg_l, _ = _hgrn(z, lay, lb_tabs, hgrn_norm_g[l], hg_tabs, batch=n_lat, t_len=t_len, row0=m_ctx,
                          s0=state_hgrn[:, l])

        q_na, k_na, v_na, *st_na = _na_prep(
            z, lay, na_q_norm_g[l], na_k_norm_g[l], tp,
            _CtxState(st_na, (NA_HEADS * NA_DH, NA_HEADS * NA_DH), l, depth, n_ctx, s_len, tp))
        o_na_c = _flash(q_na, k_na, v_na, batch=n_ctx, heads=NA_HEADS, t_len=s_len, s_len=s_len, row0=0,
                        dq=NA_DH, dv=NA_DH, tq=tq_c, tk=tq_c, heads_per_step=NA_HEADS, name="na_attn_ctx")
        o_na_l = _na_latent(q_na, k_na, v_na, kc_na, vc_na, na_bias, l, batch=n_lat, t_len=t_len, p_len=p_len, row0=m_ctx)

        merged = _merge((o_mla_c, o_mla_l), (o_hg_c, o_hg_l), (o_na_c, o_na_l), z, lay, w_branch_b, l, tp, tn)
        x = _proj_residual(merged, w_out_b, l, x, g1, group_fn(tp), tp, d)

        x = _ffn(x, norm2_g[l], sc2, sh2, w_up_b, ffn_conv_w[l], ffn_conv_b[l], w_down_b, l, g2, group_of_tile, tm,
                 _pow2_tile(512, ffn), m_ctx, s_len, t_len)

    y_prompt = x[:m_ctx].reshape(n_ctx, s_len, d)
    y_sample = x[m_ctx:].reshape(n_lat, t_len, d)
    na_shape = (n_ctx, depth, s_len, NA_HEADS, NA_DH)
    return (y_prompt, y_sample, st_mla[0], st_mla[1], st_na[0].reshape(na_shape), st_na[1].reshape(na_shape), st_hg)
```
